```python
import jax, jax.numpy as jnp
from jax import lax
import numpy as np

D_MODEL = 1024
BATCH = 8
SEQ = 8192
DEPTH = 1

MLA_HEADS = 8
MLA_Q_LORA = 256
MLA_KV_LORA = 128
MLA_NOPE_DIM = 64
MLA_ROPE_DIM = 32
MLA_V_DIM = 64
FOX_HEADS = 8
FOX_HEAD_DIM = 64
D_FF = 4 * D_MODEL
Q_BLOCK = 128
ROPE_THETA = 10000.0
NORM_EPS = 1e-6
MAX_POS_OFFSET = 1024
FORGET_BIAS_LO = 1.0
FORGET_BIAS_HI = 6.0

IN_SPLITS = (MLA_Q_LORA, MLA_KV_LORA, MLA_ROPE_DIM,
             FOX_HEADS * FOX_HEAD_DIM, FOX_HEADS * FOX_HEAD_DIM, FOX_HEADS * FOX_HEAD_DIM,
             FOX_HEADS, D_MODEL, D_MODEL)
D_IN = sum(IN_SPLITS)
FORGET_COL_START = sum(IN_SPLITS[:6])

kernel_name = "hybrid_mla_fox_gated_block"


def rms_norm(x, g):
    xf = x.astype(jnp.float32)
    y = xf * lax.rsqrt(jnp.mean(xf * xf, axis=-1, keepdims=True) + NORM_EPS)
    return (y * g.astype(jnp.float32)).astype(x.dtype)


def rope(x, positions):
    half = x.shape[-1] // 2
    inv_freq = ROPE_THETA ** (-jnp.arange(half, dtype=jnp.float32) / half)
    ang = positions.astype(jnp.float32)[:, None, :, None] * inv_freq
    cos, sin = jnp.cos(ang), jnp.sin(ang)
    xf = x.astype(jnp.float32)
    x1, x2 = xf[..., :half], xf[..., half:]
    out = jnp.concatenate([x1 * cos - x2 * sin, x2 * cos + x1 * sin], axis=-1)
    return out.astype(x.dtype)


def causal_block_attention(q, k, v, scale, log_f_cum=None):
    B, H, S, _ = q.shape
    dv = v.shape[-1]
    n_blocks = S // Q_BLOCK
    kf = k.astype(jnp.float32)
    vf = v.astype(jnp.float32)
    k_pos = jnp.arange(S)

    def one_block(i):
        start = i * Q_BLOCK
        qb = lax.dynamic_slice_in_dim(q, start, Q_BLOCK, axis=2).astype(jnp.float32)
        s = jnp.einsum('bhqd,bhkd->bhqk', qb, kf) * scale
        if log_f_cum is not None:
            fq = lax.dynamic_slice_in_dim(log_f_cum, start, Q_BLOCK, axis=2)
            s = s + fq[..., :, None] - log_f_cum[..., None, :]
        q_pos = start + jnp.arange(Q_BLOCK)
        mask = k_pos[None, :] <= q_pos[:, None]
        s = jnp.where(mask, s, -jnp.inf)
        p = jax.nn.softmax(s, axis=-1)
        return jnp.einsum('bhqk,bhkd->bhqd', p, vf)

    out = lax.map(one_block, jnp.arange(n_blocks))
    out = jnp.moveaxis(out, 0, 2).reshape(B, H, S, dv)
    return out.astype(v.dtype)


def split_heads(t, n_heads):
    B, S, _ = t.shape
    return t.reshape(B, S, n_heads, -1).transpose(0, 2, 1, 3)


def merge_heads(t):
    B, H, S, d = t.shape
    return t.transpose(0, 2, 1, 3).reshape(B, S, H * d)


def _fwd_setup_inputs(seed: int = 0) -> dict:
    key = jax.random.key(seed)
    ks = jax.random.split(key, 20)
    L = DEPTH

    def w(k, shape, fan_in):
        return jax.random.normal(k, shape, jnp.float32) * fan_in ** -0.5

    def gain(k, shape):
        return 1.0 + 0.05 * jax.random.normal(k, shape, jnp.float32)

    x = jax.random.normal(ks[0], (BATCH, SEQ, D_MODEL), jnp.float32)
    offset = jax.random.randint(ks[1], (BATCH, 1), 0, MAX_POS_OFFSET, dtype=jnp.int32)
    positions = (offset + jnp.arange(SEQ, dtype=jnp.int32)[None, :]).astype(jnp.int32)

    b_in = 0.02 * jax.random.normal(ks[2], (L, D_IN), jnp.float32)
    forget_bias = jnp.linspace(FORGET_BIAS_LO, FORGET_BIAS_HI, FOX_HEADS, dtype=jnp.float32)
    b_in = b_in.at[:, FORGET_COL_START:FORGET_COL_START + FOX_HEADS].add(forget_bias)

    return {
        "x": x,
        "positions": positions,
        "ln_pre_mix": gain(ks[3], (L, D_MODEL)),
        "ln_post_mix": gain(ks[4], (L, D_MODEL)),
        "ln_pre_mlp": gain(ks[5], (L, D_MODEL)),
        "ln_post_mlp": gain(ks[6], (L, D_MODEL)),
        "w_in": w(ks[7], (L, D_MODEL, D_IN), D_MODEL),
        "b_in": b_in,
        "q_a_norm": gain(ks[8], (L, MLA_Q_LORA)),
        "w_uq": w(ks[9], (L, MLA_Q_LORA, MLA_HEADS * (MLA_NOPE_DIM + MLA_ROPE_DIM)), MLA_Q_LORA),
        "kv_a_norm": gain(ks[10], (L, MLA_KV_LORA)),
        "w_uk": w(ks[11], (L, MLA_KV_LORA, MLA_HEADS * MLA_NOPE_DIM), MLA_KV_LORA),
        "w_uv": w(ks[12], (L, MLA_KV_LORA, MLA_HEADS * MLA_V_DIM), MLA_KV_LORA),
        "w_o_mla": w(ks[13], (L, MLA_HEADS * MLA_V_DIM, D_MODEL), MLA_HEADS * MLA_V_DIM),
        "w_o_fox": w(ks[14], (L, FOX_HEADS * FOX_HEAD_DIM, D_MODEL), FOX_HEADS * FOX_HEAD_DIM),
        "w_out": w(ks[15], (L, D_MODEL, D_MODEL), D_MODEL),
        "w_ff1": w(ks[16], (L, D_MODEL, D_FF), D_MODEL),
        "w_ff2": w(ks[17], (L, D_FF, D_MODEL), D_FF),
    }


def _fwd_reference(x, positions, ln_pre_mix, ln_post_mix, ln_pre_mlp, ln_post_mlp,
              w_in, b_in, q_a_norm, w_uq, kv_a_norm, w_uk, w_uv,
              w_o_mla, w_o_fox, w_out, w_ff1, w_ff2):
    split_idx = [int(v) for v in np.cumsum(IN_SPLITS)[:-1]]
    for l in range(DEPTH):
        h = rms_norm(x, ln_pre_mix[l])
        z = jnp.einsum('bsd,de->bse', h, w_in[l]) + b_in[l]
        (c_q, c_kv, k_r, fq, fk, fv, f_logit, ga_logit, gb_logit) = jnp.split(z, split_idx, axis=-1)

        c_q = rms_norm(c_q, q_a_norm[l])
        q_full = split_heads(jnp.einsum('bsr,re->bse', c_q, w_uq[l]), MLA_HEADS)
        q_nope, q_rot = q_full[..., :MLA_NOPE_DIM], q_full[..., MLA_NOPE_DIM:]
        q_rot = rope(q_rot, positions)
        c_kv = rms_norm(c_kv, kv_a_norm[l])
        k_nope = split_heads(jnp.einsum('bsr,re->bse', c_kv, w_uk[l]), MLA_HEADS)
        v_mla = split_heads(jnp.einsum('bsr,re->bse', c_kv, w_uv[l]), MLA_HEADS)
        k_rot = rope(k_r[:, None, :, :], positions)
        B, H, S, _ = k_nope.shape
        q_mla = jnp.concatenate([q_nope, q_rot], axis=-1)
        k_mla = jnp.concatenate([k_nope, jnp.broadcast_to(k_rot, (B, H, S, MLA_ROPE_DIM))], axis=-1)
        o_mla = causal_block_attention(q_mla, k_mla, v_mla, (MLA_NOPE_DIM + MLA_ROPE_DIM) ** -0.5)
        y_mla = jnp.einsum('bse,ed->bsd', merge_heads(o_mla), w_o_mla[l])

        q_fox = split_heads(fq, FOX_HEADS)
        k_fox = split_heads(fk, FOX_HEADS)
        v_fox = split_heads(fv, FOX_HEADS)
        log_f = jax.nn.log_sigmoid(f_logit.astype(jnp.float32))
        log_f_cum = jnp.cumsum(log_f, axis=1).transpose(0, 2, 1)
        o_fox = causal_block_attention(q_fox, k_fox, v_fox, FOX_HEAD_DIM ** -0.5, log_f_cum)
        y_fox = jnp.einsum('bse,ed->bsd', merge_heads(o_fox), w_o_fox[l])

        merged = jax.nn.sigmoid(ga_logit) * y_mla + jax.nn.sigmoid(gb_logit) * y_fox
        mix_out = jnp.einsum('bsd,de->bse', merged, w_out[l])
        x = x + rms_norm(mix_out, ln_post_mix[l])

        h2 = rms_norm(x, ln_pre_mlp[l])
        u = jnp.einsum('bsd,df->bsf', h2, w_ff1[l])
        m = jnp.einsum('bsf,fd->bsd', jnp.square(jax.nn.relu(u)), w_ff2[l])
        x = x + rms_norm(m, ln_post_mlp[l])
    return x


import jax as _jax
import jax.numpy as _jnp

TWIN_FORMAT = 'train_step'
FWD_PARAMS = ['x', 'positions', 'ln_pre_mix', 'ln_post_mix', 'ln_pre_mlp', 'ln_post_mlp', 'w_in', 'b_in', 'q_a_norm', 'w_uq', 'kv_a_norm', 'w_uk', 'w_uv', 'w_o_mla', 'w_o_fox', 'w_out', 'w_ff1', 'w_ff2']
TWIN_WEIGHTS = ['ln_pre_mix', 'ln_post_mix', 'ln_pre_mlp', 'ln_post_mlp', 'w_in', 'b_in', 'q_a_norm', 'w_uq', 'kv_a_norm', 'w_uk', 'w_uv', 'w_o_mla', 'w_o_fox', 'w_out', 'w_ff1', 'w_ff2']
TWIN_DIFF_INPUT = 'x'
TWIN_INPUTS = ['x', 'positions', 'ln_pre_mix', 'ln_post_mix', 'ln_pre_mlp', 'ln_post_mlp', 'w_in', 'b_in', 'q_a_norm', 'w_uq', 'kv_a_norm', 'w_uk', 'w_uv', 'w_o_mla', 'w_o_fox', 'w_out', 'w_ff1', 'w_ff2', 'loss_target', 'm_ln_pre_mix', 'm_ln_post_mix', 'm_ln_pre_mlp', 'm_ln_post_mlp', 'm_w_in', 'm_b_in', 'm_q_a_norm', 'm_w_uq', 'm_kv_a_norm', 'm_w_uk', 'm_w_uv', 'm_w_o_mla', 'm_w_o_fox', 'm_w_out', 'm_w_ff1', 'm_w_ff2', 'v_ln_pre_mix', 'v_ln_post_mix', 'v_ln_pre_mlp', 'v_ln_post_mlp', 'v_w_in', 'v_b_in', 'v_q_a_norm', 'v_w_uq', 'v_kv_a_norm', 'v_w_uk', 'v_w_uv', 'v_w_o_mla', 'v_w_o_fox', 'v_w_out', 'v_w_ff1', 'v_w_ff2']
TWIN_OUTPUTS = ['loss', 'grad_x', 'grad_ln_pre_mix', 'grad_ln_post_mix', 'grad_ln_pre_mlp', 'grad_ln_post_mlp', 'grad_w_in', 'grad_b_in', 'grad_q_a_norm', 'grad_w_uq', 'grad_kv_a_norm', 'grad_w_uk', 'grad_w_uv', 'grad_w_o_mla', 'grad_w_o_fox', 'grad_w_out', 'grad_w_ff1', 'grad_w_ff2', 'delta_ln_pre_mix', 'delta_ln_post_mix', 'delta_ln_pre_mlp', 'delta_ln_post_mlp', 'delta_w_in', 'delta_b_in', 'delta_q_a_norm', 'delta_w_uq', 'delta_kv_a_norm', 'delta_w_uk', 'delta_w_uv', 'delta_w_o_mla', 'delta_w_o_fox', 'delta_w_out', 'delta_w_ff1', 'delta_w_ff2', 'new_m_ln_pre_mix', 'new_m_ln_post_mix', 'new_m_ln_pre_mlp', 'new_m_ln_post_mlp', 'new_m_w_in', 'new_m_b_in', 'new_m_q_a_norm', 'new_m_w_uq', 'new_m_kv_a_norm', 'new_m_w_uk', 'new_m_w_uv', 'new_m_w_o_mla', 'new_m_w_o_fox', 'new_m_w_out', 'new_m_w_ff1', 'new_m_w_ff2', 'new_v_ln_pre_mix', 'new_v_ln_post_mix', 'new_v_ln_pre_mlp', 'new_v_ln_post_mlp', 'new_v_w_in', 'new_v_b_in', 'new_v_q_a_norm', 'new_v_w_uq', 'new_v_kv_a_norm', 'new_v_w_uk', 'new_v_w_uv', 'new_v_w_o_mla', 'new_v_w_o_fox', 'new_v_w_out', 'new_v_w_ff1', 'new_v_w_ff2']
TWIN_LEAF_KINDS = {'loss': 'loss', 'grad_x': 'grad_x', 'grad_ln_pre_mix': 'grad_w', 'grad_ln_post_mix': 'grad_w', 'grad_ln_pre_mlp': 'grad_w', 'grad_ln_post_mlp': 'grad_w', 'grad_w_in': 'grad_w', 'grad_b_in': 'grad_w', 'grad_q_a_norm': 'grad_w', 'grad_w_uq': 'grad_w', 'grad_kv_a_norm': 'grad_w', 'grad_w_uk': 'grad_w', 'grad_w_uv': 'grad_w', 'grad_w_o_mla': 'grad_w', 'grad_w_o_fox': 'grad_w', 'grad_w_out': 'grad_w', 'grad_w_ff1': 'grad_w', 'grad_w_ff2': 'grad_w', 'delta_ln_pre_mix': 'delta_w', 'delta_ln_post_mix': 'delta_w', 'delta_ln_pre_mlp': 'delta_w', 'delta_ln_post_mlp': 'delta_w', 'delta_w_in': 'delta_w', 'delta_b_in': 'delta_w', 'delta_q_a_norm': 'delta_w', 'delta_w_uq': 'delta_w', 'delta_kv_a_norm': 'delta_w', 'delta_w_uk': 'delta_w', 'delta_w_uv': 'delta_w', 'delta_w_o_mla': 'delta_w', 'delta_w_o_fox': 'delta_w', 'delta_w_out': 'delta_w', 'delta_w_ff1': 'delta_w', 'delta_w_ff2': 'delta_w', 'new_m_ln_pre_mix': 'new_m', 'new_m_ln_post_mix': 'new_m', 'new_m_ln_pre_mlp': 'new_m', 'new_m_ln_post_mlp': 'new_m', 'new_m_w_in': 'new_m', 'new_m_b_in': 'new_m', 'new_m_q_a_norm': 'new_m', 'new_m_w_uq': 'new_m', 'new_m_kv_a_norm': 'new_m', 'new_m_w_uk': 'new_m', 'new_m_w_uv': 'new_m', 'new_m_w_o_mla': 'new_m', 'new_m_w_o_fox': 'new_m', 'new_m_w_out': 'new_m', 'new_m_w_ff1': 'new_m', 'new_m_w_ff2': 'new_m', 'new_v_ln_pre_mix': 'new_v', 'new_v_ln_post_mix': 'new_v', 'new_v_ln_pre_mlp': 'new_v', 'new_v_ln_post_mlp': 'new_v', 'new_v_w_in': 'new_v', 'new_v_b_in': 'new_v', 'new_v_q_a_norm': 'new_v', 'new_v_w_uq': 'new_v', 'new_v_kv_a_norm': 'new_v', 'new_v_w_uk': 'new_v', 'new_v_w_uv': 'new_v', 'new_v_w_o_mla': 'new_v', 'new_v_w_o_fox': 'new_v', 'new_v_w_out': 'new_v', 'new_v_w_ff1': 'new_v', 'new_v_w_ff2': 'new_v'}


def _forward(args):
    return _fwd_reference(*[args[k] for k in FWD_PARAMS])


def _output_shape():
    def fwd():
        inp = _fwd_setup_inputs(0)
        return _fwd_reference(*[inp[k] for k in FWD_PARAMS])
    out = _jax.eval_shape(fwd)
    return out.shape, out.dtype

N_MICROBATCH = 1
ADAM_LR = 0.001
ADAM_B1 = 0.9
ADAM_B2 = 0.999
ADAM_EPS = 1e-08
ADAM_WD = 0.01
ADAM_STEP = 10
PER_EXAMPLE_BATCH_AXIS = {'x': 0, 'positions': 0, 'loss_target': 0}
SHARED_INPUTS = []
_WEIGHT_DTYPES = {'ln_pre_mix': _jnp.float32, 'ln_post_mix': _jnp.float32, 'ln_pre_mlp': _jnp.float32, 'ln_post_mlp': _jnp.float32, 'w_in': _jnp.float32, 'b_in': _jnp.float32, 'q_a_norm': _jnp.float32, 'w_uq': _jnp.float32, 'kv_a_norm': _jnp.float32, 'w_uk': _jnp.float32, 'w_uv': _jnp.float32, 'w_o_mla': _jnp.float32, 'w_o_fox': _jnp.float32, 'w_out': _jnp.float32, 'w_ff1': _jnp.float32, 'w_ff2': _jnp.float32}
MOMENT_SCALE = {'ln_pre_mix': 1.329447e+00, 'ln_post_mix': 6.429810e+01, 'ln_pre_mlp': 2.069553e+00, 'ln_post_mlp': 6.589424e+01, 'w_in': 6.482502e-01, 'b_in': 4.827829e+01, 'q_a_norm': 6.264555e-01, 'w_uq': 3.441735e-01, 'kv_a_norm': 5.631699e+00, 'w_uk': 4.608030e-01, 'w_uv': 2.133117e+00, 'w_o_mla': 1.546168e+00, 'w_o_fox': 2.241955e+00, 'w_out': 2.847939e+00, 'w_ff1': 9.996912e-01, 'w_ff2': 3.023350e+00}


def _to_microbatches(a, axis):
    t = _jnp.moveaxis(a, axis, 0)
    t = t.reshape((N_MICROBATCH, t.shape[0] // N_MICROBATCH) + t.shape[1:])
    return _jnp.moveaxis(t, 1, axis + 1)


def setup_inputs(seed: int = 0) -> dict:
    inp = _fwd_setup_inputs(seed)
    key = _jax.random.fold_in(_jax.random.key(seed), 7919)
    shape, _ = _output_shape()
    out = dict(inp)
    out["loss_target"] = _jax.random.normal(_jax.random.fold_in(key, 0), shape, _jnp.float32)
    for i, name in enumerate(TWIN_WEIGHTS):
        w = inp[name].astype(_jnp.float32)
        if MOMENT_SCALE is None:
            s = _jnp.sqrt(_jnp.mean(_jnp.square(w)) + 1e-30)
        else:
            s = MOMENT_SCALE[name]
        km, kv = _jax.random.split(_jax.random.fold_in(key, i + 1))
        out[name] = w
        out["m_" + name] = s * _jax.random.normal(km, w.shape, _jnp.float32)
        out["v_" + name] = (s * s) * _jax.random.uniform(kv, w.shape, _jnp.float32, 0.5, 1.5)
    if N_MICROBATCH > 1:
        for name, axis in PER_EXAMPLE_BATCH_AXIS.items():
            out[name] = _to_microbatches(out[name], axis)
    return {'x': out['x'], 'positions': out['positions'], 'ln_pre_mix': out['ln_pre_mix'], 'ln_post_mix': out['ln_post_mix'], 'ln_pre_mlp': out['ln_pre_mlp'], 'ln_post_mlp': out['ln_post_mlp'], 'w_in': out['w_in'], 'b_in': out['b_in'], 'q_a_norm': out['q_a_norm'], 'w_uq': out['w_uq'], 'kv_a_norm': out['kv_a_norm'], 'w_uk': out['w_uk'], 'w_uv': out['w_uv'], 'w_o_mla': out['w_o_mla'], 'w_o_fox': out['w_o_fox'], 'w_out': out['w_out'], 'w_ff1': out['w_ff1'], 'w_ff2': out['w_ff2'], 'loss_target': out['loss_target'], 'm_ln_pre_mix': out['m_ln_pre_mix'], 'm_ln_post_mix': out['m_ln_post_mix'], 'm_ln_pre_mlp': out['m_ln_pre_mlp'], 'm_ln_post_mlp': out['m_ln_post_mlp'], 'm_w_in': out['m_w_in'], 'm_b_in': out['m_b_in'], 'm_q_a_norm': out['m_q_a_norm'], 'm_w_uq': out['m_w_uq'], 'm_kv_a_norm': out['m_kv_a_norm'], 'm_w_uk': out['m_w_uk'], 'm_w_uv': out['m_w_uv'], 'm_w_o_mla': out['m_w_o_mla'], 'm_w_o_fox': out['m_w_o_fox'], 'm_w_out': out['m_w_out'], 'm_w_ff1': out['m_w_ff1'], 'm_w_ff2': out['m_w_ff2'], 'v_ln_pre_mix': out['v_ln_pre_mix'], 'v_ln_post_mix': out['v_ln_post_mix'], 'v_ln_pre_mlp': out['v_ln_pre_mlp'], 'v_ln_post_mlp': out['v_ln_post_mlp'], 'v_w_in': out['v_w_in'], 'v_b_in': out['v_b_in'], 'v_q_a_norm': out['v_q_a_norm'], 'v_w_uq': out['v_w_uq'], 'v_kv_a_norm': out['v_kv_a_norm'], 'v_w_uk': out['v_w_uk'], 'v_w_uv': out['v_w_uv'], 'v_w_o_mla': out['v_w_o_mla'], 'v_w_o_fox': out['v_w_o_fox'], 'v_w_out': out['v_w_out'], 'v_w_ff1': out['v_w_ff1'], 'v_w_ff2': out['v_w_ff2']}


def _loss(weights, diff, rest, loss_target):
    with _jax.named_scope("forward"):
        args = {**rest, TWIN_DIFF_INPUT: diff, **{k: w.astype(_WEIGHT_DTYPES[k]) for k, w in weights.items()}}
        y = _forward(args)
    with _jax.named_scope("loss_head"):
        err = _jnp.square(y.astype(_jnp.float32) - loss_target)
        return 0.5 * _jnp.sum(_jnp.mean(err, axis=-1)) if err.ndim else 0.5 * err


def _adamw(w, g, m, v):
    m = ADAM_B1 * m + (1.0 - ADAM_B1) * g
    v = ADAM_B2 * v + (1.0 - ADAM_B2) * _jnp.square(g)
    m_hat = m / (1.0 - ADAM_B1 ** ADAM_STEP)
    v_hat = v / (1.0 - ADAM_B2 ** ADAM_STEP)
    delta = -ADAM_LR * (m_hat / (_jnp.sqrt(v_hat) + ADAM_EPS) + ADAM_WD * w)
    return delta, m, v


def reference(x, positions, ln_pre_mix, ln_post_mix, ln_pre_mlp, ln_post_mlp, w_in, b_in, q_a_norm, w_uq, kv_a_norm, w_uk, w_uv, w_o_mla, w_o_fox, w_out, w_ff1, w_ff2, loss_target, m_ln_pre_mix, m_ln_post_mix, m_ln_pre_mlp, m_ln_post_mlp, m_w_in, m_b_in, m_q_a_norm, m_w_uq, m_kv_a_norm, m_w_uk, m_w_uv, m_w_o_mla, m_w_o_fox, m_w_out, m_w_ff1, m_w_ff2, v_ln_pre_mix, v_ln_post_mix, v_ln_pre_mlp, v_ln_post_mlp, v_w_in, v_b_in, v_q_a_norm, v_w_uq, v_kv_a_norm, v_w_uk, v_w_uv, v_w_o_mla, v_w_o_fox, v_w_out, v_w_ff1, v_w_ff2):
    given = dict(x=x, positions=positions, ln_pre_mix=ln_pre_mix, ln_post_mix=ln_post_mix, ln_pre_mlp=ln_pre_mlp, ln_post_mlp=ln_post_mlp, w_in=w_in, b_in=b_in, q_a_norm=q_a_norm, w_uq=w_uq, kv_a_norm=kv_a_norm, w_uk=w_uk, w_uv=w_uv, w_o_mla=w_o_mla, w_o_fox=w_o_fox, w_out=w_out, w_ff1=w_ff1, w_ff2=w_ff2, loss_target=loss_target, m_ln_pre_mix=m_ln_pre_mix, m_ln_post_mix=m_ln_post_mix, m_ln_pre_mlp=m_ln_pre_mlp, m_ln_post_mlp=m_ln_post_mlp, m_w_in=m_w_in, m_b_in=m_b_in, m_q_a_norm=m_q_a_norm, m_w_uq=m_w_uq, m_kv_a_norm=m_kv_a_norm, m_w_uk=m_w_uk, m_w_uv=m_w_uv, m_w_o_mla=m_w_o_mla, m_w_o_fox=m_w_o_fox, m_w_out=m_w_out, m_w_ff1=m_w_ff1, m_w_ff2=m_w_ff2, v_ln_pre_mix=v_ln_pre_mix, v_ln_post_mix=v_ln_post_mix, v_ln_pre_mlp=v_ln_pre_mlp, v_ln_post_mlp=v_ln_post_mlp, v_w_in=v_w_in, v_b_in=v_b_in, v_q_a_norm=v_q_a_norm, v_w_uq=v_w_uq, v_kv_a_norm=v_kv_a_norm, v_w_uk=v_w_uk, v_w_uv=v_w_uv, v_w_o_mla=v_w_o_mla, v_w_o_fox=v_w_o_fox, v_w_out=v_w_out, v_w_ff1=v_w_ff1, v_w_ff2=v_w_ff2)
    weights = {n: given[n] for n in TWIN_WEIGHTS}
    shared = {n: given[n] for n in SHARED_INPUTS}
    per_example = {n: given[n] for n in ['x', 'positions']}
    grad_fn = _jax.value_and_grad(_loss, argnums=(0, 1))

    def one_microbatch(ex, loss_target):
        ex = dict(ex)
        diff = ex.pop(TWIN_DIFF_INPUT)
        return grad_fn(weights, diff, {**shared, **ex}, loss_target)

    if N_MICROBATCH == 1:
        loss, (grad_w, grad_x) = one_microbatch(per_example, given["loss_target"])
    else:
        def body(carry, xs):
            loss_sum, grad_sum = carry
            l_k, (gw_k, gx_k) = one_microbatch(xs[0], xs[1])
            with _jax.named_scope("update"):
                return (loss_sum + l_k, _jax.tree.map(_jnp.add, grad_sum, gw_k)), gx_k

        init = (_jnp.zeros((), _jnp.float32), _jax.tree.map(_jnp.zeros_like, weights))
        (loss, grad_w), grad_x = _jax.lax.scan(body, init, (per_example, given["loss_target"]))
    with _jax.named_scope("update"):
        delta_w, new_m, new_v = {}, {}, {}
        for n in TWIN_WEIGHTS:
            delta_w[n], new_m[n], new_v[n] = _adamw(weights[n], grad_w[n], given["m_" + n], given["v_" + n])
    return (loss, grad_x, *[grad_w[n] for n in TWIN_WEIGHTS], *[delta_w[n] for n in TWIN_WEIGHTS],
            *[new_m[n] for n in TWIN_WEIGHTS], *[new_v[n] for n in TWIN_WEIGHTS])
```

```python
import numpy as np
import jax
import jax.numpy as jnp
from jax import lax
from jax.experimental import pallas as pl
from jax.experimental.pallas import tpu as pltpu

F32 = jnp.float32
BF16 = jnp.bfloat16
MESH = pl.DeviceIdType.MESH

D_MODEL = 1024
N_HEADS = 8
Q_LORA = 256
KV_LORA = 128
NOPE = 64
ROPE = 32
HEAD_V = 64
FOX_D = 64
D_FF = 4096
D_IN = 4008
D_IN_PAD = 4096
ROPE_THETA = 10000.0
NORM_EPS = 1e-6
N_DEV = 8

ADAM_LR = 0.001
ADAM_B1 = 0.9
ADAM_B2 = 0.999
ADAM_EPS = 1e-08
ADAM_WD = 0.01
ADAM_STEP = 10

LANES = 128
ROW_TILE = 512
ATTN_TILE = 512
VMEM_LIMIT = 48 * 1024 * 1024

P_GA, P_GB, P_FQ, P_FK, P_FV, P_CQ, P_CKV, P_SMALL = 0, 1024, 2048, 2560, 3072, 3584, 3840, 3968
S_FL = 32

SHARDED = (
    ("w_in", 1024, 501, True), ("w_uq", 256, 96, True), ("w_uk", 128, 64, True), ("w_uv", 128, 64, True),
    ("w_o_mla", 512, 128, True), ("w_o_fox", 512, 128, True), ("w_out", 128, 1024, False),
    ("w_ff1", 1024, 512, True), ("w_ff2", 512, 1024, False),
)
PACK_ELEMS = sum(r * c for _, r, c, _ in SHARDED)
PACK_ROWS = -(-PACK_ELEMS // (LANES * 128)) * 128
SMALL_ROWS = 80
SMALL_LOSS_ROW = 72


def _cparams(sem=None):
    return pltpu.CompilerParams(dimension_semantics=sem, vmem_limit_bytes=VMEM_LIMIT)


def _mm(a, b, *, name, ta=False, bias=None, extras=(), epi=None, outs=(BF16,), tm=ROW_TILE, tn=1024, tk=1024):
    if ta:
        K, M = a.shape
        tk = min(512, K)
        tm = min(1024, M)
    else:
        M, K = a.shape
        tm = min(tm, M)
        tk = min(tk, K)
    N = b.shape[1]
    tn = min(tn, N)
    nk = K // tk
    n_ex = len(extras)
    has_bias = bias is not None

    def body(*refs):
        a_ref, b_ref = refs[0], refs[1]
        pos = 2
        bias_ref = None
        if has_bias:
            bias_ref = refs[pos]
            pos += 1
        ex_refs = refs[pos:pos + n_ex]
        pos += n_ex
        o_refs = refs[pos:pos + len(outs)]
        pos += len(outs)
        av = a_ref[...].astype(BF16)
        bv = b_ref[...].astype(BF16)
        if ta:
            part = lax.dot_general(av, bv, (((0,), (0,)), ((), ())), preferred_element_type=F32)
        else:
            part = jnp.dot(av, bv, preferred_element_type=F32)

        def finish(acc):
            if has_bias:
                acc = acc + bias_ref[...]
            res = (acc,) if epi is None else epi(acc, *[r[...] for r in ex_refs])
            for o_ref, val in zip(o_refs, res):
                o_ref[...] = val.astype(o_ref.dtype)

        if nk == 1:
            finish(part)
        else:
            acc_ref = refs[pos]
            k = pl.program_id(2)

            @pl.when(k == 0)
            def _():
                acc_ref[...] = part

            @pl.when(k > 0)
            def _():
                acc_ref[...] += part

            @pl.when(k == nk - 1)
            def _():
                finish(acc_ref[...])

    if ta:
        a_spec = pl.BlockSpec((tk, tm), lambda i, j, k: (k, i))
    else:
        a_spec = pl.BlockSpec((tm, tk), lambda i, j, k: (i, k))
    in_specs = [a_spec, pl.BlockSpec((tk, tn), lambda i, j, k: (k, j))]
    args = [a, b]
    if has_bias:
        in_specs.append(pl.BlockSpec((1, tn), lambda i, j, k: (0, j)))
        args.append(bias)
    for e in extras:
        in_specs.append(pl.BlockSpec((tm, tn), lambda i, j, k: (i, j)))
        args.append(e)
    res = pl.pallas_call(
        body, name=name, grid=(M // tm, N // tn, nk),
        in_specs=in_specs,
        out_specs=[pl.BlockSpec((tm, tn), lambda i, j, k: (i, j)) for _ in outs],
        out_shape=[jax.ShapeDtypeStruct((M, N), dt) for dt in outs],
        scratch_shapes=[pltpu.VMEM((tm, tn), F32)] if nk > 1 else [],
        compiler_params=_cparams(("parallel", "parallel", "arbitrary")),
    )(*args)
    return res[0] if len(outs) == 1 else res


def _rowwise(fn, rows, bcasts, outs, accs=(), *, name, tm=ROW_TILE):
    arrs, specs = [], []
    for r in rows:
        if isinstance(r, tuple):
            arr, w, cb = r
            specs.append(pl.BlockSpec((tm, w), lambda i, cb=cb: (i, cb)))
        else:
            arr = r
            specs.append(pl.BlockSpec((tm, arr.shape[1]), lambda i: (i, 0)))
        arrs.append(arr)
    T = arrs[0].shape[0]
    tm = min(tm, T)
    specs = [pl.BlockSpec((tm,) + tuple(s.block_shape[1:]), s.index_map) for s in specs]
    for b in bcasts:
        arrs.append(b)
        specs.append(pl.BlockSpec(b.shape, lambda i: (0, 0)))
    n_in, n_out = len(arrs), len(outs)

    def body(*refs):
        vals = [r[...] for r in refs[:n_in]]
        res = fn(*vals)
        if not isinstance(res, (tuple, list)):
            res = (res,)
        for o_ref, val in zip(refs[n_in:n_in + n_out], res[:n_out]):
            o_ref[...] = val.astype(o_ref.dtype)
        i = pl.program_id(0)
        for a_ref, val in zip(refs[n_in + n_out:], res[n_out:]):
            col = jnp.sum(val.astype(F32), axis=0, keepdims=True)

            @pl.when(i == 0)
            def _(a_ref=a_ref, col=col):
                a_ref[...] = col

            @pl.when(i > 0)
            def _(a_ref=a_ref, col=col):
                a_ref[...] += col

    res = pl.pallas_call(
        body, name=name, grid=(T // tm,),
        in_specs=specs,
        out_specs=[pl.BlockSpec((tm, c), lambda i: (i, 0)) for c, _ in outs]
        + [pl.BlockSpec((1, c), lambda i: (0, 0)) for c in accs],
        out_shape=[jax.ShapeDtypeStruct((T, c), dt) for c, dt in outs]
        + [jax.ShapeDtypeStruct((1, c), F32) for c in accs],
        compiler_params=_cparams(("arbitrary",)),
    )(*arrs)
    return res


def _dot3(tri, x):
    hi = x.astype(BF16)
    r1 = x - hi.astype(F32)
    mid = r1.astype(BF16)
    lo = (r1 - mid.astype(F32)).astype(BF16)
    d = lambda t: jnp.dot(tri, t, preferred_element_type=F32)
    return d(hi) + d(mid) + d(lo)


def _seq_cumsum(x, fn, *, reverse, name, tm=256):
    T, C = x.shape
    tm = min(tm, T)
    n = T // tm

    def body(x_ref, o_ref, carry_ref):
        i = pl.program_id(0)

        @pl.when(i == 0)
        def _():
            carry_ref[...] = jnp.zeros_like(carry_ref)

        v = fn(x_ref[...])
        r = lax.broadcasted_iota(jnp.int32, (tm, tm), 0)
        c = lax.broadcasted_iota(jnp.int32, (tm, tm), 1)
        tri = jnp.where((r <= c) if reverse else (r >= c), 1.0, 0.0).astype(BF16)
        carry = carry_ref[0:1, :]
        o_ref[...] = _dot3(tri, v) + carry
        carry_ref[0:1, :] = carry + jnp.sum(v, axis=0, keepdims=True)

    idx = (lambda i: (n - 1 - i, 0)) if reverse else (lambda i: (i, 0))
    return pl.pallas_call(
        body, name=name, grid=(n,),
        in_specs=[pl.BlockSpec((tm, C), idx)],
        out_specs=pl.BlockSpec((tm, C), idx),
        out_shape=jax.ShapeDtypeStruct((T, C), F32),
        scratch_shapes=[pltpu.VMEM((8, C), F32)],
        compiler_params=_cparams(("arbitrary",)),
    )(x)


_NT = (((1,), (1,)), ((), ()))


def _flash_fwd(q, k, v, scale, fcol=None, frow=None, *, name, tq=ATTN_TILE):
    H, T, dk = q.shape
    dv = v.shape[-1]
    tq = min(tq, T)
    nq = T // tq
    fox = fcol is not None
    k4 = k.reshape(H, nq, tq, dk)
    v4 = v.reshape(H, nq, tq, dv)

    def body(*refs):
        if fox:
            q_ref, k_ref, v_ref, fc_ref, fr_ref, o_ref, lse_ref, m_s, l_s, acc_s = refs
        else:
            q_ref, k_ref, v_ref, o_ref, lse_ref, m_s, l_s, acc_s = refs
        i = pl.program_id(1)
        m_s[...] = jnp.full(m_s.shape, -jnp.inf, F32)
        l_s[...] = jnp.zeros_like(l_s)
        acc_s[...] = jnp.zeros_like(acc_s)
        qv = q_ref[0]

        def step(j, masked):
            kj = k_ref[0, j]
            vj = v_ref[0, j]
            s = lax.dot_general(qv, kj, _NT, preferred_element_type=F32) * scale
            if fox:
                s = s + fc_ref[0] - fr_ref[0, j]
            if masked:
                r = lax.broadcasted_iota(jnp.int32, (tq, tq), 0)
                c = lax.broadcasted_iota(jnp.int32, (tq, tq), 1)
                s = jnp.where(c <= r, s, -jnp.inf)
            m_prev = m_s[...]
            m_new = jnp.maximum(m_prev, jnp.max(s, axis=1, keepdims=True))
            p = jnp.exp(s - m_new)
            alpha = jnp.exp(m_prev - m_new)
            l_s[...] = alpha * l_s[...] + jnp.sum(p, axis=1, keepdims=True)
            acc_s[...] = alpha * acc_s[...] + jnp.dot(p.astype(BF16), vj, preferred_element_type=F32)
            m_s[...] = m_new

        def loop_body(j, carry):
            step(j, False)
            return carry

        lax.fori_loop(0, i, loop_body, 0)
        step(i, True)
        l = l_s[...]
        o_ref[0] = (acc_s[...] / l).astype(o_ref.dtype)
        lse_ref[0] = m_s[...] + jnp.log(l)

    in_specs = [
        pl.BlockSpec((1, tq, dk), lambda h, i: (h, i, 0)),
        pl.BlockSpec((1, nq, tq, dk), lambda h, i: (h, 0, 0, 0)),
        pl.BlockSpec((1, nq, tq, dv), lambda h, i: (h, 0, 0, 0)),
    ]
    args = [q, k4, v4]
    if fox:
        in_specs += [pl.BlockSpec((1, tq, 1), lambda h, i: (h, i, 0)),
                     pl.BlockSpec((1, nq, 1, tq), lambda h, i: (h, 0, 0, 0))]
        args += [fcol, frow]
    return pl.pallas_call(
        body, name=name, grid=(H, nq),
        in_specs=in_specs,
        out_specs=[pl.BlockSpec((1, tq, dv), lambda h, i: (h, i, 0)),
                   pl.BlockSpec((1, tq, 1), lambda h, i: (h, i, 0))],
        out_shape=[jax.ShapeDtypeStruct((H, T, dv), BF16), jax.ShapeDtypeStruct((H, T, 1), F32)],
        scratch_shapes=[pltpu.VMEM((tq, 1), F32), pltpu.VMEM((tq, 1), F32), pltpu.VMEM((tq, dv), F32)],
        compiler_params=_cparams(("parallel", "arbitrary")),
    )(*args)


def _flash_dq(q, k, v, o, do, lse, scale, fcol=None, frow=None, *, name, tq=ATTN_TILE):
    H, T, dk = q.shape
    dv = v.shape[-1]
    tq = min(tq, T)
    nq = T // tq
    fox = fcol is not None
    k4 = k.reshape(H, nq, tq, dk)
    v4 = v.reshape(H, nq, tq, dv)

    def body(*refs):
        if fox:
            (q_ref, k_ref, v_ref, o_ref, do_ref, lse_ref, fc_ref, fr_ref, dq_ref, dl_ref, df_ref,
             acc_s, df_s) = refs
        else:
            q_ref, k_ref, v_ref, o_ref, do_ref, lse_ref, dq_ref, dl_ref, acc_s = refs
        i = pl.program_id(1)
        qv = q_ref[0]
        dov = do_ref[0]
        lse_c = lse_ref[0]
        delta = jnp.sum(dov.astype(F32) * o_ref[0].astype(F32), axis=1, keepdims=True)
        dl_ref[0] = delta
        acc_s[...] = jnp.zeros_like(acc_s)
        if fox:
            df_s[...] = jnp.zeros_like(df_s)

        def step(j, masked):
            kj = k_ref[0, j]
            vj = v_ref[0, j]
            s = lax.dot_general(qv, kj, _NT, preferred_element_type=F32) * scale
            if fox:
                s = s + fc_ref[0] - fr_ref[0, j]
            if masked:
                r = lax.broadcasted_iota(jnp.int32, (tq, tq), 0)
                c = lax.broadcasted_iota(jnp.int32, (tq, tq), 1)
                s = jnp.where(c <= r, s, -jnp.inf)
            p = jnp.exp(s - lse_c)
            dp = lax.dot_general(dov, vj, _NT, preferred_element_type=F32)
            ds = p * (dp - delta)
            acc_s[...] += jnp.dot(ds.astype(BF16), kj, preferred_element_type=F32)
            if fox:
                df_s[...] += jnp.sum(ds, axis=1, keepdims=True)

        def loop_body(j, carry):
            step(j, False)
            return carry

        lax.fori_loop(0, i, loop_body, 0)
        step(i, True)
        dq_ref[0] = acc_s[...] * scale
        if fox:
            df_ref[0] = df_s[...]

    in_specs = [
        pl.BlockSpec((1, tq, dk), lambda h, i: (h, i, 0)),
        pl.BlockSpec((1, nq, tq, dk), lambda h, i: (h, 0, 0, 0)),
        pl.BlockSpec((1, nq, tq, dv), lambda h, i: (h, 0, 0, 0)),
        pl.BlockSpec((1, tq, dv), lambda h, i: (h, i, 0)),
        pl.BlockSpec((1, tq, dv), lambda h, i: (h, i, 0)),
        pl.BlockSpec((1, tq, 1), lambda h, i: (h, i, 0)),
    ]
    args = [q, k4, v4, o, do, lse]
    if fox:
        in_specs += [pl.BlockSpec((1, tq, 1), lambda h, i: (h, i, 0)),
                     pl.BlockSpec((1, nq, 1, tq), lambda h, i: (h, 0, 0, 0))]
        args += [fcol, frow]
    col = pl.BlockSpec((1, tq, 1), lambda h, i: (h, i, 0))
    col_shape = jax.ShapeDtypeStruct((H, T, 1), F32)
    return pl.pallas_call(
        body, name=name, grid=(H, nq),
        in_specs=in_specs,
        out_specs=[pl.BlockSpec((1, tq, dk), lambda h, i: (h, i, 0)), col] + ([col] if fox else []),
        out_shape=[jax.ShapeDtypeStruct((H, T, dk), F32), col_shape] + ([col_shape] if fox else []),
        scratch_shapes=[pltpu.VMEM((tq, dk), F32)] + ([pltpu.VMEM((tq, 1), F32)] if fox else []),
        compiler_params=_cparams(("parallel", "arbitrary")),
    )(*args)


def _flash_dkv(q, k, v, do, lse_row, delta_row, scale, fcol=None, frow=None, *, name, tq=ATTN_TILE):
    H, T, dk = q.shape
    dv = v.shape[-1]
    tq = min(tq, T)
    nq = T // tq
    fox = fcol is not None
    q4 = q.reshape(H, nq, tq, dk)
    do4 = do.reshape(H, nq, tq, dv)

    def body(*refs):
        if fox:
            (k_ref, v_ref, q_ref, do_ref, lse_ref, dl_ref, fc_ref, fr_ref,
             dk_ref, dv_ref, df_ref, dk_s, dv_s, df_s) = refs
        else:
            k_ref, v_ref, q_ref, do_ref, lse_ref, dl_ref, dk_ref, dv_ref, dk_s, dv_s = refs
        j = pl.program_id(1)
        kj = k_ref[0]
        vj = v_ref[0]
        dk_s[...] = jnp.zeros_like(dk_s)
        dv_s[...] = jnp.zeros_like(dv_s)
        if fox:
            df_s[...] = jnp.zeros_like(df_s)

        def step(i, masked):
            qi = q_ref[0, i]
            doi = do_ref[0, i]
            st = lax.dot_general(kj, qi, _NT, preferred_element_type=F32) * scale
            if fox:
                st = st + fr_ref[0, i] - fc_ref[0]
            if masked:
                r = lax.broadcasted_iota(jnp.int32, (tq, tq), 0)
                c = lax.broadcasted_iota(jnp.int32, (tq, tq), 1)
                st = jnp.where(c >= r, st, -jnp.inf)
            pt = jnp.exp(st - lse_ref[0, i])
            dv_s[...] += jnp.dot(pt.astype(BF16), doi, preferred_element_type=F32)
            dpt = lax.dot_general(vj, doi, _NT, preferred_element_type=F32)
            dst = pt * (dpt - dl_ref[0, i])
            dk_s[...] += jnp.dot(dst.astype(BF16), qi, preferred_element_type=F32)
            if fox:
                df_s[...] -= jnp.sum(dst, axis=1, keepdims=True)

        def loop_body(i, carry):
            step(i, False)
            return carry

        step(j, True)
        lax.fori_loop(j + 1, nq, loop_body, 0)
        dk_ref[0] = dk_s[...] * scale
        dv_ref[0] = dv_s[...]
        if fox:
            df_ref[0] = df_s[...]

    row4 = pl.BlockSpec((1, nq, 1, tq), lambda h, j: (h, 0, 0, 0))
    in_specs = [
        pl.BlockSpec((1, tq, dk), lambda h, j: (h, j, 0)),
        pl.BlockSpec((1, tq, dv), lambda h, j: (h, j, 0)),
        pl.BlockSpec((1, nq, tq, dk), lambda h, j: (h, 0, 0, 0)),
        pl.BlockSpec((1, nq, tq, dv), lambda h, j: (h, 0, 0, 0)),
        row4, row4,
    ]
    args = [k, v, q4, do4, lse_row, delta_row]
    out_specs = [pl.BlockSpec((1, tq, dk), lambda h, j: (h, j, 0)),
                 pl.BlockSpec((1, tq, dv), lambda h, j: (h, j, 0))]
    out_shape = [jax.ShapeDtypeStruct((H, T, dk), F32), jax.ShapeDtypeStruct((H, T, dv), F32)]
    scratch = [pltpu.VMEM((tq, dk), F32), pltpu.VMEM((tq, dv), F32)]
    if fox:
        in_specs += [pl.BlockSpec((1, tq, 1), lambda h, j: (h, j, 0)), row4]
        args += [fcol, frow]
        out_specs.append(pl.BlockSpec((1, tq, 1), lambda h, j: (h, j, 0)))
        out_shape.append(jax.ShapeDtypeStruct((H, T, 1), F32))
        scratch.append(pltpu.VMEM((tq, 1), F32))
    return pl.pallas_call(
        body, name=name, grid=(H, nq),
        in_specs=in_specs, out_specs=out_specs, out_shape=out_shape, scratch_shapes=scratch,
        compiler_params=_cparams(("parallel", "arbitrary")),
    )(*args)


def _peers():
    x, y, c = lax.axis_index("x"), lax.axis_index("y"), lax.axis_index("c")
    me = 4 * x + 2 * y + c
    out = []
    for k in range(1, N_DEV):
        px = (1 - x) if (k & 4) else x
        py = (1 - y) if (k & 2) else y
        pc = (1 - c) if (k & 1) else c
        out.append(((px, py, pc), 4 * px + 2 * py + pc))
    return me, out


def _allgather(shard):
    R = shard.shape[0]

    def body(x_ref, o_ref, send_sems, recv_sems, local_sem):
        me, peers = _peers()
        mine = pltpu.make_async_copy(x_ref, o_ref.at[me], local_sem)
        mine.start()
        sends = []
        for k, (dev, _) in enumerate(peers):
            cp = pltpu.make_async_remote_copy(src_ref=x_ref, dst_ref=o_ref.at[me], send_sem=send_sems.at[k],
                                              recv_sem=recv_sems.at[k], device_id=dev, device_id_type=MESH)
            cp.start()
            sends.append(cp)
        for k, (dev, lin) in enumerate(peers):
            pltpu.make_async_remote_copy(src_ref=x_ref, dst_ref=o_ref.at[lin], send_sem=send_sems.at[k],
                                         recv_sem=recv_sems.at[k], device_id=dev, device_id_type=MESH).wait_recv()
        for cp in sends:
            cp.wait_send()
        mine.wait()

    return pl.pallas_call(
        body, name="allgather_weights",
        in_specs=[pl.BlockSpec(memory_space=pl.ANY)],
        out_specs=pl.BlockSpec(memory_space=pl.ANY),
        out_shape=jax.ShapeDtypeStruct((N_DEV, R, LANES), shard.dtype),
        scratch_shapes=[pltpu.SemaphoreType.DMA((N_DEV - 1,)), pltpu.SemaphoreType.DMA((N_DEV - 1,)),
                        pltpu.SemaphoreType.DMA],
    )(shard)


def _exchange_pieces(pieces):
    R = pieces.shape[1]

    def body(g_ref, o_ref, send_sems, recv_sems, local_sem):
        me, peers = _peers()
        mine = pltpu.make_async_copy(g_ref.at[me], o_ref.at[me], local_sem)
        mine.start()
        sends = []
        for k, (dev, lin) in enumerate(peers):
            cp = pltpu.make_async_remote_copy(src_ref=g_ref.at[lin], dst_ref=o_ref.at[me], send_sem=send_sems.at[k],
                                              recv_sem=recv_sems.at[k], device_id=dev, device_id_type=MESH)
            cp.start()
            sends.append(cp)
        for k, (dev, lin) in enumerate(peers):
            pltpu.make_async_remote_copy(src_ref=g_ref.at[lin], dst_ref=o_ref.at[lin], send_sem=send_sems.at[k],
                                         recv_sem=recv_sems.at[k], device_id=dev, device_id_type=MESH).wait_recv()
        for cp in sends:
            cp.wait_send()
        mine.wait()

    return pl.pallas_call(
        body, name="exchange_grad_pieces",
        in_specs=[pl.BlockSpec(memory_space=pl.ANY)],
        out_specs=pl.BlockSpec(memory_space=pl.ANY),
        out_shape=jax.ShapeDtypeStruct((N_DEV, R, LANES), pieces.dtype),
        scratch_shapes=[pltpu.SemaphoreType.DMA((N_DEV - 1,)), pltpu.SemaphoreType.DMA((N_DEV - 1,)),
                        pltpu.SemaphoreType.DMA],
    )(pieces)


def _adamw(w, g, m, v):
    m2 = ADAM_B1 * m + (1.0 - ADAM_B1) * g
    v2 = ADAM_B2 * v + (1.0 - ADAM_B2) * (g * g)
    m_hat = m2 / (1.0 - ADAM_B1 ** ADAM_STEP)
    v_hat = v2 / (1.0 - ADAM_B2 ** ADAM_STEP)
    delta = -ADAM_LR * (m_hat / (jnp.sqrt(v_hat) + ADAM_EPS) + ADAM_WD * w)
    return delta, m2, v2


def _sum_adamw(pieces, w, m, v, *, tr=1824):
    R = w.shape[0]
    tr = min(tr, R)
    while R % tr:
        tr //= 2

    def body(p_ref, w_ref, m_ref, v_ref, g_ref, d_ref, m2_ref, v2_ref):
        g = p_ref[0].astype(F32)
        for s in range(1, N_DEV):
            g = g + p_ref[s].astype(F32)
        delta, m2, v2 = _adamw(w_ref[...], g, m_ref[...], v_ref[...])
        g_ref[...] = g
        d_ref[...] = delta
        m2_ref[...] = m2
        v2_ref[...] = v2

    row = pl.BlockSpec((tr, LANES), lambda i: (i, 0))
    return pl.pallas_call(
        body, name="sum_pieces_adamw", grid=(R // tr,),
        in_specs=[pl.BlockSpec((N_DEV, tr, LANES), lambda i: (0, i, 0)), row, row, row],
        out_specs=[row, row, row, row],
        out_shape=[jax.ShapeDtypeStruct((R, LANES), F32)] * 4,
        compiler_params=_cparams(("parallel",)),
    )(pieces, w, m, v)


def _small_allreduce_adamw(part, w, m, v):
    shape = part.shape

    def body(p_ref, w_ref, m_ref, v_ref, g_ref, d_ref, m2_ref, v2_ref, loss_ref, gath, send_sems, recv_sems):
        me, peers = _peers()
        gath[me] = p_ref[...]
        sends = []
        for k, (dev, _) in enumerate(peers):
            cp = pltpu.make_async_remote_copy(src_ref=p_ref, dst_ref=gath.at[me], send_sem=send_sems.at[k],
                                              recv_sem=recv_sems.at[k], device_id=dev, device_id_type=MESH)
            cp.start()
            sends.append(cp)
        for k, (dev, lin) in enumerate(peers):
            pltpu.make_async_remote_copy(src_ref=p_ref, dst_ref=gath.at[lin], send_sem=send_sems.at[k],
                                         recv_sem=recv_sems.at[k], device_id=dev, device_id_type=MESH).wait_recv()
        for cp in sends:
            cp.wait_send()
        g = gath[0]
        for s in range(1, N_DEV):
            g = g + gath[s]
        delta, m2, v2 = _adamw(w_ref[...], g, m_ref[...], v_ref[...])
        g_ref[...] = g
        d_ref[...] = delta
        m2_ref[...] = m2
        v2_ref[...] = v2
        sq = jnp.sum(g[SMALL_LOSS_ROW:SMALL_LOSS_ROW + 8, :], axis=1, keepdims=True)
        tot = jnp.sum(sq, axis=0, keepdims=True) * (0.5 / D_MODEL)
        loss_ref[...] = jnp.broadcast_to(tot, loss_ref.shape)

    vm = pl.BlockSpec(memory_space=pltpu.VMEM)
    return pl.pallas_call(
        body, name="small_allreduce_adamw",
        in_specs=[vm, vm, vm, vm],
        out_specs=[vm, vm, vm, vm, vm],
        out_shape=[jax.ShapeDtypeStruct(shape, F32)] * 4 + [jax.ShapeDtypeStruct((8, LANES), F32)],
        scratch_shapes=[pltpu.VMEM((N_DEV,) + shape, F32),
                        pltpu.SemaphoreType.DMA((N_DEV - 1,)), pltpu.SemaphoreType.DMA((N_DEV - 1,))],
    )(part, w, m, v)


def _perm_in_cols(w):
    z = lambda n: jnp.zeros(w.shape[:-1] + (n,), w.dtype)
    small = jnp.concatenate([w[..., 384:400], z(16), w[..., 1952:1960], z(24), w[..., 400:416], z(48)], -1)
    return jnp.concatenate([w[..., 1960:2984], w[..., 2984:4008], w[..., 416:928], w[..., 928:1440],
                            w[..., 1440:1952], w[..., 0:256], w[..., 256:384], small], -1)


def _unperm_in_cols(wp):
    s = wp[..., P_SMALL:]
    return jnp.concatenate([wp[..., P_CQ:P_CQ + 256], wp[..., P_CKV:P_CKV + 128], s[..., 0:16], s[..., 64:80],
                            wp[..., P_FQ:P_FQ + 512], wp[..., P_FK:P_FK + 512], wp[..., P_FV:P_FV + 512],
                            s[..., S_FL:S_FL + 8], wp[..., P_GA:P_GA + 1024], wp[..., P_GB:P_GB + 1024]], -1)


def _perm_uq_cols(w):
    r = w.shape[0]
    w3 = w.reshape(r, N_HEADS, NOPE + ROPE)
    return jnp.concatenate([w3[:, :, :NOPE].reshape(r, 512), w3[:, :, NOPE:NOPE + 16].reshape(r, 128),
                            w3[:, :, NOPE + 16:].reshape(r, 128)], -1)


def _unperm_uq_cols(wp):
    r = wp.shape[0]
    return jnp.concatenate([wp[:, :512].reshape(r, N_HEADS, NOPE), wp[:, 512:640].reshape(r, N_HEADS, 16),
                            wp[:, 640:].reshape(r, N_HEADS, 16)], -1).reshape(r, 768)


def _to_heads(t, d):
    return t.reshape(t.shape[0], N_HEADS, d).transpose(1, 0, 2)


def _from_heads(t):
    return t.transpose(1, 0, 2).reshape(t.shape[1], -1)


def _pack_shards(shards, dtype):
    flat = jnp.concatenate([s.reshape(-1).astype(dtype) for s in shards])
    flat = jnp.pad(flat, (0, PACK_ROWS * LANES - PACK_ELEMS))
    return flat.reshape(PACK_ROWS, LANES)


def _unpack_shards(packed):
    flat = packed.reshape(-1)
    out, off = [], 0
    for _, r, c, _ in SHARDED:
        out.append(flat[off:off + r * c].reshape(1, r, c))
        off += r * c
    return out


def _unpack_full(gathered):
    flat = gathered.reshape(N_DEV, -1)
    out, off = {}, 0
    for name, r, c, by_col in SHARDED:
        blk = flat[:, off:off + r * c].reshape(N_DEV, r, c)
        out[name] = blk.transpose(1, 0, 2).reshape(r, N_DEV * c) if by_col else blk.reshape(N_DEV * r, c)
        off += r * c
    return out


def _pack_full(grads, dtype):
    segs = []
    for name, r, c, by_col in SHARDED:
        g = grads[name]
        if by_col:
            g = g.reshape(r, N_DEV, c).transpose(1, 0, 2)
        segs.append(g.reshape(N_DEV, r * c).astype(dtype))
    flat = jnp.concatenate(segs, axis=1)
    flat = jnp.pad(flat, ((0, 0), (0, PACK_ROWS * LANES - PACK_ELEMS)))
    return flat.reshape(N_DEV, PACK_ROWS, LANES)


SMALL_LAYOUT = (("ln_pre_mix", 1024, 0), ("ln_post_mix", 1024, 8), ("ln_pre_mlp", 1024, 16),
                ("ln_post_mlp", 1024, 24), ("b_in", 4008, 32), ("q_a_norm", 256, 64), ("kv_a_norm", 128, 66))


def _pack_small(vals, extra=None):
    rows = []
    for name, n, _ in SMALL_LAYOUT:
        v = vals[name].reshape(-1).astype(F32)
        pad = -n % LANES
        rows.append(jnp.pad(v, (0, pad)).reshape(-1, LANES))
    rows.append(jnp.zeros((SMALL_LOSS_ROW - 67, LANES), F32))
    rows.append(jnp.zeros((8, LANES), F32) if extra is None else extra.reshape(8, LANES))
    return jnp.concatenate(rows, axis=0)


def _unpack_small(packed):
    out = {}
    for name, n, r0 in SMALL_LAYOUT:
        nr = -(-n // LANES)
        out[name] = packed[r0:r0 + nr].reshape(-1)[:n].reshape(1, n)
    return out


def _rms(xf, g):
    r = lax.rsqrt(jnp.mean(xf * xf, axis=-1, keepdims=True) + NORM_EPS)
    return (xf * r) * g


def _rms_bwd(xf, g, dy):
    r = lax.rsqrt(jnp.mean(xf * xf, axis=-1, keepdims=True) + NORM_EPS)
    xhat = xf * r
    dxhat = dy * g
    dx = r * (dxhat - xhat * jnp.mean(dxhat * xhat, axis=-1, keepdims=True))
    return dx, dy * xhat


def _sigmoid(t):
    return 1.0 / (1.0 + jnp.exp(-t))


def _log_sigmoid(t):
    return jnp.minimum(t, 0.0) - jnp.log(1.0 + jnp.exp(-jnp.abs(t)))


def _lane_sign():
    lane = lax.broadcasted_iota(jnp.int32, (1, LANES), 1)
    return jnp.where(lane < 64, -1.0, 1.0).astype(F32), lane


def kernel(x, positions, ln_pre_mix, ln_post_mix, ln_pre_mlp, ln_post_mlp, w_in, b_in, q_a_norm, w_uq, kv_a_norm, w_uk, w_uv, w_o_mla, w_o_fox, w_out, w_ff1, w_ff2, loss_target, m_ln_pre_mix, m_ln_post_mix, m_ln_pre_mlp, m_ln_post_mlp, m_w_in, m_b_in, m_q_a_norm, m_w_uq, m_kv_a_norm, m_w_uk, m_w_uv, m_w_o_mla, m_w_o_fox, m_w_out, m_w_ff1, m_w_ff2, v_ln_pre_mix, v_ln_post_mix, v_ln_pre_mlp, v_ln_post_mlp, v_w_in, v_b_in, v_q_a_norm, v_w_uq, v_kv_a_norm, v_w_uk, v_w_uv, v_w_o_mla, v_w_o_fox, v_w_out, v_w_ff1, v_w_ff2):
    T = x.shape[1]
    x2 = x.reshape(T, D_MODEL)
    tgt = loss_target.reshape(T, D_MODEL)
    w_sh = dict(w_in=w_in, w_uq=w_uq, w_uk=w_uk, w_uv=w_uv, w_o_mla=w_o_mla, w_o_fox=w_o_fox, w_out=w_out,
                w_ff1=w_ff1, w_ff2=w_ff2)
    m_sh = dict(w_in=m_w_in, w_uq=m_w_uq, w_uk=m_w_uk, w_uv=m_w_uv, w_o_mla=m_w_o_mla, w_o_fox=m_w_o_fox,
                w_out=m_w_out, w_ff1=m_w_ff1, w_ff2=m_w_ff2)
    v_sh = dict(w_in=v_w_in, w_uq=v_w_uq, w_uk=v_w_uk, w_uv=v_w_uv, w_o_mla=v_w_o_mla, w_o_fox=v_w_o_fox,
                w_out=v_w_out, w_ff1=v_w_ff1, w_ff2=v_w_ff2)
    small_w = dict(ln_pre_mix=ln_pre_mix, ln_post_mix=ln_post_mix, ln_pre_mlp=ln_pre_mlp, ln_post_mlp=ln_post_mlp,
                   b_in=b_in, q_a_norm=q_a_norm, kv_a_norm=kv_a_norm)
    small_m = dict(ln_pre_mix=m_ln_pre_mix, ln_post_mix=m_ln_post_mix, ln_pre_mlp=m_ln_pre_mlp,
                   ln_post_mlp=m_ln_post_mlp, b_in=m_b_in, q_a_norm=m_q_a_norm, kv_a_norm=m_kv_a_norm)
    small_v = dict(ln_pre_mix=v_ln_pre_mix, ln_post_mix=v_ln_post_mix, ln_pre_mlp=v_ln_pre_mlp,
                   ln_post_mlp=v_ln_post_mlp, b_in=v_b_in, q_a_norm=v_q_a_norm, kv_a_norm=v_kv_a_norm)
    names = [n for n, _, _, _ in SHARDED]

    gathered = _allgather(_pack_shards([w_sh[n] for n in names], BF16))
    W = _unpack_full(gathered)
    w_in_p = _perm_in_cols(W["w_in"])
    b_in_p = _perm_in_cols(b_in.astype(F32))
    w_uq_p = _perm_uq_cols(W["w_uq"])
    w_kv = jnp.concatenate([W["w_uk"], W["w_uv"]], axis=1)
    g1, g2, g3, g4 = ln_pre_mix, ln_post_mix, ln_pre_mlp, ln_post_mlp
    gq, gkv = q_a_norm, kv_a_norm

    (h,) = _rowwise(lambda xv, g: _rms(xv, g), [x2], [g1], [(D_MODEL, BF16)], name="pre_mix_norm")
    z = _mm(h, w_in_p, bias=b_in_p, name="in_proj")
    zs = _mm(h, w_in_p[:, P_SMALL:], bias=b_in_p[:, P_SMALL:], outs=(F32,), name="in_proj_small")

    def lora_norm(cq, ckv, a, b):
        return _rms(cq.astype(F32), a), _rms(ckv.astype(F32), b)

    cqn, ckvn = _rowwise(lora_norm, [(z, 256, P_CQ // 256), (z, 128, P_CKV // 128)], [gq, gkv],
                         [(Q_LORA, BF16), (KV_LORA, BF16)], name="lora_norm")
    q768 = _mm(cqn, w_uq_p, outs=(F32,), name="q_up")
    kv = _mm(ckvn, w_kv, name="kv_up")

    half = ROPE // 2
    inv_freq = ROPE_THETA ** (-jnp.arange(half, dtype=F32) / half)
    inv128 = jnp.tile(inv_freq, LANES // half).reshape(1, LANES)
    pos_col = positions.reshape(T, 1).astype(F32)

    def rope_tables(p, f):
        ang = p * f
        return jnp.cos(ang), jnp.sin(ang)

    cos_t, sin_t = _rowwise(rope_tables, [pos_col], [inv128], [(LANES, F32), (LANES, F32)], name="rope_tables")

    def rope_fwd(qn, q1, q2, s, cs, sn):
        sign, lane = _lane_sign()
        o1 = q1 * cs - q2 * sn
        o2 = q2 * cs + q1 * sn
        ks = s * cs + pltpu.roll(s, 64, 1) * sn * sign
        ks = jnp.where((lane < 16) | ((lane >= 64) & (lane < 80)), ks, 0.0)
        return jnp.concatenate([qn, o1, o2], axis=1), ks

    q_r, k_rs = _rowwise(rope_fwd, [(q768, 512, 0), (q768, 128, 4), (q768, 128, 5), zs, cos_t, sin_t], [],
                         [(768, BF16), (LANES, BF16)], name="rope_fwd")
    f_cum = _seq_cumsum(zs, _log_sigmoid, reverse=False, name="forget_cumsum")

    q_mla = jnp.concatenate([q_r[:, :512].reshape(T, N_HEADS, NOPE), q_r[:, 512:640].reshape(T, N_HEADS, 16),
                             q_r[:, 640:].reshape(T, N_HEADS, 16)], -1).transpose(1, 0, 2)
    k_rot = jnp.concatenate([k_rs[:, 0:16], k_rs[:, 64:80]], -1)
    k_mla = jnp.concatenate([kv[:, :512].reshape(T, N_HEADS, NOPE),
                             jnp.broadcast_to(k_rot[:, None, :], (T, N_HEADS, ROPE))], -1).transpose(1, 0, 2)
    v_mla = _to_heads(kv[:, 512:], HEAD_V)
    q_fox = _to_heads(z[:, P_FQ:P_FQ + 512], FOX_D)
    k_fox = _to_heads(z[:, P_FK:P_FK + 512], FOX_D)
    v_fox = _to_heads(z[:, P_FV:P_FV + 512], FOX_D)
    tq = min(ATTN_TILE, T)
    f_heads = f_cum[:, S_FL:S_FL + N_HEADS].T
    f_col = f_heads.reshape(N_HEADS, T, 1)
    f_row = f_heads.reshape(N_HEADS, T // tq, 1, tq)

    mla_scale = float((NOPE + ROPE) ** -0.5)
    fox_scale = float(FOX_D ** -0.5)
    o_mla, lse_mla = _flash_fwd(q_mla, k_mla, v_mla, mla_scale, name="mla_attn_fwd", tq=tq)
    o_fox, lse_fox = _flash_fwd(q_fox, k_fox, v_fox, fox_scale, f_col, f_row, name="fox_attn_fwd", tq=tq)
    o_mla_f = _from_heads(o_mla)
    o_fox_f = _from_heads(o_fox)
    y_mla = _mm(o_mla_f, W["w_o_mla"], outs=(F32,), name="o_proj_mla")
    y_fox = _mm(o_fox_f, W["w_o_fox"], outs=(F32,), name="o_proj_fox")

    def gate_merge(ga, gb, ya, yb):
        return _sigmoid(ga.astype(F32)) * ya + _sigmoid(gb.astype(F32)) * yb

    (merged,) = _rowwise(gate_merge, [(z, 1024, 0), (z, 1024, 1), y_mla, y_fox], [], [(D_MODEL, BF16)],
                         name="gate_merge")
    mix = _mm(merged, W["w_out"], outs=(F32,), name="out_proj")

    def post_mix(xv, mv, a, b):
        x1v = xv + _rms(mv, a)
        return x1v, _rms(x1v, b)

    x1, h2 = _rowwise(post_mix, [x2, mix], [g2, g3], [(D_MODEL, F32), (D_MODEL, BF16)], name="post_mix_norm")

    def relu2(acc):
        r = jnp.maximum(acc, 0.0)
        return r * r, acc

    act, u = _mm(h2, W["w_ff1"], epi=relu2, outs=(BF16, BF16), name="ff1")
    mlp = _mm(act, W["w_ff2"], outs=(F32,), name="ff2")

    def loss_bwd(x1v, mv, tv, g):
        y = x1v + _rms(mv, g)
        d = y - tv
        dy = d * (1.0 / D_MODEL)
        dm, dg = _rms_bwd(mv, g, dy)
        return dy, dm, dg, d * d

    dy, dm, dg4, loss_cols = _rowwise(loss_bwd, [x1, mlp, tgt], [g4], [(D_MODEL, F32), (D_MODEL, BF16)],
                                      [D_MODEL, D_MODEL], name="loss_bwd")

    def relu2_bwd(acc, uv):
        return (acc * (2.0 * jnp.maximum(uv.astype(F32), 0.0)),)

    du = _mm(dm, W["w_ff2"].T, extras=(u,), epi=relu2_bwd, name="ff2_bwd")
    dw_ff2 = _mm(act, dm, ta=True, outs=(F32,), name="ff2_wgrad")
    dh2 = _mm(du, W["w_ff1"].T, outs=(F32,), name="ff1_bwd")
    dw_ff1 = _mm(h2, du, ta=True, outs=(F32,), name="ff1_wgrad")

    def post_mix_bwd(x1v, dh2v, dyv, mv, a, b):
        d3, dg3v = _rms_bwd(x1v, b, dh2v)
        dx1v = dyv + d3
        dmixv, dg2v = _rms_bwd(mv, a, dx1v)
        return dx1v, dmixv, dg3v, dg2v

    dx1, dmix, dg3, dg2 = _rowwise(post_mix_bwd, [x1, dh2, dy, mix], [g2, g3], [(D_MODEL, F32), (D_MODEL, BF16)],
                                   [D_MODEL, D_MODEL], name="post_mix_bwd")
    dmerged = _mm(dmix, W["w_out"].T, outs=(F32,), name="out_proj_bwd")
    dw_out = _mm(merged, dmix, ta=True, outs=(F32,), name="out_proj_wgrad")

    def gate_bwd(dmg, ga, gb, ya, yb):
        sa = _sigmoid(ga.astype(F32))
        sb = _sigmoid(gb.astype(F32))
        return dmg * sa, dmg * sb, dmg * ya * sa * (1.0 - sa), dmg * yb * sb * (1.0 - sb)

    dy_mla, dy_fox, dga, dgb = _rowwise(gate_bwd, [dmerged, (z, 1024, 0), (z, 1024, 1), y_mla, y_fox], [],
                                        [(D_MODEL, BF16)] * 4, name="gate_bwd")
    do_mla = _to_heads(_mm(dy_mla, W["w_o_mla"].T, name="o_proj_mla_bwd"), HEAD_V)
    do_fox = _to_heads(_mm(dy_fox, W["w_o_fox"].T, name="o_proj_fox_bwd"), FOX_D)
    dw_o_mla = _mm(o_mla_f, dy_mla, ta=True, outs=(F32,), name="o_proj_mla_wgrad")
    dw_o_fox = _mm(o_fox_f, dy_fox, ta=True, outs=(F32,), name="o_proj_fox_wgrad")

    row4 = lambda t: t.reshape(N_HEADS, T // tq, 1, tq)
    dq_mla, dl_mla = _flash_dq(q_mla, k_mla, v_mla, o_mla, do_mla, lse_mla, mla_scale, name="mla_attn_dq", tq=tq)
    dk_mla, dv_mla = _flash_dkv(q_mla, k_mla, v_mla, do_mla, row4(lse_mla), row4(dl_mla), mla_scale,
                                name="mla_attn_dkv", tq=tq)
    dq_fox, dl_fox, dfq_fox = _flash_dq(q_fox, k_fox, v_fox, o_fox, do_fox, lse_fox, fox_scale, f_col, f_row,
                                        name="fox_attn_dq", tq=tq)
    dk_fox, dv_fox, df_fox = _flash_dkv(q_fox, k_fox, v_fox, do_fox, row4(lse_fox), row4(dl_fox), fox_scale,
                                        f_col, f_row, name="fox_attn_dkv", tq=tq)

    dq_t = dq_mla.transpose(1, 0, 2)
    dq768 = jnp.concatenate([dq_t[:, :, :NOPE].reshape(T, 512), dq_t[:, :, NOPE:NOPE + 16].reshape(T, 128),
                             dq_t[:, :, NOPE + 16:].reshape(T, 128)], -1)
    dk_t = dk_mla.transpose(1, 0, 2)
    dk_rot = jnp.sum(dk_t[:, :, NOPE:], axis=1)
    zc = lambda n: jnp.zeros((T, n), F32)
    d_small_in = jnp.concatenate([dk_rot[:, :16], zc(16), (dfq_fox + df_fox).reshape(N_HEADS, T).T, zc(24),
                                  dk_rot[:, 16:], zc(48)], -1)
    df_rev = _seq_cumsum(d_small_in, lambda t: t, reverse=True, name="forget_cumsum_bwd")

    def rope_bwd(dn, d1, d2, ds, dfr, s, cs, sn):
        sign, lane = _lane_sign()
        o1 = d1 * cs + d2 * sn
        o2 = d2 * cs - d1 * sn
        dk_r = ds * cs - pltpu.roll(ds, 64, 1) * sn * sign
        dfl = dfr * _sigmoid(-s)
        small = jnp.where((lane < 16) | ((lane >= 64) & (lane < 80)), dk_r,
                          jnp.where((lane >= S_FL) & (lane < S_FL + N_HEADS), dfl, 0.0))
        return jnp.concatenate([dn, o1, o2], axis=1), small

    dq768_b, d_small = _rowwise(rope_bwd, [(dq768, 512, 0), (dq768, 128, 4), (dq768, 128, 5), d_small_in, df_rev,
                                           zs, cos_t, sin_t], [], [(768, BF16), (LANES, BF16)], name="rope_bwd")
    dcqn = _mm(dq768_b, w_uq_p.T, outs=(F32,), name="q_up_bwd")
    dw_uq_p = _mm(cqn, dq768_b, ta=True, outs=(F32,), name="q_up_wgrad")
    dkv = jnp.concatenate([dk_t[:, :, :NOPE].reshape(T, 512), _from_heads(dv_mla)], -1).astype(BF16)
    dckvn = _mm(dkv, w_kv.T, outs=(F32,), name="kv_up_bwd")
    dw_kv = _mm(ckvn, dkv, ta=True, outs=(F32,), name="kv_up_wgrad")

    def lora_norm_bwd(cq, ckv, dq_, dkv_, a, b):
        d1, dga_ = _rms_bwd(cq.astype(F32), a, dq_)
        d2, dgb_ = _rms_bwd(ckv.astype(F32), b, dkv_)
        return d1, d2, dga_, dgb_

    dcq, dckv, dgq, dgkv = _rowwise(lora_norm_bwd, [(z, 256, P_CQ // 256), (z, 128, P_CKV // 128), dcqn, dckvn],
                                    [gq, gkv], [(Q_LORA, BF16), (KV_LORA, BF16)], [Q_LORA, KV_LORA],
                                    name="lora_norm_bwd")
    dz = jnp.concatenate([dga, dgb, _from_heads(dq_fox).astype(BF16), _from_heads(dk_fox).astype(BF16),
                          _from_heads(dv_fox).astype(BF16), dcq, dckv, d_small], -1)
    (db_in_p,) = _rowwise(lambda t: (t,), [dz], [], [], [D_IN_PAD], name="in_bias_grad")
    dh = _mm(dz, w_in_p.T, outs=(F32,), name="in_proj_bwd")
    dw_in_p = _mm(h, dz, ta=True, outs=(F32,), name="in_proj_wgrad")

    def pre_mix_bwd(xv, dhv, dx1v, g):
        d, dg = _rms_bwd(xv, g, dhv)
        return dx1v + d, dg

    grad_x, dg1 = _rowwise(pre_mix_bwd, [x2, dh, dx1], [g1], [(D_MODEL, F32)], [D_MODEL], name="pre_mix_bwd")

    full_grads = dict(w_in=_unperm_in_cols(dw_in_p), w_uq=_unperm_uq_cols(dw_uq_p), w_uk=dw_kv[:, :512],
                      w_uv=dw_kv[:, 512:], w_o_mla=dw_o_mla, w_o_fox=dw_o_fox, w_out=dw_out, w_ff1=dw_ff1,
                      w_ff2=dw_ff2)
    pieces = _exchange_pieces(_pack_full(full_grads, BF16))
    g_p, d_p, m_p, v_p = _sum_adamw(pieces, _pack_shards([w_sh[n] for n in names], F32),
                                    _pack_shards([m_sh[n] for n in names], F32),
                                    _pack_shards([v_sh[n] for n in names], F32))
    grad_sh = dict(zip(names, _unpack_shards(g_p)))
    delta_sh = dict(zip(names, _unpack_shards(d_p)))
    newm_sh = dict(zip(names, _unpack_shards(m_p)))
    newv_sh = dict(zip(names, _unpack_shards(v_p)))

    small_part = _pack_small(dict(ln_pre_mix=dg1, ln_post_mix=dg2, ln_pre_mlp=dg3, ln_post_mlp=dg4,
                                  b_in=_unperm_in_cols(db_in_p), q_a_norm=dgq, kv_a_norm=dgkv), extra=loss_cols)
    sg, sd, sm, sv, loss_blk = _small_allreduce_adamw(small_part, _pack_small(small_w), _pack_small(small_m),
                                                      _pack_small(small_v))
    grad_sm, delta_sm, newm_sm, newv_sm = (_unpack_small(t) for t in (sg, sd, sm, sv))
    loss = loss_blk[0, 0]

    order = ["ln_pre_mix", "ln_post_mix", "ln_pre_mlp", "ln_post_mlp", "w_in", "b_in", "q_a_norm", "w_uq",
             "kv_a_norm", "w_uk", "w_uv", "w_o_mla", "w_o_fox", "w_out", "w_ff1", "w_ff2"]

    def pick(sm_d, sh_d):
        return [sm_d[n] if n in sm_d else sh_d[n] for n in order]

    return (loss, grad_x.reshape(1, T, D_MODEL), *pick(grad_sm, grad_sh), *pick(delta_sm, delta_sh),
            *pick(newm_sm, newm_sh), *pick(newv_sm, newv_sh))
```

```python
import numpy as np
import jax
import jax.numpy as jnp
from jax import lax
from jax.experimental import pallas as pl
from jax.experimental.pallas import tpu as pltpu

F32 = jnp.float32
BF16 = jnp.bfloat16
MESH = pl.DeviceIdType.MESH

D_MODEL = 1024
N_HEADS = 8
Q_LORA = 256
KV_LORA = 128
NOPE = 64
ROPE = 32
HEAD_V = 64
FOX_D = 64
D_FF = 4096
D_IN = 4008
D_IN_PAD = 4096
ROPE_THETA = 10000.0
NORM_EPS = 1e-6
N_DEV = 8

ADAM_LR = 0.001
ADAM_B1 = 0.9
ADAM_B2 = 0.999
ADAM_EPS = 1e-08
ADAM_WD = 0.01
ADAM_STEP = 10

LANES = 128
ROW_TILE = 512
ATTN_TILE = 512
VMEM_LIMIT = 48 * 1024 * 1024
ATTN_BWD_VMEM_LIMIT = 58 * 1024 * 1024

P_GA, P_GB, P_FQ, P_FK, P_FV, P_CQ, P_CKV, P_SMALL = 0, 1024, 2048, 2560, 3072, 3584, 3840, 3968
S_FL = 32

SHARDED = (
    ("w_in", 1024, 501, True), ("w_uq", 256, 96, True), ("w_uk", 128, 64, True), ("w_uv", 128, 64, True),
    ("w_o_mla", 512, 128, True), ("w_o_fox", 512, 128, True), ("w_out", 128, 1024, False),
    ("w_ff1", 1024, 512, True), ("w_ff2", 512, 1024, False),
)
PACK_ELEMS = sum(r * c for _, r, c, _ in SHARDED)
PACK_ROWS = -(-PACK_ELEMS // (LANES * 128)) * 128
SMALL_ROWS = 80
SMALL_LOSS_ROW = 72


def _cparams(sem=None):
    return pltpu.CompilerParams(dimension_semantics=sem, vmem_limit_bytes=VMEM_LIMIT)


def _mm(a, b, *, name, ta=False, bias=None, extras=(), epi=None, outs=(BF16,), tm=ROW_TILE, tn=1024, tk=1024):
    if ta:
        K, M = a.shape
        tk = min(512, K)
        tm = min(1024, M)
    else:
        M, K = a.shape
        tm = min(tm, M)
        tk = min(tk, K)
    N = b.shape[1]
    tn = min(tn, N)
    nk = K // tk
    n_ex = len(extras)
    has_bias = bias is not None

    def body(*refs):
        a_ref, b_ref = refs[0], refs[1]
        pos = 2
        bias_ref = None
        if has_bias:
            bias_ref = refs[pos]
            pos += 1
        ex_refs = refs[pos:pos + n_ex]
        pos += n_ex
        o_refs = refs[pos:pos + len(outs)]
        pos += len(outs)
        av = a_ref[...].astype(BF16)
        bv = b_ref[...].astype(BF16)
        if ta:
            part = lax.dot_general(av, bv, (((0,), (0,)), ((), ())), preferred_element_type=F32)
        else:
            part = jnp.dot(av, bv, preferred_element_type=F32)

        def finish(acc):
            if has_bias:
                acc = acc + bias_ref[...]
            res = (acc,) if epi is None else epi(acc, *[r[...] for r in ex_refs])
            for o_ref, val in zip(o_refs, res):
                o_ref[...] = val.astype(o_ref.dtype)

        if nk == 1:
            finish(part)
        else:
            acc_ref = refs[pos]
            k = pl.program_id(2)

            @pl.when(k == 0)
            def _():
                acc_ref[...] = part

            @pl.when(k > 0)
            def _():
                acc_ref[...] += part

            @pl.when(k == nk - 1)
            def _():
                finish(acc_ref[...])

    if ta:
        a_spec = pl.BlockSpec((tk, tm), lambda i, j, k: (k, i))
    else:
        a_spec = pl.BlockSpec((tm, tk), lambda i, j, k: (i, k))
    in_specs = [a_spec, pl.BlockSpec((tk, tn), lambda i, j, k: (k, j))]
    args = [a, b]
    if has_bias:
        in_specs.append(pl.BlockSpec((1, tn), lambda i, j, k: (0, j)))
        args.append(bias)
    for e in extras:
        in_specs.append(pl.BlockSpec((tm, tn), lambda i, j, k: (i, j)))
        args.append(e)
    res = pl.pallas_call(
        body, name=name, grid=(M // tm, N // tn, nk),
        in_specs=in_specs,
        out_specs=[pl.BlockSpec((tm, tn), lambda i, j, k: (i, j)) for _ in outs],
        out_shape=[jax.ShapeDtypeStruct((M, N), dt) for dt in outs],
        scratch_shapes=[pltpu.VMEM((tm, tn), F32)] if nk > 1 else [],
        compiler_params=_cparams(("parallel", "parallel", "arbitrary")),
    )(*args)
    return res[0] if len(outs) == 1 else res


def _rowwise(fn, rows, bcasts, outs, accs=(), *, name, tm=ROW_TILE):
    arrs, specs = [], []
    for r in rows:
        if isinstance(r, tuple):
            arr, w, cb = r
            specs.append(pl.BlockSpec((tm, w), lambda i, cb=cb: (i, cb)))
        else:
            arr = r
            specs.append(pl.BlockSpec((tm, arr.shape[1]), lambda i: (i, 0)))
        arrs.append(arr)
    T = arrs[0].shape[0]
    tm = min(tm, T)
    specs = [pl.BlockSpec((tm,) + tuple(s.block_shape[1:]), s.index_map) for s in specs]
    for b in bcasts:
        arrs.append(b)
        specs.append(pl.BlockSpec(b.shape, lambda i: (0, 0)))
    n_in, n_out = len(arrs), len(outs)

    def body(*refs):
        vals = [r[...] for r in refs[:n_in]]
        res = fn(*vals)
        if not isinstance(res, (tuple, list)):
            res = (res,)
        for o_ref, val in zip(refs[n_in:n_in + n_out], res[:n_out]):
            o_ref[...] = val.astype(o_ref.dtype)
        i = pl.program_id(0)
        for a_ref, val in zip(refs[n_in + n_out:], res[n_out:]):
            col = jnp.sum(val.astype(F32), axis=0, keepdims=True)

            @pl.when(i == 0)
            def _(a_ref=a_ref, col=col):
                a_ref[...] = col

            @pl.when(i > 0)
            def _(a_ref=a_ref, col=col):
                a_ref[...] += col

    res = pl.pallas_call(
        body, name=name, grid=(T // tm,),
        in_specs=specs,
        out_specs=[pl.BlockSpec((tm, c), lambda i: (i, 0)) for c, _ in outs]
        + [pl.BlockSpec((1, c), lambda i: (0, 0)) for c in accs],
        out_shape=[jax.ShapeDtypeStruct((T, c), dt) for c, dt in outs]
        + [jax.ShapeDtypeStruct((1, c), F32) for c in accs],
        compiler_params=_cparams(("arbitrary",)),
    )(*arrs)
    return res


def _dot3(x, sel, left):
    hi = x.astype(BF16)
    r1 = x - hi.astype(F32)
    mid = r1.astype(BF16)
    lo = (r1 - mid.astype(F32)).astype(BF16)
    if left:
        d = lambda t: jnp.dot(sel, t, preferred_element_type=F32)
    else:
        d = lambda t: jnp.dot(t, sel, preferred_element_type=F32)
    return d(hi) + d(mid) + d(lo)


def _seq_cumsum(x, fn, *, reverse, name, tm=256):
    T, C = x.shape
    tm = min(tm, T)
    n = T // tm

    def body(x_ref, o_ref, carry_ref):
        i = pl.program_id(0)

        @pl.when(i == 0)
        def _():
            carry_ref[...] = jnp.zeros_like(carry_ref)

        v = fn(x_ref[...])
        r = lax.broadcasted_iota(jnp.int32, (tm, tm), 0)
        c = lax.broadcasted_iota(jnp.int32, (tm, tm), 1)
        tri = jnp.where((r <= c) if reverse else (r >= c), 1.0, 0.0).astype(BF16)
        carry = carry_ref[0:1, :]
        o_ref[...] = _dot3(v, tri, left=True) + carry
        carry_ref[0:1, :] = carry + jnp.sum(v, axis=0, keepdims=True)

    idx = (lambda i: (n - 1 - i, 0)) if reverse else (lambda i: (i, 0))
    return pl.pallas_call(
        body, name=name, grid=(n,),
        in_specs=[pl.BlockSpec((tm, C), idx)],
        out_specs=pl.BlockSpec((tm, C), idx),
        out_shape=jax.ShapeDtypeStruct((T, C), F32),
        scratch_shapes=[pltpu.VMEM((8, C), F32)],
        compiler_params=_cparams(("arbitrary",)),
    )(x)


_NT = (((1,), (1,)), ((), ()))
_TN = (((0,), (0,)), ((), ()))
LOG2E = 1.4426950408889634


def _head_mask(width, e):
    lane = lax.broadcasted_iota(jnp.int32, (1, width), 1)
    m = (lane >= 64 * e) & (lane < 64 * (e + 1))
    if width == 256:
        l = (lane - 128) & 63
        m = m | ((lane >= 128) & (l >= 16 * e) & (l < 16 * (e + 1)))
    return m


def _attn_fwd(q, k, v, qcb, kcb, vcb, cw, scale, frow=None, *, name, tq):
    T = q.shape[0]
    nq = T // tq
    nc = tq // LANES
    fox = frow is not None
    c = scale * LOG2E

    def body(*refs):
        if fox:
            q_ref, k_ref, v_ref, fr_ref, o_ref, lse_ref, m_s, l_s, acc_s = refs
        else:
            q_ref, k_ref, v_ref, o_ref, lse_ref, m_s, l_s, acc_s = refs
        i = pl.program_id(1)
        m_s[...] = jnp.full(m_s.shape, -jnp.inf, F32)
        l_s[...] = jnp.zeros_like(l_s)
        acc_s[...] = jnp.zeros_like(acc_s)
        qv = q_ref[...]
        qm = [jnp.where(_head_mask(cw, e), qv, jnp.zeros_like(qv)) for e in range(2)]

        def step(j, masked):
            rows = pl.ds(pl.multiple_of(j * tq, tq), tq)
            kj = k_ref[rows, :]
            vj = v_ref[rows, :]
            for e in range(2):
                s = lax.dot_general(qm[e], kj, _NT, preferred_element_type=F32) * c
                if fox:
                    s = s - fr_ref[e, j] * LOG2E
                if masked:
                    r = lax.broadcasted_iota(jnp.int32, (tq, tq), 0)
                    cc = lax.broadcasted_iota(jnp.int32, (tq, tq), 1)
                    s = jnp.where(cc <= r, s, -jnp.inf)
                m_prev = m_s[e]
                m_new = jnp.maximum(m_prev, jnp.max(s, axis=1, keepdims=True))
                alpha = jnp.exp2(m_prev - m_new)
                ps = [jnp.exp2(s[:, t * LANES:(t + 1) * LANES] - m_new) for t in range(nc)]
                l_add = ps[0]
                for t in range(1, nc):
                    l_add = l_add + ps[t]
                l_s[e] = alpha * l_s[e] + l_add
                p = jnp.concatenate(ps, axis=1).astype(BF16)
                acc_s[e] = alpha * acc_s[e] + jnp.dot(p, vj, preferred_element_type=F32)
                m_s[e] = m_new

        def loop_body(j, carry):
            step(j, False)
            return carry

        lax.fori_loop(0, i, loop_body, 0)
        step(i, True)
        outs = []
        for e in range(2):
            l = jnp.sum(l_s[e], axis=1, keepdims=True)
            outs.append(acc_s[e] / l)
            lse_ref[e] = m_s[e][:, 0:1] + jnp.log(l) * LOG2E
        o_ref[...] = jnp.where(_head_mask(LANES, 0), outs[0], outs[1]).astype(o_ref.dtype)

    in_specs = [
        pl.BlockSpec((tq, cw), lambda hp, i: (i, qcb + hp)),
        pl.BlockSpec((T, cw), lambda hp, i: (0, kcb + hp)),
        pl.BlockSpec((T, LANES), lambda hp, i: (0, vcb + hp)),
    ]
    args = [q, k, v]
    if fox:
        in_specs.append(pl.BlockSpec((2, nq, 1, tq), lambda hp, i: (hp, 0, 0, 0)))
        args.append(frow)
    return pl.pallas_call(
        body, name=name, grid=(N_HEADS // 2, nq),
        in_specs=in_specs,
        out_specs=[pl.BlockSpec((tq, LANES), lambda hp, i: (i, hp)),
                   pl.BlockSpec((2, tq, 1), lambda hp, i: (hp, i, 0))],
        out_shape=[jax.ShapeDtypeStruct((T, N_HEADS * HEAD_V), BF16), jax.ShapeDtypeStruct((N_HEADS, T, 1), F32)],
        scratch_shapes=[pltpu.VMEM((2, tq, LANES), F32)] * 3,
        compiler_params=_cparams(("parallel", "arbitrary")),
    )(*args)


def _attn_bwd(q, k, v, do, lse_row, dl_row, qcb, kcb, vcb, cw, scale, fcol=None, *, name, tq, dk_dtype):
    T = q.shape[0]
    nq = T // tq
    fox = fcol is not None
    c = scale * LOG2E

    def body(*refs):
        if fox:
            (k_ref, v_ref, q_ref, do_ref, lse_ref, dl_ref, fc_ref,
             dq_ref, dk_ref, dv_ref, dfk_ref, dfq_ref, dk_s, dv_s, df_s) = refs
        else:
            k_ref, v_ref, q_ref, do_ref, lse_ref, dl_ref, dq_ref, dk_ref, dv_ref, dk_s, dv_s = refs
        j = pl.program_id(1)

        @pl.when(j == 0)
        def _():
            dq_ref[...] = jnp.zeros_like(dq_ref)
            if fox:
                dfq_ref[...] = jnp.zeros_like(dfq_ref)

        kj = k_ref[...]
        vj = v_ref[...]
        km = [jnp.where(_head_mask(cw, e), kj, jnp.zeros_like(kj)) for e in range(2)]
        vm = [jnp.where(_head_mask(LANES, e), vj, jnp.zeros_like(vj)) for e in range(2)]
        dk_s[...] = jnp.zeros_like(dk_s)
        dv_s[...] = jnp.zeros_like(dv_s)
        if fox:
            df_s[...] = jnp.zeros_like(df_s)
            fk2 = [fc_ref[e] * LOG2E for e in range(2)]

        def step(i, masked):
            rows = pl.ds(pl.multiple_of(i * tq, tq), tq)
            qi = q_ref[rows, :]
            doi = do_ref[rows, :]
            dq_c = None
            for e in range(2):
                st = lax.dot_general(km[e], qi, _NT, preferred_element_type=F32) * c
                if fox:
                    st = st - fk2[e]
                if masked:
                    r = lax.broadcasted_iota(jnp.int32, (tq, tq), 0)
                    cc = lax.broadcasted_iota(jnp.int32, (tq, tq), 1)
                    st = jnp.where(cc >= r, st, -jnp.inf)
                pt = jnp.exp2(st - lse_ref[e, i])
                dv_s[e] += jnp.dot(pt.astype(BF16), doi, preferred_element_type=F32)
                dpt = lax.dot_general(vm[e], doi, _NT, preferred_element_type=F32)
                dst = pt * (dpt - dl_ref[e, i])
                dsb = dst.astype(BF16)
                dk_s[e] += jnp.dot(dsb, qi, preferred_element_type=F32)
                part = lax.dot_general(dsb, km[e], _TN, preferred_element_type=F32)
                dq_c = part if dq_c is None else dq_c + part
                if fox:
                    df_s[e] -= jnp.sum(dst, axis=1, keepdims=True)
                    dfq_ref[e, i] += jnp.sum(dst, axis=0, keepdims=True)
            dq_ref[rows, :] += dq_c * scale

        def loop_body(i, carry):
            step(i, False)
            return carry

        step(j, True)
        lax.fori_loop(j + 1, nq, loop_body, 0)
        m0 = _head_mask(cw, 0)
        m1 = _head_mask(cw, 1)
        dk_ref[...] = (jnp.where(m0, dk_s[0], jnp.where(m1, dk_s[1], 0.0)) * scale).astype(dk_ref.dtype)
        dv_ref[...] = jnp.where(_head_mask(LANES, 0), dv_s[0], dv_s[1]).astype(dv_ref.dtype)
        if fox:
            dfk_ref[...] = df_s[...]

    row4 = pl.BlockSpec((2, nq, 1, tq), lambda hp, j: (hp, 0, 0, 0))
    in_specs = [
        pl.BlockSpec((tq, cw), lambda hp, j: (j, kcb + hp)),
        pl.BlockSpec((tq, LANES), lambda hp, j: (j, vcb + hp)),
        pl.BlockSpec((T, cw), lambda hp, j: (0, qcb + hp)),
        pl.BlockSpec((T, LANES), lambda hp, j: (0, hp)),
        row4, row4,
    ]
    args = [k, v, q, do, lse_row, dl_row]
    out_specs = [pl.BlockSpec((T, cw), lambda hp, j: (0, hp)),
                 pl.BlockSpec((tq, cw), lambda hp, j: (j, hp)),
                 pl.BlockSpec((tq, LANES), lambda hp, j: (j, hp))]
    out_shape = [jax.ShapeDtypeStruct((T, 4 * cw), F32), jax.ShapeDtypeStruct((T, 4 * cw), dk_dtype),
                 jax.ShapeDtypeStruct((T, N_HEADS * HEAD_V), BF16)]
    scratch = [pltpu.VMEM((2, tq, cw), F32), pltpu.VMEM((2, tq, LANES), F32)]
    if fox:
        in_specs.append(pl.BlockSpec((2, tq, 1), lambda hp, j: (hp, j, 0)))
        args.append(fcol)
        out_specs += [pl.BlockSpec((2, tq, 1), lambda hp, j: (hp, j, 0)), row4]
        out_shape += [jax.ShapeDtypeStruct((N_HEADS, T, 1), F32), jax.ShapeDtypeStruct((N_HEADS, nq, 1, tq), F32)]
        scratch.append(pltpu.VMEM((2, tq, 1), F32))
    return pl.pallas_call(
        body, name=name, grid=(N_HEADS // 2, nq),
        in_specs=in_specs, out_specs=out_specs, out_shape=out_shape, scratch_shapes=scratch,
        compiler_params=pltpu.CompilerParams(dimension_semantics=("parallel", "arbitrary"),
                                             vmem_limit_bytes=ATTN_BWD_VMEM_LIMIT),
    )(*args)


def _peers():
    x, y, c = lax.axis_index("x"), lax.axis_index("y"), lax.axis_index("c")
    me = 4 * x + 2 * y + c
    out = []
    for k in range(1, N_DEV):
        px = (1 - x) if (k & 4) else x
        py = (1 - y) if (k & 2) else y
        pc = (1 - c) if (k & 1) else c
        out.append(((px, py, pc), 4 * px + 2 * py + pc))
    return me, out


def _allgather(shard):
    R = shard.shape[0]

    def body(x_ref, o_ref, send_sems, recv_sems, local_sem):
        me, peers = _peers()
        mine = pltpu.make_async_copy(x_ref, o_ref.at[me], local_sem)
        mine.start()
        sends = []
        for k, (dev, _) in enumerate(peers):
            cp = pltpu.make_async_remote_copy(src_ref=x_ref, dst_ref=o_ref.at[me], send_sem=send_sems.at[k],
                                              recv_sem=recv_sems.at[k], device_id=dev, device_id_type=MESH)
            cp.start()
            sends.append(cp)
        for k, (dev, lin) in enumerate(peers):
            pltpu.make_async_remote_copy(src_ref=x_ref, dst_ref=o_ref.at[lin], send_sem=send_sems.at[k],
                                         recv_sem=recv_sems.at[k], device_id=dev, device_id_type=MESH).wait_recv()
        for cp in sends:
            cp.wait_send()
        mine.wait()

    return pl.pallas_call(
        body, name="allgather_weights",
        in_specs=[pl.BlockSpec(memory_space=pl.ANY)],
        out_specs=pl.BlockSpec(memory_space=pl.ANY),
        out_shape=jax.ShapeDtypeStruct((N_DEV, R, LANES), shard.dtype),
        scratch_shapes=[pltpu.SemaphoreType.DMA((N_DEV - 1,)), pltpu.SemaphoreType.DMA((N_DEV - 1,)),
                        pltpu.SemaphoreType.DMA],
    )(shard)


def _exchange_pieces(pieces):
    R = pieces.shape[1]

    def body(g_ref, o_ref, send_sems, recv_sems, local_sem):
        me, peers = _peers()
        mine = pltpu.make_async_copy(g_ref.at[me], o_ref.at[me], local_sem)
        mine.start()
        sends = []
        for k, (dev, lin) in enumerate(peers):
            cp = pltpu.make_async_remote_copy(src_ref=g_ref.at[lin], dst_ref=o_ref.at[me], send_sem=send_sems.at[k],
                                              recv_sem=recv_sems.at[k], device_id=dev, device_id_type=MESH)
            cp.start()
            sends.append(cp)
        for k, (dev, lin) in enumerate(peers):
            pltpu.make_async_remote_copy(src_ref=g_ref.at[lin], dst_ref=o_ref.at[lin], send_sem=send_sems.at[k],
                                         recv_sem=recv_sems.at[k], device_id=dev, device_id_type=MESH).wait_recv()
        for cp in sends:
            cp.wait_send()
        mine.wait()

    return pl.pallas_call(
        body, name="exchange_grad_pieces",
        in_specs=[pl.BlockSpec(memory_space=pl.ANY)],
        out_specs=pl.BlockSpec(memory_space=pl.ANY),
        out_shape=jax.ShapeDtypeStruct((N_DEV, R, LANES), pieces.dtype),
        scratch_shapes=[pltpu.SemaphoreType.DMA((N_DEV - 1,)), pltpu.SemaphoreType.DMA((N_DEV - 1,)),
                        pltpu.SemaphoreType.DMA],
    )(pieces)


def _adamw(w, g, m, v):
    m2 = ADAM_B1 * m + (1.0 - ADAM_B1) * g
    v2 = ADAM_B2 * v + (1.0 - ADAM_B2) * (g * g)
    m_hat = m2 / (1.0 - ADAM_B1 ** ADAM_STEP)
    v_hat = v2 / (1.0 - ADAM_B2 ** ADAM_STEP)
    delta = -ADAM_LR * (m_hat / (jnp.sqrt(v_hat) + ADAM_EPS) + ADAM_WD * w)
    return delta, m2, v2


def _sum_adamw(pieces, w, m, v, *, tr=1824):
    R = w.shape[0]
    tr = min(tr, R)
    while R % tr:
        tr //= 2

    def body(p_ref, w_ref, m_ref, v_ref, g_ref, d_ref, m2_ref, v2_ref):
        g = p_ref[0].astype(F32)
        for s in range(1, N_DEV):
            g = g + p_ref[s].astype(F32)
        delta, m2, v2 = _adamw(w_ref[...], g, m_ref[...], v_ref[...])
        g_ref[...] = g
        d_ref[...] = delta
        m2_ref[...] = m2
        v2_ref[...] = v2

    row = pl.BlockSpec((tr, LANES), lambda i: (i, 0))
    return pl.pallas_call(
        body, name="sum_pieces_adamw", grid=(R // tr,),
        in_specs=[pl.BlockSpec((N_DEV, tr, LANES), lambda i: (0, i, 0)), row, row, row],
        out_specs=[row, row, row, row],
        out_shape=[jax.ShapeDtypeStruct((R, LANES), F32)] * 4,
        compiler_params=_cparams(("parallel",)),
    )(pieces, w, m, v)


def _small_allreduce_adamw(part, w, m, v):
    shape = part.shape

    def body(p_ref, w_ref, m_ref, v_ref, g_ref, d_ref, m2_ref, v2_ref, loss_ref, gath, send_sems, recv_sems):
        me, peers = _peers()
        gath[me] = p_ref[...]
        sends = []
        for k, (dev, _) in enumerate(peers):
            cp = pltpu.make_async_remote_copy(src_ref=p_ref, dst_ref=gath.at[me], send_sem=send_sems.at[k],
                                              recv_sem=recv_sems.at[k], device_id=dev, device_id_type=MESH)
            cp.start()
            sends.append(cp)
        for k, (dev, lin) in enumerate(peers):
            pltpu.make_async_remote_copy(src_ref=p_ref, dst_ref=gath.at[lin], send_sem=send_sems.at[k],
                                         recv_sem=recv_sems.at[k], device_id=dev, device_id_type=MESH).wait_recv()
        for cp in sends:
            cp.wait_send()
        g = gath[0]
        for s in range(1, N_DEV):
            g = g + gath[s]
        delta, m2, v2 = _adamw(w_ref[...], g, m_ref[...], v_ref[...])
        g_ref[...] = g
        d_ref[...] = delta
        m2_ref[...] = m2
        v2_ref[...] = v2
        sq = jnp.sum(g[SMALL_LOSS_ROW:SMALL_LOSS_ROW + 8, :], axis=1, keepdims=True)
        tot = jnp.sum(sq, axis=0, keepdims=True) * (0.5 / D_MODEL)
        loss_ref[...] = jnp.broadcast_to(tot, loss_ref.shape)

    vm = pl.BlockSpec(memory_space=pltpu.VMEM)
    return pl.pallas_call(
        body, name="small_allreduce_adamw",
        in_specs=[vm, vm, vm, vm],
        out_specs=[vm, vm, vm, vm, vm],
        out_shape=[jax.ShapeDtypeStruct(shape, F32)] * 4 + [jax.ShapeDtypeStruct((8, LANES), F32)],
        scratch_shapes=[pltpu.VMEM((N_DEV,) + shape, F32),
                        pltpu.SemaphoreType.DMA((N_DEV - 1,)), pltpu.SemaphoreType.DMA((N_DEV - 1,))],
    )(part, w, m, v)


def _perm_in_cols(w):
    z = lambda n: jnp.zeros(w.shape[:-1] + (n,), w.dtype)
    k1, k2 = w[..., 384:400], w[..., 400:416]
    small = jnp.concatenate([k1, k1, w[..., 1952:1960], z(24), k2, k2, z(32)], -1)
    return jnp.concatenate([w[..., 1960:2984], w[..., 2984:4008], w[..., 416:928], w[..., 928:1440],
                            w[..., 1440:1952], w[..., 0:256], w[..., 256:384], small], -1)


def _unperm_in_grad(gp):
    s = gp[..., P_SMALL:]
    return jnp.concatenate([gp[..., P_CQ:P_CQ + 256], gp[..., P_CKV:P_CKV + 128], s[..., 0:16] + s[..., 16:32],
                            s[..., 64:80] + s[..., 80:96], gp[..., P_FQ:P_FQ + 512], gp[..., P_FK:P_FK + 512],
                            gp[..., P_FV:P_FV + 512], s[..., S_FL:S_FL + 8], gp[..., P_GA:P_GA + 1024],
                            gp[..., P_GB:P_GB + 1024]], -1)


def _ext_uq_cols(w):
    r = w.shape[0]
    w3 = w.reshape(r, N_HEADS, NOPE + ROPE)
    z = jnp.zeros((r, 32), w.dtype)
    cols = []
    for hp in range(N_HEADS // 2):
        a, b = 2 * hp, 2 * hp + 1
        cols += [w3[:, a, :NOPE], w3[:, b, :NOPE], w3[:, a, NOPE:NOPE + 16], w3[:, b, NOPE:NOPE + 16], z,
                 w3[:, a, NOPE + 16:], w3[:, b, NOPE + 16:], z]
    return jnp.concatenate(cols, -1)


def _unext_uq_cols(wp):
    heads = []
    for h in range(N_HEADS):
        blk = wp[:, (h // 2) * 256:(h // 2 + 1) * 256]
        e = h % 2
        heads.append(jnp.concatenate([blk[:, 64 * e:64 * (e + 1)], blk[:, 128 + 16 * e:144 + 16 * e],
                                      blk[:, 192 + 16 * e:208 + 16 * e]], -1))
    return jnp.concatenate(heads, -1)


def _pack_shards(shards, dtype):
    flat = jnp.concatenate([s.reshape(-1).astype(dtype) for s in shards])
    flat = jnp.pad(flat, (0, PACK_ROWS * LANES - PACK_ELEMS))
    return flat.reshape(PACK_ROWS, LANES)


def _unpack_shards(packed):
    flat = packed.reshape(-1)
    out, off = [], 0
    for _, r, c, _ in SHARDED:
        out.append(flat[off:off + r * c].reshape(1, r, c))
        off += r * c
    return out


def _unpack_full(gathered):
    flat = gathered.reshape(N_DEV, -1)
    out, off = {}, 0
    for name, r, c, by_col in SHARDED:
        blk = flat[:, off:off + r * c].reshape(N_DEV, r, c)
        out[name] = blk.transpose(1, 0, 2).reshape(r, N_DEV * c) if by_col else blk.reshape(N_DEV * r, c)
        off += r * c
    return out


def _pack_full(grads, dtype):
    segs = []
    for name, r, c, by_col in SHARDED:
        g = grads[name]
        if by_col:
            g = g.reshape(r, N_DEV, c).transpose(1, 0, 2)
        segs.append(g.reshape(N_DEV, r * c).astype(dtype))
    flat = jnp.concatenate(segs, axis=1)
    flat = jnp.pad(flat, ((0, 0), (0, PACK_ROWS * LANES - PACK_ELEMS)))
    return flat.reshape(N_DEV, PACK_ROWS, LANES)


SMALL_LAYOUT = (("ln_pre_mix", 1024, 0), ("ln_post_mix", 1024, 8), ("ln_pre_mlp", 1024, 16),
                ("ln_post_mlp", 1024, 24), ("b_in", 4008, 32), ("q_a_norm", 256, 64), ("kv_a_norm", 128, 66))


def _pack_small(vals, extra=None):
    rows = []
    for name, n, _ in SMALL_LAYOUT:
        v = vals[name].reshape(-1).astype(F32)
        pad = -n % LANES
        rows.append(jnp.pad(v, (0, pad)).reshape(-1, LANES))
    rows.append(jnp.zeros((SMALL_LOSS_ROW - 67, LANES), F32))
    rows.append(jnp.zeros((8, LANES), F32) if extra is None else extra.reshape(8, LANES))
    return jnp.concatenate(rows, axis=0)


def _unpack_small(packed):
    out = {}
    for name, n, r0 in SMALL_LAYOUT:
        nr = -(-n // LANES)
        out[name] = packed[r0:r0 + nr].reshape(-1)[:n].reshape(1, n)
    return out


def _rms(xf, g):
    r = lax.rsqrt(jnp.mean(xf * xf, axis=-1, keepdims=True) + NORM_EPS)
    return (xf * r) * g


def _rms_bwd(xf, g, dy):
    r = lax.rsqrt(jnp.mean(xf * xf, axis=-1, keepdims=True) + NORM_EPS)
    xhat = xf * r
    dxhat = dy * g
    dx = r * (dxhat - xhat * jnp.mean(dxhat * xhat, axis=-1, keepdims=True))
    return dx, dy * xhat


def _sigmoid(t):
    return 1.0 / (1.0 + jnp.exp(-t))


def _log_sigmoid(t):
    return jnp.minimum(t, 0.0) - jnp.log(1.0 + jnp.exp(-jnp.abs(t)))


def _lane_sign():
    lane = lax.broadcasted_iota(jnp.int32, (1, LANES), 1)
    return jnp.where(lane < 64, -1.0, 1.0).astype(F32), lane


def _rope_lanes(lane):
    return (lane < 32) | ((lane >= 64) & (lane < 96))


def kernel(x, positions, ln_pre_mix, ln_post_mix, ln_pre_mlp, ln_post_mlp, w_in, b_in, q_a_norm, w_uq, kv_a_norm, w_uk, w_uv, w_o_mla, w_o_fox, w_out, w_ff1, w_ff2, loss_target, m_ln_pre_mix, m_ln_post_mix, m_ln_pre_mlp, m_ln_post_mlp, m_w_in, m_b_in, m_q_a_norm, m_w_uq, m_kv_a_norm, m_w_uk, m_w_uv, m_w_o_mla, m_w_o_fox, m_w_out, m_w_ff1, m_w_ff2, v_ln_pre_mix, v_ln_post_mix, v_ln_pre_mlp, v_ln_post_mlp, v_w_in, v_b_in, v_q_a_norm, v_w_uq, v_kv_a_norm, v_w_uk, v_w_uv, v_w_o_mla, v_w_o_fox, v_w_out, v_w_ff1, v_w_ff2):
    T = x.shape[1]
    x2 = x.reshape(T, D_MODEL)
    tgt = loss_target.reshape(T, D_MODEL)
    w_sh = dict(w_in=w_in, w_uq=w_uq, w_uk=w_uk, w_uv=w_uv, w_o_mla=w_o_mla, w_o_fox=w_o_fox, w_out=w_out,
                w_ff1=w_ff1, w_ff2=w_ff2)
    m_sh = dict(w_in=m_w_in, w_uq=m_w_uq, w_uk=m_w_uk, w_uv=m_w_uv, w_o_mla=m_w_o_mla, w_o_fox=m_w_o_fox,
                w_out=m_w_out, w_ff1=m_w_ff1, w_ff2=m_w_ff2)
    v_sh = dict(w_in=v_w_in, w_uq=v_w_uq, w_uk=v_w_uk, w_uv=v_w_uv, w_o_mla=v_w_o_mla, w_o_fox=v_w_o_fox,
                w_out=v_w_out, w_ff1=v_w_ff1, w_ff2=v_w_ff2)
    small_w = dict(ln_pre_mix=ln_pre_mix, ln_post_mix=ln_post_mix, ln_pre_mlp=ln_pre_mlp, ln_post_mlp=ln_post_mlp,
                   b_in=b_in, q_a_norm=q_a_norm, kv_a_norm=kv_a_norm)
    small_m = dict(ln_pre_mix=m_ln_pre_mix, ln_post_mix=m_ln_post_mix, ln_pre_mlp=m_ln_pre_mlp,
                   ln_post_mlp=m_ln_post_mlp, b_in=m_b_in, q_a_norm=m_q_a_norm, kv_a_norm=m_kv_a_norm)
    small_v = dict(ln_pre_mix=v_ln_pre_mix, ln_post_mix=v_ln_post_mix, ln_pre_mlp=v_ln_pre_mlp,
                   ln_post_mlp=v_ln_post_mlp, b_in=v_b_in, q_a_norm=v_q_a_norm, kv_a_norm=v_kv_a_norm)
    names = [n for n, _, _, _ in SHARDED]

    gathered = _allgather(_pack_shards([w_sh[n] for n in names], BF16))
    W = _unpack_full(gathered)
    w_in_p = _perm_in_cols(W["w_in"])
    b_in_p = _perm_in_cols(b_in.astype(F32))
    w_uq_x = _ext_uq_cols(W["w_uq"])
    w_kv = jnp.concatenate([W["w_uk"], W["w_uv"]], axis=1)
    g1, g2, g3, g4 = ln_pre_mix, ln_post_mix, ln_pre_mlp, ln_post_mlp
    gq, gkv = q_a_norm, kv_a_norm
    tq = min(ATTN_TILE, T)
    nq = T // tq

    (h,) = _rowwise(lambda xv, g: _rms(xv, g), [x2], [g1], [(D_MODEL, BF16)], name="pre_mix_norm")
    z = _mm(h, w_in_p, bias=b_in_p, name="in_proj")
    zs = _mm(h, w_in_p[:, P_SMALL:], bias=b_in_p[:, P_SMALL:], outs=(F32,), name="in_proj_small")

    def lora_norm(cq, ckv, a, b):
        return _rms(cq.astype(F32), a), _rms(ckv.astype(F32), b)

    cqn, ckvn = _rowwise(lora_norm, [(z, 256, P_CQ // 256), (z, 128, P_CKV // 128)], [gq, gkv],
                         [(Q_LORA, BF16), (KV_LORA, BF16)], name="lora_norm")
    q_x = _mm(cqn, w_uq_x, outs=(F32,), name="q_up")
    kv = _mm(ckvn, w_kv, name="kv_up")

    half = ROPE // 2
    inv_freq = ROPE_THETA ** (-jnp.arange(half, dtype=F32) / half)
    inv128 = jnp.tile(inv_freq, LANES // half).reshape(1, LANES)
    pos_col = positions.reshape(T, 1).astype(F32)

    def rope_tables(p, f):
        ang = p * f
        return jnp.cos(ang), jnp.sin(ang)

    cos_t, sin_t = _rowwise(rope_tables, [pos_col], [inv128], [(LANES, F32), (LANES, F32)], name="rope_tables")

    def rope_fwd(qx, kn, s, cs, sn):
        sign, lane = _lane_sign()
        rot = lambda t: t * cs + pltpu.roll(t, 64, 1) * sn * sign
        k_rot = jnp.where(_rope_lanes(lane), rot(s), 0.0).astype(BF16)
        qs, ks = [], []
        for hp in range(N_HEADS // 2):
            qs += [qx[:, 256 * hp:256 * hp + 128], rot(qx[:, 256 * hp + 128:256 * hp + 256])]
            ks += [kn[:, 128 * hp:128 * hp + 128], k_rot]
        return jnp.concatenate(qs, axis=1), jnp.concatenate(ks, axis=1)

    q_e, k_e = _rowwise(rope_fwd, [q_x, (kv, 512, 0), zs, cos_t, sin_t], [], [(1024, BF16), (1024, BF16)],
                        name="rope_fwd")
    f_cum = _seq_cumsum(zs, _log_sigmoid, reverse=False, name="forget_cumsum")
    f_heads = f_cum[:, S_FL:S_FL + N_HEADS].T
    f_col = f_heads.reshape(N_HEADS, T, 1)
    f_row = f_heads.reshape(N_HEADS, nq, 1, tq)

    mla_scale = float((NOPE + ROPE) ** -0.5)
    fox_scale = float(FOX_D ** -0.5)
    FQ, FK, FV = P_FQ // LANES, P_FK // LANES, P_FV // LANES
    o_mla, lse_mla = _attn_fwd(q_e, k_e, kv, 0, 0, 4, 256, mla_scale, name="mla_attn_fwd", tq=tq)
    o_fox, lse_fox = _attn_fwd(z, z, z, FQ, FK, FV, LANES, fox_scale, f_row, name="fox_attn_fwd", tq=tq)
    y_mla = _mm(o_mla, W["w_o_mla"], outs=(F32,), name="o_proj_mla")
    y_fox = _mm(o_fox, W["w_o_fox"], outs=(F32,), name="o_proj_fox")

    def gate_merge(ga, gb, ya, yb):
        return _sigmoid(ga.astype(F32)) * ya + _sigmoid(gb.astype(F32)) * yb

    (merged,) = _rowwise(gate_merge, [(z, 1024, 0), (z, 1024, 1), y_mla, y_fox], [], [(D_MODEL, BF16)],
                         name="gate_merge")
    mix = _mm(merged, W["w_out"], outs=(F32,), name="out_proj")

    def post_mix(xv, mv, a, b):
        x1v = xv + _rms(mv, a)
        return x1v, _rms(x1v, b)

    x1, h2 = _rowwise(post_mix, [x2, mix], [g2, g3], [(D_MODEL, F32), (D_MODEL, BF16)], name="post_mix_norm")

    def relu2(acc):
        r = jnp.maximum(acc, 0.0)
        return r * r, acc

    act, u = _mm(h2, W["w_ff1"], epi=relu2, outs=(BF16, BF16), name="ff1")
    mlp = _mm(act, W["w_ff2"], outs=(F32,), name="ff2")

    def loss_bwd(x1v, mv, tv, g):
        y = x1v + _rms(mv, g)
        d = y - tv
        dy = d * (1.0 / D_MODEL)
        dm, dg = _rms_bwd(mv, g, dy)
        return dy, dm, dg, d * d

    dy, dm, dg4, loss_cols = _rowwise(loss_bwd, [x1, mlp, tgt], [g4], [(D_MODEL, F32), (D_MODEL, BF16)],
                                      [D_MODEL, D_MODEL], name="loss_bwd")

    def relu2_bwd(acc, uv):
        return (acc * (2.0 * jnp.maximum(uv.astype(F32), 0.0)),)

    du = _mm(dm, W["w_ff2"].T, extras=(u,), epi=relu2_bwd, name="ff2_bwd")
    dw_ff2 = _mm(act, dm, ta=True, outs=(F32,), name="ff2_wgrad")
    dh2 = _mm(du, W["w_ff1"].T, outs=(F32,), name="ff1_bwd")
    dw_ff1 = _mm(h2, du, ta=True, outs=(F32,), name="ff1_wgrad")

    def post_mix_bwd(x1v, dh2v, dyv, mv, a, b):
        d3, dg3v = _rms_bwd(x1v, b, dh2v)
        dx1v = dyv + d3
        dmixv, dg2v = _rms_bwd(mv, a, dx1v)
        return dx1v, dmixv, dg3v, dg2v

    dx1, dmix, dg3, dg2 = _rowwise(post_mix_bwd, [x1, dh2, dy, mix], [g2, g3], [(D_MODEL, F32), (D_MODEL, BF16)],
                                   [D_MODEL, D_MODEL], name="post_mix_bwd")
    dmerged = _mm(dmix, W["w_out"].T, outs=(F32,), name="out_proj_bwd")
    dw_out = _mm(merged, dmix, ta=True, outs=(F32,), name="out_proj_wgrad")

    def gate_bwd(dmg, ga, gb, ya, yb):
        sa = _sigmoid(ga.astype(F32))
        sb = _sigmoid(gb.astype(F32))
        return dmg * sa, dmg * sb, dmg * ya * sa * (1.0 - sa), dmg * yb * sb * (1.0 - sb)

    dy_mla, dy_fox, dga, dgb = _rowwise(gate_bwd, [dmerged, (z, 1024, 0), (z, 1024, 1), y_mla, y_fox], [],
                                        [(D_MODEL, BF16)] * 4, name="gate_bwd")
    do_mla = _mm(dy_mla, W["w_o_mla"].T, name="o_proj_mla_bwd")
    do_fox = _mm(dy_fox, W["w_o_fox"].T, name="o_proj_fox_bwd")
    dw_o_mla = _mm(o_mla, dy_mla, ta=True, outs=(F32,), name="o_proj_mla_wgrad")
    dw_o_fox = _mm(o_fox, dy_fox, ta=True, outs=(F32,), name="o_proj_fox_wgrad")

    def head_dots(ov, dov):
        r = lax.broadcasted_iota(jnp.int32, (N_HEADS * HEAD_V, LANES), 0)
        cc = lax.broadcasted_iota(jnp.int32, (N_HEADS * HEAD_V, LANES), 1)
        sel = jnp.where((r // HEAD_V) == cc, 1.0, 0.0).astype(BF16)
        return _dot3(ov.astype(F32) * dov.astype(F32), sel, left=False)

    (dl_mla,) = _rowwise(head_dots, [o_mla, do_mla], [], [(LANES, F32)], name="mla_attn_delta")
    (dl_fox,) = _rowwise(head_dots, [o_fox, do_fox], [], [(LANES, F32)], name="fox_attn_delta")
    row4 = lambda t: t.reshape(N_HEADS, nq, 1, tq)
    heads_row4 = lambda t: row4(t[:, :N_HEADS].T)
    dq_x, dk_x, dv_mla = _attn_bwd(q_e, k_e, kv, do_mla, row4(lse_mla), heads_row4(dl_mla), 0, 0, 4, 256, mla_scale,
                                   name="mla_attn_bwd", tq=tq, dk_dtype=F32)
    dq_fox, dk_fox, dv_fox, dfk, dfq = _attn_bwd(z, z, z, do_fox, row4(lse_fox), heads_row4(dl_fox), FQ, FK, FV, LANES,
                                                 fox_scale, f_col, name="fox_attn_bwd", tq=tq, dk_dtype=BF16)
    d_f = (dfk.reshape(N_HEADS, T) + dfq.reshape(N_HEADS, T)).T
    d_f128 = jnp.pad(d_f, ((0, 0), (S_FL, LANES - S_FL - N_HEADS)))
    df_rev = _seq_cumsum(d_f128, lambda t: t, reverse=True, name="forget_cumsum_bwd")

    def rope_bwd(dqx, dkx, dvm, dfr, s, cs, sn):
        sign, lane = _lane_sign()
        rot_t = lambda t: t * cs - pltpu.roll(t, 64, 1) * sn * sign
        dqs, dks = [], []
        dk_rot = None
        for hp in range(N_HEADS // 2):
            dqs += [dqx[:, 256 * hp:256 * hp + 128], rot_t(dqx[:, 256 * hp + 128:256 * hp + 256])]
            dks.append(dkx[:, 256 * hp:256 * hp + 128])
            blk = dkx[:, 256 * hp + 128:256 * hp + 256]
            dk_rot = blk if dk_rot is None else dk_rot + blk
        dfl = dfr * _sigmoid(-s)
        small = jnp.where(_rope_lanes(lane), rot_t(dk_rot),
                          jnp.where((lane >= S_FL) & (lane < S_FL + N_HEADS), dfl, 0.0))
        return jnp.concatenate(dqs, axis=1), jnp.concatenate(dks + [dvm.astype(F32)], axis=1), small

    dq_b, dkv, d_small = _rowwise(rope_bwd, [dq_x, dk_x, dv_mla, df_rev, zs, cos_t, sin_t], [],
                                  [(1024, BF16), (1024, BF16), (LANES, BF16)], name="rope_bwd")
    dcqn = _mm(dq_b, w_uq_x.T, outs=(F32,), name="q_up_bwd")
    dw_uq_x = _mm(cqn, dq_b, ta=True, outs=(F32,), name="q_up_wgrad")
    dckvn = _mm(dkv, w_kv.T, outs=(F32,), name="kv_up_bwd")
    dw_kv = _mm(ckvn, dkv, ta=True, outs=(F32,), name="kv_up_wgrad")

    def lora_norm_bwd(cq, ckv, dq_, dkv_, a, b):
        d1, dga_ = _rms_bwd(cq.astype(F32), a, dq_)
        d2, dgb_ = _rms_bwd(ckv.astype(F32), b, dkv_)
        return d1, d2, dga_, dgb_

    dcq, dckv, dgq, dgkv = _rowwise(lora_norm_bwd, [(z, 256, P_CQ // 256), (z, 128, P_CKV // 128), dcqn, dckvn],
                                    [gq, gkv], [(Q_LORA, BF16), (KV_LORA, BF16)], [Q_LORA, KV_LORA],
                                    name="lora_norm_bwd")
    dz = jnp.concatenate([dga, dgb, dq_fox.astype(BF16), dk_fox, dv_fox, dcq, dckv, d_small], -1)
    (db_in_p,) = _rowwise(lambda t: (t,), [dz], [], [], [D_IN_PAD], name="in_bias_grad")
    dh = _mm(dz, w_in_p.T, outs=(F32,), name="in_proj_bwd")
    dw_in_p = _mm(h, dz, ta=True, outs=(F32,), name="in_proj_wgrad")

    def pre_mix_bwd(xv, dhv, dx1v, g):
        d, dg = _rms_bwd(xv, g, dhv)
        return dx1v + d, dg

    grad_x, dg1 = _rowwise(pre_mix_bwd, [x2, dh, dx1], [g1], [(D_MODEL, F32)], [D_MODEL], name="pre_mix_bwd")

    full_grads = dict(w_in=_unperm_in_grad(dw_in_p), w_uq=_unext_uq_cols(dw_uq_x), w_uk=dw_kv[:, :512],
                      w_uv=dw_kv[:, 512:], w_o_mla=dw_o_mla, w_o_fox=dw_o_fox, w_out=dw_out, w_ff1=dw_ff1,
                      w_ff2=dw_ff2)
    pieces = _exchange_pieces(_pack_full(full_grads, BF16))
    g_p, d_p, m_p, v_p = _sum_adamw(pieces, _pack_shards([w_sh[n] for n in names], F32),
                                    _pack_shards([m_sh[n] for n in names], F32),
                                    _pack_shards([v_sh[n] for n in names], F32))
    grad_sh = dict(zip(names, _unpack_shards(g_p)))
    delta_sh = dict(zip(names, _unpack_shards(d_p)))
    newm_sh = dict(zip(names, _unpack_shards(m_p)))
    newv_sh = dict(zip(names, _unpack_shards(v_p)))

    small_part = _pack_small(dict(ln_pre_mix=dg1, ln_post_mix=dg2, ln_pre_mlp=dg3, ln_post_mlp=dg4,
                                  b_in=_unperm_in_grad(db_in_p), q_a_norm=dgq, kv_a_norm=dgkv), extra=loss_cols)
    sg, sd, sm, sv, loss_blk = _small_allreduce_adamw(small_part, _pack_small(small_w), _pack_small(small_m),
                                                      _pack_small(small_v))
    grad_sm, delta_sm, newm_sm, newv_sm = (_unpack_small(t) for t in (sg, sd, sm, sv))
    loss = loss_blk[0, 0]

    order = ["ln_pre_mix", "ln_post_mix", "ln_pre_mlp", "ln_post_mlp", "w_in", "b_in", "q_a_norm", "w_uq",
             "kv_a_norm", "w_uk", "w_uv", "w_o_mla", "w_o_fox", "w_out", "w_ff1", "w_ff2"]

    def pick(sm_d, sh_d):
        return [sm_d[n] if n in sm_d else sh_d[n] for n in order]

    return (loss, grad_x.reshape(1, T, D_MODEL), *pick(grad_sm, grad_sh), *pick(delta_sm, delta_sh),
            *pick(newm_sm, newm_sh), *pick(newv_sm, newv_sh))
```

```python
import numpy as np
import jax
import jax.numpy as jnp
from jax import lax
from jax.experimental import pallas as pl
from jax.experimental.pallas import tpu as pltpu

F32 = jnp.float32
BF16 = jnp.bfloat16
MESH = pl.DeviceIdType.MESH

D_MODEL = 1024
N_HEADS = 8
Q_LORA = 256
KV_LORA = 128
NOPE = 64
ROPE = 32
HEAD_V = 64
FOX_D = 64
D_FF = 4096
D_IN = 4008
D_IN_PAD = 4096
ROPE_THETA = 10000.0
NORM_EPS = 1e-6
N_DEV = 8

ADAM_LR = 0.001
ADAM_B1 = 0.9
ADAM_B2 = 0.999
ADAM_EPS = 1e-08
ADAM_WD = 0.01
ADAM_STEP = 10

LANES = 128
ROW_TILE = 512
ATTN_TILE = 512
VMEM_LIMIT = 48 * 1024 * 1024
ATTN_BWD_VMEM_LIMIT = 58 * 1024 * 1024

P_GA, P_GB, P_FQ, P_FK, P_FV, P_CQ, P_CKV, P_SMALL = 0, 1024, 2048, 2560, 3072, 3584, 3840, 3968
S_FL = 32

SHARDED = (
    ("w_in", 1024, 501, True), ("w_uq", 256, 96, True), ("w_uk", 128, 64, True), ("w_uv", 128, 64, True),
    ("w_o_mla", 512, 128, True), ("w_o_fox", 512, 128, True), ("w_out", 128, 1024, False),
    ("w_ff1", 1024, 512, True), ("w_ff2", 512, 1024, False),
)
GROUP_A = SHARDED[:4]
GROUP_B = SHARDED[4:]


def _pack_rows(group):
    return -(-sum(r * c for _, r, c, _ in group) // (LANES * 64)) * 64
SMALL_ROWS = 80
SMALL_LOSS_ROW = 72


def _cparams(sem=None):
    return pltpu.CompilerParams(dimension_semantics=sem, vmem_limit_bytes=VMEM_LIMIT)


def _mm(a, b, *, name, ta=False, bias=None, colscale=None, extras=(), epi=None, outs=(BF16,), tm=ROW_TILE, tn=1024,
        tk=1024):
    if ta:
        K, M = a.shape
        tm = min(1024, M)
    else:
        M, K = a.shape
        tm = min(tm, M)
    tk = min(tk, K)
    N = b.shape[1]
    tn = min(tn, N)
    nk = K // tk
    n_ex = len(extras)
    has_bias = bias is not None
    has_scale = colscale is not None

    def body(*refs):
        a_ref, b_ref = refs[0], refs[1]
        pos = 2
        bias_ref = scale_ref = None
        if has_bias:
            bias_ref = refs[pos]
            pos += 1
        if has_scale:
            scale_ref = refs[pos]
            pos += 1
        ex_refs = refs[pos:pos + n_ex]
        pos += n_ex
        o_refs = refs[pos:pos + len(outs)]
        pos += len(outs)
        av = a_ref[...].astype(BF16)
        bv = b_ref[...].astype(BF16)
        if ta:
            part = lax.dot_general(av, bv, (((0,), (0,)), ((), ())), preferred_element_type=F32)
        else:
            part = jnp.dot(av, bv, preferred_element_type=F32)

        def finish(acc):
            if has_bias:
                acc = acc + bias_ref[...]
            if has_scale:
                acc = acc * scale_ref[...]
            res = (acc,) if epi is None else epi(acc, *[r[...] for r in ex_refs])
            for o_ref, val in zip(o_refs, res):
                o_ref[...] = val.astype(o_ref.dtype)

        if nk == 1:
            finish(part)
        else:
            acc_ref = refs[pos]
            k = pl.program_id(2)

            @pl.when(k == 0)
            def _():
                acc_ref[...] = part

            @pl.when(k > 0)
            def _():
                acc_ref[...] += part

            @pl.when(k == nk - 1)
            def _():
                finish(acc_ref[...])

    if ta:
        a_spec = pl.BlockSpec((tk, tm), lambda i, j, k: (k, i))
    else:
        a_spec = pl.BlockSpec((tm, tk), lambda i, j, k: (i, k))
    in_specs = [a_spec, pl.BlockSpec((tk, tn), lambda i, j, k: (k, j))]
    args = [a, b]
    for row in (bias, colscale):
        if row is not None:
            in_specs.append(pl.BlockSpec((1, tn), lambda i, j, k: (0, j)))
            args.append(row)
    for e in extras:
        in_specs.append(pl.BlockSpec((tm, tn), lambda i, j, k: (i, j)))
        args.append(e)
    res = pl.pallas_call(
        body, name=name, grid=(M // tm, N // tn, nk),
        in_specs=in_specs,
        out_specs=[pl.BlockSpec((tm, tn), lambda i, j, k: (i, j)) for _ in outs],
        out_shape=[jax.ShapeDtypeStruct((M, N), dt) for dt in outs],
        scratch_shapes=[pltpu.VMEM((tm, tn), F32)] if nk > 1 else [],
        compiler_params=_cparams(("parallel", "parallel", "arbitrary")),
    )(*args)
    return res[0] if len(outs) == 1 else res


def _rowwise(fn, rows, bcasts, outs, accs=(), *, name, tm=ROW_TILE):
    arrs, specs = [], []
    for r in rows:
        if isinstance(r, tuple):
            arr, w, cb = r
            specs.append(pl.BlockSpec((tm, w), lambda i, cb=cb: (i, cb)))
        else:
            arr = r
            specs.append(pl.BlockSpec((tm, arr.shape[1]), lambda i: (i, 0)))
        arrs.append(arr)
    T = arrs[0].shape[0]
    tm = min(tm, T)
    specs = [pl.BlockSpec((tm,) + tuple(s.block_shape[1:]), s.index_map) for s in specs]
    for b in bcasts:
        arrs.append(b)
        specs.append(pl.BlockSpec(b.shape, lambda i: (0, 0)))
    n_in, n_out = len(arrs), len(outs)

    def body(*refs):
        vals = [r[...] for r in refs[:n_in]]
        res = fn(*vals)
        if not isinstance(res, (tuple, list)):
            res = (res,)
        for o_ref, val in zip(refs[n_in:n_in + n_out], res[:n_out]):
            o_ref[...] = val.astype(o_ref.dtype)
        i = pl.program_id(0)
        for a_ref, val in zip(refs[n_in + n_out:], res[n_out:]):
            col = jnp.sum(val.astype(F32), axis=0, keepdims=True)

            @pl.when(i == 0)
            def _(a_ref=a_ref, col=col):
                a_ref[...] = col

            @pl.when(i > 0)
            def _(a_ref=a_ref, col=col):
                a_ref[...] += col

    res = pl.pallas_call(
        body, name=name, grid=(T // tm,),
        in_specs=specs,
        out_specs=[pl.BlockSpec((tm, c), lambda i: (i, 0)) for c, _ in outs]
        + [pl.BlockSpec((1, c), lambda i: (0, 0)) for c in accs],
        out_shape=[jax.ShapeDtypeStruct((T, c), dt) for c, dt in outs]
        + [jax.ShapeDtypeStruct((1, c), F32) for c in accs],
        compiler_params=_cparams(("arbitrary",)),
    )(*arrs)
    return res


def _dot3(x, sel, left):
    hi = x.astype(BF16)
    r1 = x - hi.astype(F32)
    mid = r1.astype(BF16)
    lo = (r1 - mid.astype(F32)).astype(BF16)
    if left:
        d = lambda t: jnp.dot(sel, t, preferred_element_type=F32)
    else:
        d = lambda t: jnp.dot(t, sel, preferred_element_type=F32)
    return d(hi) + d(mid) + d(lo)


def _seq_cumsum(x, fn, *, reverse, name, tm=256):
    T, C = x.shape
    tm = min(tm, T)
    n = T // tm

    def body(x_ref, o_ref, carry_ref):
        i = pl.program_id(0)

        @pl.when(i == 0)
        def _():
            carry_ref[...] = jnp.zeros_like(carry_ref)

        v = fn(x_ref[...])
        r = lax.broadcasted_iota(jnp.int32, (tm, tm), 0)
        c = lax.broadcasted_iota(jnp.int32, (tm, tm), 1)
        tri = jnp.where((r <= c) if reverse else (r >= c), 1.0, 0.0).astype(BF16)
        carry = carry_ref[0:1, :]
        o_ref[...] = _dot3(v, tri, left=True) + carry
        carry_ref[0:1, :] = carry + jnp.sum(v, axis=0, keepdims=True)

    idx = (lambda i: (n - 1 - i, 0)) if reverse else (lambda i: (i, 0))
    return pl.pallas_call(
        body, name=name, grid=(n,),
        in_specs=[pl.BlockSpec((tm, C), idx)],
        out_specs=pl.BlockSpec((tm, C), idx),
        out_shape=jax.ShapeDtypeStruct((T, C), F32),
        scratch_shapes=[pltpu.VMEM((8, C), F32)],
        compiler_params=_cparams(("arbitrary",)),
    )(x)


_NT = (((1,), (1,)), ((), ()))
_TN = (((0,), (0,)), ((), ()))
LOG2E = 1.4426950408889634


def _head_mask(width, e):
    lane = lax.broadcasted_iota(jnp.int32, (1, width), 1)
    m = (lane >= 64 * e) & (lane < 64 * (e + 1))
    if width == 256:
        l = (lane - 128) & 63
        m = m | ((lane >= 128) & (l >= 16 * e) & (l < 16 * (e + 1)))
    return m


def _attn_fwd(q, k, v, qcb, kcb, vcb, cw, frow=None, *, name, tq, gather=None):
    T = q.shape[0]
    nq = T // tq
    nc = tq // LANES
    fox = frow is not None
    comm = gather is not None

    def body(*refs):
        refs = list(refs)
        q_ref, k_ref, v_ref = refs[:3]
        pos = 3
        if fox:
            fr_ref = refs[pos]
            pos += 1
        if comm:
            c_src = refs[pos]
            pos += 1
        o_ref, lse_ref = refs[pos:pos + 2]
        pos += 2
        if comm:
            c_dst = refs[pos]
            pos += 1
        m_s, l_s, acc_s = refs[pos:pos + 3]
        c_sems = refs[pos + 3:]
        i = pl.program_id(1)
        if comm:
            @pl.when((pl.program_id(0) == 0) & (i == 0))
            def _():
                _comm_start("gather", c_src, c_dst, *c_sems)
        m_s[...] = jnp.full(m_s.shape, -jnp.inf, F32)
        l_s[...] = jnp.zeros_like(l_s)
        acc_s[...] = jnp.zeros_like(acc_s)
        qv = q_ref[...]
        qm = [jnp.where(_head_mask(cw, e), qv, jnp.zeros_like(qv)) for e in range(2)]

        def step(j, masked):
            rows = pl.ds(pl.multiple_of(j * tq, tq), tq)
            kj = k_ref[rows, :]
            vj = v_ref[rows, :]
            for e in range(2):
                s = lax.dot_general(qm[e], kj, _NT, preferred_element_type=F32)
                if fox:
                    s = s - fr_ref[e, j] * LOG2E
                if masked:
                    r = lax.broadcasted_iota(jnp.int32, (tq, tq), 0)
                    cc = lax.broadcasted_iota(jnp.int32, (tq, tq), 1)
                    s = jnp.where(cc <= r, s, -jnp.inf)
                m_prev = m_s[e]
                m_new = jnp.maximum(m_prev, jnp.max(s, axis=1, keepdims=True))
                alpha = jnp.exp2(m_prev - m_new)
                ps = [jnp.exp2(s[:, t * LANES:(t + 1) * LANES] - m_new) for t in range(nc)]
                l_add = ps[0]
                for t in range(1, nc):
                    l_add = l_add + ps[t]
                l_s[e] = alpha * l_s[e] + l_add
                p = jnp.concatenate(ps, axis=1).astype(BF16)
                acc_s[e] = alpha * acc_s[e] + jnp.dot(p, vj, preferred_element_type=F32)
                m_s[e] = m_new

        def loop_body(j, carry):
            step(j, False)
            return carry

        lax.fori_loop(0, i, loop_body, 0)
        step(i, True)
        outs = []
        for e in range(2):
            l = jnp.sum(l_s[e], axis=1, keepdims=True)
            outs.append(acc_s[e] / l)
            lse_ref[e, 0] = _col_to_row(m_s[e] + jnp.log(l) * LOG2E)
        o_ref[...] = jnp.where(_head_mask(LANES, 0), outs[0], outs[1]).astype(o_ref.dtype)
        if comm:
            @pl.when((pl.program_id(0) == N_HEADS // 2 - 1) & (i == nq - 1))
            def _():
                _comm_wait("gather", c_src, c_dst, *c_sems)

    in_specs = [
        pl.BlockSpec((tq, cw), lambda hp, i: (i, qcb + hp)),
        pl.BlockSpec((T, cw), lambda hp, i: (0, kcb + hp)),
        pl.BlockSpec((T, LANES), lambda hp, i: (0, vcb + hp)),
    ]
    args = [q, k, v]
    if fox:
        in_specs.append(pl.BlockSpec((2, nq, 1, tq), lambda hp, i: (hp, 0, 0, 0)))
        args.append(frow)
    out_specs = [pl.BlockSpec((tq, LANES), lambda hp, i: (i, hp)),
                 pl.BlockSpec((2, 1, 1, tq), lambda hp, i: (hp, i, 0, 0))]
    out_shape = [jax.ShapeDtypeStruct((T, N_HEADS * HEAD_V), BF16), jax.ShapeDtypeStruct((N_HEADS, nq, 1, tq), F32)]
    scratch = [pltpu.VMEM((2, tq, LANES), F32)] * 3
    if comm:
        in_specs.append(pl.BlockSpec(memory_space=pl.ANY))
        args.append(gather)
        out_specs.append(pl.BlockSpec(memory_space=pl.ANY))
        out_shape.append(jax.ShapeDtypeStruct((N_DEV,) + gather.shape, gather.dtype))
        scratch += _comm_sems()
    return pl.pallas_call(
        body, name=name, grid=(N_HEADS // 2, nq),
        in_specs=in_specs, out_specs=out_specs, out_shape=out_shape, scratch_shapes=scratch,
        compiler_params=_cparams(("arbitrary", "arbitrary")),
    )(*args)


def _attn_bwd(q, k, v, do, lse_row, dl_row, qcb, kcb, vcb, cw, scale, fcum=None, *, name, tq, dk_dtype,
              exchange=None):
    T = q.shape[0]
    nq = T // tq
    fox = fcum is not None
    comm = exchange is not None

    def body(*refs):
        refs = list(refs)
        k_ref, v_ref, q_ref, do_ref, lse_ref, dl_ref = refs[:6]
        pos = 6
        if fox:
            fc_ref = refs[pos]
            pos += 1
        if comm:
            c_src = refs[pos]
            pos += 1
        dq_ref, dk_ref, dv_ref = refs[pos:pos + 3]
        pos += 3
        if fox:
            dfk_ref, dfq_ref = refs[pos:pos + 2]
            pos += 2
        if comm:
            c_dst = refs[pos]
            pos += 1
        dk_s, dv_s = refs[pos:pos + 2]
        pos += 2
        if fox:
            df_s = refs[pos]
            pos += 1
        c_sems = refs[pos:]
        j = pl.program_id(1)
        if comm:
            @pl.when((pl.program_id(0) == 0) & (j == 0))
            def _():
                _comm_start("exchange", c_src, c_dst, *c_sems)

        @pl.when(j == 0)
        def _():
            dq_ref[...] = jnp.zeros_like(dq_ref)
            if fox:
                dfq_ref[...] = jnp.zeros_like(dfq_ref)

        kj = k_ref[...]
        vj = v_ref[...]
        km = [jnp.where(_head_mask(cw, e), kj, jnp.zeros_like(kj)) for e in range(2)]
        vm = [jnp.where(_head_mask(LANES, e), vj, jnp.zeros_like(vj)) for e in range(2)]
        dk_s[...] = jnp.zeros_like(dk_s)
        dv_s[...] = jnp.zeros_like(dv_s)
        if fox:
            df_s[...] = jnp.zeros_like(df_s)
            lane = lax.broadcasted_iota(jnp.int32, (1, LANES), 1)
            fblk = fc_ref[...]
            fk2 = [jnp.sum(jnp.where(lane == S_FL + 2 * pl.program_id(0) + e, fblk, 0.0), axis=1, keepdims=True)
                   * LOG2E for e in range(2)]

        def step(i, masked):
            rows = pl.ds(pl.multiple_of(i * tq, tq), tq)
            qi = q_ref[rows, :]
            doi = do_ref[rows, :]
            dq_c = None
            for e in range(2):
                st = lax.dot_general(km[e], qi, _NT, preferred_element_type=F32)
                if fox:
                    st = st - fk2[e]
                if masked:
                    r = lax.broadcasted_iota(jnp.int32, (tq, tq), 0)
                    cc = lax.broadcasted_iota(jnp.int32, (tq, tq), 1)
                    st = jnp.where(cc >= r, st, -jnp.inf)
                pt = jnp.exp2(st - lse_ref[e, i])
                dv_s[e] += jnp.dot(pt.astype(BF16), doi, preferred_element_type=F32)
                dpt = lax.dot_general(vm[e], doi, _NT, preferred_element_type=F32)
                dst = pt * (dpt - dl_ref[e, i])
                dsb = dst.astype(BF16)
                dk_s[e] += jnp.dot(dsb, qi, preferred_element_type=F32)
                part = lax.dot_general(dsb, km[e], _TN, preferred_element_type=F32)
                dq_c = part if dq_c is None else dq_c + part
                if fox:
                    df_s[e] -= jnp.sum(dst, axis=1, keepdims=True)
                    dfq_ref[e, i] += jnp.sum(dst, axis=0, keepdims=True)
            dq_ref[rows, :] += dq_c * scale

        def loop_body(i, carry):
            step(i, False)
            return carry

        step(j, True)
        lax.fori_loop(j + 1, nq, loop_body, 0)
        m0 = _head_mask(cw, 0)
        m1 = _head_mask(cw, 1)
        dk_ref[...] = (jnp.where(m0, dk_s[0], jnp.where(m1, dk_s[1], 0.0)) * (1.0 / LOG2E)).astype(dk_ref.dtype)
        dv_ref[...] = jnp.where(_head_mask(LANES, 0), dv_s[0], dv_s[1]).astype(dv_ref.dtype)
        if fox:
            for e in range(2):
                dfk_ref[e, 0] = _col_to_row(jnp.broadcast_to(df_s[e], (tq, LANES)))
        if comm:
            @pl.when((pl.program_id(0) == N_HEADS // 2 - 1) & (j == nq - 1))
            def _():
                _comm_wait("exchange", c_src, c_dst, *c_sems)

    row4 = pl.BlockSpec((2, nq, 1, tq), lambda hp, j: (hp, 0, 0, 0))
    in_specs = [
        pl.BlockSpec((tq, cw), lambda hp, j: (j, kcb + hp)),
        pl.BlockSpec((tq, LANES), lambda hp, j: (j, vcb + hp)),
        pl.BlockSpec((T, cw), lambda hp, j: (0, qcb + hp)),
        pl.BlockSpec((T, LANES), lambda hp, j: (0, hp)),
        row4, row4,
    ]
    args = [k, v, q, do, lse_row, dl_row]
    out_specs = [pl.BlockSpec((T, cw), lambda hp, j: (0, hp)),
                 pl.BlockSpec((tq, cw), lambda hp, j: (j, hp)),
                 pl.BlockSpec((tq, LANES), lambda hp, j: (j, hp))]
    out_shape = [jax.ShapeDtypeStruct((T, 4 * cw), F32), jax.ShapeDtypeStruct((T, 4 * cw), dk_dtype),
                 jax.ShapeDtypeStruct((T, N_HEADS * HEAD_V), BF16)]
    scratch = [pltpu.VMEM((2, tq, cw), F32), pltpu.VMEM((2, tq, LANES), F32)]
    if fox:
        in_specs.append(pl.BlockSpec((tq, LANES), lambda hp, j: (j, 0)))
        args.append(fcum)
        out_specs += [pl.BlockSpec((2, 1, 1, tq), lambda hp, j: (hp, j, 0, 0)), row4]
        out_shape += [jax.ShapeDtypeStruct((N_HEADS, nq, 1, tq), F32)] * 2
        scratch.append(pltpu.VMEM((2, tq, 1), F32))
    if comm:
        in_specs.append(pl.BlockSpec(memory_space=pl.ANY))
        args.append(exchange)
        out_specs.append(pl.BlockSpec(memory_space=pl.ANY))
        out_shape.append(jax.ShapeDtypeStruct(exchange.shape, exchange.dtype))
        scratch += _comm_sems()
    return pl.pallas_call(
        body, name=name, grid=(N_HEADS // 2, nq),
        in_specs=in_specs, out_specs=out_specs, out_shape=out_shape, scratch_shapes=scratch,
        compiler_params=pltpu.CompilerParams(dimension_semantics=("arbitrary", "arbitrary"),
                                             vmem_limit_bytes=ATTN_BWD_VMEM_LIMIT),
    )(*args)


def _peers():
    x, y, c = lax.axis_index("x"), lax.axis_index("y"), lax.axis_index("c")
    me = 4 * x + 2 * y + c
    out = []
    for k in range(1, N_DEV):
        px = (1 - x) if (k & 4) else x
        py = (1 - y) if (k & 2) else y
        pc = (1 - c) if (k & 1) else c
        out.append(((px, py, pc), 4 * px + 2 * py + pc))
    return me, out


def _comm_sems():
    return [pltpu.SemaphoreType.DMA((N_DEV - 1,)), pltpu.SemaphoreType.DMA((N_DEV - 1,)), pltpu.SemaphoreType.DMA]


def _comm_copies(kind, src_ref, dst_ref, send_sems, recv_sems, local_sem):
    me, peers = _peers()
    local = pltpu.make_async_copy(src_ref if kind == "gather" else src_ref.at[me], dst_ref.at[me], local_sem)
    sends, recvs = [], []
    for k, (dev, lin) in enumerate(peers):
        src = src_ref if kind == "gather" else src_ref.at[lin]
        sends.append(pltpu.make_async_remote_copy(src_ref=src, dst_ref=dst_ref.at[me], send_sem=send_sems.at[k],
                                                  recv_sem=recv_sems.at[k], device_id=dev, device_id_type=MESH))
        recvs.append(pltpu.make_async_remote_copy(src_ref=src, dst_ref=dst_ref.at[lin], send_sem=send_sems.at[k],
                                                  recv_sem=recv_sems.at[k], device_id=dev, device_id_type=MESH))
    return local, sends, recvs


def _comm_start(kind, src_ref, dst_ref, send_sems, recv_sems, local_sem):
    local, sends, _ = _comm_copies(kind, src_ref, dst_ref, send_sems, recv_sems, local_sem)
    local.start()
    for cp in sends:
        cp.start()


def _comm_wait(kind, src_ref, dst_ref, send_sems, recv_sems, local_sem):
    local, sends, recvs = _comm_copies(kind, src_ref, dst_ref, send_sems, recv_sems, local_sem)
    for cp in recvs:
        cp.wait_recv()
    for cp in sends:
        cp.wait_send()
    local.wait()


def _comm_call(kind, src, name):
    def body(x_ref, o_ref, send_sems, recv_sems, local_sem):
        _comm_start(kind, x_ref, o_ref, send_sems, recv_sems, local_sem)
        _comm_wait(kind, x_ref, o_ref, send_sems, recv_sems, local_sem)

    shape = (N_DEV,) + src.shape if kind == "gather" else src.shape
    return pl.pallas_call(
        body, name=name,
        in_specs=[pl.BlockSpec(memory_space=pl.ANY)],
        out_specs=pl.BlockSpec(memory_space=pl.ANY),
        out_shape=jax.ShapeDtypeStruct(shape, src.dtype),
        scratch_shapes=_comm_sems(),
    )(src)


def _col_to_row(x):
    return x.T[0:1, :]


def _adamw(w, g, m, v):
    m2 = ADAM_B1 * m + (1.0 - ADAM_B1) * g
    v2 = ADAM_B2 * v + (1.0 - ADAM_B2) * (g * g)
    m_hat = m2 / (1.0 - ADAM_B1 ** ADAM_STEP)
    v_hat = v2 / (1.0 - ADAM_B2 ** ADAM_STEP)
    delta = -ADAM_LR * (m_hat / (jnp.sqrt(v_hat) + ADAM_EPS) + ADAM_WD * w)
    return delta, m2, v2


def _sum_adamw(pieces, w, m, v, *, name):
    R = w.shape[0]
    tr = R // 4 if R % 64 == 0 else R

    def body(p_ref, w_ref, m_ref, v_ref, g_ref, d_ref, m2_ref, v2_ref):
        g = p_ref[0].astype(F32)
        for s in range(1, N_DEV):
            g = g + p_ref[s].astype(F32)
        delta, m2, v2 = _adamw(w_ref[...], g, m_ref[...], v_ref[...])
        g_ref[...] = g
        d_ref[...] = delta
        m2_ref[...] = m2
        v2_ref[...] = v2

    row = pl.BlockSpec((tr, LANES), lambda i: (i, 0))
    return pl.pallas_call(
        body, name=name, grid=(R // tr,),
        in_specs=[pl.BlockSpec((N_DEV, tr, LANES), lambda i: (0, i, 0)), row, row, row],
        out_specs=[row, row, row, row],
        out_shape=[jax.ShapeDtypeStruct((R, LANES), F32)] * 4,
        compiler_params=_cparams(("parallel",)),
    )(pieces, w, m, v)


def _small_allreduce_adamw(part, w, m, v):
    shape = part.shape

    def body(p_ref, w_ref, m_ref, v_ref, g_ref, d_ref, m2_ref, v2_ref, loss_ref, gath, send_sems, recv_sems):
        me, peers = _peers()
        gath[me] = p_ref[...]
        sends = []
        for k, (dev, _) in enumerate(peers):
            cp = pltpu.make_async_remote_copy(src_ref=p_ref, dst_ref=gath.at[me], send_sem=send_sems.at[k],
                                              recv_sem=recv_sems.at[k], device_id=dev, device_id_type=MESH)
            cp.start()
            sends.append(cp)
        for k, (dev, lin) in enumerate(peers):
            pltpu.make_async_remote_copy(src_ref=p_ref, dst_ref=gath.at[lin], send_sem=send_sems.at[k],
                                         recv_sem=recv_sems.at[k], device_id=dev, device_id_type=MESH).wait_recv()
        for cp in sends:
            cp.wait_send()
        g = gath[0]
        for s in range(1, N_DEV):
            g = g + gath[s]
        delta, m2, v2 = _adamw(w_ref[...], g, m_ref[...], v_ref[...])
        g_ref[...] = g
        d_ref[...] = delta
        m2_ref[...] = m2
        v2_ref[...] = v2
        sq = jnp.sum(g[SMALL_LOSS_ROW:SMALL_LOSS_ROW + 8, :], axis=1, keepdims=True)
        tot = jnp.sum(sq, axis=0, keepdims=True) * (0.5 / D_MODEL)
        loss_ref[...] = jnp.broadcast_to(tot, loss_ref.shape)

    vm = pl.BlockSpec(memory_space=pltpu.VMEM)
    return pl.pallas_call(
        body, name="small_allreduce_adamw",
        in_specs=[vm, vm, vm, vm],
        out_specs=[vm, vm, vm, vm, vm],
        out_shape=[jax.ShapeDtypeStruct(shape, F32)] * 4 + [jax.ShapeDtypeStruct((8, LANES), F32)],
        scratch_shapes=[pltpu.VMEM((N_DEV,) + shape, F32),
                        pltpu.SemaphoreType.DMA((N_DEV - 1,)), pltpu.SemaphoreType.DMA((N_DEV - 1,))],
    )(part, w, m, v)


def _perm_in_cols(w):
    z = lambda n: jnp.zeros(w.shape[:-1] + (n,), w.dtype)
    k1, k2 = w[..., 384:400], w[..., 400:416]
    small = jnp.concatenate([k1, k1, w[..., 1952:1960], z(24), k2, k2, z(32)], -1)
    return jnp.concatenate([w[..., 1960:2984], w[..., 2984:4008], w[..., 416:928], w[..., 928:1440],
                            w[..., 1440:1952], w[..., 0:256], w[..., 256:384], small], -1)


def _unperm_in_grad(gp):
    s = gp[..., P_SMALL:]
    return jnp.concatenate([gp[..., P_CQ:P_CQ + 256], gp[..., P_CKV:P_CKV + 128], s[..., 0:16] + s[..., 16:32],
                            s[..., 64:80] + s[..., 80:96], gp[..., P_FQ:P_FQ + 512], gp[..., P_FK:P_FK + 512],
                            gp[..., P_FV:P_FV + 512], s[..., S_FL:S_FL + 8], gp[..., P_GA:P_GA + 1024],
                            gp[..., P_GB:P_GB + 1024]], -1)


def _ext_uq_cols(w):
    r = w.shape[0]
    w3 = w.reshape(r, N_HEADS, NOPE + ROPE)
    z = jnp.zeros((r, 32), w.dtype)
    cols = []
    for hp in range(N_HEADS // 2):
        a, b = 2 * hp, 2 * hp + 1
        cols += [w3[:, a, :NOPE], w3[:, b, :NOPE], w3[:, a, NOPE:NOPE + 16], w3[:, b, NOPE:NOPE + 16], z,
                 w3[:, a, NOPE + 16:], w3[:, b, NOPE + 16:], z]
    return jnp.concatenate(cols, -1)


def _unext_uq_cols(wp):
    heads = []
    for h in range(N_HEADS):
        blk = wp[:, (h // 2) * 256:(h // 2 + 1) * 256]
        e = h % 2
        heads.append(jnp.concatenate([blk[:, 64 * e:64 * (e + 1)], blk[:, 128 + 16 * e:144 + 16 * e],
                                      blk[:, 192 + 16 * e:208 + 16 * e]], -1))
    return jnp.concatenate(heads, -1)


def _pack_shards(group, shards, dtype):
    rows = _pack_rows(group)
    flat = jnp.concatenate([shards[n].reshape(-1).astype(dtype) for n, _, _, _ in group])
    flat = jnp.pad(flat, (0, rows * LANES - flat.shape[0]))
    return flat.reshape(rows, LANES)


def _unpack_shards(group, packed):
    flat = packed.reshape(-1)
    out, off = {}, 0
    for name, r, c, _ in group:
        out[name] = flat[off:off + r * c].reshape(1, r, c)
        off += r * c
    return out


def _unpack_full(group, gathered):
    flat = gathered.reshape(N_DEV, -1)
    out, off = {}, 0
    for name, r, c, by_col in group:
        blk = flat[:, off:off + r * c].reshape(N_DEV, r, c)
        out[name] = blk.transpose(1, 0, 2).reshape(r, N_DEV * c) if by_col else blk.reshape(N_DEV * r, c)
        off += r * c
    return out


def _pack_full(group, grads, dtype):
    rows = _pack_rows(group)
    segs = []
    for name, r, c, by_col in group:
        g = grads[name]
        if by_col:
            g = g.reshape(r, N_DEV, c).transpose(1, 0, 2)
        segs.append(g.reshape(N_DEV, r * c).astype(dtype))
    flat = jnp.concatenate(segs, axis=1)
    flat = jnp.pad(flat, ((0, 0), (0, rows * LANES - flat.shape[1])))
    return flat.reshape(N_DEV, rows, LANES)


SMALL_LAYOUT = (("ln_pre_mix", 1024, 0), ("ln_post_mix", 1024, 8), ("ln_pre_mlp", 1024, 16),
                ("ln_post_mlp", 1024, 24), ("b_in", 4008, 32), ("q_a_norm", 256, 64), ("kv_a_norm", 128, 66))


def _pack_small(vals, extra=None):
    rows = []
    for name, n, _ in SMALL_LAYOUT:
        v = vals[name].reshape(-1).astype(F32)
        pad = -n % LANES
        rows.append(jnp.pad(v, (0, pad)).reshape(-1, LANES))
    rows.append(jnp.zeros((SMALL_LOSS_ROW - 67, LANES), F32))
    rows.append(jnp.zeros((8, LANES), F32) if extra is None else extra.reshape(8, LANES))
    return jnp.concatenate(rows, axis=0)


def _unpack_small(packed):
    out = {}
    for name, n, r0 in SMALL_LAYOUT:
        nr = -(-n // LANES)
        out[name] = packed[r0:r0 + nr].reshape(-1)[:n].reshape(1, n)
    return out


def _rms(xf, g):
    r = lax.rsqrt(jnp.mean(xf * xf, axis=-1, keepdims=True) + NORM_EPS)
    return (xf * r) * g


def _rms_bwd(xf, g, dy):
    r = lax.rsqrt(jnp.mean(xf * xf, axis=-1, keepdims=True) + NORM_EPS)
    xhat = xf * r
    dxhat = dy * g
    dx = r * (dxhat - xhat * jnp.mean(dxhat * xhat, axis=-1, keepdims=True))
    return dx, dy * xhat


def _sigmoid(t):
    return 1.0 / (1.0 + jnp.exp(-t))


def _log_sigmoid(t):
    return jnp.minimum(t, 0.0) - jnp.log(1.0 + jnp.exp(-jnp.abs(t)))


def _lane_sign():
    lane = lax.broadcasted_iota(jnp.int32, (1, LANES), 1)
    return jnp.where(lane < 64, -1.0, 1.0).astype(F32), lane


def _rope_lanes(lane):
    return (lane < 32) | ((lane >= 64) & (lane < 96))


def kernel(x, positions, ln_pre_mix, ln_post_mix, ln_pre_mlp, ln_post_mlp, w_in, b_in, q_a_norm, w_uq, kv_a_norm, w_uk, w_uv, w_o_mla, w_o_fox, w_out, w_ff1, w_ff2, loss_target, m_ln_pre_mix, m_ln_post_mix, m_ln_pre_mlp, m_ln_post_mlp, m_w_in, m_b_in, m_q_a_norm, m_w_uq, m_kv_a_norm, m_w_uk, m_w_uv, m_w_o_mla, m_w_o_fox, m_w_out, m_w_ff1, m_w_ff2, v_ln_pre_mix, v_ln_post_mix, v_ln_pre_mlp, v_ln_post_mlp, v_w_in, v_b_in, v_q_a_norm, v_w_uq, v_kv_a_norm, v_w_uk, v_w_uv, v_w_o_mla, v_w_o_fox, v_w_out, v_w_ff1, v_w_ff2):
    T = x.shape[1]
    x2 = x.reshape(T, D_MODEL)
    tgt = loss_target.reshape(T, D_MODEL)
    w_sh = dict(w_in=w_in, w_uq=w_uq, w_uk=w_uk, w_uv=w_uv, w_o_mla=w_o_mla, w_o_fox=w_o_fox, w_out=w_out,
                w_ff1=w_ff1, w_ff2=w_ff2)
    m_sh = dict(w_in=m_w_in, w_uq=m_w_uq, w_uk=m_w_uk, w_uv=m_w_uv, w_o_mla=m_w_o_mla, w_o_fox=m_w_o_fox,
                w_out=m_w_out, w_ff1=m_w_ff1, w_ff2=m_w_ff2)
    v_sh = dict(w_in=v_w_in, w_uq=v_w_uq, w_uk=v_w_uk, w_uv=v_w_uv, w_o_mla=v_w_o_mla, w_o_fox=v_w_o_fox,
                w_out=v_w_out, w_ff1=v_w_ff1, w_ff2=v_w_ff2)
    small_w = dict(ln_pre_mix=ln_pre_mix, ln_post_mix=ln_post_mix, ln_pre_mlp=ln_pre_mlp, ln_post_mlp=ln_post_mlp,
                   b_in=b_in, q_a_norm=q_a_norm, kv_a_norm=kv_a_norm)
    small_m = dict(ln_pre_mix=m_ln_pre_mix, ln_post_mix=m_ln_post_mix, ln_pre_mlp=m_ln_pre_mlp,
                   ln_post_mlp=m_ln_post_mlp, b_in=m_b_in, q_a_norm=m_q_a_norm, kv_a_norm=m_kv_a_norm)
    small_v = dict(ln_pre_mix=v_ln_pre_mix, ln_post_mix=v_ln_post_mix, ln_pre_mlp=v_ln_pre_mlp,
                   ln_post_mlp=v_ln_post_mlp, b_in=v_b_in, q_a_norm=v_q_a_norm, kv_a_norm=v_kv_a_norm)
    mla_scale = float((NOPE + ROPE) ** -0.5)
    fox_scale = float(FOX_D ** -0.5)

    W = _unpack_full(GROUP_A, _comm_call("gather", _pack_shards(GROUP_A, w_sh, BF16), "allgather_weights_a"))
    w_in_p = _perm_in_cols(W["w_in"])
    b_in_p = _perm_in_cols(b_in.astype(F32))
    w_uq_x = _ext_uq_cols(W["w_uq"])
    w_kv = jnp.concatenate([W["w_uk"], W["w_uv"]], axis=1)
    g1, g2, g3, g4 = ln_pre_mix, ln_post_mix, ln_pre_mlp, ln_post_mlp
    gq, gkv = q_a_norm, kv_a_norm
    tq = min(ATTN_TILE, T)
    nq = T // tq

    (h,) = _rowwise(lambda xv, g: _rms(xv, g), [x2], [g1], [(D_MODEL, BF16)], name="pre_mix_norm")
    q_cols = jnp.ones((1, D_IN_PAD), F32).at[:, P_FQ:P_FQ + 512].set(fox_scale * LOG2E)
    z = _mm(h, w_in_p, bias=b_in_p, colscale=q_cols, tm=2048, name="in_proj")
    zs = _mm(h, w_in_p[:, P_SMALL:], bias=b_in_p[:, P_SMALL:], outs=(F32,), name="in_proj_small")

    def lora_norm(cq, ckv, a, b):
        return _rms(cq.astype(F32), a), _rms(ckv.astype(F32), b)

    cqn, ckvn = _rowwise(lora_norm, [(z, 256, P_CQ // 256), (z, 128, P_CKV // 128)], [gq, gkv],
                         [(Q_LORA, BF16), (KV_LORA, BF16)], name="lora_norm")
    q_x = _mm(cqn, w_uq_x, outs=(F32,), name="q_up")
    kv = _mm(ckvn, w_kv, name="kv_up")

    half = ROPE // 2
    inv_freq = ROPE_THETA ** (-jnp.arange(half, dtype=F32) / half)
    inv128 = jnp.tile(inv_freq, LANES // half).reshape(1, LANES)
    pos_col = positions.reshape(T, 1).astype(F32)

    def rope_tables(p, f):
        ang = p * f
        return jnp.cos(ang), jnp.sin(ang)

    cos_t, sin_t = _rowwise(rope_tables, [pos_col], [inv128], [(LANES, F32), (LANES, F32)], name="rope_tables")

    def rope_fwd(qx, kn, s, cs, sn):
        sign, lane = _lane_sign()
        rot = lambda t: t * cs + pltpu.roll(t, 64, 1) * sn * sign
        k_rot = jnp.where(_rope_lanes(lane), rot(s), 0.0).astype(BF16)
        qs, ks = [], []
        for hp in range(N_HEADS // 2):
            qs += [qx[:, 256 * hp:256 * hp + 128], rot(qx[:, 256 * hp + 128:256 * hp + 256])]
            ks += [kn[:, 128 * hp:128 * hp + 128], k_rot]
        return jnp.concatenate(qs, axis=1) * (mla_scale * LOG2E), jnp.concatenate(ks, axis=1)

    q_e, k_e = _rowwise(rope_fwd, [q_x, (kv, 512, 0), zs, cos_t, sin_t], [], [(1024, BF16), (1024, BF16)],
                        name="rope_fwd")
    f_cum = _seq_cumsum(zs, _log_sigmoid, reverse=False, name="forget_cumsum")
    f_row = f_cum[:, S_FL:S_FL + N_HEADS].T.reshape(N_HEADS, nq, 1, tq)

    FQ, FK, FV = P_FQ // LANES, P_FK // LANES, P_FV // LANES
    o_mla, lse_mla, gathered_b = _attn_fwd(q_e, k_e, kv, 0, 0, 4, 256, name="mla_attn_fwd", tq=tq,
                                           gather=_pack_shards(GROUP_B, w_sh, BF16))
    W.update(_unpack_full(GROUP_B, gathered_b))
    o_fox, lse_fox = _attn_fwd(z, z, z, FQ, FK, FV, LANES, f_row, name="fox_attn_fwd", tq=tq)
    y_mla = _mm(o_mla, W["w_o_mla"], outs=(F32,), name="o_proj_mla")
    y_fox = _mm(o_fox, W["w_o_fox"], outs=(F32,), name="o_proj_fox")

    def gate_merge(ga, gb, ya, yb):
        return _sigmoid(ga.astype(F32)) * ya + _sigmoid(gb.astype(F32)) * yb

    (merged,) = _rowwise(gate_merge, [(z, 1024, 0), (z, 1024, 1), y_mla, y_fox], [], [(D_MODEL, BF16)],
                         name="gate_merge")
    mix = _mm(merged, W["w_out"], outs=(F32,), name="out_proj")

    def post_mix(xv, mv, a, b):
        x1v = xv + _rms(mv, a)
        return x1v, _rms(x1v, b)

    x1, h2 = _rowwise(post_mix, [x2, mix], [g2, g3], [(D_MODEL, F32), (D_MODEL, BF16)], name="post_mix_norm")

    def relu2(acc):
        r = jnp.maximum(acc, 0.0)
        return r * r, acc

    act, u = _mm(h2, W["w_ff1"], epi=relu2, outs=(BF16, BF16), tm=2048, name="ff1")
    mlp = _mm(act, W["w_ff2"], outs=(F32,), tm=1024, name="ff2")

    def loss_bwd(x1v, mv, tv, g):
        y = x1v + _rms(mv, g)
        d = y - tv
        dy = d * (1.0 / D_MODEL)
        dm, dg = _rms_bwd(mv, g, dy)
        return dy, dm, dg, d * d

    dy, dm, dg4, loss_cols = _rowwise(loss_bwd, [x1, mlp, tgt], [g4], [(D_MODEL, F32), (D_MODEL, BF16)],
                                      [D_MODEL, D_MODEL], name="loss_bwd")

    def relu2_bwd(acc, uv):
        return (acc * (2.0 * jnp.maximum(uv.astype(F32), 0.0)),)

    du = _mm(dm, W["w_ff2"].T, extras=(u,), epi=relu2_bwd, tm=2048, name="ff2_bwd")
    dw_ff2 = _mm(act, dm, ta=True, outs=(F32,), name="ff2_wgrad")
    dh2 = _mm(du, W["w_ff1"].T, outs=(F32,), tm=1024, name="ff1_bwd")
    dw_ff1 = _mm(h2, du, ta=True, outs=(F32,), name="ff1_wgrad")

    def post_mix_bwd(x1v, dh2v, dyv, mv, a, b):
        d3, dg3v = _rms_bwd(x1v, b, dh2v)
        dx1v = dyv + d3
        dmixv, dg2v = _rms_bwd(mv, a, dx1v)
        return dx1v, dmixv, dg3v, dg2v

    dx1, dmix, dg3, dg2 = _rowwise(post_mix_bwd, [x1, dh2, dy, mix], [g2, g3], [(D_MODEL, F32), (D_MODEL, BF16)],
                                   [D_MODEL, D_MODEL], name="post_mix_bwd")
    dmerged = _mm(dmix, W["w_out"].T, outs=(F32,), name="out_proj_bwd")
    dw_out = _mm(merged, dmix, ta=True, outs=(F32,), name="out_proj_wgrad")

    def gate_bwd(dmg, ga, gb, ya, yb):
        sa = _sigmoid(ga.astype(F32))
        sb = _sigmoid(gb.astype(F32))
        return dmg * sa, dmg * sb, dmg * ya * sa * (1.0 - sa), dmg * yb * sb * (1.0 - sb)

    dy_mla, dy_fox, dga, dgb = _rowwise(gate_bwd, [dmerged, (z, 1024, 0), (z, 1024, 1), y_mla, y_fox], [],
                                        [(D_MODEL, BF16)] * 4, name="gate_bwd")
    do_mla = _mm(dy_mla, W["w_o_mla"].T, name="o_proj_mla_bwd")
    do_fox = _mm(dy_fox, W["w_o_fox"].T, name="o_proj_fox_bwd")
    dw_o_mla = _mm(o_mla, dy_mla, ta=True, outs=(F32,), name="o_proj_mla_wgrad")
    dw_o_fox = _mm(o_fox, dy_fox, ta=True, outs=(F32,), name="o_proj_fox_wgrad")

    def head_dots(ov, dov):
        r = lax.broadcasted_iota(jnp.int32, (N_HEADS * HEAD_V, LANES), 0)
        cc = lax.broadcasted_iota(jnp.int32, (N_HEADS * HEAD_V, LANES), 1)
        sel = jnp.where((r // HEAD_V) == cc, 1.0, 0.0).astype(BF16)
        return _dot3(ov.astype(F32) * dov.astype(F32), sel, left=False)

    (dl_mla,) = _rowwise(head_dots, [o_mla, do_mla], [], [(LANES, F32)], name="mla_attn_delta")
    (dl_fox,) = _rowwise(head_dots, [o_fox, do_fox], [], [(LANES, F32)], name="fox_attn_delta")
    heads_row4 = lambda t: t[:, :N_HEADS].T.reshape(N_HEADS, nq, 1, tq)
    grads_b = dict(w_o_mla=dw_o_mla, w_o_fox=dw_o_fox, w_out=dw_out, w_ff1=dw_ff1, w_ff2=dw_ff2)
    dq_x, dk_x, dv_mla, pieces_b = _attn_bwd(q_e, k_e, kv, do_mla, lse_mla, heads_row4(dl_mla), 0, 0, 4, 256,
                                             mla_scale, name="mla_attn_bwd", tq=tq, dk_dtype=F32,
                                             exchange=_pack_full(GROUP_B, grads_b, BF16))
    dq_fox, dk_fox, dv_fox, dfk, dfq = _attn_bwd(z, z, z, do_fox, lse_fox, heads_row4(dl_fox), FQ, FK, FV, LANES,
                                                 fox_scale, f_cum, name="fox_attn_bwd", tq=tq, dk_dtype=BF16)
    d_f = (dfk + dfq).reshape(N_HEADS, T).T
    d_f128 = jnp.pad(d_f, ((0, 0), (S_FL, LANES - S_FL - N_HEADS)))
    df_rev = _seq_cumsum(d_f128, lambda t: t, reverse=True, name="forget_cumsum_bwd")

    def rope_bwd(dqx, dkx, dvm, dfr, s, cs, sn):
        sign, lane = _lane_sign()
        rot_t = lambda t: t * cs - pltpu.roll(t, 64, 1) * sn * sign
        dqs, dks = [], []
        dk_rot = None
        for hp in range(N_HEADS // 2):
            dqs += [dqx[:, 256 * hp:256 * hp + 128], rot_t(dqx[:, 256 * hp + 128:256 * hp + 256])]
            dks.append(dkx[:, 256 * hp:256 * hp + 128])
            blk = dkx[:, 256 * hp + 128:256 * hp + 256]
            dk_rot = blk if dk_rot is None else dk_rot + blk
        dfl = dfr * _sigmoid(-s)
        small = jnp.where(_rope_lanes(lane), rot_t(dk_rot),
                          jnp.where((lane >= S_FL) & (lane < S_FL + N_HEADS), dfl, 0.0))
        return jnp.concatenate(dqs, axis=1), jnp.concatenate(dks + [dvm.astype(F32)], axis=1), small

    dq_b, dkv, d_small = _rowwise(rope_bwd, [dq_x, dk_x, dv_mla, df_rev, zs, cos_t, sin_t], [],
                                  [(1024, BF16), (1024, BF16), (LANES, BF16)], name="rope_bwd")
    dcqn = _mm(dq_b, w_uq_x.T, outs=(F32,), name="q_up_bwd")
    dw_uq_x = _mm(cqn, dq_b, ta=True, outs=(F32,), name="q_up_wgrad")
    dckvn = _mm(dkv, w_kv.T, outs=(F32,), name="kv_up_bwd")
    dw_kv = _mm(ckvn, dkv, ta=True, outs=(F32,), name="kv_up_wgrad")

    def lora_norm_bwd(cq, ckv, dq_, dkv_, a, b):
        d1, dga_ = _rms_bwd(cq.astype(F32), a, dq_)
        d2, dgb_ = _rms_bwd(ckv.astype(F32), b, dkv_)
        return d1, d2, dga_, dgb_

    dcq, dckv, dgq, dgkv = _rowwise(lora_norm_bwd, [(z, 256, P_CQ // 256), (z, 128, P_CKV // 128), dcqn, dckvn],
                                    [gq, gkv], [(Q_LORA, BF16), (KV_LORA, BF16)], [Q_LORA, KV_LORA],
                                    name="lora_norm_bwd")
    dz = jnp.concatenate([dga, dgb, dq_fox.astype(BF16), dk_fox, dv_fox, dcq, dckv, d_small], -1)
    (db_in_p,) = _rowwise(lambda t: (t,), [dz], [], [], [D_IN_PAD], name="in_bias_grad")
    dh = _mm(dz, w_in_p.T, outs=(F32,), tm=1024, name="in_proj_bwd")
    dw_in_p = _mm(h, dz, ta=True, outs=(F32,), name="in_proj_wgrad")

    def pre_mix_bwd(xv, dhv, dx1v, g):
        d, dg = _rms_bwd(xv, g, dhv)
        return dx1v + d, dg

    grad_x, dg1 = _rowwise(pre_mix_bwd, [x2, dh, dx1], [g1], [(D_MODEL, F32)], [D_MODEL], name="pre_mix_bwd")

    grads_a = dict(w_in=_unperm_in_grad(dw_in_p), w_uq=_unext_uq_cols(dw_uq_x), w_uk=dw_kv[:, :512],
                   w_uv=dw_kv[:, 512:])
    pieces_a = _comm_call("exchange", _pack_full(GROUP_A, grads_a, BF16), "exchange_grad_pieces_a")
    grad_sh, delta_sh, newm_sh, newv_sh = {}, {}, {}, {}
    for group, pieces, tag in ((GROUP_A, pieces_a, "a"), (GROUP_B, pieces_b, "b")):
        packed = _sum_adamw(pieces, _pack_shards(group, w_sh, F32), _pack_shards(group, m_sh, F32),
                            _pack_shards(group, v_sh, F32), name="sum_pieces_adamw_" + tag)
        for dst, arr in zip((grad_sh, delta_sh, newm_sh, newv_sh), packed):
            dst.update(_unpack_shards(group, arr))

    small_part = _pack_small(dict(ln_pre_mix=dg1, ln_post_mix=dg2, ln_pre_mlp=dg3, ln_post_mlp=dg4,
                                  b_in=_unperm_in_grad(db_in_p), q_a_norm=dgq, kv_a_norm=dgkv), extra=loss_cols)
    sg, sd, sm, sv, loss_blk = _small_allreduce_adamw(small_part, _pack_small(small_w), _pack_small(small_m),
                                                      _pack_small(small_v))
    grad_sm, delta_sm, newm_sm, newv_sm = (_unpack_small(t) for t in (sg, sd, sm, sv))
    loss = loss_blk[0, 0]

    order = ["ln_pre_mix", "ln_post_mix", "ln_pre_mlp", "ln_post_mlp", "w_in", "b_in", "q_a_norm", "w_uq",
             "kv_a_norm", "w_uk", "w_uv", "w_o_mla", "w_o_fox", "w_out", "w_ff1", "w_ff2"]

    def pick(sm_d, sh_d):
        return [sm_d[n] if n in sm_d else sh_d[n] for n in order]

    return (loss, grad_x.reshape(1, T, D_MODEL), *pick(grad_sm, grad_sh), *pick(delta_sm, delta_sh),
            *pick(newm_sm, newm_sh), *pick(newv_sm, newv_sh))
```

```python
import numpy as np
import jax
import jax.numpy as jnp
from jax import lax
from jax.experimental import pallas as pl
from jax.experimental.pallas import tpu as pltpu

F32 = jnp.float32
BF16 = jnp.bfloat16
MESH = pl.DeviceIdType.MESH

D_MODEL = 1024
N_HEADS = 8
Q_LORA = 256
KV_LORA = 128
NOPE = 64
ROPE = 32
HEAD_V = 64
FOX_D = 64
D_FF = 4096
D_IN = 4008
D_IN_PAD = 4096
ROPE_THETA = 10000.0
NORM_EPS = 1e-6
N_DEV = 8

ADAM_LR = 0.001
ADAM_B1 = 0.9
ADAM_B2 = 0.999
ADAM_EPS = 1e-08
ADAM_WD = 0.01
ADAM_STEP = 10

LANES = 128
ROW_TILE = 512
ATTN_TILE = 512
VMEM_LIMIT = 48 * 1024 * 1024
ATTN_BWD_VMEM_LIMIT = 58 * 1024 * 1024

P_GA, P_GB, P_FQ, P_FK, P_FV, P_CQ, P_CKV, P_SMALL = 0, 1024, 2048, 2560, 3072, 3584, 3840, 3968
S_FL = 32

SHARDED = (
    ("w_in", 1024, 501, True), ("w_uq", 256, 96, True), ("w_uk", 128, 64, True), ("w_uv", 128, 64, True),
    ("w_o_mla", 512, 128, True), ("w_o_fox", 512, 128, True), ("w_out", 128, 1024, False),
    ("w_ff1", 1024, 512, True), ("w_ff2", 512, 1024, False),
)
GROUP_A = SHARDED[:4]
GROUP_B = SHARDED[4:]
SMALL_ROWS = 80
SMALL_LOSS_ROW = 72


def _pack_rows(group):
    return -(-sum(r * c for _, r, c, _ in group) // (LANES * 64)) * 64


def _cparams(sem=None):
    return pltpu.CompilerParams(dimension_semantics=sem, vmem_limit_bytes=VMEM_LIMIT)


def _mm(a, b, *, name, ta=False, bias=None, colscale=None, extras=(), epi=None, outs=(BF16,), tm=ROW_TILE, tn=1024,
        tk=1024):
    if ta:
        K, M = a.shape
        tm = min(1024, M)
    else:
        M, K = a.shape
        tm = min(tm, M)
    tk = min(tk, K)
    N = b.shape[1]
    tn = min(tn, N)
    nk = K // tk
    n_ex = len(extras)
    has_bias = bias is not None
    has_scale = colscale is not None

    def body(*refs):
        a_ref, b_ref = refs[0], refs[1]
        pos = 2
        bias_ref = scale_ref = None
        if has_bias:
            bias_ref = refs[pos]
            pos += 1
        if has_scale:
            scale_ref = refs[pos]
            pos += 1
        ex_refs = refs[pos:pos + n_ex]
        pos += n_ex
        o_refs = refs[pos:pos + len(outs)]
        pos += len(outs)
        av = a_ref[...].astype(BF16)
        bv = b_ref[...].astype(BF16)
        if ta:
            part = lax.dot_general(av, bv, (((0,), (0,)), ((), ())), preferred_element_type=F32)
        else:
            part = jnp.dot(av, bv, preferred_element_type=F32)

        def finish(acc):
            if has_bias:
                acc = acc + bias_ref[...]
            if has_scale:
                acc = acc * scale_ref[...]
            res = (acc,) if epi is None else epi(acc, *[r[...] for r in ex_refs])
            for o_ref, val in zip(o_refs, res):
                o_ref[...] = val.astype(o_ref.dtype)

        if nk == 1:
            finish(part)
        else:
            acc_ref = refs[pos]
            k = pl.program_id(2)

            @pl.when(k == 0)
            def _():
                acc_ref[...] = part

            @pl.when(k > 0)
            def _():
                acc_ref[...] += part

            @pl.when(k == nk - 1)
            def _():
                finish(acc_ref[...])

    if ta:
        a_spec = pl.BlockSpec((tk, tm), lambda i, j, k: (k, i))
    else:
        a_spec = pl.BlockSpec((tm, tk), lambda i, j, k: (i, k))
    in_specs = [a_spec, pl.BlockSpec((tk, tn), lambda i, j, k: (k, j))]
    args = [a, b]
    for row in (bias, colscale):
        if row is not None:
            in_specs.append(pl.BlockSpec((1, tn), lambda i, j, k: (0, j)))
            args.append(row)
    for e in extras:
        in_specs.append(pl.BlockSpec((tm, tn), lambda i, j, k: (i, j)))
        args.append(e)
    res = pl.pallas_call(
        body, name=name, grid=(M // tm, N // tn, nk),
        in_specs=in_specs,
        out_specs=[pl.BlockSpec((tm, tn), lambda i, j, k: (i, j)) for _ in outs],
        out_shape=[jax.ShapeDtypeStruct((M, N), dt) for dt in outs],
        scratch_shapes=[pltpu.VMEM((tm, tn), F32)] if nk > 1 else [],
        compiler_params=_cparams(("parallel", "parallel", "arbitrary")),
    )(*args)
    return res[0] if len(outs) == 1 else res


def _rowwise(fn, rows, bcasts, outs, accs=(), *, name, tm=ROW_TILE):
    arrs, specs = [], []
    for r in rows:
        if isinstance(r, tuple):
            arr, w, cb = r
            specs.append(pl.BlockSpec((tm, w), lambda i, cb=cb: (i, cb)))
        else:
            arr = r
            specs.append(pl.BlockSpec((tm, arr.shape[1]), lambda i: (i, 0)))
        arrs.append(arr)
    T = arrs[0].shape[0]
    tm = min(tm, T)
    specs = [pl.BlockSpec((tm,) + tuple(s.block_shape[1:]), s.index_map) for s in specs]
    for b in bcasts:
        arrs.append(b)
        specs.append(pl.BlockSpec(b.shape, lambda i: (0, 0)))
    n_in, n_out = len(arrs), len(outs)

    def body(*refs):
        vals = [r[...] for r in refs[:n_in]]
        res = fn(*vals)
        if not isinstance(res, (tuple, list)):
            res = (res,)
        for o_ref, val in zip(refs[n_in:n_in + n_out], res[:n_out]):
            o_ref[...] = val.astype(o_ref.dtype)
        i = pl.program_id(0)
        for a_ref, val in zip(refs[n_in + n_out:], res[n_out:]):
            col = jnp.sum(val.astype(F32), axis=0, keepdims=True)

            @pl.when(i == 0)
            def _(a_ref=a_ref, col=col):
                a_ref[...] = col

            @pl.when(i > 0)
            def _(a_ref=a_ref, col=col):
                a_ref[...] += col

    res = pl.pallas_call(
        body, name=name, grid=(T // tm,),
        in_specs=specs,
        out_specs=[pl.BlockSpec((tm, c), lambda i: (i, 0)) for c, _ in outs]
        + [pl.BlockSpec((1, c), lambda i: (0, 0)) for c in accs],
        out_shape=[jax.ShapeDtypeStruct((T, c), dt) for c, dt in outs]
        + [jax.ShapeDtypeStruct((1, c), F32) for c in accs],
        compiler_params=_cparams(("arbitrary",)),
    )(*arrs)
    return res


def _dot3(x, sel, left):
    hi = x.astype(BF16)
    r1 = x - hi.astype(F32)
    mid = r1.astype(BF16)
    lo = (r1 - mid.astype(F32)).astype(BF16)
    if left:
        d = lambda t: jnp.dot(sel, t, preferred_element_type=F32)
    else:
        d = lambda t: jnp.dot(t, sel, preferred_element_type=F32)
    return d(hi) + d(mid) + d(lo)


def _seq_cumsum(x, fn, *, reverse, name, tm=256):
    T, C = x.shape
    tm = min(tm, T)
    n = T // tm

    def body(x_ref, o_ref, carry_ref):
        i = pl.program_id(0)

        @pl.when(i == 0)
        def _():
            carry_ref[...] = jnp.zeros_like(carry_ref)

        v = fn(x_ref[...])
        r = lax.broadcasted_iota(jnp.int32, (tm, tm), 0)
        c = lax.broadcasted_iota(jnp.int32, (tm, tm), 1)
        tri = jnp.where((r <= c) if reverse else (r >= c), 1.0, 0.0).astype(BF16)
        carry = carry_ref[0:1, :]
        o_ref[...] = _dot3(v, tri, left=True) + carry
        carry_ref[0:1, :] = carry + jnp.sum(v, axis=0, keepdims=True)

    idx = (lambda i: (n - 1 - i, 0)) if reverse else (lambda i: (i, 0))
    return pl.pallas_call(
        body, name=name, grid=(n,),
        in_specs=[pl.BlockSpec((tm, C), idx)],
        out_specs=pl.BlockSpec((tm, C), idx),
        out_shape=jax.ShapeDtypeStruct((T, C), F32),
        scratch_shapes=[pltpu.VMEM((8, C), F32)],
        compiler_params=_cparams(("arbitrary",)),
    )(x)


_NT = (((1,), (1,)), ((), ()))
LOG2E = 1.4426950408889634


def _head_mask(width, e):
    lane = lax.broadcasted_iota(jnp.int32, (1, width), 1)
    return (lane >= 64 * e) & (lane < 64 * (e + 1))


def _attn_fwd(qt4, k_aug, vt4, *, name, tq, gather=None):
    nq = qt4.shape[1]
    T = nq * tq
    comm = gather is not None

    def body(*refs):
        refs = list(refs)
        q_ref, k_ref, v_ref = refs[:3]
        pos = 3
        if comm:
            c_src = refs[pos]
            pos += 1
        o_ref, lse_ref = refs[pos:pos + 2]
        pos += 2
        if comm:
            c_dst = refs[pos]
            pos += 1
        m_s, l_s, acc_s = refs[pos:pos + 3]
        c_sems = refs[pos + 3:]
        i = pl.program_id(1)
        if comm:
            @pl.when((pl.program_id(0) == 0) & (i == 0))
            def _():
                _comm_start("gather", c_src, c_dst, *c_sems)

        m_s[...] = jnp.full(m_s.shape, -jnp.inf, F32)
        l_s[...] = jnp.zeros_like(l_s)
        acc_s[...] = jnp.zeros_like(acc_s)
        qt = [q_ref[e, 0] for e in range(2)]

        def step(j, masked):
            kj = k_ref[pl.ds(pl.multiple_of(j * tq, tq), tq), :]
            vtj = v_ref[0, j]
            for e in range(2):
                st = jnp.dot(kj[:, e * LANES:(e + 1) * LANES], qt[e], preferred_element_type=F32)
                if masked:
                    r = lax.broadcasted_iota(jnp.int32, (tq, tq), 0)
                    cc = lax.broadcasted_iota(jnp.int32, (tq, tq), 1)
                    st = jnp.where(cc >= r, st, -jnp.inf)
                m_prev = m_s[e]
                m_new = jnp.maximum(m_prev, jnp.max(st, axis=0, keepdims=True))
                alpha = jnp.exp2(m_prev - m_new)
                pt = jnp.exp2(st - m_new)
                l_s[e] = alpha * l_s[e] + jnp.sum(pt, axis=0, keepdims=True)
                acc_s[e] = alpha * acc_s[e] + jnp.dot(vtj[e * HEAD_V:(e + 1) * HEAD_V, :], pt.astype(BF16),
                                                      preferred_element_type=F32)
                m_s[e] = m_new

        def loop_body(j, carry):
            step(j, False)
            return carry

        lax.fori_loop(0, i, loop_body, 0)
        step(i, True)
        ot = jnp.concatenate([acc_s[e] / l_s[e] for e in range(2)], axis=0)
        o_ref[...] = ot.T.astype(o_ref.dtype)
        for e in range(2):
            lse_ref[e, 0] = m_s[e] + jnp.log(l_s[e]) * LOG2E
        if comm:
            @pl.when((pl.program_id(0) == N_HEADS // 2 - 1) & (i == nq - 1))
            def _():
                _comm_wait("gather", c_src, c_dst, *c_sems)

    in_specs = [
        pl.BlockSpec((2, 1, LANES, tq), lambda hp, i: (hp, i, 0, 0)),
        pl.BlockSpec((T, 2 * LANES), lambda hp, i: (0, hp)),
        pl.BlockSpec((1, nq, LANES, tq), lambda hp, i: (hp, 0, 0, 0)),
    ]
    args = [qt4, k_aug, vt4]
    out_specs = [pl.BlockSpec((tq, LANES), lambda hp, i: (i, hp)),
                 pl.BlockSpec((2, 1, 1, tq), lambda hp, i: (hp, i, 0, 0))]
    out_shape = [jax.ShapeDtypeStruct((T, N_HEADS * HEAD_V), BF16), jax.ShapeDtypeStruct((N_HEADS, nq, 1, tq), F32)]
    scratch = [pltpu.VMEM((2, 1, tq), F32), pltpu.VMEM((2, 1, tq), F32), pltpu.VMEM((2, HEAD_V, tq), F32)]
    if comm:
        in_specs.append(pl.BlockSpec(memory_space=pl.ANY))
        args.append(gather)
        out_specs.append(pl.BlockSpec(memory_space=pl.ANY))
        out_shape.append(jax.ShapeDtypeStruct((N_DEV,) + gather.shape, gather.dtype))
        scratch += _comm_sems()
    return pl.pallas_call(
        body, name=name, grid=(N_HEADS // 2, nq),
        in_specs=in_specs, out_specs=out_specs, out_shape=out_shape, scratch_shapes=scratch,
        compiler_params=_cparams(("arbitrary", "arbitrary")),
    )(*args)


def _attn_bwd(qt4, k_aug, v, vcb, dot4, lse_row, dl_row, scale, own_rows, *, name, tq, exchange=None):
    nq = qt4.shape[1]
    T = nq * tq
    comm = exchange is not None

    def body(*refs):
        refs = list(refs)
        k_ref, v_ref, q_ref, do_ref, lse_ref, dl_ref = refs[:6]
        pos = 6
        if comm:
            c_src = refs[pos]
            pos += 1
        dq_ref, dk_ref, dv_ref = refs[pos:pos + 3]
        pos += 3
        if comm:
            c_dst = refs[pos]
            pos += 1
        dk_s, dv_s = refs[pos:pos + 2]
        c_sems = refs[pos + 2:]
        j = pl.program_id(1)
        if comm:
            @pl.when((pl.program_id(0) == 0) & (j == 0))
            def _():
                _comm_start("exchange", c_src, c_dst, *c_sems)

        @pl.when(j == 0)
        def _():
            dq_ref[...] = jnp.zeros_like(dq_ref)

        kj = k_ref[...]
        vj = v_ref[...]
        ka = [kj[:, e * LANES:(e + 1) * LANES] for e in range(2)]
        row = lax.broadcasted_iota(jnp.int32, (LANES, 1), 0)
        own = row < own_rows
        kat = [(ka[e].astype(F32).T * jnp.where(own, scale, 1.0)).astype(BF16) for e in range(2)]
        vm = [jnp.where(_head_mask(LANES, e), vj, jnp.zeros_like(vj)) for e in range(2)]
        dk_s[...] = jnp.zeros_like(dk_s)
        dv_s[...] = jnp.zeros_like(dv_s)

        def step(i, masked):
            dot_i = do_ref[0, i]
            for e in range(2):
                qt_i = q_ref[e, i]
                st = jnp.dot(ka[e], qt_i, preferred_element_type=F32)
                if masked:
                    r = lax.broadcasted_iota(jnp.int32, (tq, tq), 0)
                    cc = lax.broadcasted_iota(jnp.int32, (tq, tq), 1)
                    st = jnp.where(cc >= r, st, -jnp.inf)
                pt = jnp.exp2(st - lse_ref[e, i])
                dv_s[e] += lax.dot_general(dot_i[e * HEAD_V:(e + 1) * HEAD_V, :], pt.astype(BF16), _NT,
                                           preferred_element_type=F32)
                dpt = jnp.dot(vm[e], dot_i, preferred_element_type=F32)
                dsb = (pt * (dpt - dl_ref[e, i])).astype(BF16)
                dk_s[e] += lax.dot_general(qt_i, dsb, _NT, preferred_element_type=F32)
                dq_ref[e, i] += jnp.dot(kat[e], dsb, preferred_element_type=F32)

        def loop_body(i, carry):
            step(i, False)
            return carry

        step(j, True)
        lax.fori_loop(j + 1, nq, loop_body, 0)
        for e in range(2):
            dk_ref[e, 0] = dk_s[e] * jnp.where(own, 1.0 / LOG2E, 1.0)
        dv_ref[...] = jnp.concatenate([dv_s[0], dv_s[1]], axis=0).T.astype(dv_ref.dtype)
        if comm:
            @pl.when((pl.program_id(0) == N_HEADS // 2 - 1) & (j == nq - 1))
            def _():
                _comm_wait("exchange", c_src, c_dst, *c_sems)

    row4 = pl.BlockSpec((2, nq, 1, tq), lambda hp, j: (hp, 0, 0, 0))
    in_specs = [
        pl.BlockSpec((tq, 2 * LANES), lambda hp, j: (j, hp)),
        pl.BlockSpec((tq, LANES), lambda hp, j: (j, vcb + hp)),
        pl.BlockSpec((2, nq, LANES, tq), lambda hp, j: (hp, 0, 0, 0)),
        pl.BlockSpec((1, nq, LANES, tq), lambda hp, j: (hp, 0, 0, 0)),
        row4, row4,
    ]
    args = [k_aug, v, qt4, dot4, lse_row, dl_row]
    out_specs = [pl.BlockSpec((2, nq, LANES, tq), lambda hp, j: (hp, 0, 0, 0)),
                 pl.BlockSpec((2, 1, LANES, tq), lambda hp, j: (hp, j, 0, 0)),
                 pl.BlockSpec((tq, LANES), lambda hp, j: (j, hp))]
    out_shape = [jax.ShapeDtypeStruct((N_HEADS, nq, LANES, tq), F32), jax.ShapeDtypeStruct((N_HEADS, nq, LANES, tq), F32),
                 jax.ShapeDtypeStruct((T, N_HEADS * HEAD_V), BF16)]
    scratch = [pltpu.VMEM((2, LANES, tq), F32), pltpu.VMEM((2, HEAD_V, tq), F32)]
    if comm:
        in_specs.append(pl.BlockSpec(memory_space=pl.ANY))
        args.append(exchange)
        out_specs.append(pl.BlockSpec(memory_space=pl.ANY))
        out_shape.append(jax.ShapeDtypeStruct(exchange.shape, exchange.dtype))
        scratch += _comm_sems()
    return pl.pallas_call(
        body, name=name, grid=(N_HEADS // 2, nq),
        in_specs=in_specs, out_specs=out_specs, out_shape=out_shape, scratch_shapes=scratch,
        compiler_params=pltpu.CompilerParams(dimension_semantics=("arbitrary", "arbitrary"),
                                             vmem_limit_bytes=ATTN_BWD_VMEM_LIMIT),
    )(*args)


def _peers():
    x, y, c = lax.axis_index("x"), lax.axis_index("y"), lax.axis_index("c")
    me = 4 * x + 2 * y + c
    out = []
    for k in range(1, N_DEV):
        px = (1 - x) if (k & 4) else x
        py = (1 - y) if (k & 2) else y
        pc = (1 - c) if (k & 1) else c
        out.append(((px, py, pc), 4 * px + 2 * py + pc))
    return me, out


def _comm_sems():
    return [pltpu.SemaphoreType.DMA((N_DEV - 1,)), pltpu.SemaphoreType.DMA((N_DEV - 1,)), pltpu.SemaphoreType.DMA]


def _comm_copies(kind, src_ref, dst_ref, send_sems, recv_sems, local_sem):
    me, peers = _peers()
    local = pltpu.make_async_copy(src_ref if kind == "gather" else src_ref.at[me], dst_ref.at[me], local_sem)
    sends, recvs = [], []
    for k, (dev, lin) in enumerate(peers):
        src = src_ref if kind == "gather" else src_ref.at[lin]
        sends.append(pltpu.make_async_remote_copy(src_ref=src, dst_ref=dst_ref.at[me], send_sem=send_sems.at[k],
                                                  recv_sem=recv_sems.at[k], device_id=dev, device_id_type=MESH))
        recvs.append(pltpu.make_async_remote_copy(src_ref=src, dst_ref=dst_ref.at[lin], send_sem=send_sems.at[k],
                                                  recv_sem=recv_sems.at[k], device_id=dev, device_id_type=MESH))
    return local, sends, recvs


def _comm_start(kind, src_ref, dst_ref, send_sems, recv_sems, local_sem):
    local, sends, _ = _comm_copies(kind, src_ref, dst_ref, send_sems, recv_sems, local_sem)
    local.start()
    for cp in sends:
        cp.start()


def _comm_wait(kind, src_ref, dst_ref, send_sems, recv_sems, local_sem):
    local, sends, recvs = _comm_copies(kind, src_ref, dst_ref, send_sems, recv_sems, local_sem)
    for cp in recvs:
        cp.wait_recv()
    for cp in sends:
        cp.wait_send()
    local.wait()


def _comm_call(kind, src, name):
    def body(x_ref, o_ref, send_sems, recv_sems, local_sem):
        _comm_start(kind, x_ref, o_ref, send_sems, recv_sems, local_sem)
        _comm_wait(kind, x_ref, o_ref, send_sems, recv_sems, local_sem)

    shape = (N_DEV,) + src.shape if kind == "gather" else src.shape
    return pl.pallas_call(
        body, name=name,
        in_specs=[pl.BlockSpec(memory_space=pl.ANY)],
        out_specs=pl.BlockSpec(memory_space=pl.ANY),
        out_shape=jax.ShapeDtypeStruct(shape, src.dtype),
        scratch_shapes=_comm_sems(),
    )(src)


def _adamw(w, g, m, v):
    m2 = ADAM_B1 * m + (1.0 - ADAM_B1) * g
    v2 = ADAM_B2 * v + (1.0 - ADAM_B2) * (g * g)
    m_hat = m2 / (1.0 - ADAM_B1 ** ADAM_STEP)
    v_hat = v2 / (1.0 - ADAM_B2 ** ADAM_STEP)
    delta = -ADAM_LR * (m_hat / (jnp.sqrt(v_hat) + ADAM_EPS) + ADAM_WD * w)
    return delta, m2, v2


def _sum_adamw(pieces, w, m, v, *, name):
    R = w.shape[0]
    tr = R // 4 if R % 64 == 0 else R

    def body(p_ref, w_ref, m_ref, v_ref, g_ref, d_ref, m2_ref, v2_ref):
        g = p_ref[0].astype(F32)
        for s in range(1, N_DEV):
            g = g + p_ref[s].astype(F32)
        delta, m2, v2 = _adamw(w_ref[...], g, m_ref[...], v_ref[...])
        g_ref[...] = g
        d_ref[...] = delta
        m2_ref[...] = m2
        v2_ref[...] = v2

    row = pl.BlockSpec((tr, LANES), lambda i: (i, 0))
    return pl.pallas_call(
        body, name=name, grid=(R // tr,),
        in_specs=[pl.BlockSpec((N_DEV, tr, LANES), lambda i: (0, i, 0)), row, row, row],
        out_specs=[row, row, row, row],
        out_shape=[jax.ShapeDtypeStruct((R, LANES), F32)] * 4,
        compiler_params=_cparams(("parallel",)),
    )(pieces, w, m, v)


def _small_allreduce_adamw(part, w, m, v):
    shape = part.shape

    def body(p_ref, w_ref, m_ref, v_ref, g_ref, d_ref, m2_ref, v2_ref, loss_ref, gath, send_sems, recv_sems):
        me, peers = _peers()
        gath[me] = p_ref[...]
        sends = []
        for k, (dev, _) in enumerate(peers):
            cp = pltpu.make_async_remote_copy(src_ref=p_ref, dst_ref=gath.at[me], send_sem=send_sems.at[k],
                                              recv_sem=recv_sems.at[k], device_id=dev, device_id_type=MESH)
            cp.start()
            sends.append(cp)
        for k, (dev, lin) in enumerate(peers):
            pltpu.make_async_remote_copy(src_ref=p_ref, dst_ref=gath.at[lin], send_sem=send_sems.at[k],
                                         recv_sem=recv_sems.at[k], device_id=dev, device_id_type=MESH).wait_recv()
        for cp in sends:
            cp.wait_send()
        g = gath[0]
        for s in range(1, N_DEV):
            g = g + gath[s]
        delta, m2, v2 = _adamw(w_ref[...], g, m_ref[...], v_ref[...])
        g_ref[...] = g
        d_ref[...] = delta
        m2_ref[...] = m2
        v2_ref[...] = v2
        sq = jnp.sum(g[SMALL_LOSS_ROW:SMALL_LOSS_ROW + 8, :], axis=1, keepdims=True)
        tot = jnp.sum(sq, axis=0, keepdims=True) * (0.5 / D_MODEL)
        loss_ref[...] = jnp.broadcast_to(tot, loss_ref.shape)

    vm = pl.BlockSpec(memory_space=pltpu.VMEM)
    return pl.pallas_call(
        body, name="small_allreduce_adamw",
        in_specs=[vm, vm, vm, vm],
        out_specs=[vm, vm, vm, vm, vm],
        out_shape=[jax.ShapeDtypeStruct(shape, F32)] * 4 + [jax.ShapeDtypeStruct((8, LANES), F32)],
        scratch_shapes=[pltpu.VMEM((N_DEV,) + shape, F32),
                        pltpu.SemaphoreType.DMA((N_DEV - 1,)), pltpu.SemaphoreType.DMA((N_DEV - 1,))],
    )(part, w, m, v)


def _perm_in_cols(w):
    z = lambda n: jnp.zeros(w.shape[:-1] + (n,), w.dtype)
    small = jnp.concatenate([w[..., 384:400], z(16), w[..., 1952:1960], z(24), w[..., 400:416], z(48)], -1)
    return jnp.concatenate([w[..., 1960:2984], w[..., 2984:4008], w[..., 416:928], w[..., 928:1440],
                            w[..., 1440:1952], w[..., 0:256], w[..., 256:384], small], -1)


def _unperm_in_cols(wp):
    s = wp[..., P_SMALL:]
    return jnp.concatenate([wp[..., P_CQ:P_CQ + 256], wp[..., P_CKV:P_CKV + 128], s[..., 0:16], s[..., 64:80],
                            wp[..., P_FQ:P_FQ + 512], wp[..., P_FK:P_FK + 512], wp[..., P_FV:P_FV + 512],
                            s[..., S_FL:S_FL + 8], wp[..., P_GA:P_GA + 1024], wp[..., P_GB:P_GB + 1024]], -1)


def _aug_uq_cols(w):
    r = w.shape[0]
    w3 = w.reshape(r, N_HEADS, NOPE + ROPE)
    z = jnp.zeros((r, N_HEADS, 32), w.dtype)
    return jnp.concatenate([w3[:, :, 64:80], w3[:, :, 0:48], w3[:, :, 80:96], w3[:, :, 48:64], z], -1).reshape(r, 1024)


def _unaug_uq_cols(wp):
    r = wp.shape[0]
    w3 = wp.reshape(r, N_HEADS, LANES)
    return jnp.concatenate([w3[:, :, 16:64], w3[:, :, 80:96], w3[:, :, 0:16], w3[:, :, 64:80]], -1).reshape(r, 768)


def _aug_uk_cols(w):
    r = w.shape[0]
    w3 = w.reshape(r, N_HEADS, NOPE)
    z = lambda n: jnp.zeros((r, N_HEADS, n), w.dtype)
    return jnp.concatenate([z(16), w3[:, :, 0:48], z(16), w3[:, :, 48:64], z(32)], -1).reshape(r, 1024)


def _unaug_uk_cols(wp):
    r = wp.shape[0]
    w3 = wp.reshape(r, N_HEADS, LANES)
    return jnp.concatenate([w3[:, :, 16:64], w3[:, :, 80:96]], -1).reshape(r, 512)


def _to_t4(x, tq):
    T, c = x.shape
    return x.reshape(T // tq, tq, c // LANES, LANES).transpose(2, 0, 3, 1)


def _from_t4(y):
    n, nq, r, tq = y.shape
    return y.transpose(1, 3, 0, 2).reshape(nq * tq, n * r)


def _pack_shards(group, shards, dtype):
    rows = _pack_rows(group)
    flat = jnp.concatenate([shards[n].reshape(-1).astype(dtype) for n, _, _, _ in group])
    flat = jnp.pad(flat, (0, rows * LANES - flat.shape[0]))
    return flat.reshape(rows, LANES)


def _unpack_shards(group, packed):
    flat = packed.reshape(-1)
    out, off = {}, 0
    for name, r, c, _ in group:
        out[name] = flat[off:off + r * c].reshape(1, r, c)
        off += r * c
    return out


def _unpack_full(group, gathered):
    flat = gathered.reshape(N_DEV, -1)
    out, off = {}, 0
    for name, r, c, by_col in group:
        blk = flat[:, off:off + r * c].reshape(N_DEV, r, c)
        out[name] = blk.transpose(1, 0, 2).reshape(r, N_DEV * c) if by_col else blk.reshape(N_DEV * r, c)
        off += r * c
    return out


def _pack_full(group, grads, dtype):
    rows = _pack_rows(group)
    segs = []
    for name, r, c, by_col in group:
        g = grads[name]
        if by_col:
            g = g.reshape(r, N_DEV, c).transpose(1, 0, 2)
        segs.append(g.reshape(N_DEV, r * c).astype(dtype))
    flat = jnp.concatenate(segs, axis=1)
    flat = jnp.pad(flat, ((0, 0), (0, rows * LANES - flat.shape[1])))
    return flat.reshape(N_DEV, rows, LANES)


SMALL_LAYOUT = (("ln_pre_mix", 1024, 0), ("ln_post_mix", 1024, 8), ("ln_pre_mlp", 1024, 16),
                ("ln_post_mlp", 1024, 24), ("b_in", 4008, 32), ("q_a_norm", 256, 64), ("kv_a_norm", 128, 66))


def _pack_small(vals, extra=None):
    rows = []
    for name, n, _ in SMALL_LAYOUT:
        v = vals[name].reshape(-1).astype(F32)
        pad = -n % LANES
        rows.append(jnp.pad(v, (0, pad)).reshape(-1, LANES))
    rows.append(jnp.zeros((SMALL_LOSS_ROW - 67, LANES), F32))
    rows.append(jnp.zeros((8, LANES), F32) if extra is None else extra.reshape(8, LANES))
    return jnp.concatenate(rows, axis=0)


def _unpack_small(packed):
    out = {}
    for name, n, r0 in SMALL_LAYOUT:
        nr = -(-n // LANES)
        out[name] = packed[r0:r0 + nr].reshape(-1)[:n].reshape(1, n)
    return out


def _rms(xf, g):
    r = lax.rsqrt(jnp.mean(xf * xf, axis=-1, keepdims=True) + NORM_EPS)
    return (xf * r) * g


def _rms_bwd(xf, g, dy):
    r = lax.rsqrt(jnp.mean(xf * xf, axis=-1, keepdims=True) + NORM_EPS)
    xhat = xf * r
    dxhat = dy * g
    dx = r * (dxhat - xhat * jnp.mean(dxhat * xhat, axis=-1, keepdims=True))
    return dx, dy * xhat


def _sigmoid(t):
    return 1.0 / (1.0 + jnp.exp(-t))


def _log_sigmoid(t):
    return jnp.minimum(t, 0.0) - jnp.log(1.0 + jnp.exp(-jnp.abs(t)))


def _lane_sign():
    lane = lax.broadcasted_iota(jnp.int32, (1, LANES), 1)
    return jnp.where(lane < 64, -1.0, 1.0).astype(F32), lane


def _rope_lanes(lane):
    return (lane < 16) | ((lane >= 64) & (lane < 80))


def kernel(x, positions, ln_pre_mix, ln_post_mix, ln_pre_mlp, ln_post_mlp, w_in, b_in, q_a_norm, w_uq, kv_a_norm, w_uk, w_uv, w_o_mla, w_o_fox, w_out, w_ff1, w_ff2, loss_target, m_ln_pre_mix, m_ln_post_mix, m_ln_pre_mlp, m_ln_post_mlp, m_w_in, m_b_in, m_q_a_norm, m_w_uq, m_kv_a_norm, m_w_uk, m_w_uv, m_w_o_mla, m_w_o_fox, m_w_out, m_w_ff1, m_w_ff2, v_ln_pre_mix, v_ln_post_mix, v_ln_pre_mlp, v_ln_post_mlp, v_w_in, v_b_in, v_q_a_norm, v_w_uq, v_kv_a_norm, v_w_uk, v_w_uv, v_w_o_mla, v_w_o_fox, v_w_out, v_w_ff1, v_w_ff2):
    T = x.shape[1]
    x2 = x.reshape(T, D_MODEL)
    tgt = loss_target.reshape(T, D_MODEL)
    w_sh = dict(w_in=w_in, w_uq=w_uq, w_uk=w_uk, w_uv=w_uv, w_o_mla=w_o_mla, w_o_fox=w_o_fox, w_out=w_out,
                w_ff1=w_ff1, w_ff2=w_ff2)
    m_sh = dict(w_in=m_w_in, w_uq=m_w_uq, w_uk=m_w_uk, w_uv=m_w_uv, w_o_mla=m_w_o_mla, w_o_fox=m_w_o_fox,
                w_out=m_w_out, w_ff1=m_w_ff1, w_ff2=m_w_ff2)
    v_sh = dict(w_in=v_w_in, w_uq=v_w_uq, w_uk=v_w_uk, w_uv=v_w_uv, w_o_mla=v_w_o_mla, w_o_fox=v_w_o_fox,
                w_out=v_w_out, w_ff1=v_w_ff1, w_ff2=v_w_ff2)
    small_w = dict(ln_pre_mix=ln_pre_mix, ln_post_mix=ln_post_mix, ln_pre_mlp=ln_pre_mlp, ln_post_mlp=ln_post_mlp,
                   b_in=b_in, q_a_norm=q_a_norm, kv_a_norm=kv_a_norm)
    small_m = dict(ln_pre_mix=m_ln_pre_mix, ln_post_mix=m_ln_post_mix, ln_pre_mlp=m_ln_pre_mlp,
                   ln_post_mlp=m_ln_post_mlp, b_in=m_b_in, q_a_norm=m_q_a_norm, kv_a_norm=m_kv_a_norm)
    small_v = dict(ln_pre_mix=v_ln_pre_mix, ln_post_mix=v_ln_post_mix, ln_pre_mlp=v_ln_pre_mlp,
                   ln_post_mlp=v_ln_post_mlp, b_in=v_b_in, q_a_norm=v_q_a_norm, kv_a_norm=v_kv_a_norm)
    mla_scale = float((NOPE + ROPE) ** -0.5)
    fox_scale = float(FOX_D ** -0.5)

    W = _unpack_full(GROUP_A, _comm_call("gather", _pack_shards(GROUP_A, w_sh, BF16), "allgather_weights_a"))
    w_in_p = _perm_in_cols(W["w_in"])
    b_in_p = _perm_in_cols(b_in.astype(F32))
    w_uq_a = _aug_uq_cols(W["w_uq"])
    w_kv_a = jnp.concatenate([_aug_uk_cols(W["w_uk"]), W["w_uv"]], axis=1)
    g1, g2, g3, g4 = ln_pre_mix, ln_post_mix, ln_pre_mlp, ln_post_mlp
    gq, gkv = q_a_norm, kv_a_norm
    tq = min(ATTN_TILE, T)
    nq = T // tq

    (h,) = _rowwise(lambda xv, g: _rms(xv, g), [x2], [g1], [(D_MODEL, BF16)], name="pre_mix_norm")
    q_cols = jnp.ones((1, D_IN_PAD), F32).at[:, P_FQ:P_FQ + 512].set(fox_scale * LOG2E)
    z = _mm(h, w_in_p, bias=b_in_p, colscale=q_cols, tm=2048, name="in_proj")
    zs = _mm(h, w_in_p[:, P_SMALL:], bias=b_in_p[:, P_SMALL:], outs=(F32,), name="in_proj_small")

    def lora_norm(cq, ckv, a, b):
        return _rms(cq.astype(F32), a), _rms(ckv.astype(F32), b)

    cqn, ckvn = _rowwise(lora_norm, [(z, 256, P_CQ // 256), (z, 128, P_CKV // 128)], [gq, gkv],
                         [(Q_LORA, BF16), (KV_LORA, BF16)], name="lora_norm")
    q_x = _mm(cqn, w_uq_a, outs=(F32,), name="q_up")
    kv = _mm(ckvn, w_kv_a, tn=512, name="kv_up")

    half = ROPE // 2
    inv_freq = ROPE_THETA ** (-jnp.arange(half, dtype=F32) / half)
    inv128 = jnp.tile(inv_freq, LANES // half).reshape(1, LANES)
    pos_col = positions.reshape(T, 1).astype(F32)

    def rope_tables(p, f):
        ang = p * f
        return jnp.cos(ang), jnp.sin(ang)

    cos_t, sin_t = _rowwise(rope_tables, [pos_col], [inv128], [(LANES, F32), (LANES, F32)], name="rope_tables")

    def rope_fwd(qx, kn, s, cs, sn):
        sign, lane = _lane_sign()
        rope_l = _rope_lanes(lane)
        rot = lambda t: t * cs + pltpu.roll(t, 64, 1) * sn * sign
        k_rot = jnp.where(rope_l, rot(s), 0.0)
        qs, ks = [], []
        for hd in range(N_HEADS):
            qb = qx[:, LANES * hd:LANES * (hd + 1)]
            qs.append(jnp.where(rope_l, rot(qb), qb))
            ks.append(kn[:, LANES * hd:LANES * (hd + 1)].astype(F32) + k_rot)
        return jnp.concatenate(qs, axis=1) * (mla_scale * LOG2E), jnp.concatenate(ks, axis=1)

    q_am, k_am = _rowwise(rope_fwd, [q_x, (kv, 1024, 0), zs, cos_t, sin_t], [], [(1024, BF16), (1024, BF16)],
                          name="rope_fwd")
    f_cum = _seq_cumsum(zs, _log_sigmoid, reverse=False, name="forget_cumsum")

    def fox_aug(qf, kf, fc):
        lane = lax.broadcasted_iota(jnp.int32, (1, LANES), 1)
        qs, ks = [], []
        for hd in range(N_HEADS):
            qp = qf[:, LANES * (hd // 2):LANES * (hd // 2 + 1)].astype(F32)
            kp = kf[:, LANES * (hd // 2):LANES * (hd // 2 + 1)].astype(F32)
            if hd % 2:
                qp = pltpu.roll(qp, 64, 1)
                kp = pltpu.roll(kp, 64, 1)
            fb = jnp.broadcast_to(fc[:, S_FL + hd:S_FL + hd + 1] * (-LOG2E), qp.shape)
            hi = fb.astype(BF16).astype(F32)
            mid = (fb - hi).astype(BF16).astype(F32)
            lo = fb - hi - mid
            qs.append(jnp.where(lane < 64, qp, jnp.where(lane < 67, 1.0, 0.0)))
            ks.append(jnp.where(lane < 64, kp, jnp.where(lane == 64, hi, jnp.where(lane == 65, mid, jnp.where(
                lane == 66, lo, jnp.where(lane == 67, 1.0, 0.0))))))
        return jnp.concatenate(qs, axis=1), jnp.concatenate(ks, axis=1)

    q_af, k_af = _rowwise(fox_aug, [(z, 512, P_FQ // 512), (z, 512, P_FK // 512), f_cum], [],
                          [(1024, BF16), (1024, BF16)], name="fox_aug")
    qt4_m, qt4_f = _to_t4(q_am, tq), _to_t4(q_af, tq)
    vt4_m, vt4_f = _to_t4(kv[:, 1024:], tq), _to_t4(z[:, P_FV:P_FV + 512], tq)
    o_mla, lse_mla, gathered_b = _attn_fwd(qt4_m, k_am, vt4_m, name="mla_attn_fwd", tq=tq,
                                           gather=_pack_shards(GROUP_B, w_sh, BF16))
    W.update(_unpack_full(GROUP_B, gathered_b))
    o_fox, lse_fox = _attn_fwd(qt4_f, k_af, vt4_f, name="fox_attn_fwd", tq=tq)
    y_mla = _mm(o_mla, W["w_o_mla"], outs=(F32,), name="o_proj_mla")
    y_fox = _mm(o_fox, W["w_o_fox"], outs=(F32,), name="o_proj_fox")

    def gate_merge(ga, gb, ya, yb):
        return _sigmoid(ga.astype(F32)) * ya + _sigmoid(gb.astype(F32)) * yb

    (merged,) = _rowwise(gate_merge, [(z, 1024, 0), (z, 1024, 1), y_mla, y_fox], [], [(D_MODEL, BF16)],
                         name="gate_merge")
    mix = _mm(merged, W["w_out"], outs=(F32,), name="out_proj")

    def post_mix(xv, mv, a, b):
        x1v = xv + _rms(mv, a)
        return x1v, _rms(x1v, b)

    x1, h2 = _rowwise(post_mix, [x2, mix], [g2, g3], [(D_MODEL, F32), (D_MODEL, BF16)], name="post_mix_norm")

    def relu2(acc):
        r = jnp.maximum(acc, 0.0)
        return r * r, acc

    act, u = _mm(h2, W["w_ff1"], epi=relu2, outs=(BF16, BF16), tm=2048, name="ff1")
    mlp = _mm(act, W["w_ff2"], outs=(F32,), tm=1024, name="ff2")

    def loss_bwd(x1v, mv, tv, g):
        y = x1v + _rms(mv, g)
        d = y - tv
        dy = d * (1.0 / D_MODEL)
        dm, dg = _rms_bwd(mv, g, dy)
        return dy, dm, dg, d * d

    dy, dm, dg4, loss_cols = _rowwise(loss_bwd, [x1, mlp, tgt], [g4], [(D_MODEL, F32), (D_MODEL, BF16)],
                                      [D_MODEL, D_MODEL], name="loss_bwd")

    def relu2_bwd(acc, uv):
        return (acc * (2.0 * jnp.maximum(uv.astype(F32), 0.0)),)

    du = _mm(dm, W["w_ff2"].T, extras=(u,), epi=relu2_bwd, tm=2048, name="ff2_bwd")
    dw_ff2 = _mm(act, dm, ta=True, outs=(F32,), name="ff2_wgrad")
    dh2 = _mm(du, W["w_ff1"].T, outs=(F32,), tm=1024, name="ff1_bwd")
    dw_ff1 = _mm(h2, du, ta=True, outs=(F32,), name="ff1_wgrad")

    def post_mix_bwd(x1v, dh2v, dyv, mv, a, b):
        d3, dg3v = _rms_bwd(x1v, b, dh2v)
        dx1v = dyv + d3
        dmixv, dg2v = _rms_bwd(mv, a, dx1v)
        return dx1v, dmixv, dg3v, dg2v

    dx1, dmix, dg3, dg2 = _rowwise(post_mix_bwd, [x1, dh2, dy, mix], [g2, g3], [(D_MODEL, F32), (D_MODEL, BF16)],
                                   [D_MODEL, D_MODEL], name="post_mix_bwd")
    dmerged = _mm(dmix, W["w_out"].T, outs=(F32,), name="out_proj_bwd")
    dw_out = _mm(merged, dmix, ta=True, outs=(F32,), name="out_proj_wgrad")

    def gate_bwd(dmg, ga, gb, ya, yb):
        sa = _sigmoid(ga.astype(F32))
        sb = _sigmoid(gb.astype(F32))
        return dmg * sa, dmg * sb, dmg * ya * sa * (1.0 - sa), dmg * yb * sb * (1.0 - sb)

    dy_mla, dy_fox, dga, dgb = _rowwise(gate_bwd, [dmerged, (z, 1024, 0), (z, 1024, 1), y_mla, y_fox], [],
                                        [(D_MODEL, BF16)] * 4, name="gate_bwd")
    do_mla = _mm(dy_mla, W["w_o_mla"].T, name="o_proj_mla_bwd")
    do_fox = _mm(dy_fox, W["w_o_fox"].T, name="o_proj_fox_bwd")
    dw_o_mla = _mm(o_mla, dy_mla, ta=True, outs=(F32,), name="o_proj_mla_wgrad")
    dw_o_fox = _mm(o_fox, dy_fox, ta=True, outs=(F32,), name="o_proj_fox_wgrad")

    def head_dots(ov, dov):
        r = lax.broadcasted_iota(jnp.int32, (N_HEADS * HEAD_V, LANES), 0)
        cc = lax.broadcasted_iota(jnp.int32, (N_HEADS * HEAD_V, LANES), 1)
        sel = jnp.where((r // HEAD_V) == cc, 1.0, 0.0).astype(BF16)
        return _dot3(ov.astype(F32) * dov.astype(F32), sel, left=False)

    (dl_mla,) = _rowwise(head_dots, [o_mla, do_mla], [], [(LANES, F32)], name="mla_attn_delta")
    (dl_fox,) = _rowwise(head_dots, [o_fox, do_fox], [], [(LANES, F32)], name="fox_attn_delta")
    heads_row4 = lambda t: t[:, :N_HEADS].T.reshape(N_HEADS, nq, 1, tq)
    grads_b = dict(w_o_mla=dw_o_mla, w_o_fox=dw_o_fox, w_out=dw_out, w_ff1=dw_ff1, w_ff2=dw_ff2)
    dqt_m, dkt_m, dv_mla, pieces_b = _attn_bwd(qt4_m, k_am, kv, 1024 // LANES, _to_t4(do_mla, tq), lse_mla,
                                               heads_row4(dl_mla), mla_scale, NOPE + ROPE, name="mla_attn_bwd", tq=tq,
                                               exchange=_pack_full(GROUP_B, grads_b, BF16))
    dqt_f, dkt_f, dv_fox = _attn_bwd(qt4_f, k_af, z, P_FV // LANES, _to_t4(do_fox, tq), lse_fox, heads_row4(dl_fox),
                                     fox_scale, FOX_D, name="fox_attn_bwd", tq=tq)
    dq_fox = _from_t4(dqt_f[:, :, :FOX_D, :]).astype(BF16)
    dk_fox = _from_t4(dkt_f[:, :, :FOX_D, :]).astype(BF16)
    d_f = (dqt_f[:, :, 67, :] - dkt_f[:, :, 64, :]).reshape(N_HEADS, T).T
    d_f128 = jnp.pad(d_f, ((0, 0), (S_FL, LANES - S_FL - N_HEADS)))
    df_rev = _seq_cumsum(d_f128, lambda t: t, reverse=True, name="forget_cumsum_bwd")

    def rope_bwd(dqa, dka, dvm, dfr, s, cs, sn):
        sign, lane = _lane_sign()
        rope_l = _rope_lanes(lane)
        rot_t = lambda t: t * cs - pltpu.roll(t, 64, 1) * sn * sign
        dqs = []
        dk_rot = None
        for hd in range(N_HEADS):
            blk = dqa[:, LANES * hd:LANES * (hd + 1)]
            dqs.append(jnp.where(rope_l, rot_t(blk), blk))
            blk = dka[:, LANES * hd:LANES * (hd + 1)]
            dk_rot = blk if dk_rot is None else dk_rot + blk
        dfl = dfr * _sigmoid(-s)
        small = jnp.where(rope_l, rot_t(dk_rot), jnp.where((lane >= S_FL) & (lane < S_FL + N_HEADS), dfl, 0.0))
        return jnp.concatenate(dqs, axis=1), jnp.concatenate([dka, dvm.astype(F32)], axis=1), small

    dq_b, dkv, d_small = _rowwise(rope_bwd, [_from_t4(dqt_m), _from_t4(dkt_m), dv_mla, df_rev, zs, cos_t, sin_t], [],
                                  [(1024, BF16), (1536, BF16), (LANES, BF16)], name="rope_bwd")
    dcqn = _mm(dq_b, w_uq_a.T, outs=(F32,), name="q_up_bwd")
    dw_uq_a = _mm(cqn, dq_b, ta=True, outs=(F32,), name="q_up_wgrad")
    dckvn = _mm(dkv, w_kv_a.T, outs=(F32,), tk=512, name="kv_up_bwd")
    dw_kv_a = _mm(ckvn, dkv, ta=True, outs=(F32,), tn=512, name="kv_up_wgrad")

    def lora_norm_bwd(cq, ckv, dq_, dkv_, a, b):
        d1, dga_ = _rms_bwd(cq.astype(F32), a, dq_)
        d2, dgb_ = _rms_bwd(ckv.astype(F32), b, dkv_)
        return d1, d2, dga_, dgb_

    dcq, dckv, dgq, dgkv = _rowwise(lora_norm_bwd, [(z, 256, P_CQ // 256), (z, 128, P_CKV // 128), dcqn, dckvn],
                                    [gq, gkv], [(Q_LORA, BF16), (KV_LORA, BF16)], [Q_LORA, KV_LORA],
                                    name="lora_norm_bwd")
    dz = jnp.concatenate([dga, dgb, dq_fox, dk_fox, dv_fox, dcq, dckv, d_small], -1)
    (db_in_p,) = _rowwise(lambda t: (t,), [dz], [], [], [D_IN_PAD], name="in_bias_grad")
    dh = _mm(dz, w_in_p.T, outs=(F32,), tm=1024, name="in_proj_bwd")
    dw_in_p = _mm(h, dz, ta=True, outs=(F32,), name="in_proj_wgrad")

    def pre_mix_bwd(xv, dhv, dx1v, g):
        d, dg = _rms_bwd(xv, g, dhv)
        return dx1v + d, dg

    grad_x, dg1 = _rowwise(pre_mix_bwd, [x2, dh, dx1], [g1], [(D_MODEL, F32)], [D_MODEL], name="pre_mix_bwd")

    grads_a = dict(w_in=_unperm_in_cols(dw_in_p), w_uq=_unaug_uq_cols(dw_uq_a),
                   w_uk=_unaug_uk_cols(dw_kv_a[:, :1024]), w_uv=dw_kv_a[:, 1024:])
    pieces_a = _comm_call("exchange", _pack_full(GROUP_A, grads_a, BF16), "exchange_grad_pieces_a")
    grad_sh, delta_sh, newm_sh, newv_sh = {}, {}, {}, {}
    for group, pieces, tag in ((GROUP_A, pieces_a, "a"), (GROUP_B, pieces_b, "b")):
        packed = _sum_adamw(pieces, _pack_shards(group, w_sh, F32), _pack_shards(group, m_sh, F32),
                            _pack_shards(group, v_sh, F32), name="sum_pieces_adamw_" + tag)
        for dst, arr in zip((grad_sh, delta_sh, newm_sh, newv_sh), packed):
            dst.update(_unpack_shards(group, arr))

    small_part = _pack_small(dict(ln_pre_mix=dg1, ln_post_mix=dg2, ln_pre_mlp=dg3, ln_post_mlp=dg4,
                                  b_in=_unperm_in_cols(db_in_p), q_a_norm=dgq, kv_a_norm=dgkv), extra=loss_cols)
    sg, sd, sm, sv, loss_blk = _small_allreduce_adamw(small_part, _pack_small(small_w), _pack_small(small_m),
                                                      _pack_small(small_v))
    grad_sm, delta_sm, newm_sm, newv_sm = (_unpack_small(t) for t in (sg, sd, sm, sv))
    loss = loss_blk[0, 0]

    order = ["ln_pre_mix", "ln_post_mix", "ln_pre_mlp", "ln_post_mlp", "w_in", "b_in", "q_a_norm", "w_uq",
             "kv_a_norm", "w_uk", "w_uv", "w_o_mla", "w_o_fox", "w_out", "w_ff1", "w_ff2"]

    def pick(sm_d, sh_d):
        return [sm_d[n] if n in sm_d else sh_d[n] for n in order]

    return (loss, grad_x.reshape(1, T, D_MODEL), *pick(grad_sm, grad_sh), *pick(delta_sm, delta_sh),
            *pick(newm_sm, newm_sh), *pick(newv_sm, newv_sh))
```

```python
import numpy as np
import jax
import jax.numpy as jnp
from jax import lax
from jax.experimental import pallas as pl
from jax.experimental.pallas import tpu as pltpu

F32 = jnp.float32
BF16 = jnp.bfloat16
MESH = pl.DeviceIdType.MESH

D_MODEL = 1024
N_HEADS = 8
Q_LORA = 256
KV_LORA = 128
NOPE = 64
ROPE = 32
HEAD_V = 64
FOX_D = 64
D_FF = 4096
D_IN = 4008
D_IN_PAD = 4096
ROPE_THETA = 10000.0
NORM_EPS = 1e-6
N_DEV = 8

ADAM_LR = 0.001
ADAM_B1 = 0.9
ADAM_B2 = 0.999
ADAM_EPS = 1e-08
ADAM_WD = 0.01
ADAM_STEP = 10

LANES = 128
ROW_TILE = 512
ATTN_TILE = 512
ATTN_CHUNK = 256
VMEM_LIMIT = 48 * 1024 * 1024
ATTN_BWD_VMEM_LIMIT = 58 * 1024 * 1024

P_GA, P_GB, P_FQ, P_FK, P_FV, P_CQ, P_CKV, P_SMALL = 0, 1024, 2048, 2560, 3072, 3584, 3840, 3968
S_FL = 32

SHARDED = (
    ("w_in", 1024, 501, True), ("w_uq", 256, 96, True), ("w_uk", 128, 64, True), ("w_uv", 128, 64, True),
    ("w_o_mla", 512, 128, True), ("w_o_fox", 512, 128, True), ("w_out", 128, 1024, False),
    ("w_ff1", 1024, 512, True), ("w_ff2", 512, 1024, False),
)
GROUP_A = SHARDED[:4]
GROUP_B = SHARDED[4:]
SMALL_ROWS = 80
SMALL_LOSS_ROW = 72


def _pack_rows(group):
    return -(-sum(r * c for _, r, c, _ in group) // (LANES * 64)) * 64


def _cparams(sem=None):
    return pltpu.CompilerParams(dimension_semantics=sem, vmem_limit_bytes=VMEM_LIMIT)


def _mm(a, b, *, name, ta=False, bias=None, colscale=None, extras=(), epi=None, outs=(BF16,), tm=ROW_TILE, tn=1024,
        tk=1024):
    if ta:
        K, M = a.shape
        tm = min(1024, M)
    else:
        M, K = a.shape
        tm = min(tm, M)
    tk = min(tk, K)
    N = b.shape[1]
    tn = min(tn, N)
    nk = K // tk
    n_ex = len(extras)
    has_bias = bias is not None
    has_scale = colscale is not None

    def body(*refs):
        a_ref, b_ref = refs[0], refs[1]
        pos = 2
        bias_ref = scale_ref = None
        if has_bias:
            bias_ref = refs[pos]
            pos += 1
        if has_scale:
            scale_ref = refs[pos]
            pos += 1
        ex_refs = refs[pos:pos + n_ex]
        pos += n_ex
        o_refs = refs[pos:pos + len(outs)]
        pos += len(outs)
        av = a_ref[...].astype(BF16)
        bv = b_ref[...].astype(BF16)
        if ta:
            part = lax.dot_general(av, bv, (((0,), (0,)), ((), ())), preferred_element_type=F32)
        else:
            part = jnp.dot(av, bv, preferred_element_type=F32)

        def finish(acc):
            if has_bias:
                acc = acc + bias_ref[...]
            if has_scale:
                acc = acc * scale_ref[...]
            res = (acc,) if epi is None else epi(acc, *[r[...] for r in ex_refs])
            for o_ref, val in zip(o_refs, res):
                o_ref[...] = val.astype(o_ref.dtype)

        if nk == 1:
            finish(part)
        else:
            acc_ref = refs[pos]
            k = pl.program_id(2)

            @pl.when(k == 0)
            def _():
                acc_ref[...] = part

            @pl.when(k > 0)
            def _():
                acc_ref[...] += part

            @pl.when(k == nk - 1)
            def _():
                finish(acc_ref[...])

    if ta:
        a_spec = pl.BlockSpec((tk, tm), lambda i, j, k: (k, i))
    else:
        a_spec = pl.BlockSpec((tm, tk), lambda i, j, k: (i, k))
    in_specs = [a_spec, pl.BlockSpec((tk, tn), lambda i, j, k: (k, j))]
    args = [a, b]
    for row in (bias, colscale):
        if row is not None:
            in_specs.append(pl.BlockSpec((1, tn), lambda i, j, k: (0, j)))
            args.append(row)
    for e in extras:
        in_specs.append(pl.BlockSpec((tm, tn), lambda i, j, k: (i, j)))
        args.append(e)
    res = pl.pallas_call(
        body, name=name, grid=(M // tm, N // tn, nk),
        in_specs=in_specs,
        out_specs=[pl.BlockSpec((tm, tn), lambda i, j, k: (i, j)) for _ in outs],
        out_shape=[jax.ShapeDtypeStruct((M, N), dt) for dt in outs],
        scratch_shapes=[pltpu.VMEM((tm, tn), F32)] if nk > 1 else [],
        compiler_params=_cparams(("parallel", "parallel", "arbitrary")),
    )(*args)
    return res[0] if len(outs) == 1 else res


def _rowwise(fn, rows, bcasts, outs, accs=(), *, name, tm=ROW_TILE):
    arrs, specs = [], []
    for r in rows:
        if isinstance(r, tuple):
            arr, w, cb = r
            specs.append(pl.BlockSpec((tm, w), lambda i, cb=cb: (i, cb)))
        else:
            arr = r
            specs.append(pl.BlockSpec((tm, arr.shape[1]), lambda i: (i, 0)))
        arrs.append(arr)
    T = arrs[0].shape[0]
    tm = min(tm, T)
    specs = [pl.BlockSpec((tm,) + tuple(s.block_shape[1:]), s.index_map) for s in specs]
    for b in bcasts:
        arrs.append(b)
        specs.append(pl.BlockSpec(b.shape, lambda i: (0, 0)))
    n_in, n_out = len(arrs), len(outs)

    def body(*refs):
        vals = [r[...] for r in refs[:n_in]]
        res = fn(*vals)
        if not isinstance(res, (tuple, list)):
            res = (res,)
        for o_ref, val in zip(refs[n_in:n_in + n_out], res[:n_out]):
            o_ref[...] = val.astype(o_ref.dtype)
        i = pl.program_id(0)
        for a_ref, val in zip(refs[n_in + n_out:], res[n_out:]):
            col = jnp.sum(val.astype(F32), axis=0, keepdims=True)

            @pl.when(i == 0)
            def _(a_ref=a_ref, col=col):
                a_ref[...] = col

            @pl.when(i > 0)
            def _(a_ref=a_ref, col=col):
                a_ref[...] += col

    res = pl.pallas_call(
        body, name=name, grid=(T // tm,),
        in_specs=specs,
        out_specs=[pl.BlockSpec((tm, c), lambda i: (i, 0)) for c, _ in outs]
        + [pl.BlockSpec((1, c), lambda i: (0, 0)) for c in accs],
        out_shape=[jax.ShapeDtypeStruct((T, c), dt) for c, dt in outs]
        + [jax.ShapeDtypeStruct((1, c), F32) for c in accs],
        compiler_params=_cparams(("arbitrary",)),
    )(*arrs)
    return res


def _dot3(x, sel, left):
    hi = x.astype(BF16)
    r1 = x - hi.astype(F32)
    mid = r1.astype(BF16)
    lo = (r1 - mid.astype(F32)).astype(BF16)
    if left:
        d = lambda t: jnp.dot(sel, t, preferred_element_type=F32)
    else:
        d = lambda t: jnp.dot(t, sel, preferred_element_type=F32)
    return d(hi) + d(mid) + d(lo)


def _seq_cumsum(x, fn, *, reverse, name, tm=256):
    T, C = x.shape
    tm = min(tm, T)
    n = T // tm

    def body(x_ref, o_ref, carry_ref):
        i = pl.program_id(0)

        @pl.when(i == 0)
        def _():
            carry_ref[...] = jnp.zeros_like(carry_ref)

        v = fn(x_ref[...])
        r = lax.broadcasted_iota(jnp.int32, (tm, tm), 0)
        c = lax.broadcasted_iota(jnp.int32, (tm, tm), 1)
        tri = jnp.where((r <= c) if reverse else (r >= c), 1.0, 0.0).astype(BF16)
        carry = carry_ref[0:1, :]
        o_ref[...] = _dot3(v, tri, left=True) + carry
        carry_ref[0:1, :] = carry + jnp.sum(v, axis=0, keepdims=True)

    idx = (lambda i: (n - 1 - i, 0)) if reverse else (lambda i: (i, 0))
    return pl.pallas_call(
        body, name=name, grid=(n,),
        in_specs=[pl.BlockSpec((tm, C), idx)],
        out_specs=pl.BlockSpec((tm, C), idx),
        out_shape=jax.ShapeDtypeStruct((T, C), F32),
        scratch_shapes=[pltpu.VMEM((8, C), F32)],
        compiler_params=_cparams(("arbitrary",)),
    )(x)


_NT = (((1,), (1,)), ((), ()))
LOG2E = 1.4426950408889634


def _head_mask(width, e):
    lane = lax.broadcasted_iota(jnp.int32, (1, width), 1)
    return (lane >= 64 * e) & (lane < 64 * (e + 1))


def _attn_fwd(qt4, k_aug, vt4, *, name, tq, gather=None):
    nq = qt4.shape[1]
    T = nq * tq
    cw = min(ATTN_CHUNK, tq)
    comm = gather is not None

    def body(*refs):
        refs = list(refs)
        q_ref, k_ref, v_ref = refs[:3]
        pos = 3
        if comm:
            c_src = refs[pos]
            pos += 1
        o_ref, lse_ref = refs[pos:pos + 2]
        pos += 2
        if comm:
            c_dst = refs[pos]
            pos += 1
        m_s, l_s, acc_s = refs[pos:pos + 2], refs[pos + 2:pos + 4], refs[pos + 4:pos + 6]
        c_sems = refs[pos + 6:]
        i = pl.program_id(1)
        if comm:
            @pl.when((pl.program_id(0) == 0) & (i == 0))
            def _():
                _comm_start("gather", c_src, c_dst, *c_sems)

        for e in range(2):
            m_s[e][...] = jnp.full(m_s[e].shape, -jnp.inf, F32)
            l_s[e][...] = jnp.zeros_like(l_s[e])
            acc_s[e][...] = jnp.zeros_like(acc_s[e])
        qt = [q_ref[e, 0] for e in range(2)]

        def step(j, masked):
            kj = k_ref[pl.ds(pl.multiple_of(j * tq, tq), tq), :]
            vtj = v_ref[0, j]
            work = [(e, slice(c * cw, (c + 1) * cw)) for e in range(2) for c in range(tq // cw)]
            sts = [jnp.dot(kj[:, e * LANES:(e + 1) * LANES], qt[e][:, cs], preferred_element_type=F32)
                   for e, cs in work]
            for (e, cs), st in zip(work, sts):
                if masked:
                    r = lax.broadcasted_iota(jnp.int32, (tq, cw), 0)
                    cc = lax.broadcasted_iota(jnp.int32, (tq, cw), 1) + cs.start
                    st = jnp.where(cc >= r, st, -jnp.inf)
                m_prev = m_s[e][:, cs]
                m_new = jnp.maximum(m_prev, jnp.max(st, axis=0, keepdims=True))
                alpha = jnp.exp2(m_prev - m_new)
                pt = jnp.exp2(st - m_new)
                l_s[e][:, cs] = alpha * l_s[e][:, cs] + jnp.sum(pt, axis=0, keepdims=True)
                acc_s[e][:, cs] = alpha * acc_s[e][:, cs] + jnp.dot(vtj[e * HEAD_V:(e + 1) * HEAD_V, :], pt.astype(BF16),
                                                                    preferred_element_type=F32)
                m_s[e][:, cs] = m_new

        def loop_body(j, carry):
            step(j, False)
            return carry

        lax.fori_loop(0, i, loop_body, 0)
        step(i, True)
        ot = jnp.concatenate([acc_s[e][...] / l_s[e][...] for e in range(2)], axis=0)
        o_ref[...] = ot.T.astype(o_ref.dtype)
        for e in range(2):
            lse_ref[e, 0] = m_s[e][...] + jnp.log(l_s[e][...]) * LOG2E
        if comm:
            @pl.when((pl.program_id(0) == N_HEADS // 2 - 1) & (i == nq - 1))
            def _():
                _comm_wait("gather", c_src, c_dst, *c_sems)

    in_specs = [
        pl.BlockSpec((2, 1, LANES, tq), lambda hp, i: (hp, i, 0, 0)),
        pl.BlockSpec((T, 2 * LANES), lambda hp, i: (0, hp)),
        pl.BlockSpec((1, nq, LANES, tq), lambda hp, i: (hp, 0, 0, 0)),
    ]
    args = [qt4, k_aug, vt4]
    out_specs = [pl.BlockSpec((tq, LANES), lambda hp, i: (i, hp)),
                 pl.BlockSpec((2, 1, 1, tq), lambda hp, i: (hp, i, 0, 0))]
    out_shape = [jax.ShapeDtypeStruct((T, N_HEADS * HEAD_V), BF16), jax.ShapeDtypeStruct((N_HEADS, nq, 1, tq), F32)]
    scratch = [pltpu.VMEM((1, tq), F32)] * 4 + [pltpu.VMEM((HEAD_V, tq), F32)] * 2
    if comm:
        in_specs.append(pl.BlockSpec(memory_space=pl.ANY))
        args.append(gather)
        out_specs.append(pl.BlockSpec(memory_space=pl.ANY))
        out_shape.append(jax.ShapeDtypeStruct((N_DEV,) + gather.shape, gather.dtype))
        scratch += _comm_sems()
    return pl.pallas_call(
        body, name=name, grid=(N_HEADS // 2, nq),
        in_specs=in_specs, out_specs=out_specs, out_shape=out_shape, scratch_shapes=scratch,
        compiler_params=_cparams(("arbitrary", "arbitrary")),
    )(*args)


def _attn_bwd(qt4, k_aug, v, vcb, dot4, lse_row, dl_row, scale, own_rows, *, name, tq, exchange=None):
    nq = qt4.shape[1]
    T = nq * tq
    cw = min(ATTN_CHUNK, tq)
    comm = exchange is not None

    def body(*refs):
        refs = list(refs)
        k_ref, v_ref, q_ref, do_ref, lse_ref, dl_ref = refs[:6]
        pos = 6
        if comm:
            c_src = refs[pos]
            pos += 1
        dq_ref, dk_ref, dv_ref = refs[pos:pos + 3]
        pos += 3
        if comm:
            c_dst = refs[pos]
            pos += 1
        dk_s, dv_s = refs[pos:pos + 2], refs[pos + 2:pos + 4]
        c_sems = refs[pos + 4:]
        j = pl.program_id(1)
        if comm:
            @pl.when((pl.program_id(0) == 0) & (j == 0))
            def _():
                _comm_start("exchange", c_src, c_dst, *c_sems)

        @pl.when(j == 0)
        def _():
            dq_ref[...] = jnp.zeros_like(dq_ref)

        kj = k_ref[...]
        vj = v_ref[...]
        ka = [kj[:, e * LANES:(e + 1) * LANES] for e in range(2)]
        row = lax.broadcasted_iota(jnp.int32, (LANES, 1), 0)
        own = row < own_rows
        kat = [(ka[e].astype(F32).T * jnp.where(own, scale, 1.0)).astype(BF16) for e in range(2)]
        vm = [jnp.where(_head_mask(LANES, e), vj, jnp.zeros_like(vj)) for e in range(2)]
        for e in range(2):
            dk_s[e][...] = jnp.zeros_like(dk_s[e])
            dv_s[e][...] = jnp.zeros_like(dv_s[e])

        def step(i, masked):
            dot_i = do_ref[0, i]
            qts = [q_ref[e, i] for e in range(2)]
            lse = [lse_ref[e, i] for e in range(2)]
            dl = [dl_ref[e, i] for e in range(2)]
            work = [(e, slice(c * cw, (c + 1) * cw)) for e in range(2) for c in range(tq // cw)]
            scores = lambda e, cs: (jnp.dot(ka[e], qts[e][:, cs], preferred_element_type=F32),
                                    jnp.dot(vm[e], dot_i[:, cs], preferred_element_type=F32))
            nxt = scores(*work[0])
            dqs = [[], []]
            for n, (e, cs) in enumerate(work):
                st, dpt = nxt
                if n + 1 < len(work):
                    nxt = scores(*work[n + 1])
                if masked:
                    r = lax.broadcasted_iota(jnp.int32, (tq, cw), 0)
                    cc = lax.broadcasted_iota(jnp.int32, (tq, cw), 1) + cs.start
                    st = jnp.where(cc >= r, st, -jnp.inf)
                pt = jnp.exp2(st - lse[e][:, cs])
                dv_s[e][...] += lax.dot_general(dot_i[e * HEAD_V:(e + 1) * HEAD_V, cs], pt.astype(BF16), _NT,
                                                preferred_element_type=F32)
                dsb = (pt * (dpt - dl[e][:, cs])).astype(BF16)
                dk_s[e][...] += lax.dot_general(qts[e][:, cs], dsb, _NT, preferred_element_type=F32)
                dqs[e].append(jnp.dot(kat[e], dsb, preferred_element_type=F32))
            for e in range(2):
                dq_ref[e, i] += jnp.concatenate(dqs[e], axis=1)

        def loop_body(i, carry):
            step(i, False)
            return carry

        step(j, True)
        lax.fori_loop(j + 1, nq, loop_body, 0)
        for e in range(2):
            dk_ref[e, 0] = dk_s[e][...] * jnp.where(own, 1.0 / LOG2E, 1.0)
        dv_ref[...] = jnp.concatenate([dv_s[0][...], dv_s[1][...]], axis=0).T.astype(dv_ref.dtype)
        if comm:
            @pl.when((pl.program_id(0) == N_HEADS // 2 - 1) & (j == nq - 1))
            def _():
                _comm_wait("exchange", c_src, c_dst, *c_sems)

    row4 = pl.BlockSpec((2, nq, 1, tq), lambda hp, j: (hp, 0, 0, 0))
    in_specs = [
        pl.BlockSpec((tq, 2 * LANES), lambda hp, j: (j, hp)),
        pl.BlockSpec((tq, LANES), lambda hp, j: (j, vcb + hp)),
        pl.BlockSpec((2, nq, LANES, tq), lambda hp, j: (hp, 0, 0, 0)),
        pl.BlockSpec((1, nq, LANES, tq), lambda hp, j: (hp, 0, 0, 0)),
        row4, row4,
    ]
    args = [k_aug, v, qt4, dot4, lse_row, dl_row]
    out_specs = [pl.BlockSpec((2, nq, LANES, tq), lambda hp, j: (hp, 0, 0, 0)),
                 pl.BlockSpec((2, 1, LANES, tq), lambda hp, j: (hp, j, 0, 0)),
                 pl.BlockSpec((tq, LANES), lambda hp, j: (j, hp))]
    out_shape = [jax.ShapeDtypeStruct((N_HEADS, nq, LANES, tq), F32), jax.ShapeDtypeStruct((N_HEADS, nq, LANES, tq), F32),
                 jax.ShapeDtypeStruct((T, N_HEADS * HEAD_V), BF16)]
    scratch = [pltpu.VMEM((LANES, tq), F32)] * 2 + [pltpu.VMEM((HEAD_V, tq), F32)] * 2
    if comm:
        in_specs.append(pl.BlockSpec(memory_space=pl.ANY))
        args.append(exchange)
        out_specs.append(pl.BlockSpec(memory_space=pl.ANY))
        out_shape.append(jax.ShapeDtypeStruct(exchange.shape, exchange.dtype))
        scratch += _comm_sems()
    return pl.pallas_call(
        body, name=name, grid=(N_HEADS // 2, nq),
        in_specs=in_specs, out_specs=out_specs, out_shape=out_shape, scratch_shapes=scratch,
        compiler_params=pltpu.CompilerParams(dimension_semantics=("arbitrary", "arbitrary"),
                                             vmem_limit_bytes=ATTN_BWD_VMEM_LIMIT),
    )(*args)


def _peers():
    x, y, c = lax.axis_index("x"), lax.axis_index("y"), lax.axis_index("c")
    me = 4 * x + 2 * y + c
    out = []
    for k in range(1, N_DEV):
        px = (1 - x) if (k & 4) else x
        py = (1 - y) if (k & 2) else y
        pc = (1 - c) if (k & 1) else c
        out.append(((px, py, pc), 4 * px + 2 * py + pc))
    return me, out


def _comm_sems():
    return [pltpu.SemaphoreType.DMA((N_DEV - 1,)), pltpu.SemaphoreType.DMA((N_DEV - 1,)), pltpu.SemaphoreType.DMA]


def _comm_copies(kind, src_ref, dst_ref, send_sems, recv_sems, local_sem):
    me, peers = _peers()
    local = pltpu.make_async_copy(src_ref if kind == "gather" else src_ref.at[me], dst_ref.at[me], local_sem)
    sends, recvs = [], []
    for k, (dev, lin) in enumerate(peers):
        src = src_ref if kind == "gather" else src_ref.at[lin]
        sends.append(pltpu.make_async_remote_copy(src_ref=src, dst_ref=dst_ref.at[me], send_sem=send_sems.at[k],
                                                  recv_sem=recv_sems.at[k], device_id=dev, device_id_type=MESH))
        recvs.append(pltpu.make_async_remote_copy(src_ref=src, dst_ref=dst_ref.at[lin], send_sem=send_sems.at[k],
                                                  recv_sem=recv_sems.at[k], device_id=dev, device_id_type=MESH))
    return local, sends, recvs


def _comm_start(kind, src_ref, dst_ref, send_sems, recv_sems, local_sem):
    local, sends, _ = _comm_copies(kind, src_ref, dst_ref, send_sems, recv_sems, local_sem)
    local.start()
    for cp in sends:
        cp.start()


def _comm_wait(kind, src_ref, dst_ref, send_sems, recv_sems, local_sem):
    local, sends, recvs = _comm_copies(kind, src_ref, dst_ref, send_sems, recv_sems, local_sem)
    for cp in recvs:
        cp.wait_recv()
    for cp in sends:
        cp.wait_send()
    local.wait()


def _comm_call(kind, src, name):
    def body(x_ref, o_ref, send_sems, recv_sems, local_sem):
        _comm_start(kind, x_ref, o_ref, send_sems, recv_sems, local_sem)
        _comm_wait(kind, x_ref, o_ref, send_sems, recv_sems, local_sem)

    shape = (N_DEV,) + src.shape if kind == "gather" else src.shape
    return pl.pallas_call(
        body, name=name,
        in_specs=[pl.BlockSpec(memory_space=pl.ANY)],
        out_specs=pl.BlockSpec(memory_space=pl.ANY),
        out_shape=jax.ShapeDtypeStruct(shape, src.dtype),
        scratch_shapes=_comm_sems(),
    )(src)


def _adamw(w, g, m, v):
    m2 = ADAM_B1 * m + (1.0 - ADAM_B1) * g
    v2 = ADAM_B2 * v + (1.0 - ADAM_B2) * (g * g)
    m_hat = m2 / (1.0 - ADAM_B1 ** ADAM_STEP)
    v_hat = v2 / (1.0 - ADAM_B2 ** ADAM_STEP)
    delta = -ADAM_LR * (m_hat / (jnp.sqrt(v_hat) + ADAM_EPS) + ADAM_WD * w)
    return delta, m2, v2


def _sum_adamw(pieces, w, m, v, *, name):
    R = w.shape[0]
    tr = R // 4 if R % 64 == 0 else R

    def body(p_ref, w_ref, m_ref, v_ref, g_ref, d_ref, m2_ref, v2_ref):
        g = p_ref[0].astype(F32)
        for s in range(1, N_DEV):
            g = g + p_ref[s].astype(F32)
        delta, m2, v2 = _adamw(w_ref[...], g, m_ref[...], v_ref[...])
        g_ref[...] = g
        d_ref[...] = delta
        m2_ref[...] = m2
        v2_ref[...] = v2

    row = pl.BlockSpec((tr, LANES), lambda i: (i, 0))
    return pl.pallas_call(
        body, name=name, grid=(R // tr,),
        in_specs=[pl.BlockSpec((N_DEV, tr, LANES), lambda i: (0, i, 0)), row, row, row],
        out_specs=[row, row, row, row],
        out_shape=[jax.ShapeDtypeStruct((R, LANES), F32)] * 4,
        compiler_params=_cparams(("parallel",)),
    )(pieces, w, m, v)


def _small_allreduce_adamw(part, w, m, v):
    shape = part.shape

    def body(p_ref, w_ref, m_ref, v_ref, g_ref, d_ref, m2_ref, v2_ref, loss_ref, gath, send_sems, recv_sems):
        me, peers = _peers()
        gath[me] = p_ref[...]
        sends = []
        for k, (dev, _) in enumerate(peers):
            cp = pltpu.make_async_remote_copy(src_ref=p_ref, dst_ref=gath.at[me], send_sem=send_sems.at[k],
                                              recv_sem=recv_sems.at[k], device_id=dev, device_id_type=MESH)
            cp.start()
            sends.append(cp)
        for k, (dev, lin) in enumerate(peers):
            pltpu.make_async_remote_copy(src_ref=p_ref, dst_ref=gath.at[lin], send_sem=send_sems.at[k],
                                         recv_sem=recv_sems.at[k], device_id=dev, device_id_type=MESH).wait_recv()
        for cp in sends:
            cp.wait_send()
        g = gath[0]
        for s in range(1, N_DEV):
            g = g + gath[s]
        delta, m2, v2 = _adamw(w_ref[...], g, m_ref[...], v_ref[...])
        g_ref[...] = g
        d_ref[...] = delta
        m2_ref[...] = m2
        v2_ref[...] = v2
        sq = jnp.sum(g[SMALL_LOSS_ROW:SMALL_LOSS_ROW + 8, :], axis=1, keepdims=True)
        tot = jnp.sum(sq, axis=0, keepdims=True) * (0.5 / D_MODEL)
        loss_ref[...] = jnp.broadcast_to(tot, loss_ref.shape)

    vm = pl.BlockSpec(memory_space=pltpu.VMEM)
    return pl.pallas_call(
        body, name="small_allreduce_adamw",
        in_specs=[vm, vm, vm, vm],
        out_specs=[vm, vm, vm, vm, vm],
        out_shape=[jax.ShapeDtypeStruct(shape, F32)] * 4 + [jax.ShapeDtypeStruct((8, LANES), F32)],
        scratch_shapes=[pltpu.VMEM((N_DEV,) + shape, F32),
                        pltpu.SemaphoreType.DMA((N_DEV - 1,)), pltpu.SemaphoreType.DMA((N_DEV - 1,))],
    )(part, w, m, v)


def _perm_in_cols(w):
    z = lambda n: jnp.zeros(w.shape[:-1] + (n,), w.dtype)
    small = jnp.concatenate([w[..., 384:400], z(16), w[..., 1952:1960], z(24), w[..., 400:416], z(48)], -1)
    return jnp.concatenate([w[..., 1960:2984], w[..., 2984:4008], w[..., 416:928], w[..., 928:1440],
                            w[..., 1440:1952], w[..., 0:256], w[..., 256:384], small], -1)


def _unperm_in_cols(wp):
    s = wp[..., P_SMALL:]
    return jnp.concatenate([wp[..., P_CQ:P_CQ + 256], wp[..., P_CKV:P_CKV + 128], s[..., 0:16], s[..., 64:80],
                            wp[..., P_FQ:P_FQ + 512], wp[..., P_FK:P_FK + 512], wp[..., P_FV:P_FV + 512],
                            s[..., S_FL:S_FL + 8], wp[..., P_GA:P_GA + 1024], wp[..., P_GB:P_GB + 1024]], -1)


def _aug_uq_cols(w):
    r = w.shape[0]
    w3 = w.reshape(r, N_HEADS, NOPE + ROPE)
    z = jnp.zeros((r, N_HEADS, 32), w.dtype)
    return jnp.concatenate([w3[:, :, 64:80], w3[:, :, 0:48], w3[:, :, 80:96], w3[:, :, 48:64], z], -1).reshape(r, 1024)


def _unaug_uq_cols(wp):
    r = wp.shape[0]
    w3 = wp.reshape(r, N_HEADS, LANES)
    return jnp.concatenate([w3[:, :, 16:64], w3[:, :, 80:96], w3[:, :, 0:16], w3[:, :, 64:80]], -1).reshape(r, 768)


def _aug_uk_cols(w):
    r = w.shape[0]
    w3 = w.reshape(r, N_HEADS, NOPE)
    z = lambda n: jnp.zeros((r, N_HEADS, n), w.dtype)
    return jnp.concatenate([z(16), w3[:, :, 0:48], z(16), w3[:, :, 48:64], z(32)], -1).reshape(r, 1024)


def _unaug_uk_cols(wp):
    r = wp.shape[0]
    w3 = wp.reshape(r, N_HEADS, LANES)
    return jnp.concatenate([w3[:, :, 16:64], w3[:, :, 80:96]], -1).reshape(r, 512)


def _to_t4(x, tq):
    T, c = x.shape
    return x.reshape(T // tq, tq, c // LANES, LANES).transpose(2, 0, 3, 1)


def _from_t4(y):
    n, nq, r, tq = y.shape
    return y.transpose(1, 3, 0, 2).reshape(nq * tq, n * r)


def _pack_shards(group, shards, dtype):
    rows = _pack_rows(group)
    flat = jnp.concatenate([shards[n].reshape(-1).astype(dtype) for n, _, _, _ in group])
    flat = jnp.pad(flat, (0, rows * LANES - flat.shape[0]))
    return flat.reshape(rows, LANES)


def _unpack_shards(group, packed):
    flat = packed.reshape(-1)
    out, off = {}, 0
    for name, r, c, _ in group:
        out[name] = flat[off:off + r * c].reshape(1, r, c)
        off += r * c
    return out


def _unpack_full(group, gathered):
    flat = gathered.reshape(N_DEV, -1)
    out, off = {}, 0
    for name, r, c, by_col in group:
        blk = flat[:, off:off + r * c].reshape(N_DEV, r, c)
        out[name] = blk.transpose(1, 0, 2).reshape(r, N_DEV * c) if by_col else blk.reshape(N_DEV * r, c)
        off += r * c
    return out


def _pack_full(group, grads, dtype):
    rows = _pack_rows(group)
    segs = []
    for name, r, c, by_col in group:
        g = grads[name]
        if by_col:
            g = g.reshape(r, N_DEV, c).transpose(1, 0, 2)
        segs.append(g.reshape(N_DEV, r * c).astype(dtype))
    flat = jnp.concatenate(segs, axis=1)
    flat = jnp.pad(flat, ((0, 0), (0, rows * LANES - flat.shape[1])))
    return flat.reshape(N_DEV, rows, LANES)


SMALL_LAYOUT = (("ln_pre_mix", 1024, 0), ("ln_post_mix", 1024, 8), ("ln_pre_mlp", 1024, 16),
                ("ln_post_mlp", 1024, 24), ("b_in", 4008, 32), ("q_a_norm", 256, 64), ("kv_a_norm", 128, 66))


def _pack_small(vals, extra=None):
    rows = []
    for name, n, _ in SMALL_LAYOUT:
        v = vals[name].reshape(-1).astype(F32)
        pad = -n % LANES
        rows.append(jnp.pad(v, (0, pad)).reshape(-1, LANES))
    rows.append(jnp.zeros((SMALL_LOSS_ROW - 67, LANES), F32))
    rows.append(jnp.zeros((8, LANES), F32) if extra is None else extra.reshape(8, LANES))
    return jnp.concatenate(rows, axis=0)


def _unpack_small(packed):
    out = {}
    for name, n, r0 in SMALL_LAYOUT:
        nr = -(-n // LANES)
        out[name] = packed[r0:r0 + nr].reshape(-1)[:n].reshape(1, n)
    return out


def _rms(xf, g):
    r = lax.rsqrt(jnp.mean(xf * xf, axis=-1, keepdims=True) + NORM_EPS)
    return (xf * r) * g


def _rms_bwd(xf, g, dy):
    r = lax.rsqrt(jnp.mean(xf * xf, axis=-1, keepdims=True) + NORM_EPS)
    xhat = xf * r
    dxhat = dy * g
    dx = r * (dxhat - xhat * jnp.mean(dxhat * xhat, axis=-1, keepdims=True))
    return dx, dy * xhat


def _sigmoid(t):
    return 1.0 / (1.0 + jnp.exp(-t))


def _log_sigmoid(t):
    return jnp.minimum(t, 0.0) - jnp.log(1.0 + jnp.exp(-jnp.abs(t)))


def _lane_sign():
    lane = lax.broadcasted_iota(jnp.int32, (1, LANES), 1)
    return jnp.where(lane < 64, -1.0, 1.0).astype(F32), lane


def _rope_lanes(lane):
    return (lane < 16) | ((lane >= 64) & (lane < 80))


def kernel(x, positions, ln_pre_mix, ln_post_mix, ln_pre_mlp, ln_post_mlp, w_in, b_in, q_a_norm, w_uq, kv_a_norm, w_uk, w_uv, w_o_mla, w_o_fox, w_out, w_ff1, w_ff2, loss_target, m_ln_pre_mix, m_ln_post_mix, m_ln_pre_mlp, m_ln_post_mlp, m_w_in, m_b_in, m_q_a_norm, m_w_uq, m_kv_a_norm, m_w_uk, m_w_uv, m_w_o_mla, m_w_o_fox, m_w_out, m_w_ff1, m_w_ff2, v_ln_pre_mix, v_ln_post_mix, v_ln_pre_mlp, v_ln_post_mlp, v_w_in, v_b_in, v_q_a_norm, v_w_uq, v_kv_a_norm, v_w_uk, v_w_uv, v_w_o_mla, v_w_o_fox, v_w_out, v_w_ff1, v_w_ff2):
    T = x.shape[1]
    x2 = x.reshape(T, D_MODEL)
    tgt = loss_target.reshape(T, D_MODEL)
    w_sh = dict(w_in=w_in, w_uq=w_uq, w_uk=w_uk, w_uv=w_uv, w_o_mla=w_o_mla, w_o_fox=w_o_fox, w_out=w_out,
                w_ff1=w_ff1, w_ff2=w_ff2)
    m_sh = dict(w_in=m_w_in, w_uq=m_w_uq, w_uk=m_w_uk, w_uv=m_w_uv, w_o_mla=m_w_o_mla, w_o_fox=m_w_o_fox,
                w_out=m_w_out, w_ff1=m_w_ff1, w_ff2=m_w_ff2)
    v_sh = dict(w_in=v_w_in, w_uq=v_w_uq, w_uk=v_w_uk, w_uv=v_w_uv, w_o_mla=v_w_o_mla, w_o_fox=v_w_o_fox,
                w_out=v_w_out, w_ff1=v_w_ff1, w_ff2=v_w_ff2)
    small_w = dict(ln_pre_mix=ln_pre_mix, ln_post_mix=ln_post_mix, ln_pre_mlp=ln_pre_mlp, ln_post_mlp=ln_post_mlp,
                   b_in=b_in, q_a_norm=q_a_norm, kv_a_norm=kv_a_norm)
    small_m = dict(ln_pre_mix=m_ln_pre_mix, ln_post_mix=m_ln_post_mix, ln_pre_mlp=m_ln_pre_mlp,
                   ln_post_mlp=m_ln_post_mlp, b_in=m_b_in, q_a_norm=m_q_a_norm, kv_a_norm=m_kv_a_norm)
    small_v = dict(ln_pre_mix=v_ln_pre_mix, ln_post_mix=v_ln_post_mix, ln_pre_mlp=v_ln_pre_mlp,
                   ln_post_mlp=v_ln_post_mlp, b_in=v_b_in, q_a_norm=v_q_a_norm, kv_a_norm=v_kv_a_norm)
    mla_scale = float((NOPE + ROPE) ** -0.5)
    fox_scale = float(FOX_D ** -0.5)

    W = _unpack_full(GROUP_A, _comm_call("gather", _pack_shards(GROUP_A, w_sh, BF16), "allgather_weights_a"))
    w_in_p = _perm_in_cols(W["w_in"])
    b_in_p = _perm_in_cols(b_in.astype(F32))
    w_uq_a = _aug_uq_cols(W["w_uq"])
    w_kv_a = jnp.concatenate([_aug_uk_cols(W["w_uk"]), W["w_uv"]], axis=1)
    g1, g2, g3, g4 = ln_pre_mix, ln_post_mix, ln_pre_mlp, ln_post_mlp
    gq, gkv = q_a_norm, kv_a_norm
    tq = min(ATTN_TILE, T)
    nq = T // tq

    (h,) = _rowwise(lambda xv, g: _rms(xv, g), [x2], [g1], [(D_MODEL, BF16)], name="pre_mix_norm")
    q_cols = jnp.ones((1, D_IN_PAD), F32).at[:, P_FQ:P_FQ + 512].set(fox_scale * LOG2E)
    z = _mm(h, w_in_p, bias=b_in_p, colscale=q_cols, tm=2048, name="in_proj")
    zs = _mm(h, w_in_p[:, P_SMALL:], bias=b_in_p[:, P_SMALL:], outs=(F32,), name="in_proj_small")

    def lora_norm(cq, ckv, a, b):
        return _rms(cq.astype(F32), a), _rms(ckv.astype(F32), b)

    cqn, ckvn = _rowwise(lora_norm, [(z, 256, P_CQ // 256), (z, 128, P_CKV // 128)], [gq, gkv],
                         [(Q_LORA, BF16), (KV_LORA, BF16)], name="lora_norm")
    q_x = _mm(cqn, w_uq_a, outs=(F32,), name="q_up")
    kv = _mm(ckvn, w_kv_a, tn=512, name="kv_up")

    half = ROPE // 2
    inv_freq = ROPE_THETA ** (-jnp.arange(half, dtype=F32) / half)
    inv128 = jnp.tile(inv_freq, LANES // half).reshape(1, LANES)
    pos_col = positions.reshape(T, 1).astype(F32)

    def rope_tables(p, f):
        ang = p * f
        return jnp.cos(ang), jnp.sin(ang)

    cos_t, sin_t = _rowwise(rope_tables, [pos_col], [inv128], [(LANES, F32), (LANES, F32)], name="rope_tables")

    def rope_fwd(qx, kn, s, cs, sn):
        sign, lane = _lane_sign()
        rope_l = _rope_lanes(lane)
        rot = lambda t: t * cs + pltpu.roll(t, 64, 1) * sn * sign
        k_rot = jnp.where(rope_l, rot(s), 0.0)
        qs, ks = [], []
        for hd in range(N_HEADS):
            qb = qx[:, LANES * hd:LANES * (hd + 1)]
            qs.append(jnp.where(rope_l, rot(qb), qb))
            ks.append(kn[:, LANES * hd:LANES * (hd + 1)].astype(F32) + k_rot)
        return jnp.concatenate(qs, axis=1) * (mla_scale * LOG2E), jnp.concatenate(ks, axis=1)

    q_am, k_am = _rowwise(rope_fwd, [q_x, (kv, 1024, 0), zs, cos_t, sin_t], [], [(1024, BF16), (1024, BF16)],
                          name="rope_fwd")
    f_cum = _seq_cumsum(zs, _log_sigmoid, reverse=False, name="forget_cumsum")

    def fox_aug(qf, kf, fc):
        lane = lax.broadcasted_iota(jnp.int32, (1, LANES), 1)
        qs, ks = [], []
        for hd in range(N_HEADS):
            qp = qf[:, LANES * (hd // 2):LANES * (hd // 2 + 1)].astype(F32)
            kp = kf[:, LANES * (hd // 2):LANES * (hd // 2 + 1)].astype(F32)
            if hd % 2:
                qp = pltpu.roll(qp, 64, 1)
                kp = pltpu.roll(kp, 64, 1)
            fb = jnp.broadcast_to(fc[:, S_FL + hd:S_FL + hd + 1] * (-LOG2E), qp.shape)
            hi = fb.astype(BF16).astype(F32)
            mid = (fb - hi).astype(BF16).astype(F32)
            lo = fb - hi - mid
            qs.append(jnp.where(lane < 64, qp, jnp.where(lane < 67, 1.0, 0.0)))
            ks.append(jnp.where(lane < 64, kp, jnp.where(lane == 64, hi, jnp.where(lane == 65, mid, jnp.where(
                lane == 66, lo, jnp.where(lane == 67, 1.0, 0.0))))))
        return jnp.concatenate(qs, axis=1), jnp.concatenate(ks, axis=1)

    q_af, k_af = _rowwise(fox_aug, [(z, 512, P_FQ // 512), (z, 512, P_FK // 512), f_cum], [],
                          [(1024, BF16), (1024, BF16)], name="fox_aug")
    qt4_m, qt4_f = _to_t4(q_am, tq), _to_t4(q_af, tq)
    vt4_m, vt4_f = _to_t4(kv[:, 1024:], tq), _to_t4(z[:, P_FV:P_FV + 512], tq)
    o_mla, lse_mla, gathered_b = _attn_fwd(qt4_m, k_am, vt4_m, name="mla_attn_fwd", tq=tq,
                                           gather=_pack_shards(GROUP_B, w_sh, BF16))
    W.update(_unpack_full(GROUP_B, gathered_b))
    o_fox, lse_fox = _attn_fwd(qt4_f, k_af, vt4_f, name="fox_attn_fwd", tq=tq)
    y_mla = _mm(o_mla, W["w_o_mla"], outs=(F32,), name="o_proj_mla")
    y_fox = _mm(o_fox, W["w_o_fox"], outs=(F32,), name="o_proj_fox")

    def gate_merge(ga, gb, ya, yb):
        return _sigmoid(ga.astype(F32)) * ya + _sigmoid(gb.astype(F32)) * yb

    (merged,) = _rowwise(gate_merge, [(z, 1024, 0), (z, 1024, 1), y_mla, y_fox], [], [(D_MODEL, BF16)],
                         name="gate_merge")
    mix = _mm(merged, W["w_out"], outs=(F32,), name="out_proj")

    def post_mix(xv, mv, a, b):
        x1v = xv + _rms(mv, a)
        return x1v, _rms(x1v, b)

    x1, h2 = _rowwise(post_mix, [x2, mix], [g2, g3], [(D_MODEL, F32), (D_MODEL, BF16)], name="post_mix_norm")

    def relu2(acc):
        r = jnp.maximum(acc, 0.0)
        return r * r, acc

    act, u = _mm(h2, W["w_ff1"], epi=relu2, outs=(BF16, BF16), tm=2048, name="ff1")
    mlp = _mm(act, W["w_ff2"], outs=(F32,), tm=1024, name="ff2")

    def loss_bwd(x1v, mv, tv, g):
        y = x1v + _rms(mv, g)
        d = y - tv
        dy = d * (1.0 / D_MODEL)
        dm, dg = _rms_bwd(mv, g, dy)
        return dy, dm, dg, d * d

    dy, dm, dg4, loss_cols = _rowwise(loss_bwd, [x1, mlp, tgt], [g4], [(D_MODEL, F32), (D_MODEL, BF16)],
                                      [D_MODEL, D_MODEL], name="loss_bwd")

    def relu2_bwd(acc, uv):
        return (acc * (2.0 * jnp.maximum(uv.astype(F32), 0.0)),)

    du = _mm(dm, W["w_ff2"].T, extras=(u,), epi=relu2_bwd, tm=2048, name="ff2_bwd")
    dw_ff2 = _mm(act, dm, ta=True, outs=(F32,), name="ff2_wgrad")
    dh2 = _mm(du, W["w_ff1"].T, outs=(F32,), tm=1024, name="ff1_bwd")
    dw_ff1 = _mm(h2, du, ta=True, outs=(F32,), name="ff1_wgrad")

    def post_mix_bwd(x1v, dh2v, dyv, mv, a, b):
        d3, dg3v = _rms_bwd(x1v, b, dh2v)
        dx1v = dyv + d3
        dmixv, dg2v = _rms_bwd(mv, a, dx1v)
        return dx1v, dmixv, dg3v, dg2v

    dx1, dmix, dg3, dg2 = _rowwise(post_mix_bwd, [x1, dh2, dy, mix], [g2, g3], [(D_MODEL, F32), (D_MODEL, BF16)],
                                   [D_MODEL, D_MODEL], name="post_mix_bwd")
    dmerged = _mm(dmix, W["w_out"].T, outs=(F32,), name="out_proj_bwd")
    dw_out = _mm(merged, dmix, ta=True, outs=(F32,), name="out_proj_wgrad")

    def gate_bwd(dmg, ga, gb, ya, yb):
        sa = _sigmoid(ga.astype(F32))
        sb = _sigmoid(gb.astype(F32))
        return dmg * sa, dmg * sb, dmg * ya * sa * (1.0 - sa), dmg * yb * sb * (1.0 - sb)

    dy_mla, dy_fox, dga, dgb = _rowwise(gate_bwd, [dmerged, (z, 1024, 0), (z, 1024, 1), y_mla, y_fox], [],
                                        [(D_MODEL, BF16)] * 4, name="gate_bwd")
    do_mla = _mm(dy_mla, W["w_o_mla"].T, name="o_proj_mla_bwd")
    do_fox = _mm(dy_fox, W["w_o_fox"].T, name="o_proj_fox_bwd")
    dw_o_mla = _mm(o_mla, dy_mla, ta=True, outs=(F32,), name="o_proj_mla_wgrad")
    dw_o_fox = _mm(o_fox, dy_fox, ta=True, outs=(F32,), name="o_proj_fox_wgrad")

    def head_dots(ov, dov):
        r = lax.broadcasted_iota(jnp.int32, (N_HEADS * HEAD_V, LANES), 0)
        cc = lax.broadcasted_iota(jnp.int32, (N_HEADS * HEAD_V, LANES), 1)
        sel = jnp.where((r // HEAD_V) == cc, 1.0, 0.0).astype(BF16)
        return _dot3(ov.astype(F32) * dov.astype(F32), sel, left=False)

    (dl_mla,) = _rowwise(head_dots, [o_mla, do_mla], [], [(LANES, F32)], name="mla_attn_delta")
    (dl_fox,) = _rowwise(head_dots, [o_fox, do_fox], [], [(LANES, F32)], name="fox_attn_delta")
    heads_row4 = lambda t: t[:, :N_HEADS].T.reshape(N_HEADS, nq, 1, tq)
    grads_b = dict(w_o_mla=dw_o_mla, w_o_fox=dw_o_fox, w_out=dw_out, w_ff1=dw_ff1, w_ff2=dw_ff2)
    dqt_m, dkt_m, dv_mla, pieces_b = _attn_bwd(qt4_m, k_am, kv, 1024 // LANES, _to_t4(do_mla, tq), lse_mla,
                                               heads_row4(dl_mla), mla_scale, NOPE + ROPE, name="mla_attn_bwd", tq=tq,
                                               exchange=_pack_full(GROUP_B, grads_b, BF16))
    dqt_f, dkt_f, dv_fox = _attn_bwd(qt4_f, k_af, z, P_FV // LANES, _to_t4(do_fox, tq), lse_fox, heads_row4(dl_fox),
                                     fox_scale, FOX_D, name="fox_attn_bwd", tq=tq)
    dq_fox = _from_t4(dqt_f[:, :, :FOX_D, :]).astype(BF16)
    dk_fox = _from_t4(dkt_f[:, :, :FOX_D, :]).astype(BF16)
    d_f = (dqt_f[:, :, 67, :] - dkt_f[:, :, 64, :]).reshape(N_HEADS, T).T
    d_f128 = jnp.pad(d_f, ((0, 0), (S_FL, LANES - S_FL - N_HEADS)))
    df_rev = _seq_cumsum(d_f128, lambda t: t, reverse=True, name="forget_cumsum_bwd")

    def rope_bwd(dqa, dka, dvm, dfr, s, cs, sn):
        sign, lane = _lane_sign()
        rope_l = _rope_lanes(lane)
        rot_t = lambda t: t * cs - pltpu.roll(t, 64, 1) * sn * sign
        dqs = []
        dk_rot = None
        for hd in range(N_HEADS):
            blk = dqa[:, LANES * hd:LANES * (hd + 1)]
            dqs.append(jnp.where(rope_l, rot_t(blk), blk))
            blk = dka[:, LANES * hd:LANES * (hd + 1)]
            dk_rot = blk if dk_rot is None else dk_rot + blk
        dfl = dfr * _sigmoid(-s)
        small = jnp.where(rope_l, rot_t(dk_rot), jnp.where((lane >= S_FL) & (lane < S_FL + N_HEADS), dfl, 0.0))
        return jnp.concatenate(dqs, axis=1), jnp.concatenate([dka, dvm.astype(F32)], axis=1), small

    dq_b, dkv, d_small = _rowwise(rope_bwd, [_from_t4(dqt_m), _from_t4(dkt_m), dv_mla, df_rev, zs, cos_t, sin_t], [],
                                  [(1024, BF16), (1536, BF16), (LANES, BF16)], name="rope_bwd")
    dcqn = _mm(dq_b, w_uq_a.T, outs=(F32,), name="q_up_bwd")
    dw_uq_a = _mm(cqn, dq_b, ta=True, outs=(F32,), name="q_up_wgrad")
    dckvn = _mm(dkv, w_kv_a.T, outs=(F32,), tk=512, name="kv_up_bwd")
    dw_kv_a = _mm(ckvn, dkv, ta=True, outs=(F32,), tn=512, name="kv_up_wgrad")

    def lora_norm_bwd(cq, ckv, dq_, dkv_, a, b):
        d1, dga_ = _rms_bwd(cq.astype(F32), a, dq_)
        d2, dgb_ = _rms_bwd(ckv.astype(F32), b, dkv_)
        return d1, d2, dga_, dgb_

    dcq, dckv, dgq, dgkv = _rowwise(lora_norm_bwd, [(z, 256, P_CQ // 256), (z, 128, P_CKV // 128), dcqn, dckvn],
                                    [gq, gkv], [(Q_LORA, BF16), (KV_LORA, BF16)], [Q_LORA, KV_LORA],
                                    name="lora_norm_bwd")
    dz = jnp.concatenate([dga, dgb, dq_fox, dk_fox, dv_fox, dcq, dckv, d_small], -1)
    (db_in_p,) = _rowwise(lambda t: (t,), [dz], [], [], [D_IN_PAD], name="in_bias_grad")
    dh = _mm(dz, w_in_p.T, outs=(F32,), tm=1024, name="in_proj_bwd")
    dw_in_p = _mm(h, dz, ta=True, outs=(F32,), name="in_proj_wgrad")

    def pre_mix_bwd(xv, dhv, dx1v, g):
        d, dg = _rms_bwd(xv, g, dhv)
        return dx1v + d, dg

    grad_x, dg1 = _rowwise(pre_mix_bwd, [x2, dh, dx1], [g1], [(D_MODEL, F32)], [D_MODEL], name="pre_mix_bwd")

    grads_a = dict(w_in=_unperm_in_cols(dw_in_p), w_uq=_unaug_uq_cols(dw_uq_a),
                   w_uk=_unaug_uk_cols(dw_kv_a[:, :1024]), w_uv=dw_kv_a[:, 1024:])
    pieces_a = _comm_call("exchange", _pack_full(GROUP_A, grads_a, BF16), "exchange_grad_pieces_a")
    grad_sh, delta_sh, newm_sh, newv_sh = {}, {}, {}, {}
    for group, pieces, tag in ((GROUP_A, pieces_a, "a"), (GROUP_B, pieces_b, "b")):
        packed = _sum_adamw(pieces, _pack_shards(group, w_sh, F32), _pack_shards(group, m_sh, F32),
                            _pack_shards(group, v_sh, F32), name="sum_pieces_adamw_" + tag)
        for dst, arr in zip((grad_sh, delta_sh, newm_sh, newv_sh), packed):
            dst.update(_unpack_shards(group, arr))

    small_part = _pack_small(dict(ln_pre_mix=dg1, ln_post_mix=dg2, ln_pre_mlp=dg3, ln_post_mlp=dg4,
                                  b_in=_unperm_in_cols(db_in_p), q_a_norm=dgq, kv_a_norm=dgkv), extra=loss_cols)
    sg, sd, sm, sv, loss_blk = _small_allreduce_adamw(small_part, _pack_small(small_w), _pack_small(small_m),
                                                      _pack_small(small_v))
    grad_sm, delta_sm, newm_sm, newv_sm = (_unpack_small(t) for t in (sg, sd, sm, sv))
    loss = loss_blk[0, 0]

    order = ["ln_pre_mix", "ln_post_mix", "ln_pre_mlp", "ln_post_mlp", "w_in", "b_in", "q_a_norm", "w_uq",
             "kv_a_norm", "w_uk", "w_uv", "w_o_mla", "w_o_fox", "w_out", "w_ff1", "w_ff2"]

    def pick(sm_d, sh_d):
        return [sm_d[n] if n in sm_d else sh_d[n] for n in order]

    return (loss, grad_x.reshape(1, T, D_MODEL), *pick(grad_sm, grad_sh), *pick(delta_sm, delta_sh),
            *pick(newm_sm, newm_sh), *pick(newv_sm, newv_sh))
```

```python
import numpy as np
import jax
import jax.numpy as jnp
from jax import lax
from jax.experimental import pallas as pl
from jax.experimental.pallas import tpu as pltpu

F32 = jnp.float32
BF16 = jnp.bfloat16
MESH = pl.DeviceIdType.MESH

D_MODEL = 1024
N_HEADS = 8
Q_LORA = 256
KV_LORA = 128
NOPE = 64
ROPE = 32
HEAD_V = 64
FOX_D = 64
D_FF = 4096
D_IN = 4008
D_IN_PAD = 4096
ROPE_THETA = 10000.0
NORM_EPS = 1e-6
N_DEV = 8

ADAM_LR = 0.001
ADAM_B1 = 0.9
ADAM_B2 = 0.999
ADAM_EPS = 1e-08
ADAM_WD = 0.01
ADAM_STEP = 10

LANES = 128
ROW_TILE = 512
ATTN_TILE = 512
ATTN_CHUNK = 256
VMEM_LIMIT = 48 * 1024 * 1024
ATTN_BWD_VMEM_LIMIT = 58 * 1024 * 1024

P_GA, P_GB, P_FQ, P_FK, P_FV, P_CQ, P_CKV, P_SMALL = 0, 1024, 2048, 2560, 3072, 3584, 3840, 3968
S_FL = 32

SHARDED = (
    ("w_in", 1024, 501, True), ("w_uq", 256, 96, True), ("w_uk", 128, 64, True), ("w_uv", 128, 64, True),
    ("w_o_mla", 512, 128, True), ("w_o_fox", 512, 128, True), ("w_out", 128, 1024, False),
    ("w_ff1", 1024, 512, True), ("w_ff2", 512, 1024, False),
)
GROUP_A = SHARDED[:4]
GROUP_B = SHARDED[4:]
SMALL_ROWS = 80
SMALL_LOSS_ROW = 72


def _pack_rows(group):
    return -(-sum(r * c for _, r, c, _ in group) // (LANES * 64)) * 64


def _cparams(sem=None):
    return pltpu.CompilerParams(dimension_semantics=sem, vmem_limit_bytes=VMEM_LIMIT)


def _mm(a, b, *, name, ta=False, bias=None, colscale=None, extras=(), epi=None, outs=(BF16,), t4=None, tm=ROW_TILE,
        tn=1024, tk=1024):
    t4 = (False,) * len(outs) if t4 is None else t4
    if ta:
        K, M = a.shape
        tm = min(1024, M)
    else:
        M, K = a.shape
        tm = min(tm, M)
    tk = min(tk, K)
    N = b.shape[1]
    tn = min(tn, N)
    nk = K // tk
    n_ex = len(extras)
    has_bias = bias is not None
    has_scale = colscale is not None

    def body(*refs):
        a_ref, b_ref = refs[0], refs[1]
        pos = 2
        bias_ref = scale_ref = None
        if has_bias:
            bias_ref = refs[pos]
            pos += 1
        if has_scale:
            scale_ref = refs[pos]
            pos += 1
        ex_refs = refs[pos:pos + n_ex]
        pos += n_ex
        o_refs = refs[pos:pos + len(outs)]
        pos += len(outs)
        av = a_ref[...].astype(BF16)
        bv = b_ref[...].astype(BF16)
        if ta:
            part = lax.dot_general(av, bv, (((0,), (0,)), ((), ())), preferred_element_type=F32)
        else:
            part = jnp.dot(av, bv, preferred_element_type=F32)

        def finish(acc):
            if has_bias:
                acc = acc + bias_ref[...]
            if has_scale:
                acc = acc * scale_ref[...]
            res = (acc,) if epi is None else epi(acc, *[r[...] for r in ex_refs])
            for o_ref, val, t in zip(o_refs, res, t4):
                if t:
                    _t4_store(o_ref, val)
                else:
                    o_ref[...] = val.astype(o_ref.dtype)

        if nk == 1:
            finish(part)
        else:
            acc_ref = refs[pos]
            k = pl.program_id(2)

            @pl.when(k == 0)
            def _():
                acc_ref[...] = part

            @pl.when(k > 0)
            def _():
                acc_ref[...] += part

            @pl.when(k == nk - 1)
            def _():
                finish(acc_ref[...])

    if ta:
        a_spec = pl.BlockSpec((tk, tm), lambda i, j, k: (k, i))
    else:
        a_spec = pl.BlockSpec((tm, tk), lambda i, j, k: (i, k))
    in_specs = [a_spec, pl.BlockSpec((tk, tn), lambda i, j, k: (k, j))]
    args = [a, b]
    for row in (bias, colscale):
        if row is not None:
            in_specs.append(pl.BlockSpec((1, tn), lambda i, j, k: (0, j)))
            args.append(row)
    for e in extras:
        in_specs.append(pl.BlockSpec((tm, tn), lambda i, j, k: (i, j)))
        args.append(e)
    res = pl.pallas_call(
        body, name=name, grid=(M // tm, N // tn, nk),
        in_specs=in_specs,
        out_specs=[pl.BlockSpec((N // LANES, 1, LANES, tm), lambda i, j, k: (0, i, 0, 0)) if t
                   else pl.BlockSpec((tm, tn), lambda i, j, k: (i, j)) for t in t4],
        out_shape=[jax.ShapeDtypeStruct((N // LANES, M // tm, LANES, tm) if t else (M, N), dt)
                   for dt, t in zip(outs, t4)],
        scratch_shapes=[pltpu.VMEM((tm, tn), F32)] if nk > 1 else [],
        compiler_params=_cparams(("parallel", "parallel", "arbitrary")),
    )(*args)
    return res[0] if len(outs) == 1 else res


def _t4_store(o_ref, val):
    for c in range(o_ref.shape[0]):
        o_ref[c, 0] = val[:, c * LANES:(c + 1) * LANES].astype(F32).T.astype(o_ref.dtype)


def _rowwise(fn, rows, bcasts, outs, accs=(), *, name, tm=ROW_TILE):
    t4_in = [isinstance(r, tuple) and isinstance(r[0], str) for r in rows]
    T = [r[1].shape[1] * r[1].shape[3] if t else (r[0] if isinstance(r, tuple) else r).shape[0]
         for r, t in zip(rows, t4_in)][0]
    tm = min(tm, T)
    arrs, specs = [], []
    for r, t in zip(rows, t4_in):
        if t:
            arr = r[1]
            specs.append(pl.BlockSpec((arr.shape[0], 1, LANES, tm), lambda i: (0, i, 0, 0)))
        elif isinstance(r, tuple):
            arr, w, cb = r
            specs.append(pl.BlockSpec((tm, w), lambda i, cb=cb: (i, cb)))
        else:
            arr = r
            specs.append(pl.BlockSpec((tm, arr.shape[1]), lambda i: (i, 0)))
        arrs.append(arr)
    n_rows = len(arrs)
    for b in bcasts:
        arrs.append(b)
        specs.append(pl.BlockSpec(b.shape, lambda i: (0, 0)))
    n_in, n_out = len(arrs), len(outs)
    t4_out = [len(o) == 3 for o in outs]

    def body(*refs):
        vals = []
        for k, r in enumerate(refs[:n_in]):
            if k < n_rows and t4_in[k]:
                vals.append(jnp.concatenate([r[c, 0].astype(F32).T for c in range(r.shape[0])], axis=1))
            else:
                vals.append(r[...])
        res = fn(*vals)
        if not isinstance(res, (tuple, list)):
            res = (res,)
        for o_ref, val, t in zip(refs[n_in:n_in + n_out], res[:n_out], t4_out):
            if t:
                _t4_store(o_ref, val)
            else:
                o_ref[...] = val.astype(o_ref.dtype)
        i = pl.program_id(0)
        for a_ref, val in zip(refs[n_in + n_out:], res[n_out:]):
            col = jnp.sum(val.astype(F32), axis=0, keepdims=True)

            @pl.when(i == 0)
            def _(a_ref=a_ref, col=col):
                a_ref[...] = col

            @pl.when(i > 0)
            def _(a_ref=a_ref, col=col):
                a_ref[...] += col

    res = pl.pallas_call(
        body, name=name, grid=(T // tm,),
        in_specs=specs,
        out_specs=[pl.BlockSpec((o[0] // LANES, 1, LANES, tm), lambda i: (0, i, 0, 0)) if t
                   else pl.BlockSpec((tm, o[0]), lambda i: (i, 0)) for o, t in zip(outs, t4_out)]
        + [pl.BlockSpec((1, c), lambda i: (0, 0)) for c in accs],
        out_shape=[jax.ShapeDtypeStruct((o[0] // LANES, T // tm, LANES, tm) if t else (T, o[0]), o[1])
                   for o, t in zip(outs, t4_out)]
        + [jax.ShapeDtypeStruct((1, c), F32) for c in accs],
        compiler_params=_cparams(("arbitrary",)),
    )(*arrs)
    return res


def _dot3(x, sel, left):
    hi = x.astype(BF16)
    r1 = x - hi.astype(F32)
    mid = r1.astype(BF16)
    lo = (r1 - mid.astype(F32)).astype(BF16)
    if left:
        d = lambda t: jnp.dot(sel, t, preferred_element_type=F32)
    else:
        d = lambda t: jnp.dot(t, sel, preferred_element_type=F32)
    return d(hi) + d(mid) + d(lo)


def _seq_cumsum(x, fn, *, reverse, name, tm=256):
    T, C = x.shape
    tm = min(tm, T)
    n = T // tm

    def body(x_ref, o_ref, carry_ref):
        i = pl.program_id(0)

        @pl.when(i == 0)
        def _():
            carry_ref[...] = jnp.zeros_like(carry_ref)

        v = fn(x_ref[...])
        r = lax.broadcasted_iota(jnp.int32, (tm, tm), 0)
        c = lax.broadcasted_iota(jnp.int32, (tm, tm), 1)
        tri = jnp.where((r <= c) if reverse else (r >= c), 1.0, 0.0).astype(BF16)
        carry = carry_ref[0:1, :]
        o_ref[...] = _dot3(v, tri, left=True) + carry
        carry_ref[0:1, :] = carry + jnp.sum(v, axis=0, keepdims=True)

    idx = (lambda i: (n - 1 - i, 0)) if reverse else (lambda i: (i, 0))
    return pl.pallas_call(
        body, name=name, grid=(n,),
        in_specs=[pl.BlockSpec((tm, C), idx)],
        out_specs=pl.BlockSpec((tm, C), idx),
        out_shape=jax.ShapeDtypeStruct((T, C), F32),
        scratch_shapes=[pltpu.VMEM((8, C), F32)],
        compiler_params=_cparams(("arbitrary",)),
    )(x)


_NT = (((1,), (1,)), ((), ()))
LOG2E = 1.4426950408889634


def _head_mask(width, e):
    lane = lax.broadcasted_iota(jnp.int32, (1, width), 1)
    return (lane >= 64 * e) & (lane < 64 * (e + 1))


def _attn_fwd(qt4, k_aug, vt4, *, name, tq, gather=None):
    nq = qt4.shape[1]
    T = nq * tq
    cw = min(ATTN_CHUNK, tq)
    comm = gather is not None

    def body(*refs):
        refs = list(refs)
        q_ref, k_ref, v_ref = refs[:3]
        pos = 3
        if comm:
            c_src = refs[pos]
            pos += 1
        o_ref, lse_ref = refs[pos:pos + 2]
        pos += 2
        if comm:
            c_dst = refs[pos]
            pos += 1
        m_s, l_s, acc_s = refs[pos:pos + 2], refs[pos + 2:pos + 4], refs[pos + 4:pos + 6]
        st_a, st_b = refs[pos + 6:pos + 8]
        c_sems = refs[pos + 8:]
        i = pl.program_id(1)
        if comm:
            @pl.when((pl.program_id(0) == 0) & (i == 0))
            def _():
                _comm_start("gather", c_src, c_dst, *c_sems)

        for e in range(2):
            m_s[e][...] = jnp.full(m_s[e].shape, -jnp.inf, F32)
            l_s[e][...] = jnp.zeros_like(l_s[e])
            acc_s[e][...] = jnp.zeros_like(acc_s[e])
        qt = [q_ref[e, 0] for e in range(2)]

        work = [(e, slice(c * cw, (c + 1) * cw)) for e in range(2) for c in range(tq // cw)]

        def scores(j, buf):
            kj = k_ref[pl.ds(pl.multiple_of(j * tq, tq), tq), :]
            for n, (e, cs) in enumerate(work):
                buf[n] = jnp.dot(kj[:, e * LANES:(e + 1) * LANES], qt[e][:, cs], preferred_element_type=F32)

        def step(j, buf, masked):
            vtj = v_ref[0, j]
            for n, (e, cs) in enumerate(work):
                st = buf[n]
                if masked:
                    r = lax.broadcasted_iota(jnp.int32, (tq, cw), 0)
                    cc = lax.broadcasted_iota(jnp.int32, (tq, cw), 1) + cs.start
                    st = jnp.where(cc >= r, st, -jnp.inf)
                m_prev = m_s[e][:, cs]
                m_new = jnp.maximum(m_prev, jnp.max(st, axis=0, keepdims=True))
                alpha = jnp.exp2(m_prev - m_new)
                pt = jnp.exp2(st - m_new)
                l_s[e][:, cs] = alpha * l_s[e][:, cs] + jnp.sum(pt, axis=0, keepdims=True)
                acc_s[e][:, cs] = alpha * acc_s[e][:, cs] + jnp.dot(vtj[e * HEAD_V:(e + 1) * HEAD_V, :], pt.astype(BF16),
                                                                    preferred_element_type=F32)
                m_s[e][:, cs] = m_new

        def two_tiles(p, carry):
            scores(2 * p + 1, st_b)
            step(2 * p, st_a, False)
            scores(2 * p + 2, st_a)
            step(2 * p + 1, st_b, False)
            return carry

        scores(0, st_a)
        lax.fori_loop(0, i // 2, two_tiles, 0)

        @pl.when(i % 2 == 0)
        def _():
            step(i, st_a, True)

        @pl.when(i % 2 == 1)
        def _():
            scores(i, st_b)
            step(i - 1, st_a, False)
            step(i, st_b, True)

        ot = jnp.concatenate([acc_s[e][...] / l_s[e][...] for e in range(2)], axis=0)
        o_ref[...] = ot.T.astype(o_ref.dtype)
        for e in range(2):
            lse_ref[e, 0] = m_s[e][...] + jnp.log(l_s[e][...]) * LOG2E
        if comm:
            @pl.when((pl.program_id(0) == N_HEADS // 2 - 1) & (i == nq - 1))
            def _():
                _comm_wait("gather", c_src, c_dst, *c_sems)

    in_specs = [
        pl.BlockSpec((2, 1, LANES, tq), lambda hp, i: (hp, i, 0, 0)),
        pl.BlockSpec((T, 2 * LANES), lambda hp, i: (0, hp)),
        pl.BlockSpec((1, nq, LANES, tq), lambda hp, i: (hp, 0, 0, 0)),
    ]
    args = [qt4, k_aug, vt4]
    out_specs = [pl.BlockSpec((tq, LANES), lambda hp, i: (i, hp)),
                 pl.BlockSpec((2, 1, 1, tq), lambda hp, i: (hp, i, 0, 0))]
    out_shape = [jax.ShapeDtypeStruct((T, N_HEADS * HEAD_V), BF16), jax.ShapeDtypeStruct((N_HEADS, nq, 1, tq), F32)]
    scratch = ([pltpu.VMEM((1, tq), F32)] * 4 + [pltpu.VMEM((HEAD_V, tq), F32)] * 2
               + [pltpu.VMEM((2 * tq // cw, tq, cw), F32)] * 2)
    if comm:
        in_specs.append(pl.BlockSpec(memory_space=pl.ANY))
        args.append(gather)
        out_specs.append(pl.BlockSpec(memory_space=pl.ANY))
        out_shape.append(jax.ShapeDtypeStruct((N_DEV,) + gather.shape, gather.dtype))
        scratch += _comm_sems()
    return pl.pallas_call(
        body, name=name, grid=(N_HEADS // 2, nq),
        in_specs=in_specs, out_specs=out_specs, out_shape=out_shape, scratch_shapes=scratch,
        compiler_params=_cparams(("arbitrary", "arbitrary")),
    )(*args)


def _attn_bwd(qt4, k_aug, v, vcb, dot4, lse_row, dl_row, scale, own_rows, *, name, tq, exchange=None):
    nq = qt4.shape[1]
    T = nq * tq
    cw = min(ATTN_CHUNK, tq)
    comm = exchange is not None

    def body(*refs):
        refs = list(refs)
        k_ref, v_ref, q_ref, do_ref, lse_ref, dl_ref = refs[:6]
        pos = 6
        if comm:
            c_src = refs[pos]
            pos += 1
        dq_ref, dk_ref, dv_ref = refs[pos:pos + 3]
        pos += 3
        if comm:
            c_dst = refs[pos]
            pos += 1
        dk_s, dv_s = refs[pos:pos + 2], refs[pos + 2:pos + 4]
        c_sems = refs[pos + 4:]
        j = pl.program_id(1)
        if comm:
            @pl.when((pl.program_id(0) == 0) & (j == 0))
            def _():
                _comm_start("exchange", c_src, c_dst, *c_sems)

        @pl.when(j == 0)
        def _():
            dq_ref[...] = jnp.zeros_like(dq_ref)

        kj = k_ref[...]
        vj = v_ref[...]
        ka = [kj[:, e * LANES:(e + 1) * LANES] for e in range(2)]
        row = lax.broadcasted_iota(jnp.int32, (LANES, 1), 0)
        own = row < own_rows
        kat = [(ka[e].astype(F32).T * jnp.where(own, scale, 1.0)).astype(BF16) for e in range(2)]
        vm = [jnp.where(_head_mask(LANES, e), vj, jnp.zeros_like(vj)) for e in range(2)]
        for e in range(2):
            dk_s[e][...] = jnp.zeros_like(dk_s[e])
            dv_s[e][...] = jnp.zeros_like(dv_s[e])

        def step(i, masked):
            dot_i = do_ref[0, i]
            qts = [q_ref[e, i] for e in range(2)]
            lse = [lse_ref[e, i] for e in range(2)]
            dl = [dl_ref[e, i] for e in range(2)]
            work = [(e, slice(c * cw, (c + 1) * cw)) for e in range(2) for c in range(tq // cw)]
            scores = lambda e, cs: (jnp.dot(ka[e], qts[e][:, cs], preferred_element_type=F32),
                                    jnp.dot(vm[e], dot_i[:, cs], preferred_element_type=F32))
            nxt = scores(*work[0])
            dqs = [[], []]
            for n, (e, cs) in enumerate(work):
                st, dpt = nxt
                if n + 1 < len(work):
                    nxt = scores(*work[n + 1])
                if masked:
                    r = lax.broadcasted_iota(jnp.int32, (tq, cw), 0)
                    cc = lax.broadcasted_iota(jnp.int32, (tq, cw), 1) + cs.start
                    st = jnp.where(cc >= r, st, -jnp.inf)
                pt = jnp.exp2(st - lse[e][:, cs])
                dv_s[e][...] += lax.dot_general(dot_i[e * HEAD_V:(e + 1) * HEAD_V, cs], pt.astype(BF16), _NT,
                                                preferred_element_type=F32)
                dsb = (pt * (dpt - dl[e][:, cs])).astype(BF16)
                dk_s[e][...] += lax.dot_general(qts[e][:, cs], dsb, _NT, preferred_element_type=F32)
                dqs[e].append(jnp.dot(kat[e], dsb, preferred_element_type=F32))
            for e in range(2):
                dq_ref[e, i] += jnp.concatenate(dqs[e], axis=1)

        def loop_body(i, carry):
            step(i, False)
            return carry

        step(j, True)
        lax.fori_loop(j + 1, nq, loop_body, 0)
        for e in range(2):
            dk_ref[e, 0] = dk_s[e][...] * jnp.where(own, 1.0 / LOG2E, 1.0)
        dv_ref[...] = jnp.concatenate([dv_s[0][...], dv_s[1][...]], axis=0).T.astype(dv_ref.dtype)
        if comm:
            @pl.when((pl.program_id(0) == N_HEADS // 2 - 1) & (j == nq - 1))
            def _():
                _comm_wait("exchange", c_src, c_dst, *c_sems)

    row4 = pl.BlockSpec((2, nq, 1, tq), lambda hp, j: (hp, 0, 0, 0))
    in_specs = [
        pl.BlockSpec((tq, 2 * LANES), lambda hp, j: (j, hp)),
        pl.BlockSpec((tq, LANES), lambda hp, j: (j, vcb + hp)),
        pl.BlockSpec((2, nq, LANES, tq), lambda hp, j: (hp, 0, 0, 0)),
        pl.BlockSpec((1, nq, LANES, tq), lambda hp, j: (hp, 0, 0, 0)),
        row4, row4,
    ]
    args = [k_aug, v, qt4, dot4, lse_row, dl_row]
    out_specs = [pl.BlockSpec((2, nq, LANES, tq), lambda hp, j: (hp, 0, 0, 0)),
                 pl.BlockSpec((2, 1, LANES, tq), lambda hp, j: (hp, j, 0, 0)),
                 pl.BlockSpec((tq, LANES), lambda hp, j: (j, hp))]
    out_shape = [jax.ShapeDtypeStruct((N_HEADS, nq, LANES, tq), F32), jax.ShapeDtypeStruct((N_HEADS, nq, LANES, tq), F32),
                 jax.ShapeDtypeStruct((T, N_HEADS * HEAD_V), BF16)]
    scratch = [pltpu.VMEM((LANES, tq), F32)] * 2 + [pltpu.VMEM((HEAD_V, tq), F32)] * 2
    if comm:
        in_specs.append(pl.BlockSpec(memory_space=pl.ANY))
        args.append(exchange)
        out_specs.append(pl.BlockSpec(memory_space=pl.ANY))
        out_shape.append(jax.ShapeDtypeStruct(exchange.shape, exchange.dtype))
        scratch += _comm_sems()
    return pl.pallas_call(
        body, name=name, grid=(N_HEADS // 2, nq),
        in_specs=in_specs, out_specs=out_specs, out_shape=out_shape, scratch_shapes=scratch,
        compiler_params=pltpu.CompilerParams(dimension_semantics=("arbitrary", "arbitrary"),
                                             vmem_limit_bytes=ATTN_BWD_VMEM_LIMIT),
    )(*args)


def _peers():
    x, y, c = lax.axis_index("x"), lax.axis_index("y"), lax.axis_index("c")
    me = 4 * x + 2 * y + c
    out = []
    for k in range(1, N_DEV):
        px = (1 - x) if (k & 4) else x
        py = (1 - y) if (k & 2) else y
        pc = (1 - c) if (k & 1) else c
        out.append(((px, py, pc), 4 * px + 2 * py + pc))
    return me, out


def _comm_sems():
    return [pltpu.SemaphoreType.DMA((N_DEV - 1,)), pltpu.SemaphoreType.DMA((N_DEV - 1,)), pltpu.SemaphoreType.DMA]


def _comm_copies(kind, src_ref, dst_ref, send_sems, recv_sems, local_sem):
    me, peers = _peers()
    local = pltpu.make_async_copy(src_ref if kind == "gather" else src_ref.at[me], dst_ref.at[me], local_sem)
    sends, recvs = [], []
    for k, (dev, lin) in enumerate(peers):
        src = src_ref if kind == "gather" else src_ref.at[lin]
        sends.append(pltpu.make_async_remote_copy(src_ref=src, dst_ref=dst_ref.at[me], send_sem=send_sems.at[k],
                                                  recv_sem=recv_sems.at[k], device_id=dev, device_id_type=MESH))
        recvs.append(pltpu.make_async_remote_copy(src_ref=src, dst_ref=dst_ref.at[lin], send_sem=send_sems.at[k],
                                                  recv_sem=recv_sems.at[k], device_id=dev, device_id_type=MESH))
    return local, sends, recvs


def _comm_start(kind, src_ref, dst_ref, send_sems, recv_sems, local_sem):
    local, sends, _ = _comm_copies(kind, src_ref, dst_ref, send_sems, recv_sems, local_sem)
    local.start()
    for cp in sends:
        cp.start()


def _comm_wait(kind, src_ref, dst_ref, send_sems, recv_sems, local_sem):
    local, sends, recvs = _comm_copies(kind, src_ref, dst_ref, send_sems, recv_sems, local_sem)
    for cp in recvs:
        cp.wait_recv()
    for cp in sends:
        cp.wait_send()
    local.wait()


def _comm_call(kind, src, name):
    def body(x_ref, o_ref, send_sems, recv_sems, local_sem):
        _comm_start(kind, x_ref, o_ref, send_sems, recv_sems, local_sem)
        _comm_wait(kind, x_ref, o_ref, send_sems, recv_sems, local_sem)

    shape = (N_DEV,) + src.shape if kind == "gather" else src.shape
    return pl.pallas_call(
        body, name=name,
        in_specs=[pl.BlockSpec(memory_space=pl.ANY)],
        out_specs=pl.BlockSpec(memory_space=pl.ANY),
        out_shape=jax.ShapeDtypeStruct(shape, src.dtype),
        scratch_shapes=_comm_sems(),
    )(src)


def _adamw(w, g, m, v):
    m2 = ADAM_B1 * m + (1.0 - ADAM_B1) * g
    v2 = ADAM_B2 * v + (1.0 - ADAM_B2) * (g * g)
    m_hat = m2 / (1.0 - ADAM_B1 ** ADAM_STEP)
    v_hat = v2 / (1.0 - ADAM_B2 ** ADAM_STEP)
    delta = -ADAM_LR * (m_hat / (jnp.sqrt(v_hat) + ADAM_EPS) + ADAM_WD * w)
    return delta, m2, v2


def _sum_adamw(pieces, w, m, v, *, name):
    R = w.shape[0]
    tr = R // 4 if R % 64 == 0 else R

    def body(p_ref, w_ref, m_ref, v_ref, g_ref, d_ref, m2_ref, v2_ref):
        g = p_ref[0].astype(F32)
        for s in range(1, N_DEV):
            g = g + p_ref[s].astype(F32)
        delta, m2, v2 = _adamw(w_ref[...], g, m_ref[...], v_ref[...])
        g_ref[...] = g
        d_ref[...] = delta
        m2_ref[...] = m2
        v2_ref[...] = v2

    row = pl.BlockSpec((tr, LANES), lambda i: (i, 0))
    return pl.pallas_call(
        body, name=name, grid=(R // tr,),
        in_specs=[pl.BlockSpec((N_DEV, tr, LANES), lambda i: (0, i, 0)), row, row, row],
        out_specs=[row, row, row, row],
        out_shape=[jax.ShapeDtypeStruct((R, LANES), F32)] * 4,
        compiler_params=_cparams(("parallel",)),
    )(pieces, w, m, v)


def _small_allreduce_adamw(part, w, m, v):
    shape = part.shape

    def body(p_ref, w_ref, m_ref, v_ref, g_ref, d_ref, m2_ref, v2_ref, loss_ref, gath, send_sems, recv_sems):
        me, peers = _peers()
        gath[me] = p_ref[...]
        sends = []
        for k, (dev, _) in enumerate(peers):
            cp = pltpu.make_async_remote_copy(src_ref=p_ref, dst_ref=gath.at[me], send_sem=send_sems.at[k],
                                              recv_sem=recv_sems.at[k], device_id=dev, device_id_type=MESH)
            cp.start()
            sends.append(cp)
        for k, (dev, lin) in enumerate(peers):
            pltpu.make_async_remote_copy(src_ref=p_ref, dst_ref=gath.at[lin], send_sem=send_sems.at[k],
                                         recv_sem=recv_sems.at[k], device_id=dev, device_id_type=MESH).wait_recv()
        for cp in sends:
            cp.wait_send()
        g = gath[0]
        for s in range(1, N_DEV):
            g = g + gath[s]
        delta, m2, v2 = _adamw(w_ref[...], g, m_ref[...], v_ref[...])
        g_ref[...] = g
        d_ref[...] = delta
        m2_ref[...] = m2
        v2_ref[...] = v2
        sq = jnp.sum(g[SMALL_LOSS_ROW:SMALL_LOSS_ROW + 8, :], axis=1, keepdims=True)
        tot = jnp.sum(sq, axis=0, keepdims=True) * (0.5 / D_MODEL)
        loss_ref[...] = jnp.broadcast_to(tot, loss_ref.shape)

    vm = pl.BlockSpec(memory_space=pltpu.VMEM)
    return pl.pallas_call(
        body, name="small_allreduce_adamw",
        in_specs=[vm, vm, vm, vm],
        out_specs=[vm, vm, vm, vm, vm],
        out_shape=[jax.ShapeDtypeStruct(shape, F32)] * 4 + [jax.ShapeDtypeStruct((8, LANES), F32)],
        scratch_shapes=[pltpu.VMEM((N_DEV,) + shape, F32),
                        pltpu.SemaphoreType.DMA((N_DEV - 1,)), pltpu.SemaphoreType.DMA((N_DEV - 1,))],
    )(part, w, m, v)


def _perm_in_cols(w):
    z = lambda n: jnp.zeros(w.shape[:-1] + (n,), w.dtype)
    small = jnp.concatenate([w[..., 384:400], z(16), w[..., 1952:1960], z(24), w[..., 400:416], z(48)], -1)
    return jnp.concatenate([w[..., 1960:2984], w[..., 2984:4008], w[..., 416:928], w[..., 928:1440],
                            w[..., 1440:1952], w[..., 0:256], w[..., 256:384], small], -1)


def _unperm_in_cols(wp):
    s = wp[..., P_SMALL:]
    return jnp.concatenate([wp[..., P_CQ:P_CQ + 256], wp[..., P_CKV:P_CKV + 128], s[..., 0:16], s[..., 64:80],
                            wp[..., P_FQ:P_FQ + 512], wp[..., P_FK:P_FK + 512], wp[..., P_FV:P_FV + 512],
                            s[..., S_FL:S_FL + 8], wp[..., P_GA:P_GA + 1024], wp[..., P_GB:P_GB + 1024]], -1)


def _aug_uq_cols(w):
    r = w.shape[0]
    w3 = w.reshape(r, N_HEADS, NOPE + ROPE)
    z = jnp.zeros((r, N_HEADS, 32), w.dtype)
    return jnp.concatenate([w3[:, :, 64:80], w3[:, :, 0:48], w3[:, :, 80:96], w3[:, :, 48:64], z], -1).reshape(r, 1024)


def _unaug_uq_cols(wp):
    r = wp.shape[0]
    w3 = wp.reshape(r, N_HEADS, LANES)
    return jnp.concatenate([w3[:, :, 16:64], w3[:, :, 80:96], w3[:, :, 0:16], w3[:, :, 64:80]], -1).reshape(r, 768)


def _aug_uk_cols(w):
    r = w.shape[0]
    w3 = w.reshape(r, N_HEADS, NOPE)
    z = lambda n: jnp.zeros((r, N_HEADS, n), w.dtype)
    return jnp.concatenate([z(16), w3[:, :, 0:48], z(16), w3[:, :, 48:64], z(32)], -1).reshape(r, 1024)


def _unaug_uk_cols(wp):
    r = wp.shape[0]
    w3 = wp.reshape(r, N_HEADS, LANES)
    return jnp.concatenate([w3[:, :, 16:64], w3[:, :, 80:96]], -1).reshape(r, 512)


def _to_t4(x, tq):
    T, c = x.shape
    return x.reshape(T // tq, tq, c // LANES, LANES).transpose(2, 0, 3, 1)


def _from_t4(y):
    n, nq, r, tq = y.shape
    return y.transpose(1, 3, 0, 2).reshape(nq * tq, n * r)


def _pack_shards(group, shards, dtype):
    rows = _pack_rows(group)
    flat = jnp.concatenate([shards[n].reshape(-1).astype(dtype) for n, _, _, _ in group])
    flat = jnp.pad(flat, (0, rows * LANES - flat.shape[0]))
    return flat.reshape(rows, LANES)


def _unpack_shards(group, packed):
    flat = packed.reshape(-1)
    out, off = {}, 0
    for name, r, c, _ in group:
        out[name] = flat[off:off + r * c].reshape(1, r, c)
        off += r * c
    return out


def _unpack_full(group, gathered):
    flat = gathered.reshape(N_DEV, -1)
    out, off = {}, 0
    for name, r, c, by_col in group:
        blk = flat[:, off:off + r * c].reshape(N_DEV, r, c)
        out[name] = blk.transpose(1, 0, 2).reshape(r, N_DEV * c) if by_col else blk.reshape(N_DEV * r, c)
        off += r * c
    return out


def _pack_full(group, grads, dtype):
    rows = _pack_rows(group)
    segs = []
    for name, r, c, by_col in group:
        g = grads[name]
        if by_col:
            g = g.reshape(r, N_DEV, c).transpose(1, 0, 2)
        segs.append(g.reshape(N_DEV, r * c).astype(dtype))
    flat = jnp.concatenate(segs, axis=1)
    flat = jnp.pad(flat, ((0, 0), (0, rows * LANES - flat.shape[1])))
    return flat.reshape(N_DEV, rows, LANES)


SMALL_LAYOUT = (("ln_pre_mix", 1024, 0), ("ln_post_mix", 1024, 8), ("ln_pre_mlp", 1024, 16),
                ("ln_post_mlp", 1024, 24), ("b_in", 4008, 32), ("q_a_norm", 256, 64), ("kv_a_norm", 128, 66))


def _pack_small(vals, extra=None):
    rows = []
    for name, n, _ in SMALL_LAYOUT:
        v = vals[name].reshape(-1).astype(F32)
        pad = -n % LANES
        rows.append(jnp.pad(v, (0, pad)).reshape(-1, LANES))
    rows.append(jnp.zeros((SMALL_LOSS_ROW - 67, LANES), F32))
    rows.append(jnp.zeros((8, LANES), F32) if extra is None else extra.reshape(8, LANES))
    return jnp.concatenate(rows, axis=0)


def _unpack_small(packed):
    out = {}
    for name, n, r0 in SMALL_LAYOUT:
        nr = -(-n // LANES)
        out[name] = packed[r0:r0 + nr].reshape(-1)[:n].reshape(1, n)
    return out


def _rms(xf, g):
    r = lax.rsqrt(jnp.mean(xf * xf, axis=-1, keepdims=True) + NORM_EPS)
    return (xf * r) * g


def _rms_bwd(xf, g, dy):
    r = lax.rsqrt(jnp.mean(xf * xf, axis=-1, keepdims=True) + NORM_EPS)
    xhat = xf * r
    dxhat = dy * g
    dx = r * (dxhat - xhat * jnp.mean(dxhat * xhat, axis=-1, keepdims=True))
    return dx, dy * xhat


def _sigmoid(t):
    return 1.0 / (1.0 + jnp.exp(-t))


def _log_sigmoid(t):
    return jnp.minimum(t, 0.0) - jnp.log(1.0 + jnp.exp(-jnp.abs(t)))


def _lane_sign():
    lane = lax.broadcasted_iota(jnp.int32, (1, LANES), 1)
    return jnp.where(lane < 64, -1.0, 1.0).astype(F32), lane


def _rope_lanes(lane):
    return (lane < 16) | ((lane >= 64) & (lane < 80))


def kernel(x, positions, ln_pre_mix, ln_post_mix, ln_pre_mlp, ln_post_mlp, w_in, b_in, q_a_norm, w_uq, kv_a_norm, w_uk, w_uv, w_o_mla, w_o_fox, w_out, w_ff1, w_ff2, loss_target, m_ln_pre_mix, m_ln_post_mix, m_ln_pre_mlp, m_ln_post_mlp, m_w_in, m_b_in, m_q_a_norm, m_w_uq, m_kv_a_norm, m_w_uk, m_w_uv, m_w_o_mla, m_w_o_fox, m_w_out, m_w_ff1, m_w_ff2, v_ln_pre_mix, v_ln_post_mix, v_ln_pre_mlp, v_ln_post_mlp, v_w_in, v_b_in, v_q_a_norm, v_w_uq, v_kv_a_norm, v_w_uk, v_w_uv, v_w_o_mla, v_w_o_fox, v_w_out, v_w_ff1, v_w_ff2):
    T = x.shape[1]
    x2 = x.reshape(T, D_MODEL)
    tgt = loss_target.reshape(T, D_MODEL)
    w_sh = dict(w_in=w_in, w_uq=w_uq, w_uk=w_uk, w_uv=w_uv, w_o_mla=w_o_mla, w_o_fox=w_o_fox, w_out=w_out,
                w_ff1=w_ff1, w_ff2=w_ff2)
    m_sh = dict(w_in=m_w_in, w_uq=m_w_uq, w_uk=m_w_uk, w_uv=m_w_uv, w_o_mla=m_w_o_mla, w_o_fox=m_w_o_fox,
                w_out=m_w_out, w_ff1=m_w_ff1, w_ff2=m_w_ff2)
    v_sh = dict(w_in=v_w_in, w_uq=v_w_uq, w_uk=v_w_uk, w_uv=v_w_uv, w_o_mla=v_w_o_mla, w_o_fox=v_w_o_fox,
                w_out=v_w_out, w_ff1=v_w_ff1, w_ff2=v_w_ff2)
    small_w = dict(ln_pre_mix=ln_pre_mix, ln_post_mix=ln_post_mix, ln_pre_mlp=ln_pre_mlp, ln_post_mlp=ln_post_mlp,
                   b_in=b_in, q_a_norm=q_a_norm, kv_a_norm=kv_a_norm)
    small_m = dict(ln_pre_mix=m_ln_pre_mix, ln_post_mix=m_ln_post_mix, ln_pre_mlp=m_ln_pre_mlp,
                   ln_post_mlp=m_ln_post_mlp, b_in=m_b_in, q_a_norm=m_q_a_norm, kv_a_norm=m_kv_a_norm)
    small_v = dict(ln_pre_mix=v_ln_pre_mix, ln_post_mix=v_ln_post_mix, ln_pre_mlp=v_ln_pre_mlp,
                   ln_post_mlp=v_ln_post_mlp, b_in=v_b_in, q_a_norm=v_q_a_norm, kv_a_norm=v_kv_a_norm)
    mla_scale = float((NOPE + ROPE) ** -0.5)
    fox_scale = float(FOX_D ** -0.5)

    W = _unpack_full(GROUP_A, _comm_call("gather", _pack_shards(GROUP_A, w_sh, BF16), "allgather_weights_a"))
    w_in_p = _perm_in_cols(W["w_in"])
    b_in_p = _perm_in_cols(b_in.astype(F32))
    w_uq_a = _aug_uq_cols(W["w_uq"])
    w_kv_a = jnp.concatenate([_aug_uk_cols(W["w_uk"]), W["w_uv"]], axis=1)
    g1, g2, g3, g4 = ln_pre_mix, ln_post_mix, ln_pre_mlp, ln_post_mlp
    gq, gkv = q_a_norm, kv_a_norm
    tq = min(ATTN_TILE, T)
    nq = T // tq

    (h,) = _rowwise(lambda xv, g: _rms(xv, g), [x2], [g1], [(D_MODEL, BF16)], name="pre_mix_norm")
    q_cols = jnp.ones((1, D_IN_PAD), F32).at[:, P_FQ:P_FQ + 512].set(fox_scale * LOG2E)
    z = _mm(h, w_in_p, bias=b_in_p, colscale=q_cols, tm=2048, name="in_proj")
    zs = _mm(h, w_in_p[:, P_SMALL:], bias=b_in_p[:, P_SMALL:], outs=(F32,), name="in_proj_small")

    def lora_norm(cq, ckv, a, b):
        return _rms(cq.astype(F32), a), _rms(ckv.astype(F32), b)

    cqn, ckvn = _rowwise(lora_norm, [(z, 256, P_CQ // 256), (z, 128, P_CKV // 128)], [gq, gkv],
                         [(Q_LORA, BF16), (KV_LORA, BF16)], name="lora_norm")
    q_x = _mm(cqn, w_uq_a, outs=(F32,), name="q_up")
    kv = _mm(ckvn, w_kv_a, tn=512, name="kv_up")

    half = ROPE // 2
    inv_freq = ROPE_THETA ** (-jnp.arange(half, dtype=F32) / half)
    inv128 = jnp.tile(inv_freq, LANES // half).reshape(1, LANES)
    pos_col = positions.reshape(T, 1).astype(F32)

    def rope_tables(p, f):
        ang = p * f
        return jnp.cos(ang), jnp.sin(ang)

    cos_t, sin_t = _rowwise(rope_tables, [pos_col], [inv128], [(LANES, F32), (LANES, F32)], name="rope_tables")

    def rope_fwd(qx, kn, vn, s, cs, sn):
        sign, lane = _lane_sign()
        rope_l = _rope_lanes(lane)
        rot = lambda t: t * cs + pltpu.roll(t, 64, 1) * sn * sign
        k_rot = jnp.where(rope_l, rot(s), 0.0)
        qs, ks = [], []
        for hd in range(N_HEADS):
            qb = qx[:, LANES * hd:LANES * (hd + 1)]
            qs.append(jnp.where(rope_l, rot(qb), qb))
            ks.append(kn[:, LANES * hd:LANES * (hd + 1)].astype(F32) + k_rot)
        return jnp.concatenate(qs, axis=1) * (mla_scale * LOG2E), jnp.concatenate(ks, axis=1), vn

    qt4_m, k_am, vt4_m = _rowwise(rope_fwd, [q_x, (kv, 1024, 0), (kv, 512, 2), zs, cos_t, sin_t], [],
                                  [(1024, BF16, "t4"), (1024, BF16), (512, BF16, "t4")], name="rope_fwd", tm=tq)
    f_cum = _seq_cumsum(zs, _log_sigmoid, reverse=False, name="forget_cumsum")

    def fox_aug(qf, kf, vf, fc):
        lane = lax.broadcasted_iota(jnp.int32, (1, LANES), 1)
        qs, ks = [], []
        for hd in range(N_HEADS):
            qp = qf[:, LANES * (hd // 2):LANES * (hd // 2 + 1)].astype(F32)
            kp = kf[:, LANES * (hd // 2):LANES * (hd // 2 + 1)].astype(F32)
            if hd % 2:
                qp = pltpu.roll(qp, 64, 1)
                kp = pltpu.roll(kp, 64, 1)
            fb = jnp.broadcast_to(fc[:, S_FL + hd:S_FL + hd + 1] * (-LOG2E), qp.shape)
            hi = fb.astype(BF16).astype(F32)
            mid = (fb - hi).astype(BF16).astype(F32)
            lo = fb - hi - mid
            qs.append(jnp.where(lane < 64, qp, jnp.where(lane < 67, 1.0, 0.0)))
            ks.append(jnp.where(lane < 64, kp, jnp.where(lane == 64, hi, jnp.where(lane == 65, mid, jnp.where(
                lane == 66, lo, jnp.where(lane == 67, 1.0, 0.0))))))
        return jnp.concatenate(qs, axis=1), jnp.concatenate(ks, axis=1), vf

    qt4_f, k_af, vt4_f = _rowwise(fox_aug, [(z, 512, P_FQ // 512), (z, 512, P_FK // 512), (z, 512, P_FV // 512), f_cum],
                                  [], [(1024, BF16, "t4"), (1024, BF16), (512, BF16, "t4")], name="fox_aug", tm=tq)
    o_mla, lse_mla, gathered_b = _attn_fwd(qt4_m, k_am, vt4_m, name="mla_attn_fwd", tq=tq,
                                           gather=_pack_shards(GROUP_B, w_sh, BF16))
    W.update(_unpack_full(GROUP_B, gathered_b))
    o_fox, lse_fox = _attn_fwd(qt4_f, k_af, vt4_f, name="fox_attn_fwd", tq=tq)
    y_mla = _mm(o_mla, W["w_o_mla"], outs=(F32,), name="o_proj_mla")
    y_fox = _mm(o_fox, W["w_o_fox"], outs=(F32,), name="o_proj_fox")

    def gate_merge(ga, gb, ya, yb):
        return _sigmoid(ga.astype(F32)) * ya + _sigmoid(gb.astype(F32)) * yb

    (merged,) = _rowwise(gate_merge, [(z, 1024, 0), (z, 1024, 1), y_mla, y_fox], [], [(D_MODEL, BF16)],
                         name="gate_merge")
    mix = _mm(merged, W["w_out"], outs=(F32,), name="out_proj")

    def post_mix(xv, mv, a, b):
        x1v = xv + _rms(mv, a)
        return x1v, _rms(x1v, b)

    x1, h2 = _rowwise(post_mix, [x2, mix], [g2, g3], [(D_MODEL, F32), (D_MODEL, BF16)], name="post_mix_norm")

    def relu2(acc):
        r = jnp.maximum(acc, 0.0)
        return r * r, acc

    act, u = _mm(h2, W["w_ff1"], epi=relu2, outs=(BF16, BF16), tm=2048, name="ff1")
    mlp = _mm(act, W["w_ff2"], outs=(F32,), tm=1024, name="ff2")

    def loss_bwd(x1v, mv, tv, g):
        y = x1v + _rms(mv, g)
        d = y - tv
        dy = d * (1.0 / D_MODEL)
        dm, dg = _rms_bwd(mv, g, dy)
        return dy, dm, dg, d * d

    dy, dm, dg4, loss_cols = _rowwise(loss_bwd, [x1, mlp, tgt], [g4], [(D_MODEL, F32), (D_MODEL, BF16)],
                                      [D_MODEL, D_MODEL], name="loss_bwd")

    def relu2_bwd(acc, uv):
        return (acc * (2.0 * jnp.maximum(uv.astype(F32), 0.0)),)

    du = _mm(dm, W["w_ff2"].T, extras=(u,), epi=relu2_bwd, tm=2048, name="ff2_bwd")
    dw_ff2 = _mm(act, dm, ta=True, outs=(F32,), name="ff2_wgrad")
    dh2 = _mm(du, W["w_ff1"].T, outs=(F32,), tm=1024, name="ff1_bwd")
    dw_ff1 = _mm(h2, du, ta=True, outs=(F32,), name="ff1_wgrad")

    def post_mix_bwd(x1v, dh2v, dyv, mv, a, b):
        d3, dg3v = _rms_bwd(x1v, b, dh2v)
        dx1v = dyv + d3
        dmixv, dg2v = _rms_bwd(mv, a, dx1v)
        return dx1v, dmixv, dg3v, dg2v

    dx1, dmix, dg3, dg2 = _rowwise(post_mix_bwd, [x1, dh2, dy, mix], [g2, g3], [(D_MODEL, F32), (D_MODEL, BF16)],
                                   [D_MODEL, D_MODEL], name="post_mix_bwd")
    dmerged = _mm(dmix, W["w_out"].T, outs=(F32,), name="out_proj_bwd")
    dw_out = _mm(merged, dmix, ta=True, outs=(F32,), name="out_proj_wgrad")

    def gate_bwd(dmg, ga, gb, ya, yb):
        sa = _sigmoid(ga.astype(F32))
        sb = _sigmoid(gb.astype(F32))
        return dmg * sa, dmg * sb, dmg * ya * sa * (1.0 - sa), dmg * yb * sb * (1.0 - sb)

    dy_mla, dy_fox, dga, dgb = _rowwise(gate_bwd, [dmerged, (z, 1024, 0), (z, 1024, 1), y_mla, y_fox], [],
                                        [(D_MODEL, BF16)] * 4, name="gate_bwd")
    twice = lambda acc: (acc, acc)
    do_mla, dot4_m = _mm(dy_mla, W["w_o_mla"].T, epi=twice, outs=(BF16, BF16), t4=(False, True), tm=tq,
                         name="o_proj_mla_bwd")
    do_fox, dot4_f = _mm(dy_fox, W["w_o_fox"].T, epi=twice, outs=(BF16, BF16), t4=(False, True), tm=tq,
                         name="o_proj_fox_bwd")
    dw_o_mla = _mm(o_mla, dy_mla, ta=True, outs=(F32,), name="o_proj_mla_wgrad")
    dw_o_fox = _mm(o_fox, dy_fox, ta=True, outs=(F32,), name="o_proj_fox_wgrad")

    def head_dots(ov, dov):
        r = lax.broadcasted_iota(jnp.int32, (N_HEADS * HEAD_V, LANES), 0)
        cc = lax.broadcasted_iota(jnp.int32, (N_HEADS * HEAD_V, LANES), 1)
        sel = jnp.where((r // HEAD_V) == cc, 1.0, 0.0).astype(BF16)
        return _dot3(ov.astype(F32) * dov.astype(F32), sel, left=False)

    (dl_mla,) = _rowwise(head_dots, [o_mla, do_mla], [], [(LANES, F32)], name="mla_attn_delta")
    (dl_fox,) = _rowwise(head_dots, [o_fox, do_fox], [], [(LANES, F32)], name="fox_attn_delta")
    heads_row4 = lambda t: t[:, :N_HEADS].T.reshape(N_HEADS, nq, 1, tq)
    grads_b = dict(w_o_mla=dw_o_mla, w_o_fox=dw_o_fox, w_out=dw_out, w_ff1=dw_ff1, w_ff2=dw_ff2)
    dqt_m, dkt_m, dv_mla, pieces_b = _attn_bwd(qt4_m, k_am, kv, 1024 // LANES, dot4_m, lse_mla,
                                               heads_row4(dl_mla), mla_scale, NOPE + ROPE, name="mla_attn_bwd", tq=tq,
                                               exchange=_pack_full(GROUP_B, grads_b, BF16))
    dqt_f, dkt_f, dv_fox = _attn_bwd(qt4_f, k_af, z, P_FV // LANES, dot4_f, lse_fox, heads_row4(dl_fox),
                                     fox_scale, FOX_D, name="fox_attn_bwd", tq=tq)

    def fox_unpack(dqa, dka):
        lane = lax.broadcasted_iota(jnp.int32, (1, LANES), 1)
        dqs, dks = [], []
        d_f = jnp.zeros(dqa[:, :LANES].shape, F32)
        for hp in range(N_HEADS // 2):
            blk = lambda t, e: t[:, LANES * (2 * hp + e):LANES * (2 * hp + e + 1)]
            dqs.append(jnp.where(lane < 64, blk(dqa, 0), pltpu.roll(blk(dqa, 1), 64, 1)))
            dks.append(jnp.where(lane < 64, blk(dka, 0), pltpu.roll(blk(dka, 1), 64, 1)))
            for e in range(2):
                g = blk(dqa, e)[:, 67:68] - blk(dka, e)[:, 64:65]
                d_f = jnp.where(lane == S_FL + 2 * hp + e, g, d_f)
        return jnp.concatenate(dqs, axis=1), jnp.concatenate(dks, axis=1), d_f

    dq_fox, dk_fox, d_f128 = _rowwise(fox_unpack, [("t4", dqt_f), ("t4", dkt_f)], [],
                                      [(512, BF16), (512, BF16), (LANES, F32)], name="fox_unpack", tm=tq)
    df_rev = _seq_cumsum(d_f128, lambda t: t, reverse=True, name="forget_cumsum_bwd")

    def rope_bwd(dqa, dka, dvm, dfr, s, cs, sn):
        sign, lane = _lane_sign()
        rope_l = _rope_lanes(lane)
        rot_t = lambda t: t * cs - pltpu.roll(t, 64, 1) * sn * sign
        dqs = []
        dk_rot = None
        for hd in range(N_HEADS):
            blk = dqa[:, LANES * hd:LANES * (hd + 1)]
            dqs.append(jnp.where(rope_l, rot_t(blk), blk))
            blk = dka[:, LANES * hd:LANES * (hd + 1)]
            dk_rot = blk if dk_rot is None else dk_rot + blk
        dfl = dfr * _sigmoid(-s)
        small = jnp.where(rope_l, rot_t(dk_rot), jnp.where((lane >= S_FL) & (lane < S_FL + N_HEADS), dfl, 0.0))
        return jnp.concatenate(dqs, axis=1), jnp.concatenate([dka, dvm.astype(F32)], axis=1), small

    dq_b, dkv, d_small = _rowwise(rope_bwd, [("t4", dqt_m), ("t4", dkt_m), dv_mla, df_rev, zs, cos_t, sin_t], [],
                                  [(1024, BF16), (1536, BF16), (LANES, BF16)], name="rope_bwd", tm=tq)
    dcqn = _mm(dq_b, w_uq_a.T, outs=(F32,), name="q_up_bwd")
    dw_uq_a = _mm(cqn, dq_b, ta=True, outs=(F32,), name="q_up_wgrad")
    dckvn = _mm(dkv, w_kv_a.T, outs=(F32,), tk=512, name="kv_up_bwd")
    dw_kv_a = _mm(ckvn, dkv, ta=True, outs=(F32,), tn=512, name="kv_up_wgrad")

    def lora_norm_bwd(cq, ckv, dq_, dkv_, a, b):
        d1, dga_ = _rms_bwd(cq.astype(F32), a, dq_)
        d2, dgb_ = _rms_bwd(ckv.astype(F32), b, dkv_)
        return d1, d2, dga_, dgb_

    dcq, dckv, dgq, dgkv = _rowwise(lora_norm_bwd, [(z, 256, P_CQ // 256), (z, 128, P_CKV // 128), dcqn, dckvn],
                                    [gq, gkv], [(Q_LORA, BF16), (KV_LORA, BF16)], [Q_LORA, KV_LORA],
                                    name="lora_norm_bwd")
    dz = jnp.concatenate([dga, dgb, dq_fox, dk_fox, dv_fox, dcq, dckv, d_small], -1)
    (db_in_p,) = _rowwise(lambda t: (t,), [dz], [], [], [D_IN_PAD], name="in_bias_grad")
    dh = _mm(dz, w_in_p.T, outs=(F32,), tm=1024, name="in_proj_bwd")
    dw_in_p = _mm(h, dz, ta=True, outs=(F32,), name="in_proj_wgrad")

    def pre_mix_bwd(xv, dhv, dx1v, g):
        d, dg = _rms_bwd(xv, g, dhv)
        return dx1v + d, dg

    grad_x, dg1 = _rowwise(pre_mix_bwd, [x2, dh, dx1], [g1], [(D_MODEL, F32)], [D_MODEL], name="pre_mix_bwd")

    grads_a = dict(w_in=_unperm_in_cols(dw_in_p), w_uq=_unaug_uq_cols(dw_uq_a),
                   w_uk=_unaug_uk_cols(dw_kv_a[:, :1024]), w_uv=dw_kv_a[:, 1024:])
    pieces_a = _comm_call("exchange", _pack_full(GROUP_A, grads_a, BF16), "exchange_grad_pieces_a")
    grad_sh, delta_sh, newm_sh, newv_sh = {}, {}, {}, {}
    for group, pieces, tag in ((GROUP_A, pieces_a, "a"), (GROUP_B, pieces_b, "b")):
        packed = _sum_adamw(pieces, _pack_shards(group, w_sh, F32), _pack_shards(group, m_sh, F32),
                            _pack_shards(group, v_sh, F32), name="sum_pieces_adamw_" + tag)
        for dst, arr in zip((grad_sh, delta_sh, newm_sh, newv_sh), packed):
            dst.update(_unpack_shards(group, arr))

    small_part = _pack_small(dict(ln_pre_mix=dg1, ln_post_mix=dg2, ln_pre_mlp=dg3, ln_post_mlp=dg4,
                                  b_in=_unperm_in_cols(db_in_p), q_a_norm=dgq, kv_a_norm=dgkv), extra=loss_cols)
    sg, sd, sm, sv, loss_blk = _small_allreduce_adamw(small_part, _pack_small(small_w), _pack_small(small_m),
                                                      _pack_small(small_v))
    grad_sm, delta_sm, newm_sm, newv_sm = (_unpack_small(t) for t in (sg, sd, sm, sv))
    loss = loss_blk[0, 0]

    order = ["ln_pre_mix", "ln_post_mix", "ln_pre_mlp", "ln_post_mlp", "w_in", "b_in", "q_a_norm", "w_uq",
             "kv_a_norm", "w_uk", "w_uv", "w_o_mla", "w_o_fox", "w_out", "w_ff1", "w_ff2"]

    def pick(sm_d, sh_d):
        return [sm_d[n] if n in sm_d else sh_d[n] for n in order]

    return (loss, grad_x.reshape(1, T, D_MODEL), *pick(grad_sm, grad_sh), *pick(delta_sm, delta_sh),
            *pick(newm_sm, newm_sh), *pick(newv_sm, newv_sh))
```

```python
import numpy as np
import jax
import jax.numpy as jnp
from jax import lax
from jax.experimental import pallas as pl
from jax.experimental.pallas import tpu as pltpu

F32 = jnp.float32
BF16 = jnp.bfloat16
MESH = pl.DeviceIdType.MESH

D_MODEL = 1024
N_HEADS = 8
Q_LORA = 256
KV_LORA = 128
NOPE = 64
ROPE = 32
HEAD_V = 64
FOX_D = 64
D_FF = 4096
D_IN = 4008
D_IN_PAD = 4096
ROPE_THETA = 10000.0
NORM_EPS = 1e-6
N_DEV = 8

ADAM_LR = 0.001
ADAM_B1 = 0.9
ADAM_B2 = 0.999
ADAM_EPS = 1e-08
ADAM_WD = 0.01
ADAM_STEP = 10

LANES = 128
ROW_TILE = 512
ATTN_TILE = 512
ATTN_CHUNK = 256
VMEM_LIMIT = 48 * 1024 * 1024
ATTN_BWD_VMEM_LIMIT = 58 * 1024 * 1024

P_GA, P_GB, P_FQ, P_FK, P_FV, P_CQ, P_CKV, P_SMALL = 0, 1024, 2048, 2560, 3072, 3584, 3840, 3968
S_FL = 32

SHARDED = (
    ("w_in", 1024, 501, True), ("w_uq", 256, 96, True), ("w_uk", 128, 64, True), ("w_uv", 128, 64, True),
    ("w_o_mla", 512, 128, True), ("w_o_fox", 512, 128, True), ("w_out", 128, 1024, False),
    ("w_ff1", 1024, 512, True), ("w_ff2", 512, 1024, False),
)
GROUP_A = SHARDED[:4]
GROUP_B = SHARDED[4:]
SMALL_ROWS = 80
SMALL_LOSS_ROW = 72


def _pack_rows(group):
    return -(-sum(r * _packed_cols(c, by_col) for _, r, c, by_col in group) // (LANES * 64)) * 64


def _cparams(sem=None):
    return pltpu.CompilerParams(dimension_semantics=sem, vmem_limit_bytes=VMEM_LIMIT)


def _mm(a, b, *, name, ta=False, bias=None, colscale=None, extras=(), epi=None, outs=(BF16,), t4=None, tm=ROW_TILE,
        tn=1024, tk=1024, exchange=None):
    t4 = (False,) * len(outs) if t4 is None else t4
    comm = exchange is not None
    if ta:
        K, M = a.shape
        tm = min(1024, M)
    else:
        M, K = a.shape
        tm = min(tm, M)
    tk = min(tk, K)
    N = b.shape[1]
    tn = min(tn, N)
    nk = K // tk
    n_ex = len(extras)
    has_bias = bias is not None
    has_scale = colscale is not None

    def body(*refs):
        a_ref, b_ref = refs[0], refs[1]
        pos = 2
        bias_ref = scale_ref = None
        if has_bias:
            bias_ref = refs[pos]
            pos += 1
        if has_scale:
            scale_ref = refs[pos]
            pos += 1
        ex_refs = refs[pos:pos + n_ex]
        pos += n_ex
        if comm:
            c_src = refs[pos]
            pos += 1
        o_refs = refs[pos:pos + len(outs)]
        pos += len(outs)
        if comm:
            c_dst = refs[pos]
            pos += 1
            c_sems = refs[len(refs) - 3:]
            ids = [pl.program_id(d) for d in range(3)]

            @pl.when((ids[0] == 0) & (ids[1] == 0) & (ids[2] == 0))
            def _():
                _comm_start("exchange", c_src, c_dst, *c_sems)

        av = a_ref[...].astype(BF16)
        bv = b_ref[...].astype(BF16)
        if ta:
            part = lax.dot_general(av, bv, (((0,), (0,)), ((), ())), preferred_element_type=F32)
        else:
            part = jnp.dot(av, bv, preferred_element_type=F32)

        def finish(acc):
            if has_bias:
                acc = acc + bias_ref[...]
            if has_scale:
                acc = acc * scale_ref[...]
            res = (acc,) if epi is None else epi(acc, *[r[...] for r in ex_refs])
            for o_ref, val, t in zip(o_refs, res, t4):
                if t:
                    _t4_store(o_ref, val)
                else:
                    o_ref[...] = val.astype(o_ref.dtype)

        if nk == 1:
            finish(part)
        else:
            acc_ref = refs[pos]
            k = pl.program_id(2)

            @pl.when(k == 0)
            def _():
                acc_ref[...] = part

            @pl.when(k > 0)
            def _():
                acc_ref[...] += part

            @pl.when(k == nk - 1)
            def _():
                finish(acc_ref[...])

        if comm:
            @pl.when((ids[0] == M // tm - 1) & (ids[1] == N // tn - 1) & (ids[2] == nk - 1))
            def _():
                _comm_wait("exchange", c_src, c_dst, *c_sems)

    if ta:
        a_spec = pl.BlockSpec((tk, tm), lambda i, j, k: (k, i))
    else:
        a_spec = pl.BlockSpec((tm, tk), lambda i, j, k: (i, k))
    in_specs = [a_spec, pl.BlockSpec((tk, tn), lambda i, j, k: (k, j))]
    args = [a, b]
    for row in (bias, colscale):
        if row is not None:
            in_specs.append(pl.BlockSpec((1, tn), lambda i, j, k: (0, j)))
            args.append(row)
    for e in extras:
        in_specs.append(pl.BlockSpec((tm, tn), lambda i, j, k: (i, j)))
        args.append(e)
    out_specs = [pl.BlockSpec((N // LANES, 1, LANES, tm), lambda i, j, k: (0, i, 0, 0)) if t
                 else pl.BlockSpec((tm, tn), lambda i, j, k: (i, j)) for t in t4]
    out_shape = [jax.ShapeDtypeStruct((N // LANES, M // tm, LANES, tm) if t else (M, N), dt)
                 for dt, t in zip(outs, t4)]
    scratch = [pltpu.VMEM((tm, tn), F32)] if nk > 1 else []
    if comm:
        in_specs.append(pl.BlockSpec(memory_space=pl.ANY))
        args.append(exchange)
        out_specs.append(pl.BlockSpec(memory_space=pl.ANY))
        out_shape.append(jax.ShapeDtypeStruct(exchange.shape, exchange.dtype))
        scratch += _comm_sems()
    res = pl.pallas_call(
        body, name=name, grid=(M // tm, N // tn, nk),
        in_specs=in_specs, out_specs=out_specs, out_shape=out_shape, scratch_shapes=scratch,
        compiler_params=_cparams(("arbitrary",) * 3 if comm else ("parallel", "parallel", "arbitrary")),
    )(*args)
    return res[0] if len(res) == 1 else res


def _t4_store(o_ref, val):
    for c in range(o_ref.shape[0]):
        o_ref[c, 0] = val[:, c * LANES:(c + 1) * LANES].astype(F32).T.astype(o_ref.dtype)


def _rowwise(fn, rows, bcasts, outs, accs=(), *, name, tm=ROW_TILE):
    t4_in = [isinstance(r, tuple) and isinstance(r[0], str) for r in rows]
    T = [r[1].shape[1] * r[1].shape[3] if t else (r[0] if isinstance(r, tuple) else r).shape[0]
         for r, t in zip(rows, t4_in)][0]
    tm = min(tm, T)
    arrs, specs = [], []
    for r, t in zip(rows, t4_in):
        if t:
            arr = r[1]
            specs.append(pl.BlockSpec((arr.shape[0], 1, LANES, tm), lambda i: (0, i, 0, 0)))
        elif isinstance(r, tuple):
            arr, w, cb = r
            specs.append(pl.BlockSpec((tm, w), lambda i, cb=cb: (i, cb)))
        else:
            arr = r
            specs.append(pl.BlockSpec((tm, arr.shape[1]), lambda i: (i, 0)))
        arrs.append(arr)
    n_rows = len(arrs)
    for b in bcasts:
        arrs.append(b)
        specs.append(pl.BlockSpec(b.shape, lambda i: (0, 0)))
    n_in, n_out = len(arrs), len(outs)
    t4_out = [len(o) == 3 for o in outs]

    def body(*refs):
        vals = []
        for k, r in enumerate(refs[:n_in]):
            if k < n_rows and t4_in[k]:
                vals.append(jnp.concatenate([r[c, 0].astype(F32).T for c in range(r.shape[0])], axis=1))
            else:
                vals.append(r[...])
        res = fn(*vals)
        if not isinstance(res, (tuple, list)):
            res = (res,)
        for o_ref, val, t in zip(refs[n_in:n_in + n_out], res[:n_out], t4_out):
            if t:
                _t4_store(o_ref, val)
            else:
                o_ref[...] = val.astype(o_ref.dtype)
        i = pl.program_id(0)
        for a_ref, val in zip(refs[n_in + n_out:], res[n_out:]):
            col = jnp.sum(val.astype(F32), axis=0, keepdims=True)

            @pl.when(i == 0)
            def _(a_ref=a_ref, col=col):
                a_ref[...] = col

            @pl.when(i > 0)
            def _(a_ref=a_ref, col=col):
                a_ref[...] += col

    res = pl.pallas_call(
        body, name=name, grid=(T // tm,),
        in_specs=specs,
        out_specs=[pl.BlockSpec((o[0] // LANES, 1, LANES, tm), lambda i: (0, i, 0, 0)) if t
                   else pl.BlockSpec((tm, o[0]), lambda i: (i, 0)) for o, t in zip(outs, t4_out)]
        + [pl.BlockSpec((1, c), lambda i: (0, 0)) for c in accs],
        out_shape=[jax.ShapeDtypeStruct((o[0] // LANES, T // tm, LANES, tm) if t else (T, o[0]), o[1])
                   for o, t in zip(outs, t4_out)]
        + [jax.ShapeDtypeStruct((1, c), F32) for c in accs],
        compiler_params=_cparams(("arbitrary",)),
    )(*arrs)
    return res


def _dot3(x, sel, left):
    hi = x.astype(BF16)
    r1 = x - hi.astype(F32)
    mid = r1.astype(BF16)
    lo = (r1 - mid.astype(F32)).astype(BF16)
    if left:
        d = lambda t: jnp.dot(sel, t, preferred_element_type=F32)
    else:
        d = lambda t: jnp.dot(t, sel, preferred_element_type=F32)
    return d(hi) + d(mid) + d(lo)


def _seq_cumsum(x, fn, *, reverse, name, tm=256):
    T, C = x.shape
    tm = min(tm, T)
    n = T // tm

    def body(x_ref, o_ref, carry_ref):
        i = pl.program_id(0)

        @pl.when(i == 0)
        def _():
            carry_ref[...] = jnp.zeros_like(carry_ref)

        v = fn(x_ref[...])
        r = lax.broadcasted_iota(jnp.int32, (tm, tm), 0)
        c = lax.broadcasted_iota(jnp.int32, (tm, tm), 1)
        tri = jnp.where((r <= c) if reverse else (r >= c), 1.0, 0.0).astype(BF16)
        carry = carry_ref[0:1, :]
        o_ref[...] = _dot3(v, tri, left=True) + carry
        carry_ref[0:1, :] = carry + jnp.sum(v, axis=0, keepdims=True)

    idx = (lambda i: (n - 1 - i, 0)) if reverse else (lambda i: (i, 0))
    return pl.pallas_call(
        body, name=name, grid=(n,),
        in_specs=[pl.BlockSpec((tm, C), idx)],
        out_specs=pl.BlockSpec((tm, C), idx),
        out_shape=jax.ShapeDtypeStruct((T, C), F32),
        scratch_shapes=[pltpu.VMEM((8, C), F32)],
        compiler_params=_cparams(("arbitrary",)),
    )(x)


_NT = (((1,), (1,)), ((), ()))
LOG2E = 1.4426950408889634


def _head_mask(width, e):
    lane = lax.broadcasted_iota(jnp.int32, (1, width), 1)
    return (lane >= 64 * e) & (lane < 64 * (e + 1))


def _attn_fwd(qt4, k_aug, vt4, *, name, tq, gather=None):
    nq = qt4.shape[1]
    T = nq * tq
    cw = min(ATTN_CHUNK, tq)
    comm = gather is not None

    def body(*refs):
        refs = list(refs)
        q_ref, k_ref, v_ref = refs[:3]
        pos = 3
        if comm:
            c_src = refs[pos]
            pos += 1
        o_ref, lse_ref = refs[pos:pos + 2]
        pos += 2
        if comm:
            c_dst = refs[pos]
            pos += 1
        m_s, l_s, acc_s = refs[pos:pos + 2], refs[pos + 2:pos + 4], refs[pos + 4:pos + 6]
        st_a, st_b = refs[pos + 6:pos + 8]
        c_sems = refs[pos + 8:]
        i = pl.program_id(1)
        if comm:
            @pl.when((pl.program_id(0) == 0) & (i == 0))
            def _():
                _comm_start("gather", c_src, c_dst, *c_sems)

        for e in range(2):
            m_s[e][...] = jnp.full(m_s[e].shape, -jnp.inf, F32)
            l_s[e][...] = jnp.zeros_like(l_s[e])
            acc_s[e][...] = jnp.zeros_like(acc_s[e])
        qt = [q_ref[e, 0] for e in range(2)]

        work = [(e, slice(c * cw, (c + 1) * cw)) for e in range(2) for c in range(tq // cw)]

        def scores(j, buf):
            kj = k_ref[pl.ds(pl.multiple_of(j * tq, tq), tq), :]
            for n, (e, cs) in enumerate(work):
                buf[n] = jnp.dot(kj[:, e * LANES:(e + 1) * LANES], qt[e][:, cs], preferred_element_type=F32)

        def step(j, buf, masked):
            vtj = v_ref[0, j]
            for n, (e, cs) in enumerate(work):
                st = buf[n]
                if masked:
                    r = lax.broadcasted_iota(jnp.int32, (tq, cw), 0)
                    cc = lax.broadcasted_iota(jnp.int32, (tq, cw), 1) + cs.start
                    st = jnp.where(cc >= r, st, -jnp.inf)
                m_prev = m_s[e][:, cs]
                m_new = jnp.maximum(m_prev, jnp.max(st, axis=0, keepdims=True))
                alpha = jnp.exp2(m_prev - m_new)
                pt = jnp.exp2(st - m_new)
                l_s[e][:, cs] = alpha * l_s[e][:, cs] + jnp.sum(pt, axis=0, keepdims=True)
                acc_s[e][:, cs] = alpha * acc_s[e][:, cs] + jnp.dot(vtj[e * HEAD_V:(e + 1) * HEAD_V, :], pt.astype(BF16),
                                                                    preferred_element_type=F32)
                m_s[e][:, cs] = m_new

        def two_tiles(p, carry):
            scores(2 * p + 1, st_b)
            step(2 * p, st_a, False)
            scores(2 * p + 2, st_a)
            step(2 * p + 1, st_b, False)
            return carry

        scores(0, st_a)
        lax.fori_loop(0, i // 2, two_tiles, 0)

        @pl.when(i % 2 == 0)
        def _():
            step(i, st_a, True)

        @pl.when(i % 2 == 1)
        def _():
            scores(i, st_b)
            step(i - 1, st_a, False)
            step(i, st_b, True)

        ot = jnp.concatenate([acc_s[e][...] / l_s[e][...] for e in range(2)], axis=0)
        o_ref[...] = ot.T.astype(o_ref.dtype)
        for e in range(2):
            lse_ref[e, 0] = m_s[e][...] + jnp.log(l_s[e][...]) * LOG2E
        if comm:
            @pl.when((pl.program_id(0) == N_HEADS // 2 - 1) & (i == nq - 1))
            def _():
                _comm_wait("gather", c_src, c_dst, *c_sems)

    in_specs = [
        pl.BlockSpec((2, 1, LANES, tq), lambda hp, i: (hp, i, 0, 0)),
        pl.BlockSpec((T, 2 * LANES), lambda hp, i: (0, hp)),
        pl.BlockSpec((1, nq, LANES, tq), lambda hp, i: (hp, 0, 0, 0)),
    ]
    args = [qt4, k_aug, vt4]
    out_specs = [pl.BlockSpec((tq, LANES), lambda hp, i: (i, hp)),
                 pl.BlockSpec((2, 1, 1, tq), lambda hp, i: (hp, i, 0, 0))]
    out_shape = [jax.ShapeDtypeStruct((T, N_HEADS * HEAD_V), BF16), jax.ShapeDtypeStruct((N_HEADS, nq, 1, tq), F32)]
    scratch = ([pltpu.VMEM((1, tq), F32)] * 4 + [pltpu.VMEM((HEAD_V, tq), F32)] * 2
               + [pltpu.VMEM((2 * tq // cw, tq, cw), F32)] * 2)
    if comm:
        in_specs.append(pl.BlockSpec(memory_space=pl.ANY))
        args.append(gather)
        out_specs.append(pl.BlockSpec(memory_space=pl.ANY))
        out_shape.append(jax.ShapeDtypeStruct((N_DEV,) + gather.shape, gather.dtype))
        scratch += _comm_sems()
    return pl.pallas_call(
        body, name=name, grid=(N_HEADS // 2, nq),
        in_specs=in_specs, out_specs=out_specs, out_shape=out_shape, scratch_shapes=scratch,
        compiler_params=_cparams(("arbitrary", "arbitrary")),
    )(*args)


def _attn_bwd(qt4, k_aug, v, vcb, dot4, lse_row, dl_row, scale, own_rows, *, name, tq, exchange=None):
    nq = qt4.shape[1]
    T = nq * tq
    cw = min(ATTN_CHUNK, tq)
    comm = exchange is not None

    def body(*refs):
        refs = list(refs)
        k_ref, v_ref, q_ref, do_ref, lse_ref, dl_ref = refs[:6]
        pos = 6
        if comm:
            c_src = refs[pos]
            pos += 1
        dq_ref, dk_ref, dv_ref = refs[pos:pos + 3]
        pos += 3
        if comm:
            c_dst = refs[pos]
            pos += 1
        dk_s, dv_s = refs[pos:pos + 2], refs[pos + 2:pos + 4]
        c_sems = refs[pos + 4:]
        j = pl.program_id(1)
        if comm:
            @pl.when((pl.program_id(0) == 0) & (j == 0))
            def _():
                _comm_start("exchange", c_src, c_dst, *c_sems)

        @pl.when(j == 0)
        def _():
            dq_ref[...] = jnp.zeros_like(dq_ref)

        kj = k_ref[...]
        vj = v_ref[...]
        ka = [kj[:, e * LANES:(e + 1) * LANES] for e in range(2)]
        row = lax.broadcasted_iota(jnp.int32, (LANES, 1), 0)
        own = row < own_rows
        kat = [(ka[e].astype(F32).T * jnp.where(own, scale, 1.0)).astype(BF16) for e in range(2)]
        vm = [jnp.where(_head_mask(LANES, e), vj, jnp.zeros_like(vj)) for e in range(2)]
        for e in range(2):
            dk_s[e][...] = jnp.zeros_like(dk_s[e])
            dv_s[e][...] = jnp.zeros_like(dv_s[e])

        def step(i, masked):
            dot_i = do_ref[0, i]
            qts = [q_ref[e, i] for e in range(2)]
            lse = [lse_ref[e, i] for e in range(2)]
            dl = [dl_ref[e, i] for e in range(2)]
            work = [(e, slice(c * cw, (c + 1) * cw)) for e in range(2) for c in range(tq // cw)]
            scores = lambda e, cs: (jnp.dot(ka[e], qts[e][:, cs], preferred_element_type=F32),
                                    jnp.dot(vm[e], dot_i[:, cs], preferred_element_type=F32))
            nxt = scores(*work[0])
            dqs = [[], []]
            for n, (e, cs) in enumerate(work):
                st, dpt = nxt
                if n + 1 < len(work):
                    nxt = scores(*work[n + 1])
                if masked:
                    r = lax.broadcasted_iota(jnp.int32, (tq, cw), 0)
                    cc = lax.broadcasted_iota(jnp.int32, (tq, cw), 1) + cs.start
                    st = jnp.where(cc >= r, st, -jnp.inf)
                pt = jnp.exp2(st - lse[e][:, cs])
                dv_s[e][...] += lax.dot_general(dot_i[e * HEAD_V:(e + 1) * HEAD_V, cs], pt.astype(BF16), _NT,
                                                preferred_element_type=F32)
                dsb = (pt * (dpt - dl[e][:, cs])).astype(BF16)
                dk_s[e][...] += lax.dot_general(qts[e][:, cs], dsb, _NT, preferred_element_type=F32)
                dqs[e].append(jnp.dot(kat[e], dsb, preferred_element_type=F32))
            for e in range(2):
                dq_ref[e, i] += jnp.concatenate(dqs[e], axis=1)

        def loop_body(i, carry):
            step(i, False)
            return carry

        step(j, True)
        lax.fori_loop(j + 1, nq, loop_body, 0)
        for e in range(2):
            dk_ref[e, 0] = dk_s[e][...] * jnp.where(own, 1.0 / LOG2E, 1.0)
        dv_ref[...] = jnp.concatenate([dv_s[0][...], dv_s[1][...]], axis=0).T.astype(dv_ref.dtype)
        if comm:
            @pl.when((pl.program_id(0) == N_HEADS // 2 - 1) & (j == nq - 1))
            def _():
                _comm_wait("exchange", c_src, c_dst, *c_sems)

    row4 = pl.BlockSpec((2, nq, 1, tq), lambda hp, j: (hp, 0, 0, 0))
    in_specs = [
        pl.BlockSpec((tq, 2 * LANES), lambda hp, j: (j, hp)),
        pl.BlockSpec((tq, LANES), lambda hp, j: (j, vcb + hp)),
        pl.BlockSpec((2, nq, LANES, tq), lambda hp, j: (hp, 0, 0, 0)),
        pl.BlockSpec((1, nq, LANES, tq), lambda hp, j: (hp, 0, 0, 0)),
        row4, row4,
    ]
    args = [k_aug, v, qt4, dot4, lse_row, dl_row]
    out_specs = [pl.BlockSpec((2, nq, LANES, tq), lambda hp, j: (hp, 0, 0, 0)),
                 pl.BlockSpec((2, 1, LANES, tq), lambda hp, j: (hp, j, 0, 0)),
                 pl.BlockSpec((tq, LANES), lambda hp, j: (j, hp))]
    out_shape = [jax.ShapeDtypeStruct((N_HEADS, nq, LANES, tq), F32), jax.ShapeDtypeStruct((N_HEADS, nq, LANES, tq), F32),
                 jax.ShapeDtypeStruct((T, N_HEADS * HEAD_V), BF16)]
    scratch = [pltpu.VMEM((LANES, tq), F32)] * 2 + [pltpu.VMEM((HEAD_V, tq), F32)] * 2
    if comm:
        in_specs.append(pl.BlockSpec(memory_space=pl.ANY))
        args.append(exchange)
        out_specs.append(pl.BlockSpec(memory_space=pl.ANY))
        out_shape.append(jax.ShapeDtypeStruct(exchange.shape, exchange.dtype))
        scratch += _comm_sems()
    return pl.pallas_call(
        body, name=name, grid=(N_HEADS // 2, nq),
        in_specs=in_specs, out_specs=out_specs, out_shape=out_shape, scratch_shapes=scratch,
        compiler_params=pltpu.CompilerParams(dimension_semantics=("arbitrary", "arbitrary"),
                                             vmem_limit_bytes=ATTN_BWD_VMEM_LIMIT),
    )(*args)


def _peers():
    x, y, c = lax.axis_index("x"), lax.axis_index("y"), lax.axis_index("c")
    me = 4 * x + 2 * y + c
    out = []
    for k in range(1, N_DEV):
        px = (1 - x) if (k & 4) else x
        py = (1 - y) if (k & 2) else y
        pc = (1 - c) if (k & 1) else c
        out.append(((px, py, pc), 4 * px + 2 * py + pc))
    return me, out


def _comm_sems():
    return [pltpu.SemaphoreType.DMA((N_DEV - 1,)), pltpu.SemaphoreType.DMA((N_DEV - 1,)), pltpu.SemaphoreType.DMA]


def _comm_copies(kind, src_ref, dst_ref, send_sems, recv_sems, local_sem):
    me, peers = _peers()
    local = pltpu.make_async_copy(src_ref if kind == "gather" else src_ref.at[me], dst_ref.at[me], local_sem)
    sends, recvs = [], []
    for k, (dev, lin) in enumerate(peers):
        src = src_ref if kind == "gather" else src_ref.at[lin]
        sends.append(pltpu.make_async_remote_copy(src_ref=src, dst_ref=dst_ref.at[me], send_sem=send_sems.at[k],
                                                  recv_sem=recv_sems.at[k], device_id=dev, device_id_type=MESH))
        recvs.append(pltpu.make_async_remote_copy(src_ref=src, dst_ref=dst_ref.at[lin], send_sem=send_sems.at[k],
                                                  recv_sem=recv_sems.at[k], device_id=dev, device_id_type=MESH))
    return local, sends, recvs


def _comm_start(kind, src_ref, dst_ref, send_sems, recv_sems, local_sem):
    local, sends, _ = _comm_copies(kind, src_ref, dst_ref, send_sems, recv_sems, local_sem)
    local.start()
    for cp in sends:
        cp.start()


def _comm_wait(kind, src_ref, dst_ref, send_sems, recv_sems, local_sem):
    local, sends, recvs = _comm_copies(kind, src_ref, dst_ref, send_sems, recv_sems, local_sem)
    for cp in recvs:
        cp.wait_recv()
    for cp in sends:
        cp.wait_send()
    local.wait()


def _comm_call(kind, src, name):
    def body(x_ref, o_ref, send_sems, recv_sems, local_sem):
        _comm_start(kind, x_ref, o_ref, send_sems, recv_sems, local_sem)
        _comm_wait(kind, x_ref, o_ref, send_sems, recv_sems, local_sem)

    shape = (N_DEV,) + src.shape if kind == "gather" else src.shape
    return pl.pallas_call(
        body, name=name,
        in_specs=[pl.BlockSpec(memory_space=pl.ANY)],
        out_specs=pl.BlockSpec(memory_space=pl.ANY),
        out_shape=jax.ShapeDtypeStruct(shape, src.dtype),
        scratch_shapes=_comm_sems(),
    )(src)


def _adamw(w, g, m, v):
    m2 = ADAM_B1 * m + (1.0 - ADAM_B1) * g
    v2 = ADAM_B2 * v + (1.0 - ADAM_B2) * (g * g)
    m_hat = m2 / (1.0 - ADAM_B1 ** ADAM_STEP)
    v_hat = v2 / (1.0 - ADAM_B2 ** ADAM_STEP)
    delta = -ADAM_LR * (m_hat / (jnp.sqrt(v_hat) + ADAM_EPS) + ADAM_WD * w)
    return delta, m2, v2


def _sum_adamw(pieces, w, m, v, *, name):
    R = w.shape[0]
    tr = R // 4 if R % 64 == 0 else R

    def body(p_ref, w_ref, m_ref, v_ref, g_ref, d_ref, m2_ref, v2_ref):
        g = p_ref[0].astype(F32)
        for s in range(1, N_DEV):
            g = g + p_ref[s].astype(F32)
        delta, m2, v2 = _adamw(w_ref[...], g, m_ref[...], v_ref[...])
        g_ref[...] = g
        d_ref[...] = delta
        m2_ref[...] = m2
        v2_ref[...] = v2

    row = pl.BlockSpec((tr, LANES), lambda i: (i, 0))
    return pl.pallas_call(
        body, name=name, grid=(R // tr,),
        in_specs=[pl.BlockSpec((N_DEV, tr, LANES), lambda i: (0, i, 0)), row, row, row],
        out_specs=[row, row, row, row],
        out_shape=[jax.ShapeDtypeStruct((R, LANES), F32)] * 4,
        compiler_params=_cparams(("parallel",)),
    )(pieces, w, m, v)


def _small_allreduce_adamw(part, w, m, v):
    shape = part.shape

    def body(p_ref, w_ref, m_ref, v_ref, g_ref, d_ref, m2_ref, v2_ref, loss_ref, gath, send_sems, recv_sems):
        me, peers = _peers()
        gath[me] = p_ref[...]
        sends = []
        for k, (dev, _) in enumerate(peers):
            cp = pltpu.make_async_remote_copy(src_ref=p_ref, dst_ref=gath.at[me], send_sem=send_sems.at[k],
                                              recv_sem=recv_sems.at[k], device_id=dev, device_id_type=MESH)
            cp.start()
            sends.append(cp)
        for k, (dev, lin) in enumerate(peers):
            pltpu.make_async_remote_copy(src_ref=p_ref, dst_ref=gath.at[lin], send_sem=send_sems.at[k],
                                         recv_sem=recv_sems.at[k], device_id=dev, device_id_type=MESH).wait_recv()
        for cp in sends:
            cp.wait_send()
        g = gath[0]
        for s in range(1, N_DEV):
            g = g + gath[s]
        delta, m2, v2 = _adamw(w_ref[...], g, m_ref[...], v_ref[...])
        g_ref[...] = g
        d_ref[...] = delta
        m2_ref[...] = m2
        v2_ref[...] = v2
        sq = jnp.sum(g[SMALL_LOSS_ROW:SMALL_LOSS_ROW + 8, :], axis=1, keepdims=True)
        tot = jnp.sum(sq, axis=0, keepdims=True) * (0.5 / D_MODEL)
        loss_ref[...] = jnp.broadcast_to(tot, loss_ref.shape)

    vm = pl.BlockSpec(memory_space=pltpu.VMEM)
    return pl.pallas_call(
        body, name="small_allreduce_adamw",
        in_specs=[vm, vm, vm, vm],
        out_specs=[vm, vm, vm, vm, vm],
        out_shape=[jax.ShapeDtypeStruct(shape, F32)] * 4 + [jax.ShapeDtypeStruct((8, LANES), F32)],
        scratch_shapes=[pltpu.VMEM((N_DEV,) + shape, F32),
                        pltpu.SemaphoreType.DMA((N_DEV - 1,)), pltpu.SemaphoreType.DMA((N_DEV - 1,))],
    )(part, w, m, v)


def _perm_in_cols(w):
    z = lambda n: jnp.zeros(w.shape[:-1] + (n,), w.dtype)
    small = jnp.concatenate([w[..., 384:400], z(16), w[..., 1952:1960], z(24), w[..., 400:416], z(48)], -1)
    return jnp.concatenate([w[..., 1960:2984], w[..., 2984:4008], w[..., 416:928], w[..., 928:1440],
                            w[..., 1440:1952], w[..., 0:256], w[..., 256:384], small], -1)


def _unperm_in_cols(wp):
    s = wp[..., P_SMALL:]
    return jnp.concatenate([wp[..., P_CQ:P_CQ + 256], wp[..., P_CKV:P_CKV + 128], s[..., 0:16], s[..., 64:80],
                            wp[..., P_FQ:P_FQ + 512], wp[..., P_FK:P_FK + 512], wp[..., P_FV:P_FV + 512],
                            s[..., S_FL:S_FL + 8], wp[..., P_GA:P_GA + 1024], wp[..., P_GB:P_GB + 1024]], -1)


def _aug_uq_cols(w):
    r = w.shape[0]
    w3 = w.reshape(r, N_HEADS, NOPE + ROPE)
    z = jnp.zeros((r, N_HEADS, 32), w.dtype)
    return jnp.concatenate([w3[:, :, 64:80], w3[:, :, 0:48], w3[:, :, 80:96], w3[:, :, 48:64], z], -1).reshape(r, 1024)


def _unaug_uq_cols(wp):
    r = wp.shape[0]
    w3 = wp.reshape(r, N_HEADS, LANES)
    return jnp.concatenate([w3[:, :, 16:64], w3[:, :, 80:96], w3[:, :, 0:16], w3[:, :, 64:80]], -1).reshape(r, 768)


def _aug_uk_cols(w):
    r = w.shape[0]
    w3 = w.reshape(r, N_HEADS, NOPE)
    z = lambda n: jnp.zeros((r, N_HEADS, n), w.dtype)
    return jnp.concatenate([z(16), w3[:, :, 0:48], z(16), w3[:, :, 48:64], z(32)], -1).reshape(r, 1024)


def _unaug_uk_cols(wp):
    r = wp.shape[0]
    w3 = wp.reshape(r, N_HEADS, LANES)
    return jnp.concatenate([w3[:, :, 16:64], w3[:, :, 80:96]], -1).reshape(r, 512)


IN_SEGMENTS = ((0, 256, P_CQ), (256, 128, P_CKV), (384, 16, P_SMALL), (400, 16, P_SMALL + 64), (416, 512, P_FQ),
               (928, 512, P_FK), (1440, 512, P_FV), (1952, 8, P_SMALL + S_FL), (1960, 1024, P_GA), (2984, 1024, P_GB))
IN_SHARD = D_IN // N_DEV


def _perm_in_from_shards(w3):
    r = w3.shape[0]
    parts = []
    pos = 0
    for o0, n, p0 in sorted(IN_SEGMENTS, key=lambda t: t[2]):
        if p0 > pos:
            parts.append(jnp.zeros((r, p0 - pos), w3.dtype))
        a, b = o0, o0 + n
        for d in range(a // IN_SHARD, (b - 1) // IN_SHARD + 1):
            parts.append(w3[:, d, max(a, d * IN_SHARD) - d * IN_SHARD:min(b, (d + 1) * IN_SHARD) - d * IN_SHARD])
        pos = p0 + n
    parts.append(jnp.zeros((r, D_IN_PAD - pos), w3.dtype))
    return jnp.concatenate(parts, -1)


def _unperm_in_to_shards(gp):
    r = gp.shape[0]
    shards = []
    for d in range(N_DEV):
        a, b = d * IN_SHARD, (d + 1) * IN_SHARD
        parts = [gp[:, p0 + max(a, o0) - o0:p0 + min(b, o0 + n) - o0] for o0, n, p0 in IN_SEGMENTS
                 if max(a, o0) < min(b, o0 + n)]
        parts.append(jnp.zeros((r, _lane_pad(IN_SHARD) - IN_SHARD), gp.dtype))
        shards.append(jnp.concatenate(parts, -1))
    return jnp.stack(shards, 0)


def _lane_pad(c):
    return -(-c // LANES) * LANES


def _packed_cols(c, by_col):
    return _lane_pad(c) if by_col else c


def _pack_shards(group, shards, dtype):
    rows = _pack_rows(group)
    segs = []
    for name, r, c, by_col in group:
        s = shards[name].reshape(r, c).astype(dtype)
        segs.append(jnp.pad(s, ((0, 0), (0, _packed_cols(c, by_col) - c))).reshape(-1))
    flat = jnp.concatenate(segs)
    flat = jnp.pad(flat, (0, rows * LANES - flat.shape[0]))
    return flat.reshape(rows, LANES)


def _unpack_shards(group, packed):
    flat = packed.reshape(-1)
    out, off = {}, 0
    for name, r, c, by_col in group:
        cp = _packed_cols(c, by_col)
        out[name] = flat[off:off + r * cp].reshape(1, r, cp)[:, :, :c]
        off += r * cp
    return out


def _unpack_full(group, gathered):
    flat = gathered.reshape(N_DEV, -1)
    out, off = {}, 0
    for name, r, c, by_col in group:
        cp = _packed_cols(c, by_col)
        blk = flat[:, off:off + r * cp].reshape(N_DEV, r, cp)
        out[name] = blk.transpose(1, 0, 2) if by_col else blk.reshape(N_DEV * r, c)
        off += r * cp
    return out


def _full_cols(w3, c):
    return w3[:, :, :c].reshape(w3.shape[0], N_DEV * c)


def _pack_full(group, grads, dtype):
    rows = _pack_rows(group)
    segs = []
    for name, r, c, by_col in group:
        g = grads[name]
        cp = _packed_cols(c, by_col)
        if by_col and g.ndim == 2:
            g = jnp.pad(g.reshape(r, N_DEV, c), ((0, 0), (0, 0), (0, cp - c))).transpose(1, 0, 2)
        segs.append(g.reshape(N_DEV, r * cp).astype(dtype))
    flat = jnp.concatenate(segs, axis=1)
    flat = jnp.pad(flat, ((0, 0), (0, rows * LANES - flat.shape[1])))
    return flat.reshape(N_DEV, rows, LANES)


SMALL_LAYOUT = (("ln_pre_mix", 1024, 0), ("ln_post_mix", 1024, 8), ("ln_pre_mlp", 1024, 16),
                ("ln_post_mlp", 1024, 24), ("b_in", 4008, 32), ("q_a_norm", 256, 64), ("kv_a_norm", 128, 66))


def _pack_small(vals, extra=None):
    rows = []
    for name, n, _ in SMALL_LAYOUT:
        v = vals[name].reshape(-1).astype(F32)
        pad = -n % LANES
        rows.append(jnp.pad(v, (0, pad)).reshape(-1, LANES))
    rows.append(jnp.zeros((SMALL_LOSS_ROW - 67, LANES), F32))
    rows.append(jnp.zeros((8, LANES), F32) if extra is None else extra.reshape(8, LANES))
    return jnp.concatenate(rows, axis=0)


def _unpack_small(packed):
    out = {}
    for name, n, r0 in SMALL_LAYOUT:
        nr = -(-n // LANES)
        out[name] = packed[r0:r0 + nr].reshape(-1)[:n].reshape(1, n)
    return out


def _rms(xf, g):
    r = lax.rsqrt(jnp.mean(xf * xf, axis=-1, keepdims=True) + NORM_EPS)
    return (xf * r) * g


def _rms_bwd(xf, g, dy):
    r = lax.rsqrt(jnp.mean(xf * xf, axis=-1, keepdims=True) + NORM_EPS)
    xhat = xf * r
    dxhat = dy * g
    dx = r * (dxhat - xhat * jnp.mean(dxhat * xhat, axis=-1, keepdims=True))
    return dx, dy * xhat


def _sigmoid(t):
    return 1.0 / (1.0 + jnp.exp(-t))


def _log_sigmoid(t):
    return jnp.minimum(t, 0.0) - jnp.log(1.0 + jnp.exp(-jnp.abs(t)))


def _lane_sign():
    lane = lax.broadcasted_iota(jnp.int32, (1, LANES), 1)
    return jnp.where(lane < 64, -1.0, 1.0).astype(F32), lane


def _rope_lanes(lane):
    return (lane < 16) | ((lane >= 64) & (lane < 80))


def kernel(x, positions, ln_pre_mix, ln_post_mix, ln_pre_mlp, ln_post_mlp, w_in, b_in, q_a_norm, w_uq, kv_a_norm, w_uk, w_uv, w_o_mla, w_o_fox, w_out, w_ff1, w_ff2, loss_target, m_ln_pre_mix, m_ln_post_mix, m_ln_pre_mlp, m_ln_post_mlp, m_w_in, m_b_in, m_q_a_norm, m_w_uq, m_kv_a_norm, m_w_uk, m_w_uv, m_w_o_mla, m_w_o_fox, m_w_out, m_w_ff1, m_w_ff2, v_ln_pre_mix, v_ln_post_mix, v_ln_pre_mlp, v_ln_post_mlp, v_w_in, v_b_in, v_q_a_norm, v_w_uq, v_kv_a_norm, v_w_uk, v_w_uv, v_w_o_mla, v_w_o_fox, v_w_out, v_w_ff1, v_w_ff2):
    T = x.shape[1]
    x2 = x.reshape(T, D_MODEL)
    tgt = loss_target.reshape(T, D_MODEL)
    w_sh = dict(w_in=w_in, w_uq=w_uq, w_uk=w_uk, w_uv=w_uv, w_o_mla=w_o_mla, w_o_fox=w_o_fox, w_out=w_out,
                w_ff1=w_ff1, w_ff2=w_ff2)
    m_sh = dict(w_in=m_w_in, w_uq=m_w_uq, w_uk=m_w_uk, w_uv=m_w_uv, w_o_mla=m_w_o_mla, w_o_fox=m_w_o_fox,
                w_out=m_w_out, w_ff1=m_w_ff1, w_ff2=m_w_ff2)
    v_sh = dict(w_in=v_w_in, w_uq=v_w_uq, w_uk=v_w_uk, w_uv=v_w_uv, w_o_mla=v_w_o_mla, w_o_fox=v_w_o_fox,
                w_out=v_w_out, w_ff1=v_w_ff1, w_ff2=v_w_ff2)
    small_w = dict(ln_pre_mix=ln_pre_mix, ln_post_mix=ln_post_mix, ln_pre_mlp=ln_pre_mlp, ln_post_mlp=ln_post_mlp,
                   b_in=b_in, q_a_norm=q_a_norm, kv_a_norm=kv_a_norm)
    small_m = dict(ln_pre_mix=m_ln_pre_mix, ln_post_mix=m_ln_post_mix, ln_pre_mlp=m_ln_pre_mlp,
                   ln_post_mlp=m_ln_post_mlp, b_in=m_b_in, q_a_norm=m_q_a_norm, kv_a_norm=m_kv_a_norm)
    small_v = dict(ln_pre_mix=v_ln_pre_mix, ln_post_mix=v_ln_post_mix, ln_pre_mlp=v_ln_pre_mlp,
                   ln_post_mlp=v_ln_post_mlp, b_in=v_b_in, q_a_norm=v_q_a_norm, kv_a_norm=v_kv_a_norm)
    mla_scale = float((NOPE + ROPE) ** -0.5)
    fox_scale = float(FOX_D ** -0.5)

    W = _unpack_full(GROUP_A, _comm_call("gather", _pack_shards(GROUP_A, w_sh, BF16), "allgather_weights_a"))
    w_in_p = _perm_in_from_shards(W["w_in"])
    b_in_p = _perm_in_cols(b_in.astype(F32))
    w_uq_a = _aug_uq_cols(_full_cols(W["w_uq"], 96))
    w_kv_a = jnp.concatenate([_aug_uk_cols(_full_cols(W["w_uk"], 64)), _full_cols(W["w_uv"], 64)], axis=1)
    g1, g2, g3, g4 = ln_pre_mix, ln_post_mix, ln_pre_mlp, ln_post_mlp
    gq, gkv = q_a_norm, kv_a_norm
    tq = min(ATTN_TILE, T)
    nq = T // tq

    (h,) = _rowwise(lambda xv, g: _rms(xv, g), [x2], [g1], [(D_MODEL, BF16)], name="pre_mix_norm")
    q_cols = jnp.ones((1, D_IN_PAD), F32).at[:, P_FQ:P_FQ + 512].set(fox_scale * LOG2E)
    z = _mm(h, w_in_p, bias=b_in_p, colscale=q_cols, tm=2048, name="in_proj")
    zs = _mm(h, w_in_p[:, P_SMALL:], bias=b_in_p[:, P_SMALL:], outs=(F32,), name="in_proj_small")

    def lora_norm(cq, ckv, a, b):
        return _rms(cq.astype(F32), a), _rms(ckv.astype(F32), b)

    cqn, ckvn = _rowwise(lora_norm, [(z, 256, P_CQ // 256), (z, 128, P_CKV // 128)], [gq, gkv],
                         [(Q_LORA, BF16), (KV_LORA, BF16)], name="lora_norm")
    q_x = _mm(cqn, w_uq_a, outs=(F32,), name="q_up")
    kv = _mm(ckvn, w_kv_a, tn=512, name="kv_up")

    half = ROPE // 2
    inv_freq = ROPE_THETA ** (-jnp.arange(half, dtype=F32) / half)
    inv128 = jnp.tile(inv_freq, LANES // half).reshape(1, LANES)
    pos_col = positions.reshape(T, 1).astype(F32)

    def rope_tables(p, f):
        ang = p * f
        return jnp.cos(ang), jnp.sin(ang)

    cos_t, sin_t = _rowwise(rope_tables, [pos_col], [inv128], [(LANES, F32), (LANES, F32)], name="rope_tables")

    def rope_fwd(qx, kn, vn, s, cs, sn):
        sign, lane = _lane_sign()
        rope_l = _rope_lanes(lane)
        rot = lambda t: t * cs + pltpu.roll(t, 64, 1) * sn * sign
        k_rot = jnp.where(rope_l, rot(s), 0.0)
        qs, ks = [], []
        for hd in range(N_HEADS):
            qb = qx[:, LANES * hd:LANES * (hd + 1)]
            qs.append(jnp.where(rope_l, rot(qb), qb))
            ks.append(kn[:, LANES * hd:LANES * (hd + 1)].astype(F32) + k_rot)
        return jnp.concatenate(qs, axis=1) * (mla_scale * LOG2E), jnp.concatenate(ks, axis=1), vn

    qt4_m, k_am, vt4_m = _rowwise(rope_fwd, [q_x, (kv, 1024, 0), (kv, 512, 2), zs, cos_t, sin_t], [],
                                  [(1024, BF16, "t4"), (1024, BF16), (512, BF16, "t4")], name="rope_fwd", tm=tq)
    f_cum = _seq_cumsum(zs, _log_sigmoid, reverse=False, name="forget_cumsum")

    def fox_aug(qf, kf, vf, fc):
        lane = lax.broadcasted_iota(jnp.int32, (1, LANES), 1)
        qs, ks = [], []
        for hd in range(N_HEADS):
            qp = qf[:, LANES * (hd // 2):LANES * (hd // 2 + 1)].astype(F32)
            kp = kf[:, LANES * (hd // 2):LANES * (hd // 2 + 1)].astype(F32)
            if hd % 2:
                qp = pltpu.roll(qp, 64, 1)
                kp = pltpu.roll(kp, 64, 1)
            fb = jnp.broadcast_to(fc[:, S_FL + hd:S_FL + hd + 1] * (-LOG2E), qp.shape)
            hi = fb.astype(BF16).astype(F32)
            mid = (fb - hi).astype(BF16).astype(F32)
            lo = fb - hi - mid
            qs.append(jnp.where(lane < 64, qp, jnp.where(lane < 67, 1.0, 0.0)))
            ks.append(jnp.where(lane < 64, kp, jnp.where(lane == 64, hi, jnp.where(lane == 65, mid, jnp.where(
                lane == 66, lo, jnp.where(lane == 67, 1.0, 0.0))))))
        return jnp.concatenate(qs, axis=1), jnp.concatenate(ks, axis=1), vf

    qt4_f, k_af, vt4_f = _rowwise(fox_aug, [(z, 512, P_FQ // 512), (z, 512, P_FK // 512), (z, 512, P_FV // 512), f_cum],
                                  [], [(1024, BF16, "t4"), (1024, BF16), (512, BF16, "t4")], name="fox_aug", tm=tq)
    o_mla, lse_mla, gathered_b = _attn_fwd(qt4_m, k_am, vt4_m, name="mla_attn_fwd", tq=tq,
                                           gather=_pack_shards(GROUP_B, w_sh, BF16))
    W.update(_unpack_full(GROUP_B, gathered_b))
    for n, _, c, by_col in GROUP_B:
        if by_col:
            W[n] = _full_cols(W[n], c)
    o_fox, lse_fox = _attn_fwd(qt4_f, k_af, vt4_f, name="fox_attn_fwd", tq=tq)
    y_mla = _mm(o_mla, W["w_o_mla"], outs=(F32,), name="o_proj_mla")
    y_fox = _mm(o_fox, W["w_o_fox"], outs=(F32,), name="o_proj_fox")

    def gate_merge(ga, gb, ya, yb):
        return _sigmoid(ga.astype(F32)) * ya + _sigmoid(gb.astype(F32)) * yb

    (merged,) = _rowwise(gate_merge, [(z, 1024, 0), (z, 1024, 1), y_mla, y_fox], [], [(D_MODEL, BF16)],
                         name="gate_merge")
    mix = _mm(merged, W["w_out"], outs=(F32,), name="out_proj")

    def post_mix(xv, mv, a, b):
        x1v = xv + _rms(mv, a)
        return x1v, _rms(x1v, b)

    x1, h2 = _rowwise(post_mix, [x2, mix], [g2, g3], [(D_MODEL, F32), (D_MODEL, BF16)], name="post_mix_norm")

    def relu2(acc):
        r = jnp.maximum(acc, 0.0)
        return r * r, acc

    act, u = _mm(h2, W["w_ff1"], epi=relu2, outs=(BF16, BF16), tm=2048, name="ff1")
    mlp = _mm(act, W["w_ff2"], outs=(F32,), tm=1024, name="ff2")

    def loss_bwd(x1v, mv, tv, g):
        y = x1v + _rms(mv, g)
        d = y - tv
        dy = d * (1.0 / D_MODEL)
        dm, dg = _rms_bwd(mv, g, dy)
        return dy, dm, dg, d * d

    dy, dm, dg4, loss_cols = _rowwise(loss_bwd, [x1, mlp, tgt], [g4], [(D_MODEL, F32), (D_MODEL, BF16)],
                                      [D_MODEL, D_MODEL], name="loss_bwd")

    def relu2_bwd(acc, uv):
        return (acc * (2.0 * jnp.maximum(uv.astype(F32), 0.0)),)

    du = _mm(dm, W["w_ff2"].T, extras=(u,), epi=relu2_bwd, tm=2048, name="ff2_bwd")
    dw_ff2 = _mm(act, dm, ta=True, outs=(F32,), name="ff2_wgrad")
    dh2 = _mm(du, W["w_ff1"].T, outs=(F32,), tm=1024, name="ff1_bwd")
    dw_ff1 = _mm(h2, du, ta=True, outs=(F32,), name="ff1_wgrad")

    def post_mix_bwd(x1v, dh2v, dyv, mv, a, b):
        d3, dg3v = _rms_bwd(x1v, b, dh2v)
        dx1v = dyv + d3
        dmixv, dg2v = _rms_bwd(mv, a, dx1v)
        return dx1v, dmixv, dg3v, dg2v

    dx1, dmix, dg3, dg2 = _rowwise(post_mix_bwd, [x1, dh2, dy, mix], [g2, g3], [(D_MODEL, F32), (D_MODEL, BF16)],
                                   [D_MODEL, D_MODEL], name="post_mix_bwd")
    dmerged = _mm(dmix, W["w_out"].T, outs=(F32,), name="out_proj_bwd")
    dw_out = _mm(merged, dmix, ta=True, outs=(F32,), name="out_proj_wgrad")

    def gate_bwd(dmg, ga, gb, ya, yb):
        sa = _sigmoid(ga.astype(F32))
        sb = _sigmoid(gb.astype(F32))
        return dmg * sa, dmg * sb, dmg * ya * sa * (1.0 - sa), dmg * yb * sb * (1.0 - sb)

    dy_mla, dy_fox, dga, dgb = _rowwise(gate_bwd, [dmerged, (z, 1024, 0), (z, 1024, 1), y_mla, y_fox], [],
                                        [(D_MODEL, BF16)] * 4, name="gate_bwd")
    twice = lambda acc: (acc, acc)
    do_mla, dot4_m = _mm(dy_mla, W["w_o_mla"].T, epi=twice, outs=(BF16, BF16), t4=(False, True), tm=tq,
                         name="o_proj_mla_bwd")
    do_fox, dot4_f = _mm(dy_fox, W["w_o_fox"].T, epi=twice, outs=(BF16, BF16), t4=(False, True), tm=tq,
                         name="o_proj_fox_bwd")
    dw_o_mla = _mm(o_mla, dy_mla, ta=True, outs=(F32,), name="o_proj_mla_wgrad")
    dw_o_fox = _mm(o_fox, dy_fox, ta=True, outs=(F32,), name="o_proj_fox_wgrad")

    def head_dots(ov, dov):
        r = lax.broadcasted_iota(jnp.int32, (N_HEADS * HEAD_V, LANES), 0)
        cc = lax.broadcasted_iota(jnp.int32, (N_HEADS * HEAD_V, LANES), 1)
        sel = jnp.where((r // HEAD_V) == cc, 1.0, 0.0).astype(BF16)
        return _dot3(ov.astype(F32) * dov.astype(F32), sel, left=False)

    (dl_mla,) = _rowwise(head_dots, [o_mla, do_mla], [], [(LANES, F32)], name="mla_attn_delta")
    (dl_fox,) = _rowwise(head_dots, [o_fox, do_fox], [], [(LANES, F32)], name="fox_attn_delta")
    heads_row4 = lambda t: t[:, :N_HEADS].T.reshape(N_HEADS, nq, 1, tq)
    grads_b = dict(w_o_mla=dw_o_mla, w_o_fox=dw_o_fox, w_out=dw_out, w_ff1=dw_ff1, w_ff2=dw_ff2)
    dqt_m, dkt_m, dv_mla, pieces_b = _attn_bwd(qt4_m, k_am, kv, 1024 // LANES, dot4_m, lse_mla,
                                               heads_row4(dl_mla), mla_scale, NOPE + ROPE, name="mla_attn_bwd", tq=tq,
                                               exchange=_pack_full(GROUP_B, grads_b, BF16))
    dqt_f, dkt_f, dv_fox = _attn_bwd(qt4_f, k_af, z, P_FV // LANES, dot4_f, lse_fox, heads_row4(dl_fox),
                                     fox_scale, FOX_D, name="fox_attn_bwd", tq=tq)

    def fox_unpack(dqa, dka):
        lane = lax.broadcasted_iota(jnp.int32, (1, LANES), 1)
        dqs, dks = [], []
        d_f = jnp.zeros(dqa[:, :LANES].shape, F32)
        for hp in range(N_HEADS // 2):
            blk = lambda t, e: t[:, LANES * (2 * hp + e):LANES * (2 * hp + e + 1)]
            dqs.append(jnp.where(lane < 64, blk(dqa, 0), pltpu.roll(blk(dqa, 1), 64, 1)))
            dks.append(jnp.where(lane < 64, blk(dka, 0), pltpu.roll(blk(dka, 1), 64, 1)))
            for e in range(2):
                g = blk(dqa, e)[:, 67:68] - blk(dka, e)[:, 64:65]
                d_f = jnp.where(lane == S_FL + 2 * hp + e, g, d_f)
        return jnp.concatenate(dqs, axis=1), jnp.concatenate(dks, axis=1), d_f

    dq_fox, dk_fox, d_f128 = _rowwise(fox_unpack, [("t4", dqt_f), ("t4", dkt_f)], [],
                                      [(512, BF16), (512, BF16), (LANES, F32)], name="fox_unpack", tm=tq)
    df_rev = _seq_cumsum(d_f128, lambda t: t, reverse=True, name="forget_cumsum_bwd")

    def rope_bwd(dqa, dka, dvm, dfr, s, cs, sn):
        sign, lane = _lane_sign()
        rope_l = _rope_lanes(lane)
        rot_t = lambda t: t * cs - pltpu.roll(t, 64, 1) * sn * sign
        dqs = []
        dk_rot = None
        for hd in range(N_HEADS):
            blk = dqa[:, LANES * hd:LANES * (hd + 1)]
            dqs.append(jnp.where(rope_l, rot_t(blk), blk))
            blk = dka[:, LANES * hd:LANES * (hd + 1)]
            dk_rot = blk if dk_rot is None else dk_rot + blk
        dfl = dfr * _sigmoid(-s)
        small = jnp.where(rope_l, rot_t(dk_rot), jnp.where((lane >= S_FL) & (lane < S_FL + N_HEADS), dfl, 0.0))
        return jnp.concatenate(dqs, axis=1), jnp.concatenate([dka, dvm.astype(F32)], axis=1), small

    dq_b, dkv, d_small = _rowwise(rope_bwd, [("t4", dqt_m), ("t4", dkt_m), dv_mla, df_rev, zs, cos_t, sin_t], [],
                                  [(1024, BF16), (1536, BF16), (LANES, BF16)], name="rope_bwd", tm=tq)
    dcqn = _mm(dq_b, w_uq_a.T, outs=(F32,), name="q_up_bwd")
    dw_uq_a = _mm(cqn, dq_b, ta=True, outs=(F32,), name="q_up_wgrad")
    dckvn = _mm(dkv, w_kv_a.T, outs=(F32,), tk=512, name="kv_up_bwd")
    dw_kv_a = _mm(ckvn, dkv, ta=True, outs=(F32,), tn=512, name="kv_up_wgrad")

    def lora_norm_bwd(cq, ckv, dq_, dkv_, a, b):
        d1, dga_ = _rms_bwd(cq.astype(F32), a, dq_)
        d2, dgb_ = _rms_bwd(ckv.astype(F32), b, dkv_)
        return d1, d2, dga_, dgb_

    dcq, dckv, dgq, dgkv = _rowwise(lora_norm_bwd, [(z, 256, P_CQ // 256), (z, 128, P_CKV // 128), dcqn, dckvn],
                                    [gq, gkv], [(Q_LORA, BF16), (KV_LORA, BF16)], [Q_LORA, KV_LORA],
                                    name="lora_norm_bwd")
    dz = jnp.concatenate([dga, dgb, dq_fox, dk_fox, dv_fox, dcq, dckv, d_small], -1)
    (db_in_p,) = _rowwise(lambda t: (t,), [dz], [], [], [D_IN_PAD], name="in_bias_grad")
    dw_in_p = _mm(h, dz, ta=True, outs=(F32,), name="in_proj_wgrad")
    grads_a = dict(w_in=_unperm_in_to_shards(dw_in_p), w_uq=_unaug_uq_cols(dw_uq_a),
                   w_uk=_unaug_uk_cols(dw_kv_a[:, :1024]), w_uv=dw_kv_a[:, 1024:])
    dh, pieces_a = _mm(dz, w_in_p.T, outs=(F32,), tm=1024, name="in_proj_bwd",
                       exchange=_pack_full(GROUP_A, grads_a, BF16))

    def pre_mix_bwd(xv, dhv, dx1v, g):
        d, dg = _rms_bwd(xv, g, dhv)
        return dx1v + d, dg

    grad_x, dg1 = _rowwise(pre_mix_bwd, [x2, dh, dx1], [g1], [(D_MODEL, F32)], [D_MODEL], name="pre_mix_bwd")

    grad_sh, delta_sh, newm_sh, newv_sh = {}, {}, {}, {}
    for group, pieces, tag in ((GROUP_A, pieces_a, "a"), (GROUP_B, pieces_b, "b")):
        packed = _sum_adamw(pieces, _pack_shards(group, w_sh, F32), _pack_shards(group, m_sh, F32),
                            _pack_shards(group, v_sh, F32), name="sum_pieces_adamw_" + tag)
        for dst, arr in zip((grad_sh, delta_sh, newm_sh, newv_sh), packed):
            dst.update(_unpack_shards(group, arr))

    small_part = _pack_small(dict(ln_pre_mix=dg1, ln_post_mix=dg2, ln_pre_mlp=dg3, ln_post_mlp=dg4,
                                  b_in=_unperm_in_cols(db_in_p), q_a_norm=dgq, kv_a_norm=dgkv), extra=loss_cols)
    sg, sd, sm, sv, loss_blk = _small_allreduce_adamw(small_part, _pack_small(small_w), _pack_small(small_m),
                                                      _pack_small(small_v))
    grad_sm, delta_sm, newm_sm, newv_sm = (_unpack_small(t) for t in (sg, sd, sm, sv))
    loss = loss_blk[0, 0]

    order = ["ln_pre_mix", "ln_post_mix", "ln_pre_mlp", "ln_post_mlp", "w_in", "b_in", "q_a_norm", "w_uq",
             "kv_a_norm", "w_uk", "w_uv", "w_o_mla", "w_o_fox", "w_out", "w_ff1", "w_ff2"]

    def pick(sm_d, sh_d):
        return [sm_d[n] if n in sm_d else sh_d[n] for n in order]

    return (loss, grad_x.reshape(1, T, D_MODEL), *pick(grad_sm, grad_sh), *pick(delta_sm, delta_sh),
            *pick(newm_sm, newm_sh), *pick(newv_sm, newv_sh))
```

```python
import numpy as np
import jax
import jax.numpy as jnp
from jax import lax
from jax.experimental import pallas as pl
from jax.experimental.pallas import tpu as pltpu

F32 = jnp.float32
BF16 = jnp.bfloat16
MESH = pl.DeviceIdType.MESH

D_MODEL = 1024
N_HEADS = 8
Q_LORA = 256
KV_LORA = 128
NOPE = 64
ROPE = 32
HEAD_V = 64
FOX_D = 64
D_FF = 4096
D_IN = 4008
D_IN_PAD = 4096
ROPE_THETA = 10000.0
NORM_EPS = 1e-6
N_DEV = 8

ADAM_LR = 0.001
ADAM_B1 = 0.9
ADAM_B2 = 0.999
ADAM_EPS = 1e-08
ADAM_WD = 0.01
ADAM_STEP = 10

LANES = 128
ROW_TILE = 512
ATTN_TILE = 512
ATTN_CHUNK = 256
VMEM_LIMIT = 48 * 1024 * 1024
ATTN_BWD_VMEM_LIMIT = 58 * 1024 * 1024

P_GA, P_GB, P_FQ, P_FK, P_FV, P_CQ, P_CKV, P_SMALL = 0, 1024, 2048, 2560, 3072, 3584, 3840, 3968
S_FL = 32

SHARDED = (
    ("w_in", 1024, 501, True), ("w_uq", 256, 96, True), ("w_uk", 128, 64, True), ("w_uv", 128, 64, True),
    ("w_o_mla", 512, 128, True), ("w_o_fox", 512, 128, True), ("w_out", 128, 1024, False),
    ("w_ff1", 1024, 512, True), ("w_ff2", 512, 1024, False),
)
GROUP_A = SHARDED[:4]
GROUP_B = SHARDED[4:]
SMALL_ROWS = 80
SMALL_LOSS_ROW = 72


def _pack_rows(group):
    return -(-sum(r * _packed_cols(c, by_col) for _, r, c, by_col in group) // (LANES * 64)) * 64


def _cparams(sem=None):
    return pltpu.CompilerParams(dimension_semantics=sem, vmem_limit_bytes=VMEM_LIMIT)


def _mm(a, b, *, name, ta=False, tb=False, bias=None, colscale=None, extras=(), epi=None, outs=(BF16,), t4=None,
        tm=ROW_TILE, tn=1024, tk=1024, exchange=None, colsum=False):
    t4 = (False,) * len(outs) if t4 is None else t4
    comm = exchange is not None
    if ta:
        K, M = a.shape
        tm = min(1024, M)
    else:
        M, K = a.shape
        tm = min(tm, M)
    tk = min(tk, K)
    N = b.shape[0] if tb else b.shape[1]
    tn = min(tn, N)
    nk = K // tk
    assert not colsum or (ta and M == tm)
    n_ex = len(extras)
    has_bias = bias is not None
    has_scale = colscale is not None

    def body(*refs):
        a_ref, b_ref = refs[0], refs[1]
        pos = 2
        bias_ref = scale_ref = None
        if has_bias:
            bias_ref = refs[pos]
            pos += 1
        if has_scale:
            scale_ref = refs[pos]
            pos += 1
        ex_refs = refs[pos:pos + n_ex]
        pos += n_ex
        if comm:
            c_src = refs[pos]
            pos += 1
        o_refs = refs[pos:pos + len(outs)]
        pos += len(outs)
        if colsum:
            cs_ref = refs[pos]
            pos += 1
        if comm:
            c_dst = refs[pos]
            pos += 1
            c_sems = refs[len(refs) - 3:]
            ids = [pl.program_id(d) for d in range(3)]

            @pl.when((ids[0] == 0) & (ids[1] == 0) & (ids[2] == 0))
            def _():
                _comm_start("exchange", c_src, c_dst, *c_sems)

        av = a_ref[...].astype(BF16)
        bv = b_ref[...].astype(BF16)
        if ta:
            part = lax.dot_general(av, bv, (((0,), (0,)), ((), ())), preferred_element_type=F32)
        elif tb:
            part = lax.dot_general(av, bv, (((1,), (1,)), ((), ())), preferred_element_type=F32)
        else:
            part = jnp.dot(av, bv, preferred_element_type=F32)
        if colsum:
            cs = jnp.sum(bv.astype(F32), axis=0, keepdims=True)

            @pl.when(pl.program_id(2) == 0)
            def _():
                cs_ref[...] = cs

            @pl.when(pl.program_id(2) > 0)
            def _():
                cs_ref[...] += cs

        def finish(acc):
            if has_bias:
                acc = acc + bias_ref[...]
            if has_scale:
                acc = acc * scale_ref[...]
            res = (acc,) if epi is None else epi(acc, *[r[...] for r in ex_refs])
            for o_ref, val, t in zip(o_refs, res, t4):
                if t:
                    _t4_store(o_ref, val)
                else:
                    o_ref[...] = val.astype(o_ref.dtype)

        if nk == 1:
            finish(part)
        else:
            acc_ref = refs[pos]
            k = pl.program_id(2)

            @pl.when(k == 0)
            def _():
                acc_ref[...] = part

            @pl.when(k > 0)
            def _():
                acc_ref[...] += part

            @pl.when(k == nk - 1)
            def _():
                finish(acc_ref[...])

        if comm:
            @pl.when((ids[0] == M // tm - 1) & (ids[1] == N // tn - 1) & (ids[2] == nk - 1))
            def _():
                _comm_wait("exchange", c_src, c_dst, *c_sems)

    if ta:
        a_spec = pl.BlockSpec((tk, tm), lambda i, j, k: (k, i))
    else:
        a_spec = pl.BlockSpec((tm, tk), lambda i, j, k: (i, k))
    b_spec = pl.BlockSpec((tn, tk), lambda i, j, k: (j, k)) if tb else pl.BlockSpec((tk, tn), lambda i, j, k: (k, j))
    in_specs = [a_spec, b_spec]
    args = [a, b]
    for row in (bias, colscale):
        if row is not None:
            in_specs.append(pl.BlockSpec((1, tn), lambda i, j, k: (0, j)))
            args.append(row)
    for e in extras:
        in_specs.append(pl.BlockSpec((tm, tn), lambda i, j, k: (i, j)))
        args.append(e)
    out_specs = [pl.BlockSpec((N // LANES, 1, LANES, tm), lambda i, j, k: (0, i, 0, 0)) if t
                 else pl.BlockSpec((tm, tn), lambda i, j, k: (i, j)) for t in t4]
    out_shape = [jax.ShapeDtypeStruct((N // LANES, M // tm, LANES, tm) if t else (M, N), dt)
                 for dt, t in zip(outs, t4)]
    scratch = [pltpu.VMEM((tm, tn), F32)] if nk > 1 else []
    if colsum:
        out_specs.append(pl.BlockSpec((1, tn), lambda i, j, k: (0, j)))
        out_shape.append(jax.ShapeDtypeStruct((1, N), F32))
    if comm:
        in_specs.append(pl.BlockSpec(memory_space=pl.ANY))
        args.append(exchange)
        out_specs.append(pl.BlockSpec(memory_space=pl.ANY))
        out_shape.append(jax.ShapeDtypeStruct(exchange.shape, exchange.dtype))
        scratch += _comm_sems()
    res = pl.pallas_call(
        body, name=name, grid=(M // tm, N // tn, nk),
        in_specs=in_specs, out_specs=out_specs, out_shape=out_shape, scratch_shapes=scratch,
        compiler_params=_cparams(("arbitrary",) * 3 if comm else ("parallel", "parallel", "arbitrary")),
    )(*args)
    return res[0] if len(res) == 1 else res


def _t4_store(o_ref, val):
    for c in range(o_ref.shape[0]):
        o_ref[c, 0] = val[:, c * LANES:(c + 1) * LANES].astype(F32).T.astype(o_ref.dtype)


def _rowwise(fn, rows, bcasts, outs, accs=(), *, name, tm=ROW_TILE):
    t4_in = [isinstance(r, tuple) and isinstance(r[0], str) for r in rows]
    T = [r[1].shape[1] * r[1].shape[3] if t else (r[0] if isinstance(r, tuple) else r).shape[0]
         for r, t in zip(rows, t4_in)][0]
    tm = min(tm, T)
    arrs, specs = [], []
    for r, t in zip(rows, t4_in):
        if t:
            arr = r[1]
            specs.append(pl.BlockSpec((arr.shape[0], 1, LANES, tm), lambda i: (0, i, 0, 0)))
        elif isinstance(r, tuple):
            arr, w, cb = r
            specs.append(pl.BlockSpec((tm, w), lambda i, cb=cb: (i, cb)))
        else:
            arr = r
            specs.append(pl.BlockSpec((tm, arr.shape[1]), lambda i: (i, 0)))
        arrs.append(arr)
    n_rows = len(arrs)
    for b in bcasts:
        arrs.append(b)
        specs.append(pl.BlockSpec(b.shape, lambda i: (0, 0)))
    n_in, n_out = len(arrs), len(outs)
    t4_out = [len(o) == 3 for o in outs]

    def body(*refs):
        vals = []
        for k, r in enumerate(refs[:n_in]):
            if k < n_rows and t4_in[k]:
                vals.append(jnp.concatenate([r[c, 0].astype(F32).T for c in range(r.shape[0])], axis=1))
            else:
                vals.append(r[...])
        res = fn(*vals)
        if not isinstance(res, (tuple, list)):
            res = (res,)
        for o_ref, val, t in zip(refs[n_in:n_in + n_out], res[:n_out], t4_out):
            if t:
                _t4_store(o_ref, val)
            else:
                o_ref[...] = val.astype(o_ref.dtype)
        i = pl.program_id(0)
        for a_ref, val in zip(refs[n_in + n_out:], res[n_out:]):
            col = jnp.sum(val.astype(F32), axis=0, keepdims=True)

            @pl.when(i == 0)
            def _(a_ref=a_ref, col=col):
                a_ref[...] = col

            @pl.when(i > 0)
            def _(a_ref=a_ref, col=col):
                a_ref[...] += col

    res = pl.pallas_call(
        body, name=name, grid=(T // tm,),
        in_specs=specs,
        out_specs=[pl.BlockSpec((o[0] // LANES, 1, LANES, tm), lambda i: (0, i, 0, 0)) if t
                   else pl.BlockSpec((tm, o[0]), lambda i: (i, 0)) for o, t in zip(outs, t4_out)]
        + [pl.BlockSpec((1, c), lambda i: (0, 0)) for c in accs],
        out_shape=[jax.ShapeDtypeStruct((o[0] // LANES, T // tm, LANES, tm) if t else (T, o[0]), o[1])
                   for o, t in zip(outs, t4_out)]
        + [jax.ShapeDtypeStruct((1, c), F32) for c in accs],
        compiler_params=_cparams(("arbitrary",)),
    )(*arrs)
    return res


def _dot3(x, sel, left):
    hi = x.astype(BF16)
    r1 = x - hi.astype(F32)
    mid = r1.astype(BF16)
    lo = (r1 - mid.astype(F32)).astype(BF16)
    if left:
        d = lambda t: jnp.dot(sel, t, preferred_element_type=F32)
    else:
        d = lambda t: jnp.dot(t, sel, preferred_element_type=F32)
    return d(hi) + d(mid) + d(lo)


def _seq_cumsum(x, fn, *, reverse, name, tm=256):
    T, C = x.shape
    tm = min(tm, T)
    n = T // tm

    def body(x_ref, o_ref, carry_ref):
        i = pl.program_id(0)

        @pl.when(i == 0)
        def _():
            carry_ref[...] = jnp.zeros_like(carry_ref)

        v = fn(x_ref[...])
        r = lax.broadcasted_iota(jnp.int32, (tm, tm), 0)
        c = lax.broadcasted_iota(jnp.int32, (tm, tm), 1)
        tri = jnp.where((r <= c) if reverse else (r >= c), 1.0, 0.0).astype(BF16)
        carry = carry_ref[0:1, :]
        o_ref[...] = _dot3(v, tri, left=True) + carry
        carry_ref[0:1, :] = carry + jnp.sum(v, axis=0, keepdims=True)

    idx = (lambda i: (n - 1 - i, 0)) if reverse else (lambda i: (i, 0))
    return pl.pallas_call(
        body, name=name, grid=(n,),
        in_specs=[pl.BlockSpec((tm, C), idx)],
        out_specs=pl.BlockSpec((tm, C), idx),
        out_shape=jax.ShapeDtypeStruct((T, C), F32),
        scratch_shapes=[pltpu.VMEM((8, C), F32)],
        compiler_params=_cparams(("arbitrary",)),
    )(x)


_NT = (((1,), (1,)), ((), ()))
LOG2E = 1.4426950408889634


def _head_mask(width, e):
    lane = lax.broadcasted_iota(jnp.int32, (1, width), 1)
    return (lane >= 64 * e) & (lane < 64 * (e + 1))


def _attn_fwd(qt4, k_aug, vt4, *, name, tq, gather=None):
    nq = qt4.shape[1]
    T = nq * tq
    cw = min(ATTN_CHUNK, tq)
    comm = gather is not None

    def body(*refs):
        refs = list(refs)
        q_ref, k_ref, v_ref = refs[:3]
        pos = 3
        if comm:
            c_src = refs[pos]
            pos += 1
        o_ref, lse_ref = refs[pos:pos + 2]
        pos += 2
        if comm:
            c_dst = refs[pos]
            pos += 1
        m_s, l_s, acc_s = refs[pos:pos + 2], refs[pos + 2:pos + 4], refs[pos + 4:pos + 6]
        st_a, st_b = refs[pos + 6:pos + 8]
        c_sems = refs[pos + 8:]
        i = pl.program_id(1)
        if comm:
            @pl.when((pl.program_id(0) == 0) & (i == 0))
            def _():
                _comm_start("gather", c_src, c_dst, *c_sems)

        for e in range(2):
            m_s[e][...] = jnp.full(m_s[e].shape, -jnp.inf, F32)
            l_s[e][...] = jnp.zeros_like(l_s[e])
            acc_s[e][...] = jnp.zeros_like(acc_s[e])
        qt = [q_ref[e, 0] for e in range(2)]

        work = [(e, slice(c * cw, (c + 1) * cw)) for e in range(2) for c in range(tq // cw)]

        def scores(j, buf):
            kj = k_ref[pl.ds(pl.multiple_of(j * tq, tq), tq), :]
            for n, (e, cs) in enumerate(work):
                buf[n] = jnp.dot(kj[:, e * LANES:(e + 1) * LANES], qt[e][:, cs], preferred_element_type=F32)

        def step(j, buf, masked):
            vtj = v_ref[0, j]
            for n, (e, cs) in enumerate(work):
                st = buf[n]
                if masked:
                    r = lax.broadcasted_iota(jnp.int32, (tq, cw), 0)
                    cc = lax.broadcasted_iota(jnp.int32, (tq, cw), 1) + cs.start
                    st = jnp.where(cc >= r, st, -jnp.inf)
                m_prev = m_s[e][:, cs]
                m_new = jnp.maximum(m_prev, jnp.max(st, axis=0, keepdims=True))
                alpha = jnp.exp2(m_prev - m_new)
                pt = jnp.exp2(st - m_new)
                l_s[e][:, cs] = alpha * l_s[e][:, cs] + jnp.sum(pt, axis=0, keepdims=True)
                acc_s[e][:, cs] = alpha * acc_s[e][:, cs] + jnp.dot(vtj[e * HEAD_V:(e + 1) * HEAD_V, :], pt.astype(BF16),
                                                                    preferred_element_type=F32)
                m_s[e][:, cs] = m_new

        def two_tiles(p, carry):
            scores(2 * p + 1, st_b)
            step(2 * p, st_a, False)
            scores(2 * p + 2, st_a)
            step(2 * p + 1, st_b, False)
            return carry

        scores(0, st_a)
        lax.fori_loop(0, i // 2, two_tiles, 0)

        @pl.when(i % 2 == 0)
        def _():
            step(i, st_a, True)

        @pl.when(i % 2 == 1)
        def _():
            scores(i, st_b)
            step(i - 1, st_a, False)
            step(i, st_b, True)

        ot = jnp.concatenate([acc_s[e][...] / l_s[e][...] for e in range(2)], axis=0)
        o_ref[...] = ot.T.astype(o_ref.dtype)
        for e in range(2):
            lse_ref[e, 0] = m_s[e][...] + jnp.log(l_s[e][...]) * LOG2E
        if comm:
            @pl.when((pl.program_id(0) == N_HEADS // 2 - 1) & (i == nq - 1))
            def _():
                _comm_wait("gather", c_src, c_dst, *c_sems)

    in_specs = [
        pl.BlockSpec((2, 1, LANES, tq), lambda hp, i: (hp, i, 0, 0)),
        pl.BlockSpec((T, 2 * LANES), lambda hp, i: (0, hp)),
        pl.BlockSpec((1, nq, LANES, tq), lambda hp, i: (hp, 0, 0, 0)),
    ]
    args = [qt4, k_aug, vt4]
    out_specs = [pl.BlockSpec((tq, LANES), lambda hp, i: (i, hp)),
                 pl.BlockSpec((2, 1, 1, tq), lambda hp, i: (hp, i, 0, 0))]
    out_shape = [jax.ShapeDtypeStruct((T, N_HEADS * HEAD_V), BF16), jax.ShapeDtypeStruct((N_HEADS, nq, 1, tq), F32)]
    scratch = ([pltpu.VMEM((1, tq), F32)] * 4 + [pltpu.VMEM((HEAD_V, tq), F32)] * 2
               + [pltpu.VMEM((2 * tq // cw, tq, cw), F32)] * 2)
    if comm:
        in_specs.append(pl.BlockSpec(memory_space=pl.ANY))
        args.append(gather)
        out_specs.append(pl.BlockSpec(memory_space=pl.ANY))
        out_shape.append(jax.ShapeDtypeStruct((N_DEV,) + gather.shape, gather.dtype))
        scratch += _comm_sems()
    return pl.pallas_call(
        body, name=name, grid=(N_HEADS // 2, nq),
        in_specs=in_specs, out_specs=out_specs, out_shape=out_shape, scratch_shapes=scratch,
        compiler_params=_cparams(("arbitrary", "arbitrary")),
    )(*args)


def _attn_bwd(qt4, k_aug, v, vcb, dot4, lse_row, dl_row, scale, own_rows, *, name, tq, exchange=None):
    nq = qt4.shape[1]
    T = nq * tq
    cw = min(ATTN_CHUNK, tq)
    comm = exchange is not None

    def body(*refs):
        refs = list(refs)
        k_ref, v_ref, q_ref, do_ref, lse_ref, dl_ref = refs[:6]
        pos = 6
        if comm:
            c_src = refs[pos]
            pos += 1
        dq_ref, dk_ref, dv_ref = refs[pos:pos + 3]
        pos += 3
        if comm:
            c_dst = refs[pos]
            pos += 1
        dk_s, dv_s = refs[pos:pos + 2], refs[pos + 2:pos + 4]
        c_sems = refs[pos + 4:]
        j = pl.program_id(1)
        if comm:
            @pl.when((pl.program_id(0) == 0) & (j == 0))
            def _():
                _comm_start("exchange", c_src, c_dst, *c_sems)

        @pl.when(j == 0)
        def _():
            dq_ref[...] = jnp.zeros_like(dq_ref)

        kj = k_ref[...]
        vj = v_ref[...]
        ka = [kj[:, e * LANES:(e + 1) * LANES] for e in range(2)]
        row = lax.broadcasted_iota(jnp.int32, (LANES, 1), 0)
        own = row < own_rows
        kat = [(ka[e].astype(F32).T * jnp.where(own, scale, 1.0)).astype(BF16) for e in range(2)]
        vm = [jnp.where(_head_mask(LANES, e), vj, jnp.zeros_like(vj)) for e in range(2)]
        for e in range(2):
            dk_s[e][...] = jnp.zeros_like(dk_s[e])
            dv_s[e][...] = jnp.zeros_like(dv_s[e])

        def step(i, masked):
            dot_i = do_ref[0, i]
            qts = [q_ref[e, i] for e in range(2)]
            lse = [lse_ref[e, i] for e in range(2)]
            dl = [dl_ref[e, i] for e in range(2)]
            work = [(e, slice(c * cw, (c + 1) * cw)) for e in range(2) for c in range(tq // cw)]
            scores = lambda e, cs: (jnp.dot(ka[e], qts[e][:, cs], preferred_element_type=F32),
                                    jnp.dot(vm[e], dot_i[:, cs], preferred_element_type=F32))
            nxt = scores(*work[0])
            dqs = [[], []]
            for n, (e, cs) in enumerate(work):
                st, dpt = nxt
                if n + 1 < len(work):
                    nxt = scores(*work[n + 1])
                if masked:
                    r = lax.broadcasted_iota(jnp.int32, (tq, cw), 0)
                    cc = lax.broadcasted_iota(jnp.int32, (tq, cw), 1) + cs.start
                    st = jnp.where(cc >= r, st, -jnp.inf)
                pt = jnp.exp2(st - lse[e][:, cs])
                dv_s[e][...] += lax.dot_general(dot_i[e * HEAD_V:(e + 1) * HEAD_V, cs], pt.astype(BF16), _NT,
                                                preferred_element_type=F32)
                dsb = (pt * (dpt - dl[e][:, cs])).astype(BF16)
                dk_s[e][...] += lax.dot_general(qts[e][:, cs], dsb, _NT, preferred_element_type=F32)
                dqs[e].append(jnp.dot(kat[e], dsb, preferred_element_type=F32))
            for e in range(2):
                dq_ref[e, i] += jnp.concatenate(dqs[e], axis=1)

        def loop_body(i, carry):
            step(i, False)
            return carry

        step(j, True)
        lax.fori_loop(j + 1, nq, loop_body, 0)
        for e in range(2):
            dk_ref[e, 0] = dk_s[e][...] * jnp.where(own, 1.0 / LOG2E, 1.0)
        dv_ref[...] = jnp.concatenate([dv_s[0][...], dv_s[1][...]], axis=0).T.astype(dv_ref.dtype)
        if comm:
            @pl.when((pl.program_id(0) == N_HEADS // 2 - 1) & (j == nq - 1))
            def _():
                _comm_wait("exchange", c_src, c_dst, *c_sems)

    row4 = pl.BlockSpec((2, nq, 1, tq), lambda hp, j: (hp, 0, 0, 0))
    in_specs = [
        pl.BlockSpec((tq, 2 * LANES), lambda hp, j: (j, hp)),
        pl.BlockSpec((tq, LANES), lambda hp, j: (j, vcb + hp)),
        pl.BlockSpec((2, nq, LANES, tq), lambda hp, j: (hp, 0, 0, 0)),
        pl.BlockSpec((1, nq, LANES, tq), lambda hp, j: (hp, 0, 0, 0)),
        row4, row4,
    ]
    args = [k_aug, v, qt4, dot4, lse_row, dl_row]
    out_specs = [pl.BlockSpec((2, nq, LANES, tq), lambda hp, j: (hp, 0, 0, 0)),
                 pl.BlockSpec((2, 1, LANES, tq), lambda hp, j: (hp, j, 0, 0)),
                 pl.BlockSpec((tq, LANES), lambda hp, j: (j, hp))]
    out_shape = [jax.ShapeDtypeStruct((N_HEADS, nq, LANES, tq), F32), jax.ShapeDtypeStruct((N_HEADS, nq, LANES, tq), F32),
                 jax.ShapeDtypeStruct((T, N_HEADS * HEAD_V), BF16)]
    scratch = [pltpu.VMEM((LANES, tq), F32)] * 2 + [pltpu.VMEM((HEAD_V, tq), F32)] * 2
    if comm:
        in_specs.append(pl.BlockSpec(memory_space=pl.ANY))
        args.append(exchange)
        out_specs.append(pl.BlockSpec(memory_space=pl.ANY))
        out_shape.append(jax.ShapeDtypeStruct(exchange.shape, exchange.dtype))
        scratch += _comm_sems()
    return pl.pallas_call(
        body, name=name, grid=(N_HEADS // 2, nq),
        in_specs=in_specs, out_specs=out_specs, out_shape=out_shape, scratch_shapes=scratch,
        compiler_params=pltpu.CompilerParams(dimension_semantics=("arbitrary", "arbitrary"),
                                             vmem_limit_bytes=ATTN_BWD_VMEM_LIMIT),
    )(*args)


def _peers():
    x, y, c = lax.axis_index("x"), lax.axis_index("y"), lax.axis_index("c")
    me = 4 * x + 2 * y + c
    out = []
    for k in range(1, N_DEV):
        px = (1 - x) if (k & 4) else x
        py = (1 - y) if (k & 2) else y
        pc = (1 - c) if (k & 1) else c
        out.append(((px, py, pc), 4 * px + 2 * py + pc))
    return me, out


def _comm_sems():
    return [pltpu.SemaphoreType.DMA((N_DEV - 1,)), pltpu.SemaphoreType.DMA((N_DEV - 1,)), pltpu.SemaphoreType.DMA]


def _comm_copies(kind, src_ref, dst_ref, send_sems, recv_sems, local_sem):
    me, peers = _peers()
    local = pltpu.make_async_copy(src_ref if kind == "gather" else src_ref.at[me], dst_ref.at[me], local_sem)
    sends, recvs = [], []
    for k, (dev, lin) in enumerate(peers):
        src = src_ref if kind == "gather" else src_ref.at[lin]
        sends.append(pltpu.make_async_remote_copy(src_ref=src, dst_ref=dst_ref.at[me], send_sem=send_sems.at[k],
                                                  recv_sem=recv_sems.at[k], device_id=dev, device_id_type=MESH))
        recvs.append(pltpu.make_async_remote_copy(src_ref=src, dst_ref=dst_ref.at[lin], send_sem=send_sems.at[k],
                                                  recv_sem=recv_sems.at[k], device_id=dev, device_id_type=MESH))
    return local, sends, recvs


def _comm_start(kind, src_ref, dst_ref, send_sems, recv_sems, local_sem):
    local, sends, _ = _comm_copies(kind, src_ref, dst_ref, send_sems, recv_sems, local_sem)
    local.start()
    for cp in sends:
        cp.start()


def _comm_wait(kind, src_ref, dst_ref, send_sems, recv_sems, local_sem):
    local, sends, recvs = _comm_copies(kind, src_ref, dst_ref, send_sems, recv_sems, local_sem)
    for cp in recvs:
        cp.wait_recv()
    for cp in sends:
        cp.wait_send()
    local.wait()


def _comm_call(kind, src, name):
    def body(x_ref, o_ref, send_sems, recv_sems, local_sem):
        _comm_start(kind, x_ref, o_ref, send_sems, recv_sems, local_sem)
        _comm_wait(kind, x_ref, o_ref, send_sems, recv_sems, local_sem)

    shape = (N_DEV,) + src.shape if kind == "gather" else src.shape
    return pl.pallas_call(
        body, name=name,
        in_specs=[pl.BlockSpec(memory_space=pl.ANY)],
        out_specs=pl.BlockSpec(memory_space=pl.ANY),
        out_shape=jax.ShapeDtypeStruct(shape, src.dtype),
        scratch_shapes=_comm_sems(),
    )(src)


def _adamw(w, g, m, v):
    m2 = ADAM_B1 * m + (1.0 - ADAM_B1) * g
    v2 = ADAM_B2 * v + (1.0 - ADAM_B2) * (g * g)
    m_hat = m2 / (1.0 - ADAM_B1 ** ADAM_STEP)
    v_hat = v2 / (1.0 - ADAM_B2 ** ADAM_STEP)
    delta = -ADAM_LR * (m_hat / (jnp.sqrt(v_hat) + ADAM_EPS) + ADAM_WD * w)
    return delta, m2, v2


def _sum_adamw(pieces, w, m, v, *, name):
    R = w.shape[0]
    tr = R // 4 if R % 64 == 0 else R

    def body(p_ref, w_ref, m_ref, v_ref, g_ref, d_ref, m2_ref, v2_ref):
        g = p_ref[0].astype(F32)
        for s in range(1, N_DEV):
            g = g + p_ref[s].astype(F32)
        delta, m2, v2 = _adamw(w_ref[...], g, m_ref[...], v_ref[...])
        g_ref[...] = g
        d_ref[...] = delta
        m2_ref[...] = m2
        v2_ref[...] = v2

    row = pl.BlockSpec((tr, LANES), lambda i: (i, 0))
    return pl.pallas_call(
        body, name=name, grid=(R // tr,),
        in_specs=[pl.BlockSpec((N_DEV, tr, LANES), lambda i: (0, i, 0)), row, row, row],
        out_specs=[row, row, row, row],
        out_shape=[jax.ShapeDtypeStruct((R, LANES), F32)] * 4,
        compiler_params=_cparams(("parallel",)),
    )(pieces, w, m, v)


def _small_allreduce_adamw(part, w, m, v):
    shape = part.shape

    def body(p_ref, w_ref, m_ref, v_ref, g_ref, d_ref, m2_ref, v2_ref, loss_ref, gath, send_sems, recv_sems):
        me, peers = _peers()
        gath[me] = p_ref[...]
        sends = []
        for k, (dev, _) in enumerate(peers):
            cp = pltpu.make_async_remote_copy(src_ref=p_ref, dst_ref=gath.at[me], send_sem=send_sems.at[k],
                                              recv_sem=recv_sems.at[k], device_id=dev, device_id_type=MESH)
            cp.start()
            sends.append(cp)
        for k, (dev, lin) in enumerate(peers):
            pltpu.make_async_remote_copy(src_ref=p_ref, dst_ref=gath.at[lin], send_sem=send_sems.at[k],
                                         recv_sem=recv_sems.at[k], device_id=dev, device_id_type=MESH).wait_recv()
        for cp in sends:
            cp.wait_send()
        g = gath[0]
        for s in range(1, N_DEV):
            g = g + gath[s]
        delta, m2, v2 = _adamw(w_ref[...], g, m_ref[...], v_ref[...])
        g_ref[...] = g
        d_ref[...] = delta
        m2_ref[...] = m2
        v2_ref[...] = v2
        sq = jnp.sum(g[SMALL_LOSS_ROW:SMALL_LOSS_ROW + 8, :], axis=1, keepdims=True)
        tot = jnp.sum(sq, axis=0, keepdims=True) * (0.5 / D_MODEL)
        loss_ref[...] = jnp.broadcast_to(tot, loss_ref.shape)

    vm = pl.BlockSpec(memory_space=pltpu.VMEM)
    return pl.pallas_call(
        body, name="small_allreduce_adamw",
        in_specs=[vm, vm, vm, vm],
        out_specs=[vm, vm, vm, vm, vm],
        out_shape=[jax.ShapeDtypeStruct(shape, F32)] * 4 + [jax.ShapeDtypeStruct((8, LANES), F32)],
        scratch_shapes=[pltpu.VMEM((N_DEV,) + shape, F32),
                        pltpu.SemaphoreType.DMA((N_DEV - 1,)), pltpu.SemaphoreType.DMA((N_DEV - 1,))],
    )(part, w, m, v)


def _perm_in_cols(w):
    z = lambda n: jnp.zeros(w.shape[:-1] + (n,), w.dtype)
    small = jnp.concatenate([w[..., 384:400], z(16), w[..., 1952:1960], z(24), w[..., 400:416], z(48)], -1)
    return jnp.concatenate([w[..., 1960:2984], w[..., 2984:4008], w[..., 416:928], w[..., 928:1440],
                            w[..., 1440:1952], w[..., 0:256], w[..., 256:384], small], -1)


def _unperm_in_cols(wp):
    s = wp[..., P_SMALL:]
    return jnp.concatenate([wp[..., P_CQ:P_CQ + 256], wp[..., P_CKV:P_CKV + 128], s[..., 0:16], s[..., 64:80],
                            wp[..., P_FQ:P_FQ + 512], wp[..., P_FK:P_FK + 512], wp[..., P_FV:P_FV + 512],
                            s[..., S_FL:S_FL + 8], wp[..., P_GA:P_GA + 1024], wp[..., P_GB:P_GB + 1024]], -1)


def _aug_uq_cols(w):
    r = w.shape[0]
    w3 = w.reshape(r, N_HEADS, NOPE + ROPE)
    z = jnp.zeros((r, N_HEADS, 32), w.dtype)
    return jnp.concatenate([w3[:, :, 64:80], w3[:, :, 0:48], w3[:, :, 80:96], w3[:, :, 48:64], z], -1).reshape(r, 1024)


def _unaug_uq_cols(wp):
    r = wp.shape[0]
    w3 = wp.reshape(r, N_HEADS, LANES)
    return jnp.concatenate([w3[:, :, 16:64], w3[:, :, 80:96], w3[:, :, 0:16], w3[:, :, 64:80]], -1).reshape(r, 768)


def _aug_uk_cols(w):
    r = w.shape[0]
    w3 = w.reshape(r, N_HEADS, NOPE)
    z = lambda n: jnp.zeros((r, N_HEADS, n), w.dtype)
    return jnp.concatenate([z(16), w3[:, :, 0:48], z(16), w3[:, :, 48:64], z(32)], -1).reshape(r, 1024)


def _unaug_uk_cols(wp):
    r = wp.shape[0]
    w3 = wp.reshape(r, N_HEADS, LANES)
    return jnp.concatenate([w3[:, :, 16:64], w3[:, :, 80:96]], -1).reshape(r, 512)


IN_SEGMENTS = ((0, 256, P_CQ), (256, 128, P_CKV), (384, 16, P_SMALL), (400, 16, P_SMALL + 64), (416, 512, P_FQ),
               (928, 512, P_FK), (1440, 512, P_FV), (1952, 8, P_SMALL + S_FL), (1960, 1024, P_GA), (2984, 1024, P_GB))
IN_SHARD = D_IN // N_DEV


def _perm_in_from_shards(w3):
    r = w3.shape[0]
    parts = []
    pos = 0
    for o0, n, p0 in sorted(IN_SEGMENTS, key=lambda t: t[2]):
        if p0 > pos:
            parts.append(jnp.zeros((r, p0 - pos), w3.dtype))
        a, b = o0, o0 + n
        for d in range(a // IN_SHARD, (b - 1) // IN_SHARD + 1):
            parts.append(w3[:, d, max(a, d * IN_SHARD) - d * IN_SHARD:min(b, (d + 1) * IN_SHARD) - d * IN_SHARD])
        pos = p0 + n
    parts.append(jnp.zeros((r, D_IN_PAD - pos), w3.dtype))
    return jnp.concatenate(parts, -1)


def _unperm_in_to_shards(gp):
    r = gp.shape[0]
    shards = []
    for d in range(N_DEV):
        a, b = d * IN_SHARD, (d + 1) * IN_SHARD
        parts = [gp[:, p0 + max(a, o0) - o0:p0 + min(b, o0 + n) - o0] for o0, n, p0 in IN_SEGMENTS
                 if max(a, o0) < min(b, o0 + n)]
        parts.append(jnp.zeros((r, _lane_pad(IN_SHARD) - IN_SHARD), gp.dtype))
        shards.append(jnp.concatenate(parts, -1))
    return jnp.stack(shards, 0)


def _lane_pad(c):
    return -(-c // LANES) * LANES


def _packed_cols(c, by_col):
    return _lane_pad(c) if by_col else c


def _pack_shards(group, shards, dtype):
    rows = _pack_rows(group)
    segs = []
    for name, r, c, by_col in group:
        s = shards[name].reshape(r, c).astype(dtype)
        segs.append(jnp.pad(s, ((0, 0), (0, _packed_cols(c, by_col) - c))).reshape(-1))
    flat = jnp.concatenate(segs)
    flat = jnp.pad(flat, (0, rows * LANES - flat.shape[0]))
    return flat.reshape(rows, LANES)


def _unpack_shards(group, packed):
    flat = packed.reshape(-1)
    out, off = {}, 0
    for name, r, c, by_col in group:
        cp = _packed_cols(c, by_col)
        out[name] = flat[off:off + r * cp].reshape(1, r, cp)[:, :, :c]
        off += r * cp
    return out


def _unpack_full(group, gathered):
    flat = gathered.reshape(N_DEV, -1)
    out, off = {}, 0
    for name, r, c, by_col in group:
        cp = _packed_cols(c, by_col)
        blk = flat[:, off:off + r * cp].reshape(N_DEV, r, cp)
        out[name] = blk.transpose(1, 0, 2) if by_col else blk.reshape(N_DEV * r, c)
        off += r * cp
    return out


def _full_cols(w3, c):
    return w3[:, :, :c].reshape(w3.shape[0], N_DEV * c)


def _pack_full(group, grads, dtype):
    rows = _pack_rows(group)
    segs = []
    for name, r, c, by_col in group:
        g = grads[name]
        cp = _packed_cols(c, by_col)
        if by_col and g.ndim == 2:
            g = jnp.pad(g.reshape(r, N_DEV, c), ((0, 0), (0, 0), (0, cp - c))).transpose(1, 0, 2)
        segs.append(g.reshape(N_DEV, r * cp).astype(dtype))
    flat = jnp.concatenate(segs, axis=1)
    flat = jnp.pad(flat, ((0, 0), (0, rows * LANES - flat.shape[1])))
    return flat.reshape(N_DEV, rows, LANES)


SMALL_LAYOUT = (("ln_pre_mix", 1024, 0), ("ln_post_mix", 1024, 8), ("ln_pre_mlp", 1024, 16),
                ("ln_post_mlp", 1024, 24), ("b_in", 4008, 32), ("q_a_norm", 256, 64), ("kv_a_norm", 128, 66))


def _pack_small(vals, extra=None):
    rows = []
    for name, n, _ in SMALL_LAYOUT:
        v = vals[name].reshape(-1).astype(F32)
        pad = -n % LANES
        rows.append(jnp.pad(v, (0, pad)).reshape(-1, LANES))
    rows.append(jnp.zeros((SMALL_LOSS_ROW - 67, LANES), F32))
    rows.append(jnp.zeros((8, LANES), F32) if extra is None else extra.reshape(8, LANES))
    return jnp.concatenate(rows, axis=0)


def _unpack_small(packed):
    out = {}
    for name, n, r0 in SMALL_LAYOUT:
        nr = -(-n // LANES)
        out[name] = packed[r0:r0 + nr].reshape(-1)[:n].reshape(1, n)
    return out


def _rms(xf, g):
    r = lax.rsqrt(jnp.mean(xf * xf, axis=-1, keepdims=True) + NORM_EPS)
    return (xf * r) * g


def _rms_bwd(xf, g, dy):
    r = lax.rsqrt(jnp.mean(xf * xf, axis=-1, keepdims=True) + NORM_EPS)
    xhat = xf * r
    dxhat = dy * g
    dx = r * (dxhat - xhat * jnp.mean(dxhat * xhat, axis=-1, keepdims=True))
    return dx, dy * xhat


def _sigmoid(t):
    return 1.0 / (1.0 + jnp.exp(-t))


def _log_sigmoid(t):
    return jnp.minimum(t, 0.0) - jnp.log(1.0 + jnp.exp(-jnp.abs(t)))


def _lane_sign():
    lane = lax.broadcasted_iota(jnp.int32, (1, LANES), 1)
    return jnp.where(lane < 64, -1.0, 1.0).astype(F32), lane


def _rope_lanes(lane):
    return (lane < 16) | ((lane >= 64) & (lane < 80))


def kernel(x, positions, ln_pre_mix, ln_post_mix, ln_pre_mlp, ln_post_mlp, w_in, b_in, q_a_norm, w_uq, kv_a_norm, w_uk, w_uv, w_o_mla, w_o_fox, w_out, w_ff1, w_ff2, loss_target, m_ln_pre_mix, m_ln_post_mix, m_ln_pre_mlp, m_ln_post_mlp, m_w_in, m_b_in, m_q_a_norm, m_w_uq, m_kv_a_norm, m_w_uk, m_w_uv, m_w_o_mla, m_w_o_fox, m_w_out, m_w_ff1, m_w_ff2, v_ln_pre_mix, v_ln_post_mix, v_ln_pre_mlp, v_ln_post_mlp, v_w_in, v_b_in, v_q_a_norm, v_w_uq, v_kv_a_norm, v_w_uk, v_w_uv, v_w_o_mla, v_w_o_fox, v_w_out, v_w_ff1, v_w_ff2):
    T = x.shape[1]
    x2 = x.reshape(T, D_MODEL)
    tgt = loss_target.reshape(T, D_MODEL)
    w_sh = dict(w_in=w_in, w_uq=w_uq, w_uk=w_uk, w_uv=w_uv, w_o_mla=w_o_mla, w_o_fox=w_o_fox, w_out=w_out,
                w_ff1=w_ff1, w_ff2=w_ff2)
    m_sh = dict(w_in=m_w_in, w_uq=m_w_uq, w_uk=m_w_uk, w_uv=m_w_uv, w_o_mla=m_w_o_mla, w_o_fox=m_w_o_fox,
                w_out=m_w_out, w_ff1=m_w_ff1, w_ff2=m_w_ff2)
    v_sh = dict(w_in=v_w_in, w_uq=v_w_uq, w_uk=v_w_uk, w_uv=v_w_uv, w_o_mla=v_w_o_mla, w_o_fox=v_w_o_fox,
                w_out=v_w_out, w_ff1=v_w_ff1, w_ff2=v_w_ff2)
    small_w = dict(ln_pre_mix=ln_pre_mix, ln_post_mix=ln_post_mix, ln_pre_mlp=ln_pre_mlp, ln_post_mlp=ln_post_mlp,
                   b_in=b_in, q_a_norm=q_a_norm, kv_a_norm=kv_a_norm)
    small_m = dict(ln_pre_mix=m_ln_pre_mix, ln_post_mix=m_ln_post_mix, ln_pre_mlp=m_ln_pre_mlp,
                   ln_post_mlp=m_ln_post_mlp, b_in=m_b_in, q_a_norm=m_q_a_norm, kv_a_norm=m_kv_a_norm)
    small_v = dict(ln_pre_mix=v_ln_pre_mix, ln_post_mix=v_ln_post_mix, ln_pre_mlp=v_ln_pre_mlp,
                   ln_post_mlp=v_ln_post_mlp, b_in=v_b_in, q_a_norm=v_q_a_norm, kv_a_norm=v_kv_a_norm)
    mla_scale = float((NOPE + ROPE) ** -0.5)
    fox_scale = float(FOX_D ** -0.5)

    W = _unpack_full(GROUP_A, _comm_call("gather", _pack_shards(GROUP_A, w_sh, BF16), "allgather_weights_a"))
    w_in_p = _perm_in_from_shards(W["w_in"])
    b_in_p = _perm_in_cols(b_in.astype(F32))
    w_uq_a = _aug_uq_cols(_full_cols(W["w_uq"], 96))
    w_kv_a = jnp.concatenate([_aug_uk_cols(_full_cols(W["w_uk"], 64)), _full_cols(W["w_uv"], 64)], axis=1)
    g1, g2, g3, g4 = ln_pre_mix, ln_post_mix, ln_pre_mlp, ln_post_mlp
    gq, gkv = q_a_norm, kv_a_norm
    tq = min(ATTN_TILE, T)
    nq = T // tq

    (h,) = _rowwise(lambda xv, g: _rms(xv, g), [x2], [g1], [(D_MODEL, BF16)], name="pre_mix_norm")
    q_cols = jnp.ones((1, D_IN_PAD), F32).at[:, P_FQ:P_FQ + 512].set(fox_scale * LOG2E)
    z = _mm(h, w_in_p, bias=b_in_p, colscale=q_cols, tm=2048, name="in_proj")
    zs = _mm(h, w_in_p[:, P_SMALL:], bias=b_in_p[:, P_SMALL:], outs=(F32,), name="in_proj_small")

    def lora_norm(cq, ckv, a, b):
        return _rms(cq.astype(F32), a), _rms(ckv.astype(F32), b)

    cqn, ckvn = _rowwise(lora_norm, [(z, 256, P_CQ // 256), (z, 128, P_CKV // 128)], [gq, gkv],
                         [(Q_LORA, BF16), (KV_LORA, BF16)], name="lora_norm")
    q_x = _mm(cqn, w_uq_a, outs=(F32,), name="q_up")
    kv = _mm(ckvn, w_kv_a, tn=512, name="kv_up")

    half = ROPE // 2
    inv_freq = ROPE_THETA ** (-jnp.arange(half, dtype=F32) / half)
    inv128 = jnp.tile(inv_freq, LANES // half).reshape(1, LANES)
    pos_col = positions.reshape(T, 1).astype(F32)

    def rope_tables(p, f):
        ang = p * f
        return jnp.cos(ang), jnp.sin(ang)

    cos_t, sin_t = _rowwise(rope_tables, [pos_col], [inv128], [(LANES, F32), (LANES, F32)], name="rope_tables")

    def rope_fwd(qx, kn, vn, s, cs, sn):
        sign, lane = _lane_sign()
        rope_l = _rope_lanes(lane)
        rot = lambda t: t * cs + pltpu.roll(t, 64, 1) * sn * sign
        k_rot = jnp.where(rope_l, rot(s), 0.0)
        qs, ks = [], []
        for hd in range(N_HEADS):
            qb = qx[:, LANES * hd:LANES * (hd + 1)]
            qs.append(jnp.where(rope_l, rot(qb), qb))
            ks.append(kn[:, LANES * hd:LANES * (hd + 1)].astype(F32) + k_rot)
        return jnp.concatenate(qs, axis=1) * (mla_scale * LOG2E), jnp.concatenate(ks, axis=1), vn

    qt4_m, k_am, vt4_m = _rowwise(rope_fwd, [q_x, (kv, 1024, 0), (kv, 512, 2), zs, cos_t, sin_t], [],
                                  [(1024, BF16, "t4"), (1024, BF16), (512, BF16, "t4")], name="rope_fwd", tm=tq)
    f_cum = _seq_cumsum(zs, _log_sigmoid, reverse=False, name="forget_cumsum")

    def fox_aug(qf, kf, vf, fc):
        lane = lax.broadcasted_iota(jnp.int32, (1, LANES), 1)
        qs, ks = [], []
        for hd in range(N_HEADS):
            qp = qf[:, LANES * (hd // 2):LANES * (hd // 2 + 1)].astype(F32)
            kp = kf[:, LANES * (hd // 2):LANES * (hd // 2 + 1)].astype(F32)
            if hd % 2:
                qp = pltpu.roll(qp, 64, 1)
                kp = pltpu.roll(kp, 64, 1)
            fb = jnp.broadcast_to(fc[:, S_FL + hd:S_FL + hd + 1] * (-LOG2E), qp.shape)
            hi = fb.astype(BF16).astype(F32)
            mid = (fb - hi).astype(BF16).astype(F32)
            lo = fb - hi - mid
            qs.append(jnp.where(lane < 64, qp, jnp.where(lane < 67, 1.0, 0.0)))
            ks.append(jnp.where(lane < 64, kp, jnp.where(lane == 64, hi, jnp.where(lane == 65, mid, jnp.where(
                lane == 66, lo, jnp.where(lane == 67, 1.0, 0.0))))))
        return jnp.concatenate(qs, axis=1), jnp.concatenate(ks, axis=1), vf

    qt4_f, k_af, vt4_f = _rowwise(fox_aug, [(z, 512, P_FQ // 512), (z, 512, P_FK // 512), (z, 512, P_FV // 512), f_cum],
                                  [], [(1024, BF16, "t4"), (1024, BF16), (512, BF16, "t4")], name="fox_aug", tm=tq)
    o_mla, lse_mla, gathered_b = _attn_fwd(qt4_m, k_am, vt4_m, name="mla_attn_fwd", tq=tq,
                                           gather=_pack_shards(GROUP_B, w_sh, BF16))
    W.update(_unpack_full(GROUP_B, gathered_b))
    for n, _, c, by_col in GROUP_B:
        if by_col:
            W[n] = _full_cols(W[n], c)
    o_fox, lse_fox = _attn_fwd(qt4_f, k_af, vt4_f, name="fox_attn_fwd", tq=tq)
    y_mla = _mm(o_mla, W["w_o_mla"], outs=(F32,), name="o_proj_mla")
    y_fox = _mm(o_fox, W["w_o_fox"], outs=(F32,), name="o_proj_fox")

    def gate_merge(ga, gb, ya, yb):
        return _sigmoid(ga.astype(F32)) * ya + _sigmoid(gb.astype(F32)) * yb

    (merged,) = _rowwise(gate_merge, [(z, 1024, 0), (z, 1024, 1), y_mla, y_fox], [], [(D_MODEL, BF16)],
                         name="gate_merge")
    mix = _mm(merged, W["w_out"], outs=(F32,), name="out_proj")

    def post_mix(xv, mv, a, b):
        x1v = xv + _rms(mv, a)
        return x1v, _rms(x1v, b)

    x1, h2 = _rowwise(post_mix, [x2, mix], [g2, g3], [(D_MODEL, F32), (D_MODEL, BF16)], name="post_mix_norm")

    def relu2(acc):
        r = jnp.maximum(acc, 0.0)
        return r * r, acc

    act, u = _mm(h2, W["w_ff1"], epi=relu2, outs=(BF16, BF16), tm=2048, name="ff1")
    mlp = _mm(act, W["w_ff2"], outs=(F32,), tm=1024, tk=2048, name="ff2")

    def loss_bwd(x1v, mv, tv, g):
        y = x1v + _rms(mv, g)
        d = y - tv
        dy = d * (1.0 / D_MODEL)
        dm, dg = _rms_bwd(mv, g, dy)
        return dy, dm, dg, d * d

    dy, dm, dg4, loss_cols = _rowwise(loss_bwd, [x1, mlp, tgt], [g4], [(D_MODEL, F32), (D_MODEL, BF16)],
                                      [D_MODEL, D_MODEL], name="loss_bwd")

    def relu2_bwd(acc, uv):
        return (acc * (2.0 * jnp.maximum(uv.astype(F32), 0.0)),)

    du = _mm(dm, W["w_ff2"], tb=True, extras=(u,), epi=relu2_bwd, tm=2048, name="ff2_bwd")
    dw_ff2 = _mm(act, dm, ta=True, outs=(F32,), tk=2048, name="ff2_wgrad")
    dh2 = _mm(du, W["w_ff1"], tb=True, outs=(F32,), tm=1024, tk=2048, name="ff1_bwd")
    dw_ff1 = _mm(h2, du, ta=True, outs=(F32,), tk=2048, name="ff1_wgrad")

    def post_mix_bwd(x1v, dh2v, dyv, mv, a, b):
        d3, dg3v = _rms_bwd(x1v, b, dh2v)
        dx1v = dyv + d3
        dmixv, dg2v = _rms_bwd(mv, a, dx1v)
        return dx1v, dmixv, dg3v, dg2v

    dx1, dmix, dg3, dg2 = _rowwise(post_mix_bwd, [x1, dh2, dy, mix], [g2, g3], [(D_MODEL, F32), (D_MODEL, BF16)],
                                   [D_MODEL, D_MODEL], name="post_mix_bwd")
    dmerged = _mm(dmix, W["w_out"], tb=True, outs=(F32,), name="out_proj_bwd")
    dw_out = _mm(merged, dmix, ta=True, outs=(F32,), name="out_proj_wgrad")

    def gate_bwd(dmg, ga, gb, ya, yb):
        sa = _sigmoid(ga.astype(F32))
        sb = _sigmoid(gb.astype(F32))
        return dmg * sa, dmg * sb, dmg * ya * sa * (1.0 - sa), dmg * yb * sb * (1.0 - sb)

    dy_mla, dy_fox, dga, dgb = _rowwise(gate_bwd, [dmerged, (z, 1024, 0), (z, 1024, 1), y_mla, y_fox], [],
                                        [(D_MODEL, BF16)] * 4, name="gate_bwd")
    twice = lambda acc: (acc, acc)
    do_mla, dot4_m = _mm(dy_mla, W["w_o_mla"], tb=True, epi=twice, outs=(BF16, BF16), t4=(False, True), tm=tq,
                         name="o_proj_mla_bwd")
    do_fox, dot4_f = _mm(dy_fox, W["w_o_fox"], tb=True, epi=twice, outs=(BF16, BF16), t4=(False, True), tm=tq,
                         name="o_proj_fox_bwd")
    dw_o_mla = _mm(o_mla, dy_mla, ta=True, outs=(F32,), name="o_proj_mla_wgrad")
    dw_o_fox = _mm(o_fox, dy_fox, ta=True, outs=(F32,), name="o_proj_fox_wgrad")

    def head_dots(ov, dov):
        r = lax.broadcasted_iota(jnp.int32, (N_HEADS * HEAD_V, LANES), 0)
        cc = lax.broadcasted_iota(jnp.int32, (N_HEADS * HEAD_V, LANES), 1)
        sel = jnp.where((r // HEAD_V) == cc, 1.0, 0.0).astype(BF16)
        return _dot3(ov.astype(F32) * dov.astype(F32), sel, left=False)

    (dl_mla,) = _rowwise(head_dots, [o_mla, do_mla], [], [(LANES, F32)], name="mla_attn_delta")
    (dl_fox,) = _rowwise(head_dots, [o_fox, do_fox], [], [(LANES, F32)], name="fox_attn_delta")
    heads_row4 = lambda t: t[:, :N_HEADS].T.reshape(N_HEADS, nq, 1, tq)
    grads_b = dict(w_o_mla=dw_o_mla, w_o_fox=dw_o_fox, w_out=dw_out, w_ff1=dw_ff1, w_ff2=dw_ff2)
    dqt_m, dkt_m, dv_mla, pieces_b = _attn_bwd(qt4_m, k_am, kv, 1024 // LANES, dot4_m, lse_mla,
                                               heads_row4(dl_mla), mla_scale, NOPE + ROPE, name="mla_attn_bwd", tq=tq,
                                               exchange=_pack_full(GROUP_B, grads_b, BF16))
    dqt_f, dkt_f, dv_fox = _attn_bwd(qt4_f, k_af, z, P_FV // LANES, dot4_f, lse_fox, heads_row4(dl_fox),
                                     fox_scale, FOX_D, name="fox_attn_bwd", tq=tq)

    def fox_unpack(dqa, dka):
        lane = lax.broadcasted_iota(jnp.int32, (1, LANES), 1)
        dqs, dks = [], []
        d_f = jnp.zeros(dqa[:, :LANES].shape, F32)
        for hp in range(N_HEADS // 2):
            blk = lambda t, e: t[:, LANES * (2 * hp + e):LANES * (2 * hp + e + 1)]
            dqs.append(jnp.where(lane < 64, blk(dqa, 0), pltpu.roll(blk(dqa, 1), 64, 1)))
            dks.append(jnp.where(lane < 64, blk(dka, 0), pltpu.roll(blk(dka, 1), 64, 1)))
            for e in range(2):
                g = blk(dqa, e)[:, 67:68] - blk(dka, e)[:, 64:65]
                d_f = jnp.where(lane == S_FL + 2 * hp + e, g, d_f)
        return jnp.concatenate(dqs, axis=1), jnp.concatenate(dks, axis=1), d_f

    dq_fox, dk_fox, d_f128 = _rowwise(fox_unpack, [("t4", dqt_f), ("t4", dkt_f)], [],
                                      [(512, BF16), (512, BF16), (LANES, F32)], name="fox_unpack", tm=tq)
    df_rev = _seq_cumsum(d_f128, lambda t: t, reverse=True, name="forget_cumsum_bwd")

    def rope_bwd(dqa, dka, dvm, dfr, s, cs, sn):
        sign, lane = _lane_sign()
        rope_l = _rope_lanes(lane)
        rot_t = lambda t: t * cs - pltpu.roll(t, 64, 1) * sn * sign
        dqs = []
        dk_rot = None
        for hd in range(N_HEADS):
            blk = dqa[:, LANES * hd:LANES * (hd + 1)]
            dqs.append(jnp.where(rope_l, rot_t(blk), blk))
            blk = dka[:, LANES * hd:LANES * (hd + 1)]
            dk_rot = blk if dk_rot is None else dk_rot + blk
        dfl = dfr * _sigmoid(-s)
        small = jnp.where(rope_l, rot_t(dk_rot), jnp.where((lane >= S_FL) & (lane < S_FL + N_HEADS), dfl, 0.0))
        return jnp.concatenate(dqs, axis=1), jnp.concatenate([dka, dvm.astype(F32)], axis=1), small

    dq_b, dkv, d_small = _rowwise(rope_bwd, [("t4", dqt_m), ("t4", dkt_m), dv_mla, df_rev, zs, cos_t, sin_t], [],
                                  [(1024, BF16), (1536, BF16), (LANES, BF16)], name="rope_bwd", tm=tq)
    dcqn = _mm(dq_b, w_uq_a, tb=True, outs=(F32,), name="q_up_bwd")
    dw_uq_a = _mm(cqn, dq_b, ta=True, outs=(F32,), name="q_up_wgrad")
    dckvn = _mm(dkv, w_kv_a, tb=True, outs=(F32,), tk=512, name="kv_up_bwd")
    dw_kv_a = _mm(ckvn, dkv, ta=True, outs=(F32,), tn=512, name="kv_up_wgrad")

    def lora_norm_bwd(cq, ckv, dq_, dkv_, a, b):
        d1, dga_ = _rms_bwd(cq.astype(F32), a, dq_)
        d2, dgb_ = _rms_bwd(ckv.astype(F32), b, dkv_)
        return d1, d2, dga_, dgb_

    dcq, dckv, dgq, dgkv = _rowwise(lora_norm_bwd, [(z, 256, P_CQ // 256), (z, 128, P_CKV // 128), dcqn, dckvn],
                                    [gq, gkv], [(Q_LORA, BF16), (KV_LORA, BF16)], [Q_LORA, KV_LORA],
                                    name="lora_norm_bwd")
    dz = jnp.concatenate([dga, dgb, dq_fox, dk_fox, dv_fox, dcq, dckv, d_small], -1)
    dw_in_p, db_in_p = _mm(h, dz, ta=True, outs=(F32,), tk=2048, colsum=True, name="in_proj_wgrad")
    grads_a = dict(w_in=_unperm_in_to_shards(dw_in_p), w_uq=_unaug_uq_cols(dw_uq_a),
                   w_uk=_unaug_uk_cols(dw_kv_a[:, :1024]), w_uv=dw_kv_a[:, 1024:])
    dh, pieces_a = _mm(dz, w_in_p, tb=True, outs=(F32,), tm=1024, tk=2048, name="in_proj_bwd",
                       exchange=_pack_full(GROUP_A, grads_a, BF16))

    def pre_mix_bwd(xv, dhv, dx1v, g):
        d, dg = _rms_bwd(xv, g, dhv)
        return dx1v + d, dg

    grad_x, dg1 = _rowwise(pre_mix_bwd, [x2, dh, dx1], [g1], [(D_MODEL, F32)], [D_MODEL], name="pre_mix_bwd")

    grad_sh, delta_sh, newm_sh, newv_sh = {}, {}, {}, {}
    for group, pieces, tag in ((GROUP_A, pieces_a, "a"), (GROUP_B, pieces_b, "b")):
        packed = _sum_adamw(pieces, _pack_shards(group, w_sh, F32), _pack_shards(group, m_sh, F32),
                            _pack_shards(group, v_sh, F32), name="sum_pieces_adamw_" + tag)
        for dst, arr in zip((grad_sh, delta_sh, newm_sh, newv_sh), packed):
            dst.update(_unpack_shards(group, arr))

    small_part = _pack_small(dict(ln_pre_mix=dg1, ln_post_mix=dg2, ln_pre_mlp=dg3, ln_post_mlp=dg4,
                                  b_in=_unperm_in_cols(db_in_p), q_a_norm=dgq, kv_a_norm=dgkv), extra=loss_cols)
    sg, sd, sm, sv, loss_blk = _small_allreduce_adamw(small_part, _pack_small(small_w), _pack_small(small_m),
                                                      _pack_small(small_v))
    grad_sm, delta_sm, newm_sm, newv_sm = (_unpack_small(t) for t in (sg, sd, sm, sv))
    loss = loss_blk[0, 0]

    order = ["ln_pre_mix", "ln_post_mix", "ln_pre_mlp", "ln_post_mlp", "w_in", "b_in", "q_a_norm", "w_uq",
             "kv_a_norm", "w_uk", "w_uv", "w_o_mla", "w_o_fox", "w_out", "w_ff1", "w_ff2"]

    def pick(sm_d, sh_d):
        return [sm_d[n] if n in sm_d else sh_d[n] for n in order]

    return (loss, grad_x.reshape(1, T, D_MODEL), *pick(grad_sm, grad_sh), *pick(delta_sm, delta_sh),
            *pick(newm_sm, newm_sh), *pick(newv_sm, newv_sh))
```

```python
import numpy as np
import jax
import jax.numpy as jnp
from jax import lax
from jax.experimental import pallas as pl
from jax.experimental.pallas import tpu as pltpu

F32 = jnp.float32
BF16 = jnp.bfloat16
MESH = pl.DeviceIdType.MESH

D_MODEL = 1024
N_HEADS = 8
Q_LORA = 256
KV_LORA = 128
NOPE = 64
ROPE = 32
HEAD_V = 64
FOX_D = 64
D_FF = 4096
D_IN = 4008
D_IN_PAD = 4096
ROPE_THETA = 10000.0
NORM_EPS = 1e-6
N_DEV = 8

ADAM_LR = 0.001
ADAM_B1 = 0.9
ADAM_B2 = 0.999
ADAM_EPS = 1e-08
ADAM_WD = 0.01
ADAM_STEP = 10

LANES = 128
ROW_TILE = 512
ATTN_TILE = 512
ATTN_CHUNK = 256
VMEM_LIMIT = 48 * 1024 * 1024
ATTN_BWD_VMEM_LIMIT = 58 * 1024 * 1024

P_GA, P_GB, P_FQ, P_FK, P_FV, P_CQ, P_CKV, P_SMALL = 0, 1024, 2048, 2560, 3072, 3584, 3840, 3968
S_FL = 32

SHARDED = (
    ("w_in", 1024, 501, True), ("w_uq", 256, 96, True), ("w_uk", 128, 64, True), ("w_uv", 128, 64, True),
    ("w_o_mla", 512, 128, True), ("w_o_fox", 512, 128, True), ("w_out", 128, 1024, False),
    ("w_ff1", 1024, 512, True), ("w_ff2", 512, 1024, False),
)
GROUP_A = SHARDED[:4]
GROUP_B = SHARDED[4:]
SMALL_ROWS = 80
SMALL_LOSS_ROW = 72


def _pack_rows(group):
    return -(-sum(r * _packed_cols(c, by_col) for _, r, c, by_col in group) // (LANES * 64)) * 64


def _cparams(sem=None):
    return pltpu.CompilerParams(dimension_semantics=sem, vmem_limit_bytes=VMEM_LIMIT)


def _mm(a, b, *, name, ta=False, tb=False, bias=None, colscale=None, extras=(), epi=None, outs=(BF16,), t4=None,
        tm=ROW_TILE, tn=1024, tk=1024, exchange=None, colsum=False):
    t4 = (False,) * len(outs) if t4 is None else t4
    comm = exchange is not None
    if ta:
        K, M = a.shape
        tm = min(1024, M)
    else:
        M, K = a.shape
        tm = min(tm, M)
    tk = min(tk, K)
    N = b.shape[0] if tb else b.shape[1]
    tn = min(tn, N)
    nk = K // tk
    assert not colsum or (ta and M == tm)
    n_ex = len(extras)
    has_bias = bias is not None
    has_scale = colscale is not None

    def body(*refs):
        a_ref, b_ref = refs[0], refs[1]
        pos = 2
        bias_ref = scale_ref = None
        if has_bias:
            bias_ref = refs[pos]
            pos += 1
        if has_scale:
            scale_ref = refs[pos]
            pos += 1
        ex_refs = refs[pos:pos + n_ex]
        pos += n_ex
        if comm:
            c_src = refs[pos]
            pos += 1
        o_refs = refs[pos:pos + len(outs)]
        pos += len(outs)
        if colsum:
            cs_ref = refs[pos]
            pos += 1
        if comm:
            c_dst = refs[pos]
            pos += 1
            c_sems = refs[len(refs) - 3:]
            ids = [pl.program_id(d) for d in range(3)]

            @pl.when((ids[0] == 0) & (ids[1] == 0) & (ids[2] == 0))
            def _():
                _comm_start("exchange", c_src, c_dst, *c_sems)

        av = a_ref[...].astype(BF16)
        bv = b_ref[...].astype(BF16)
        if ta:
            part = lax.dot_general(av, bv, (((0,), (0,)), ((), ())), preferred_element_type=F32)
        elif tb:
            part = lax.dot_general(av, bv, (((1,), (1,)), ((), ())), preferred_element_type=F32)
        else:
            part = jnp.dot(av, bv, preferred_element_type=F32)
        if colsum:
            cs = jnp.sum(bv.astype(F32), axis=0, keepdims=True)

            @pl.when(pl.program_id(2) == 0)
            def _():
                cs_ref[...] = cs

            @pl.when(pl.program_id(2) > 0)
            def _():
                cs_ref[...] += cs

        def finish(acc):
            if has_bias:
                acc = acc + bias_ref[...]
            if has_scale:
                acc = acc * scale_ref[...]
            res = (acc,) if epi is None else epi(acc, *[r[...] for r in ex_refs])
            for o_ref, val, t in zip(o_refs, res, t4):
                if t:
                    _t4_store(o_ref, val)
                else:
                    o_ref[...] = val.astype(o_ref.dtype)

        if nk == 1:
            finish(part)
        else:
            acc_ref = refs[pos]
            k = pl.program_id(2)

            @pl.when(k == 0)
            def _():
                acc_ref[...] = part

            @pl.when(k > 0)
            def _():
                acc_ref[...] += part

            @pl.when(k == nk - 1)
            def _():
                finish(acc_ref[...])

        if comm:
            @pl.when((ids[0] == M // tm - 1) & (ids[1] == N // tn - 1) & (ids[2] == nk - 1))
            def _():
                _comm_wait("exchange", c_src, c_dst, *c_sems)

    if ta:
        a_spec = pl.BlockSpec((tk, tm), lambda i, j, k: (k, i))
    else:
        a_spec = pl.BlockSpec((tm, tk), lambda i, j, k: (i, k))
    b_spec = pl.BlockSpec((tn, tk), lambda i, j, k: (j, k)) if tb else pl.BlockSpec((tk, tn), lambda i, j, k: (k, j))
    in_specs = [a_spec, b_spec]
    args = [a, b]
    for row in (bias, colscale):
        if row is not None:
            in_specs.append(pl.BlockSpec((1, tn), lambda i, j, k: (0, j)))
            args.append(row)
    for e in extras:
        in_specs.append(pl.BlockSpec((tm, tn), lambda i, j, k: (i, j)))
        args.append(e)
    out_specs = [pl.BlockSpec((N // LANES, 1, LANES, tm), lambda i, j, k: (0, i, 0, 0)) if t
                 else pl.BlockSpec((tm, tn), lambda i, j, k: (i, j)) for t in t4]
    out_shape = [jax.ShapeDtypeStruct((N // LANES, M // tm, LANES, tm) if t else (M, N), dt)
                 for dt, t in zip(outs, t4)]
    scratch = [pltpu.VMEM((tm, tn), F32)] if nk > 1 else []
    if colsum:
        out_specs.append(pl.BlockSpec((1, tn), lambda i, j, k: (0, j)))
        out_shape.append(jax.ShapeDtypeStruct((1, N), F32))
    if comm:
        in_specs.append(pl.BlockSpec(memory_space=pl.ANY))
        args.append(exchange)
        out_specs.append(pl.BlockSpec(memory_space=pl.ANY))
        out_shape.append(jax.ShapeDtypeStruct(exchange.shape, exchange.dtype))
        scratch += _comm_sems()
    res = pl.pallas_call(
        body, name=name, grid=(M // tm, N // tn, nk),
        in_specs=in_specs, out_specs=out_specs, out_shape=out_shape, scratch_shapes=scratch,
        compiler_params=_cparams(("arbitrary",) * 3 if comm else ("parallel", "parallel", "arbitrary")),
    )(*args)
    return res[0] if len(res) == 1 else res


def _t4_store(o_ref, val):
    for c in range(o_ref.shape[0]):
        o_ref[c, 0] = val[:, c * LANES:(c + 1) * LANES].astype(F32).T.astype(o_ref.dtype)


def _rowwise(fn, rows, bcasts, outs, accs=(), *, name, tm=ROW_TILE):
    t4_in = [isinstance(r, tuple) and isinstance(r[0], str) for r in rows]
    T = [r[1].shape[1] * r[1].shape[3] if t else (r[0] if isinstance(r, tuple) else r).shape[0]
         for r, t in zip(rows, t4_in)][0]
    tm = min(tm, T)
    arrs, specs = [], []
    for r, t in zip(rows, t4_in):
        if t:
            arr = r[1]
            specs.append(pl.BlockSpec((arr.shape[0], 1, LANES, tm), lambda i: (0, i, 0, 0)))
        elif isinstance(r, tuple):
            arr, w, cb = r
            specs.append(pl.BlockSpec((tm, w), lambda i, cb=cb: (i, cb)))
        else:
            arr = r
            specs.append(pl.BlockSpec((tm, arr.shape[1]), lambda i: (i, 0)))
        arrs.append(arr)
    n_rows = len(arrs)
    for b in bcasts:
        arrs.append(b)
        specs.append(pl.BlockSpec(b.shape, lambda i: (0, 0)))
    n_in, n_out = len(arrs), len(outs)
    t4_out = [len(o) == 3 for o in outs]

    def body(*refs):
        vals = []
        for k, r in enumerate(refs[:n_in]):
            if k < n_rows and t4_in[k]:
                vals.append(jnp.concatenate([r[c, 0].astype(F32).T for c in range(r.shape[0])], axis=1))
            else:
                vals.append(r[...])
        res = fn(*vals)
        if not isinstance(res, (tuple, list)):
            res = (res,)
        for o_ref, val, t in zip(refs[n_in:n_in + n_out], res[:n_out], t4_out):
            if t:
                _t4_store(o_ref, val)
            else:
                o_ref[...] = val.astype(o_ref.dtype)
        i = pl.program_id(0)
        for a_ref, val in zip(refs[n_in + n_out:], res[n_out:]):
            col = jnp.sum(val.astype(F32), axis=0, keepdims=True)

            @pl.when(i == 0)
            def _(a_ref=a_ref, col=col):
                a_ref[...] = col

            @pl.when(i > 0)
            def _(a_ref=a_ref, col=col):
                a_ref[...] += col

    res = pl.pallas_call(
        body, name=name, grid=(T // tm,),
        in_specs=specs,
        out_specs=[pl.BlockSpec((o[0] // LANES, 1, LANES, tm), lambda i: (0, i, 0, 0)) if t
                   else pl.BlockSpec((tm, o[0]), lambda i: (i, 0)) for o, t in zip(outs, t4_out)]
        + [pl.BlockSpec((1, c), lambda i: (0, 0)) for c in accs],
        out_shape=[jax.ShapeDtypeStruct((o[0] // LANES, T // tm, LANES, tm) if t else (T, o[0]), o[1])
                   for o, t in zip(outs, t4_out)]
        + [jax.ShapeDtypeStruct((1, c), F32) for c in accs],
        compiler_params=_cparams(("arbitrary",)),
    )(*arrs)
    return res


def _dot3(x, sel, left):
    hi = x.astype(BF16)
    r1 = x - hi.astype(F32)
    mid = r1.astype(BF16)
    lo = (r1 - mid.astype(F32)).astype(BF16)
    if left:
        d = lambda t: jnp.dot(sel, t, preferred_element_type=F32)
    else:
        d = lambda t: jnp.dot(t, sel, preferred_element_type=F32)
    return d(hi) + d(mid) + d(lo)


def _seq_cumsum(x, fn, *, reverse, name, tm=256):
    T, C = x.shape
    tm = min(tm, T)
    n = T // tm

    def body(x_ref, o_ref, carry_ref):
        i = pl.program_id(0)

        @pl.when(i == 0)
        def _():
            carry_ref[...] = jnp.zeros_like(carry_ref)

        v = fn(x_ref[...])
        r = lax.broadcasted_iota(jnp.int32, (tm, tm), 0)
        c = lax.broadcasted_iota(jnp.int32, (tm, tm), 1)
        tri = jnp.where((r <= c) if reverse else (r >= c), 1.0, 0.0).astype(BF16)
        carry = carry_ref[0:1, :]
        o_ref[...] = _dot3(v, tri, left=True) + carry
        carry_ref[0:1, :] = carry + jnp.sum(v, axis=0, keepdims=True)

    idx = (lambda i: (n - 1 - i, 0)) if reverse else (lambda i: (i, 0))
    return pl.pallas_call(
        body, name=name, grid=(n,),
        in_specs=[pl.BlockSpec((tm, C), idx)],
        out_specs=pl.BlockSpec((tm, C), idx),
        out_shape=jax.ShapeDtypeStruct((T, C), F32),
        scratch_shapes=[pltpu.VMEM((8, C), F32)],
        compiler_params=_cparams(("arbitrary",)),
    )(x)


_NT = (((1,), (1,)), ((), ()))
LOG2E = 1.4426950408889634


def _head_mask(width, e):
    lane = lax.broadcasted_iota(jnp.int32, (1, width), 1)
    return (lane >= 64 * e) & (lane < 64 * (e + 1))


def _attn_fwd(qt4, k_aug, vt4, *, name, tq, gather=None):
    nq = qt4.shape[1]
    T = nq * tq
    cw = min(ATTN_CHUNK, tq)
    comm = gather is not None

    def body(*refs):
        refs = list(refs)
        q_ref, k_ref, v_ref = refs[:3]
        pos = 3
        if comm:
            c_src = refs[pos]
            pos += 1
        o_ref, lse_ref = refs[pos:pos + 2]
        pos += 2
        if comm:
            c_dst = refs[pos]
            pos += 1
        m_s, l_s, acc_s = refs[pos:pos + 2], refs[pos + 2:pos + 4], refs[pos + 4:pos + 6]
        st_a, st_b = refs[pos + 6:pos + 8]
        c_sems = refs[pos + 8:]
        i = pl.program_id(1)
        if comm:
            @pl.when((pl.program_id(0) == 0) & (i == 0))
            def _():
                _comm_start("gather", c_src, c_dst, *c_sems)

        for e in range(2):
            m_s[e][...] = jnp.full(m_s[e].shape, -jnp.inf, F32)
            l_s[e][...] = jnp.zeros_like(l_s[e])
            acc_s[e][...] = jnp.zeros_like(acc_s[e])
        qt = [q_ref[e, 0] for e in range(2)]

        work = [(e, slice(c * cw, (c + 1) * cw)) for e in range(2) for c in range(tq // cw)]

        def scores(j, buf):
            kj = k_ref[pl.ds(pl.multiple_of(j * tq, tq), tq), :]
            for n, (e, cs) in enumerate(work):
                buf[n] = jnp.dot(kj[:, e * LANES:(e + 1) * LANES], qt[e][:, cs], preferred_element_type=F32)

        def step(j, buf, masked):
            vtj = v_ref[0, j]
            for n, (e, cs) in enumerate(work):
                st = buf[n]
                if masked:
                    r = lax.broadcasted_iota(jnp.int32, (tq, cw), 0)
                    cc = lax.broadcasted_iota(jnp.int32, (tq, cw), 1) + cs.start
                    st = jnp.where(cc >= r, st, -jnp.inf)
                m_prev = m_s[e][:, cs]
                m_new = jnp.maximum(m_prev, jnp.max(st, axis=0, keepdims=True))
                alpha = jnp.exp2(m_prev - m_new)
                pt = jnp.exp2(st - m_new)
                l_s[e][:, cs] = alpha * l_s[e][:, cs] + jnp.sum(pt, axis=0, keepdims=True)
                acc_s[e][:, cs] = alpha * acc_s[e][:, cs] + jnp.dot(vtj[e * HEAD_V:(e + 1) * HEAD_V, :], pt.astype(BF16),
                                                                    preferred_element_type=F32)
                m_s[e][:, cs] = m_new

        def two_tiles(p, carry):
            scores(2 * p + 1, st_b)
            step(2 * p, st_a, False)
            scores(2 * p + 2, st_a)
            step(2 * p + 1, st_b, False)
            return carry

        scores(0, st_a)
        lax.fori_loop(0, i // 2, two_tiles, 0)

        @pl.when(i % 2 == 0)
        def _():
            step(i, st_a, True)

        @pl.when(i % 2 == 1)
        def _():
            scores(i, st_b)
            step(i - 1, st_a, False)
            step(i, st_b, True)

        ot = jnp.concatenate([acc_s[e][...] / l_s[e][...] for e in range(2)], axis=0)
        o_ref[...] = ot.T.astype(o_ref.dtype)
        for e in range(2):
            lse_ref[e, 0] = m_s[e][...] + jnp.log(l_s[e][...]) * LOG2E
        if comm:
            @pl.when((pl.program_id(0) == N_HEADS // 2 - 1) & (i == nq - 1))
            def _():
                _comm_wait("gather", c_src, c_dst, *c_sems)

    in_specs = [
        pl.BlockSpec((2, 1, LANES, tq), lambda hp, i: (hp, i, 0, 0)),
        pl.BlockSpec((T, 2 * LANES), lambda hp, i: (0, hp)),
        pl.BlockSpec((1, nq, LANES, tq), lambda hp, i: (hp, 0, 0, 0)),
    ]
    args = [qt4, k_aug, vt4]
    out_specs = [pl.BlockSpec((tq, LANES), lambda hp, i: (i, hp)),
                 pl.BlockSpec((2, 1, 1, tq), lambda hp, i: (hp, i, 0, 0))]
    out_shape = [jax.ShapeDtypeStruct((T, N_HEADS * HEAD_V), BF16), jax.ShapeDtypeStruct((N_HEADS, nq, 1, tq), F32)]
    scratch = ([pltpu.VMEM((1, tq), F32)] * 4 + [pltpu.VMEM((HEAD_V, tq), F32)] * 2
               + [pltpu.VMEM((2 * tq // cw, tq, cw), F32)] * 2)
    if comm:
        in_specs.append(pl.BlockSpec(memory_space=pl.ANY))
        args.append(gather)
        out_specs.append(pl.BlockSpec(memory_space=pl.ANY))
        out_shape.append(jax.ShapeDtypeStruct((N_DEV,) + gather.shape, gather.dtype))
        scratch += _comm_sems()
    return pl.pallas_call(
        body, name=name, grid=(N_HEADS // 2, nq),
        in_specs=in_specs, out_specs=out_specs, out_shape=out_shape, scratch_shapes=scratch,
        compiler_params=_cparams(("arbitrary", "arbitrary")),
    )(*args)


def _attn_bwd(qt4, k_aug, v, vcb, dot4, lse_row, dl_row, scale, own_rows, *, name, tq, exchange=None):
    nq = qt4.shape[1]
    T = nq * tq
    cw = min(ATTN_CHUNK, tq)
    comm = exchange is not None

    def body(*refs):
        refs = list(refs)
        k_ref, v_ref, q_ref, do_ref, lse_ref, dl_ref = refs[:6]
        pos = 6
        if comm:
            c_src = refs[pos]
            pos += 1
        dq_ref, dk_ref, dv_ref = refs[pos:pos + 3]
        pos += 3
        if comm:
            c_dst = refs[pos]
            pos += 1
        dk_s, dv_s = refs[pos:pos + 2], refs[pos + 2:pos + 4]
        c_sems = refs[pos + 4:]
        j = pl.program_id(1)
        if comm:
            @pl.when((pl.program_id(0) == 0) & (j == 0))
            def _():
                _comm_start("exchange", c_src, c_dst, *c_sems)

        @pl.when(j == 0)
        def _():
            dq_ref[...] = jnp.zeros_like(dq_ref)

        kj = k_ref[...]
        vj = v_ref[...]
        ka = [kj[:, e * LANES:(e + 1) * LANES] for e in range(2)]
        row = lax.broadcasted_iota(jnp.int32, (LANES, 1), 0)
        own = row < own_rows
        kat = [(ka[e].astype(F32).T * jnp.where(own, scale, 1.0)).astype(BF16) for e in range(2)]
        vm = [jnp.where(_head_mask(LANES, e), vj, jnp.zeros_like(vj)) for e in range(2)]
        for e in range(2):
            dk_s[e][...] = jnp.zeros_like(dk_s[e])
            dv_s[e][...] = jnp.zeros_like(dv_s[e])

        def step(i, masked):
            dot_i = do_ref[0, i]
            qts = [q_ref[e, i] for e in range(2)]
            lse = [lse_ref[e, i] for e in range(2)]
            dl = [dl_ref[e, i] for e in range(2)]
            work = [(e, slice(c * cw, (c + 1) * cw)) for e in range(2) for c in range(tq // cw)]
            scores = lambda e, cs: (jnp.dot(ka[e], qts[e][:, cs], preferred_element_type=F32),
                                    jnp.dot(vm[e], dot_i[:, cs], preferred_element_type=F32))
            nxt = scores(*work[0])
            dqs = [[], []]
            for n, (e, cs) in enumerate(work):
                st, dpt = nxt
                if n + 1 < len(work):
                    nxt = scores(*work[n + 1])
                if masked:
                    r = lax.broadcasted_iota(jnp.int32, (tq, cw), 0)
                    cc = lax.broadcasted_iota(jnp.int32, (tq, cw), 1) + cs.start
                    st = jnp.where(cc >= r, st, -jnp.inf)
                pt = jnp.exp2(st - lse[e][:, cs])
                dv_s[e][...] += lax.dot_general(dot_i[e * HEAD_V:(e + 1) * HEAD_V, cs], pt.astype(BF16), _NT,
                                                preferred_element_type=F32)
                dsb = (pt * (dpt - dl[e][:, cs])).astype(BF16)
                dk_s[e][...] += lax.dot_general(qts[e][:, cs], dsb, _NT, preferred_element_type=F32)
                dqs[e].append(jnp.dot(kat[e], dsb, preferred_element_type=F32))
            for e in range(2):
                dq_ref[e, i] += jnp.concatenate(dqs[e], axis=1)

        def loop_body(i, carry):
            step(i, False)
            return carry

        step(j, True)
        lax.fori_loop(j + 1, nq, loop_body, 0)
        for e in range(2):
            dk_ref[e, 0] = dk_s[e][...] * jnp.where(own, 1.0 / LOG2E, 1.0)
        dv_ref[...] = jnp.concatenate([dv_s[0][...], dv_s[1][...]], axis=0).T.astype(dv_ref.dtype)
        if comm:
            @pl.when((pl.program_id(0) == N_HEADS // 2 - 1) & (j == nq - 1))
            def _():
                _comm_wait("exchange", c_src, c_dst, *c_sems)

    row4 = pl.BlockSpec((2, nq, 1, tq), lambda hp, j: (hp, 0, 0, 0))
    in_specs = [
        pl.BlockSpec((tq, 2 * LANES), lambda hp, j: (j, hp)),
        pl.BlockSpec((tq, LANES), lambda hp, j: (j, vcb + hp)),
        pl.BlockSpec((2, nq, LANES, tq), lambda hp, j: (hp, 0, 0, 0)),
        pl.BlockSpec((1, nq, LANES, tq), lambda hp, j: (hp, 0, 0, 0)),
        row4, row4,
    ]
    args = [k_aug, v, qt4, dot4, lse_row, dl_row]
    out_specs = [pl.BlockSpec((2, nq, LANES, tq), lambda hp, j: (hp, 0, 0, 0)),
                 pl.BlockSpec((2, 1, LANES, tq), lambda hp, j: (hp, j, 0, 0)),
                 pl.BlockSpec((tq, LANES), lambda hp, j: (j, hp))]
    out_shape = [jax.ShapeDtypeStruct((N_HEADS, nq, LANES, tq), F32), jax.ShapeDtypeStruct((N_HEADS, nq, LANES, tq), F32),
                 jax.ShapeDtypeStruct((T, N_HEADS * HEAD_V), BF16)]
    scratch = [pltpu.VMEM((LANES, tq), F32)] * 2 + [pltpu.VMEM((HEAD_V, tq), F32)] * 2
    if comm:
        in_specs.append(pl.BlockSpec(memory_space=pl.ANY))
        args.append(exchange)
        out_specs.append(pl.BlockSpec(memory_space=pl.ANY))
        out_shape.append(jax.ShapeDtypeStruct(exchange.shape, exchange.dtype))
        scratch += _comm_sems()
    return pl.pallas_call(
        body, name=name, grid=(N_HEADS // 2, nq),
        in_specs=in_specs, out_specs=out_specs, out_shape=out_shape, scratch_shapes=scratch,
        compiler_params=pltpu.CompilerParams(dimension_semantics=("arbitrary", "arbitrary"),
                                             vmem_limit_bytes=ATTN_BWD_VMEM_LIMIT),
    )(*args)


def _peers():
    x, y, c = lax.axis_index("x"), lax.axis_index("y"), lax.axis_index("c")
    me = 4 * x + 2 * y + c
    out = []
    for k in range(1, N_DEV):
        px = (1 - x) if (k & 4) else x
        py = (1 - y) if (k & 2) else y
        pc = (1 - c) if (k & 1) else c
        out.append(((px, py, pc), 4 * px + 2 * py + pc))
    return me, out


def _comm_sems():
    return [pltpu.SemaphoreType.DMA((N_DEV - 1,)), pltpu.SemaphoreType.DMA((N_DEV - 1,)), pltpu.SemaphoreType.DMA]


def _comm_copies(kind, src_ref, dst_ref, send_sems, recv_sems, local_sem):
    me, peers = _peers()
    local = pltpu.make_async_copy(src_ref if kind == "gather" else src_ref.at[me], dst_ref.at[me], local_sem)
    sends, recvs = [], []
    for k, (dev, lin) in enumerate(peers):
        src = src_ref if kind == "gather" else src_ref.at[lin]
        sends.append(pltpu.make_async_remote_copy(src_ref=src, dst_ref=dst_ref.at[me], send_sem=send_sems.at[k],
                                                  recv_sem=recv_sems.at[k], device_id=dev, device_id_type=MESH))
        recvs.append(pltpu.make_async_remote_copy(src_ref=src, dst_ref=dst_ref.at[lin], send_sem=send_sems.at[k],
                                                  recv_sem=recv_sems.at[k], device_id=dev, device_id_type=MESH))
    return local, sends, recvs


def _comm_start(kind, src_ref, dst_ref, send_sems, recv_sems, local_sem):
    local, sends, _ = _comm_copies(kind, src_ref, dst_ref, send_sems, recv_sems, local_sem)
    local.start()
    for cp in sends:
        cp.start()


def _comm_wait(kind, src_ref, dst_ref, send_sems, recv_sems, local_sem):
    local, sends, recvs = _comm_copies(kind, src_ref, dst_ref, send_sems, recv_sems, local_sem)
    for cp in recvs:
        cp.wait_recv()
    for cp in sends:
        cp.wait_send()
    local.wait()


def _gather_via_sibling(src, name):
    def body(x_ref, o_ref, send_sems, recv_sems, local_sem):
        x, y, c = lax.axis_index("x"), lax.axis_index("y"), lax.axis_index("c")
        me, sibling = (x, y, c), (x, y, 1 - c)
        chips = [(1 - x, y), (x, 1 - y), (1 - x, 1 - y)]

        def slot(px, py, pc):
            return o_ref.at[4 * px + 2 * py + pc]

        def copy(k, block, to, src_ref=None):
            return pltpu.make_async_remote_copy(src_ref=slot(*block) if src_ref is None else src_ref,
                                                dst_ref=slot(*block), send_sem=send_sems.at[k],
                                                recv_sem=recv_sems.at[k], device_id=to, device_id_type=MESH)

        mine = pltpu.make_async_copy(x_ref, slot(*me), local_sem)
        mine.start()
        first = [copy(0, me, sibling, x_ref)] + [copy(1 + n, me, (*chip, c), x_ref) for n, chip in enumerate(chips)]
        for cp in first:
            cp.start()
        passed = [copy(4 + n, (*chip, c), sibling) for n, chip in enumerate(chips)]
        for n, chip in enumerate(chips):
            copy(1 + n, (*chip, c), me).wait_recv()
            passed[n].start()
        copy(0, sibling, me).wait_recv()
        for n, chip in enumerate(chips):
            copy(4 + n, (*chip, 1 - c), me).wait_recv()
        for cp in first + passed:
            cp.wait_send()
        mine.wait()

    return pl.pallas_call(
        body, name=name,
        in_specs=[pl.BlockSpec(memory_space=pl.ANY)],
        out_specs=pl.BlockSpec(memory_space=pl.ANY),
        out_shape=jax.ShapeDtypeStruct((N_DEV,) + src.shape, src.dtype),
        scratch_shapes=_comm_sems(),
    )(src)


def _adamw(w, g, m, v):
    m2 = ADAM_B1 * m + (1.0 - ADAM_B1) * g
    v2 = ADAM_B2 * v + (1.0 - ADAM_B2) * (g * g)
    m_hat = m2 / (1.0 - ADAM_B1 ** ADAM_STEP)
    v_hat = v2 / (1.0 - ADAM_B2 ** ADAM_STEP)
    delta = -ADAM_LR * (m_hat / (jnp.sqrt(v_hat) + ADAM_EPS) + ADAM_WD * w)
    return delta, m2, v2


def _sum_adamw(pieces, w, m, v, *, name):
    R = w.shape[0]
    tr = R // 4 if R % 64 == 0 else R

    def body(p_ref, w_ref, m_ref, v_ref, g_ref, d_ref, m2_ref, v2_ref):
        g = p_ref[0].astype(F32)
        for s in range(1, N_DEV):
            g = g + p_ref[s].astype(F32)
        delta, m2, v2 = _adamw(w_ref[...], g, m_ref[...], v_ref[...])
        g_ref[...] = g
        d_ref[...] = delta
        m2_ref[...] = m2
        v2_ref[...] = v2

    row = pl.BlockSpec((tr, LANES), lambda i: (i, 0))
    return pl.pallas_call(
        body, name=name, grid=(R // tr,),
        in_specs=[pl.BlockSpec((N_DEV, tr, LANES), lambda i: (0, i, 0)), row, row, row],
        out_specs=[row, row, row, row],
        out_shape=[jax.ShapeDtypeStruct((R, LANES), F32)] * 4,
        compiler_params=_cparams(("parallel",)),
    )(pieces, w, m, v)


def _small_allreduce_adamw(part, w, m, v):
    shape = part.shape

    def body(p_ref, w_ref, m_ref, v_ref, g_ref, d_ref, m2_ref, v2_ref, loss_ref, gath, send_sems, recv_sems):
        me, peers = _peers()
        gath[me] = p_ref[...]
        sends = []
        for k, (dev, _) in enumerate(peers):
            cp = pltpu.make_async_remote_copy(src_ref=p_ref, dst_ref=gath.at[me], send_sem=send_sems.at[k],
                                              recv_sem=recv_sems.at[k], device_id=dev, device_id_type=MESH)
            cp.start()
            sends.append(cp)
        for k, (dev, lin) in enumerate(peers):
            pltpu.make_async_remote_copy(src_ref=p_ref, dst_ref=gath.at[lin], send_sem=send_sems.at[k],
                                         recv_sem=recv_sems.at[k], device_id=dev, device_id_type=MESH).wait_recv()
        for cp in sends:
            cp.wait_send()
        g = gath[0]
        for s in range(1, N_DEV):
            g = g + gath[s]
        delta, m2, v2 = _adamw(w_ref[...], g, m_ref[...], v_ref[...])
        g_ref[...] = g
        d_ref[...] = delta
        m2_ref[...] = m2
        v2_ref[...] = v2
        sq = jnp.sum(g[SMALL_LOSS_ROW:SMALL_LOSS_ROW + 8, :], axis=1, keepdims=True)
        tot = jnp.sum(sq, axis=0, keepdims=True) * (0.5 / D_MODEL)
        loss_ref[...] = jnp.broadcast_to(tot, loss_ref.shape)

    vm = pl.BlockSpec(memory_space=pltpu.VMEM)
    return pl.pallas_call(
        body, name="small_allreduce_adamw",
        in_specs=[vm, vm, vm, vm],
        out_specs=[vm, vm, vm, vm, vm],
        out_shape=[jax.ShapeDtypeStruct(shape, F32)] * 4 + [jax.ShapeDtypeStruct((8, LANES), F32)],
        scratch_shapes=[pltpu.VMEM((N_DEV,) + shape, F32),
                        pltpu.SemaphoreType.DMA((N_DEV - 1,)), pltpu.SemaphoreType.DMA((N_DEV - 1,))],
    )(part, w, m, v)


def _perm_in_cols(w):
    z = lambda n: jnp.zeros(w.shape[:-1] + (n,), w.dtype)
    small = jnp.concatenate([w[..., 384:400], z(16), w[..., 1952:1960], z(24), w[..., 400:416], z(48)], -1)
    return jnp.concatenate([w[..., 1960:2984], w[..., 2984:4008], w[..., 416:928], w[..., 928:1440],
                            w[..., 1440:1952], w[..., 0:256], w[..., 256:384], small], -1)


def _unperm_in_cols(wp):
    s = wp[..., P_SMALL:]
    return jnp.concatenate([wp[..., P_CQ:P_CQ + 256], wp[..., P_CKV:P_CKV + 128], s[..., 0:16], s[..., 64:80],
                            wp[..., P_FQ:P_FQ + 512], wp[..., P_FK:P_FK + 512], wp[..., P_FV:P_FV + 512],
                            s[..., S_FL:S_FL + 8], wp[..., P_GA:P_GA + 1024], wp[..., P_GB:P_GB + 1024]], -1)


def _aug_uq_cols(w):
    r = w.shape[0]
    w3 = w.reshape(r, N_HEADS, NOPE + ROPE)
    z = jnp.zeros((r, N_HEADS, 32), w.dtype)
    return jnp.concatenate([w3[:, :, 64:80], w3[:, :, 0:48], w3[:, :, 80:96], w3[:, :, 48:64], z], -1).reshape(r, 1024)


def _unaug_uq_cols(wp):
    r = wp.shape[0]
    w3 = wp.reshape(r, N_HEADS, LANES)
    return jnp.concatenate([w3[:, :, 16:64], w3[:, :, 80:96], w3[:, :, 0:16], w3[:, :, 64:80]], -1).reshape(r, 768)


def _aug_uk_cols(w):
    r = w.shape[0]
    w3 = w.reshape(r, N_HEADS, NOPE)
    z = lambda n: jnp.zeros((r, N_HEADS, n), w.dtype)
    return jnp.concatenate([z(16), w3[:, :, 0:48], z(16), w3[:, :, 48:64], z(32)], -1).reshape(r, 1024)


def _unaug_uk_cols(wp):
    r = wp.shape[0]
    w3 = wp.reshape(r, N_HEADS, LANES)
    return jnp.concatenate([w3[:, :, 16:64], w3[:, :, 80:96]], -1).reshape(r, 512)


IN_SEGMENTS = ((0, 256, P_CQ), (256, 128, P_CKV), (384, 16, P_SMALL), (400, 16, P_SMALL + 64), (416, 512, P_FQ),
               (928, 512, P_FK), (1440, 512, P_FV), (1952, 8, P_SMALL + S_FL), (1960, 1024, P_GA), (2984, 1024, P_GB))
IN_SHARD = D_IN // N_DEV


def _perm_in_from_shards(w3):
    r = w3.shape[0]
    parts = []
    pos = 0
    for o0, n, p0 in sorted(IN_SEGMENTS, key=lambda t: t[2]):
        if p0 > pos:
            parts.append(jnp.zeros((r, p0 - pos), w3.dtype))
        a, b = o0, o0 + n
        for d in range(a // IN_SHARD, (b - 1) // IN_SHARD + 1):
            parts.append(w3[:, d, max(a, d * IN_SHARD) - d * IN_SHARD:min(b, (d + 1) * IN_SHARD) - d * IN_SHARD])
        pos = p0 + n
    parts.append(jnp.zeros((r, D_IN_PAD - pos), w3.dtype))
    return jnp.concatenate(parts, -1)


def _unperm_in_to_shards(gp):
    r = gp.shape[0]
    shards = []
    for d in range(N_DEV):
        a, b = d * IN_SHARD, (d + 1) * IN_SHARD
        parts = [gp[:, p0 + max(a, o0) - o0:p0 + min(b, o0 + n) - o0] for o0, n, p0 in IN_SEGMENTS
                 if max(a, o0) < min(b, o0 + n)]
        parts.append(jnp.zeros((r, _lane_pad(IN_SHARD) - IN_SHARD), gp.dtype))
        shards.append(jnp.concatenate(parts, -1))
    return jnp.stack(shards, 0)


def _lane_pad(c):
    return -(-c // LANES) * LANES


def _packed_cols(c, by_col):
    return _lane_pad(c) if by_col else c


def _pack_shards(group, shards, dtype):
    rows = _pack_rows(group)
    segs = []
    for name, r, c, by_col in group:
        s = shards[name].reshape(r, c).astype(dtype)
        segs.append(jnp.pad(s, ((0, 0), (0, _packed_cols(c, by_col) - c))).reshape(-1))
    flat = jnp.concatenate(segs)
    flat = jnp.pad(flat, (0, rows * LANES - flat.shape[0]))
    return flat.reshape(rows, LANES)


def _unpack_shards(group, packed):
    flat = packed.reshape(-1)
    out, off = {}, 0
    for name, r, c, by_col in group:
        cp = _packed_cols(c, by_col)
        out[name] = flat[off:off + r * cp].reshape(1, r, cp)[:, :, :c]
        off += r * cp
    return out


def _unpack_full(group, gathered):
    flat = gathered.reshape(N_DEV, -1)
    out, off = {}, 0
    for name, r, c, by_col in group:
        cp = _packed_cols(c, by_col)
        blk = flat[:, off:off + r * cp].reshape(N_DEV, r, cp)
        out[name] = blk.transpose(1, 0, 2) if by_col else blk.reshape(N_DEV * r, c)
        off += r * cp
    return out


def _full_cols(w3, c):
    return w3[:, :, :c].reshape(w3.shape[0], N_DEV * c)


def _pack_full(group, grads, dtype):
    rows = _pack_rows(group)
    segs = []
    for name, r, c, by_col in group:
        g = grads[name]
        cp = _packed_cols(c, by_col)
        if by_col and g.ndim == 2:
            g = jnp.pad(g.reshape(r, N_DEV, c), ((0, 0), (0, 0), (0, cp - c))).transpose(1, 0, 2)
        segs.append(g.reshape(N_DEV, r * cp).astype(dtype))
    flat = jnp.concatenate(segs, axis=1)
    flat = jnp.pad(flat, ((0, 0), (0, rows * LANES - flat.shape[1])))
    return flat.reshape(N_DEV, rows, LANES)


SMALL_LAYOUT = (("ln_pre_mix", 1024, 0), ("ln_post_mix", 1024, 8), ("ln_pre_mlp", 1024, 16),
                ("ln_post_mlp", 1024, 24), ("b_in", 4008, 32), ("q_a_norm", 256, 64), ("kv_a_norm", 128, 66))


def _pack_small(vals, extra=None):
    rows = []
    for name, n, _ in SMALL_LAYOUT:
        v = vals[name].reshape(-1).astype(F32)
        pad = -n % LANES
        rows.append(jnp.pad(v, (0, pad)).reshape(-1, LANES))
    rows.append(jnp.zeros((SMALL_LOSS_ROW - 67, LANES), F32))
    rows.append(jnp.zeros((8, LANES), F32) if extra is None else extra.reshape(8, LANES))
    return jnp.concatenate(rows, axis=0)


def _unpack_small(packed):
    out = {}
    for name, n, r0 in SMALL_LAYOUT:
        nr = -(-n // LANES)
        out[name] = packed[r0:r0 + nr].reshape(-1)[:n].reshape(1, n)
    return out


def _rms(xf, g):
    r = lax.rsqrt(jnp.mean(xf * xf, axis=-1, keepdims=True) + NORM_EPS)
    return (xf * r) * g


def _rms_bwd(xf, g, dy):
    r = lax.rsqrt(jnp.mean(xf * xf, axis=-1, keepdims=True) + NORM_EPS)
    xhat = xf * r
    dxhat = dy * g
    dx = r * (dxhat - xhat * jnp.mean(dxhat * xhat, axis=-1, keepdims=True))
    return dx, dy * xhat


def _sigmoid(t):
    return 1.0 / (1.0 + jnp.exp(-t))


def _log_sigmoid(t):
    return jnp.minimum(t, 0.0) - jnp.log(1.0 + jnp.exp(-jnp.abs(t)))


def _lane_sign():
    lane = lax.broadcasted_iota(jnp.int32, (1, LANES), 1)
    return jnp.where(lane < 64, -1.0, 1.0).astype(F32), lane


def _rope_lanes(lane):
    return (lane < 16) | ((lane >= 64) & (lane < 80))


def kernel(x, positions, ln_pre_mix, ln_post_mix, ln_pre_mlp, ln_post_mlp, w_in, b_in, q_a_norm, w_uq, kv_a_norm, w_uk, w_uv, w_o_mla, w_o_fox, w_out, w_ff1, w_ff2, loss_target, m_ln_pre_mix, m_ln_post_mix, m_ln_pre_mlp, m_ln_post_mlp, m_w_in, m_b_in, m_q_a_norm, m_w_uq, m_kv_a_norm, m_w_uk, m_w_uv, m_w_o_mla, m_w_o_fox, m_w_out, m_w_ff1, m_w_ff2, v_ln_pre_mix, v_ln_post_mix, v_ln_pre_mlp, v_ln_post_mlp, v_w_in, v_b_in, v_q_a_norm, v_w_uq, v_kv_a_norm, v_w_uk, v_w_uv, v_w_o_mla, v_w_o_fox, v_w_out, v_w_ff1, v_w_ff2):
    T = x.shape[1]
    x2 = x.reshape(T, D_MODEL)
    tgt = loss_target.reshape(T, D_MODEL)
    w_sh = dict(w_in=w_in, w_uq=w_uq, w_uk=w_uk, w_uv=w_uv, w_o_mla=w_o_mla, w_o_fox=w_o_fox, w_out=w_out,
                w_ff1=w_ff1, w_ff2=w_ff2)
    m_sh = dict(w_in=m_w_in, w_uq=m_w_uq, w_uk=m_w_uk, w_uv=m_w_uv, w_o_mla=m_w_o_mla, w_o_fox=m_w_o_fox,
                w_out=m_w_out, w_ff1=m_w_ff1, w_ff2=m_w_ff2)
    v_sh = dict(w_in=v_w_in, w_uq=v_w_uq, w_uk=v_w_uk, w_uv=v_w_uv, w_o_mla=v_w_o_mla, w_o_fox=v_w_o_fox,
                w_out=v_w_out, w_ff1=v_w_ff1, w_ff2=v_w_ff2)
    small_w = dict(ln_pre_mix=ln_pre_mix, ln_post_mix=ln_post_mix, ln_pre_mlp=ln_pre_mlp, ln_post_mlp=ln_post_mlp,
                   b_in=b_in, q_a_norm=q_a_norm, kv_a_norm=kv_a_norm)
    small_m = dict(ln_pre_mix=m_ln_pre_mix, ln_post_mix=m_ln_post_mix, ln_pre_mlp=m_ln_pre_mlp,
                   ln_post_mlp=m_ln_post_mlp, b_in=m_b_in, q_a_norm=m_q_a_norm, kv_a_norm=m_kv_a_norm)
    small_v = dict(ln_pre_mix=v_ln_pre_mix, ln_post_mix=v_ln_post_mix, ln_pre_mlp=v_ln_pre_mlp,
                   ln_post_mlp=v_ln_post_mlp, b_in=v_b_in, q_a_norm=v_q_a_norm, kv_a_norm=v_kv_a_norm)
    mla_scale = float((NOPE + ROPE) ** -0.5)
    fox_scale = float(FOX_D ** -0.5)

    W = _unpack_full(GROUP_A, _gather_via_sibling(_pack_shards(GROUP_A, w_sh, BF16), "allgather_weights_a"))
    w_in_p = _perm_in_from_shards(W["w_in"])
    b_in_p = _perm_in_cols(b_in.astype(F32))
    w_uq_a = _aug_uq_cols(_full_cols(W["w_uq"], 96))
    w_kv_a = jnp.concatenate([_aug_uk_cols(_full_cols(W["w_uk"], 64)), _full_cols(W["w_uv"], 64)], axis=1)
    g1, g2, g3, g4 = ln_pre_mix, ln_post_mix, ln_pre_mlp, ln_post_mlp
    gq, gkv = q_a_norm, kv_a_norm
    tq = min(ATTN_TILE, T)
    nq = T // tq

    (h,) = _rowwise(lambda xv, g: _rms(xv, g), [x2], [g1], [(D_MODEL, BF16)], name="pre_mix_norm")
    q_cols = jnp.ones((1, D_IN_PAD), F32).at[:, P_FQ:P_FQ + 512].set(fox_scale * LOG2E)
    z = _mm(h, w_in_p, bias=b_in_p, colscale=q_cols, tm=2048, name="in_proj")
    zs = _mm(h, w_in_p[:, P_SMALL:], bias=b_in_p[:, P_SMALL:], outs=(F32,), name="in_proj_small")

    def lora_norm(cq, ckv, a, b):
        return _rms(cq.astype(F32), a), _rms(ckv.astype(F32), b)

    cqn, ckvn = _rowwise(lora_norm, [(z, 256, P_CQ // 256), (z, 128, P_CKV // 128)], [gq, gkv],
                         [(Q_LORA, BF16), (KV_LORA, BF16)], name="lora_norm")
    q_x = _mm(cqn, w_uq_a, outs=(F32,), name="q_up")
    kv = _mm(ckvn, w_kv_a, tn=512, name="kv_up")

    half = ROPE // 2
    inv_freq = ROPE_THETA ** (-jnp.arange(half, dtype=F32) / half)
    inv128 = jnp.tile(inv_freq, LANES // half).reshape(1, LANES)
    pos_col = positions.reshape(T, 1).astype(F32)

    def rope_tables(p, f):
        ang = p * f
        return jnp.cos(ang), jnp.sin(ang)

    cos_t, sin_t = _rowwise(rope_tables, [pos_col], [inv128], [(LANES, F32), (LANES, F32)], name="rope_tables")

    def rope_fwd(qx, kn, vn, s, cs, sn):
        sign, lane = _lane_sign()
        rope_l = _rope_lanes(lane)
        rot = lambda t: t * cs + pltpu.roll(t, 64, 1) * sn * sign
        k_rot = jnp.where(rope_l, rot(s), 0.0)
        qs, ks = [], []
        for hd in range(N_HEADS):
            qb = qx[:, LANES * hd:LANES * (hd + 1)]
            qs.append(jnp.where(rope_l, rot(qb), qb))
            ks.append(kn[:, LANES * hd:LANES * (hd + 1)].astype(F32) + k_rot)
        return jnp.concatenate(qs, axis=1) * (mla_scale * LOG2E), jnp.concatenate(ks, axis=1), vn

    qt4_m, k_am, vt4_m = _rowwise(rope_fwd, [q_x, (kv, 1024, 0), (kv, 512, 2), zs, cos_t, sin_t], [],
                                  [(1024, BF16, "t4"), (1024, BF16), (512, BF16, "t4")], name="rope_fwd", tm=tq)
    f_cum = _seq_cumsum(zs, _log_sigmoid, reverse=False, name="forget_cumsum")

    def fox_aug(qf, kf, vf, fc):
        lane = lax.broadcasted_iota(jnp.int32, (1, LANES), 1)
        qs, ks = [], []
        for hd in range(N_HEADS):
            qp = qf[:, LANES * (hd // 2):LANES * (hd // 2 + 1)].astype(F32)
            kp = kf[:, LANES * (hd // 2):LANES * (hd // 2 + 1)].astype(F32)
            if hd % 2:
                qp = pltpu.roll(qp, 64, 1)
                kp = pltpu.roll(kp, 64, 1)
            fb = jnp.broadcast_to(fc[:, S_FL + hd:S_FL + hd + 1] * (-LOG2E), qp.shape)
            hi = fb.astype(BF16).astype(F32)
            mid = (fb - hi).astype(BF16).astype(F32)
            lo = fb - hi - mid
            qs.append(jnp.where(lane < 64, qp, jnp.where(lane < 67, 1.0, 0.0)))
            ks.append(jnp.where(lane < 64, kp, jnp.where(lane == 64, hi, jnp.where(lane == 65, mid, jnp.where(
                lane == 66, lo, jnp.where(lane == 67, 1.0, 0.0))))))
        return jnp.concatenate(qs, axis=1), jnp.concatenate(ks, axis=1), vf

    qt4_f, k_af, vt4_f = _rowwise(fox_aug, [(z, 512, P_FQ // 512), (z, 512, P_FK // 512), (z, 512, P_FV // 512), f_cum],
                                  [], [(1024, BF16, "t4"), (1024, BF16), (512, BF16, "t4")], name="fox_aug", tm=tq)
    o_mla, lse_mla, gathered_b = _attn_fwd(qt4_m, k_am, vt4_m, name="mla_attn_fwd", tq=tq,
                                           gather=_pack_shards(GROUP_B, w_sh, BF16))
    W.update(_unpack_full(GROUP_B, gathered_b))
    for n, _, c, by_col in GROUP_B:
        if by_col:
            W[n] = _full_cols(W[n], c)
    o_fox, lse_fox = _attn_fwd(qt4_f, k_af, vt4_f, name="fox_attn_fwd", tq=tq)
    y_mla = _mm(o_mla, W["w_o_mla"], outs=(F32,), name="o_proj_mla")
    y_fox = _mm(o_fox, W["w_o_fox"], outs=(F32,), name="o_proj_fox")

    def gate_merge(ga, gb, ya, yb):
        return _sigmoid(ga.astype(F32)) * ya + _sigmoid(gb.astype(F32)) * yb

    (merged,) = _rowwise(gate_merge, [(z, 1024, 0), (z, 1024, 1), y_mla, y_fox], [], [(D_MODEL, BF16)],
                         name="gate_merge")
    mix = _mm(merged, W["w_out"], outs=(F32,), name="out_proj")

    def post_mix(xv, mv, a, b):
        x1v = xv + _rms(mv, a)
        return x1v, _rms(x1v, b)

    x1, h2 = _rowwise(post_mix, [x2, mix], [g2, g3], [(D_MODEL, F32), (D_MODEL, BF16)], name="post_mix_norm")

    def relu2(acc):
        r = jnp.maximum(acc, 0.0)
        return r * r, acc

    act, u = _mm(h2, W["w_ff1"], epi=relu2, outs=(BF16, BF16), tm=2048, name="ff1")
    mlp = _mm(act, W["w_ff2"], outs=(F32,), tm=1024, tk=2048, name="ff2")

    def loss_bwd(x1v, mv, tv, g):
        y = x1v + _rms(mv, g)
        d = y - tv
        dy = d * (1.0 / D_MODEL)
        dm, dg = _rms_bwd(mv, g, dy)
        return dy, dm, dg, d * d

    dy, dm, dg4, loss_cols = _rowwise(loss_bwd, [x1, mlp, tgt], [g4], [(D_MODEL, F32), (D_MODEL, BF16)],
                                      [D_MODEL, D_MODEL], name="loss_bwd")

    def relu2_bwd(acc, uv):
        return (acc * (2.0 * jnp.maximum(uv.astype(F32), 0.0)),)

    du = _mm(dm, W["w_ff2"], tb=True, extras=(u,), epi=relu2_bwd, tm=2048, name="ff2_bwd")
    dw_ff2 = _mm(act, dm, ta=True, outs=(BF16,), tk=2048, name="ff2_wgrad")
    dh2 = _mm(du, W["w_ff1"], tb=True, outs=(F32,), tm=1024, tk=2048, name="ff1_bwd")
    dw_ff1 = _mm(h2, du, ta=True, outs=(BF16,), tk=2048, name="ff1_wgrad")

    def post_mix_bwd(x1v, dh2v, dyv, mv, a, b):
        d3, dg3v = _rms_bwd(x1v, b, dh2v)
        dx1v = dyv + d3
        dmixv, dg2v = _rms_bwd(mv, a, dx1v)
        return dx1v, dmixv, dg3v, dg2v

    dx1, dmix, dg3, dg2 = _rowwise(post_mix_bwd, [x1, dh2, dy, mix], [g2, g3], [(D_MODEL, F32), (D_MODEL, BF16)],
                                   [D_MODEL, D_MODEL], name="post_mix_bwd")
    dmerged = _mm(dmix, W["w_out"], tb=True, outs=(F32,), name="out_proj_bwd")
    dw_out = _mm(merged, dmix, ta=True, outs=(BF16,), name="out_proj_wgrad")

    def gate_bwd(dmg, ga, gb, ya, yb):
        sa = _sigmoid(ga.astype(F32))
        sb = _sigmoid(gb.astype(F32))
        return dmg * sa, dmg * sb, dmg * ya * sa * (1.0 - sa), dmg * yb * sb * (1.0 - sb)

    dy_mla, dy_fox, dga, dgb = _rowwise(gate_bwd, [dmerged, (z, 1024, 0), (z, 1024, 1), y_mla, y_fox], [],
                                        [(D_MODEL, BF16)] * 4, name="gate_bwd")
    twice = lambda acc: (acc, acc)
    do_mla, dot4_m = _mm(dy_mla, W["w_o_mla"], tb=True, epi=twice, outs=(BF16, BF16), t4=(False, True), tm=tq,
                         name="o_proj_mla_bwd")
    do_fox, dot4_f = _mm(dy_fox, W["w_o_fox"], tb=True, epi=twice, outs=(BF16, BF16), t4=(False, True), tm=tq,
                         name="o_proj_fox_bwd")
    dw_o_mla = _mm(o_mla, dy_mla, ta=True, outs=(BF16,), name="o_proj_mla_wgrad")
    dw_o_fox = _mm(o_fox, dy_fox, ta=True, outs=(BF16,), name="o_proj_fox_wgrad")

    def head_dots(ov, dov):
        r = lax.broadcasted_iota(jnp.int32, (N_HEADS * HEAD_V, LANES), 0)
        cc = lax.broadcasted_iota(jnp.int32, (N_HEADS * HEAD_V, LANES), 1)
        sel = jnp.where((r // HEAD_V) == cc, 1.0, 0.0).astype(BF16)
        return _dot3(ov.astype(F32) * dov.astype(F32), sel, left=False)

    (dl_mla,) = _rowwise(head_dots, [o_mla, do_mla], [], [(LANES, F32)], name="mla_attn_delta")
    (dl_fox,) = _rowwise(head_dots, [o_fox, do_fox], [], [(LANES, F32)], name="fox_attn_delta")
    heads_row4 = lambda t: t[:, :N_HEADS].T.reshape(N_HEADS, nq, 1, tq)
    grads_b = dict(w_o_mla=dw_o_mla, w_o_fox=dw_o_fox, w_out=dw_out, w_ff1=dw_ff1, w_ff2=dw_ff2)
    dqt_m, dkt_m, dv_mla, pieces_b = _attn_bwd(qt4_m, k_am, kv, 1024 // LANES, dot4_m, lse_mla,
                                               heads_row4(dl_mla), mla_scale, NOPE + ROPE, name="mla_attn_bwd", tq=tq,
                                               exchange=_pack_full(GROUP_B, grads_b, BF16))
    dqt_f, dkt_f, dv_fox = _attn_bwd(qt4_f, k_af, z, P_FV // LANES, dot4_f, lse_fox, heads_row4(dl_fox),
                                     fox_scale, FOX_D, name="fox_attn_bwd", tq=tq)

    def fox_unpack(dqa, dka):
        lane = lax.broadcasted_iota(jnp.int32, (1, LANES), 1)
        dqs, dks = [], []
        d_f = jnp.zeros(dqa[:, :LANES].shape, F32)
        for hp in range(N_HEADS // 2):
            blk = lambda t, e: t[:, LANES * (2 * hp + e):LANES * (2 * hp + e + 1)]
            dqs.append(jnp.where(lane < 64, blk(dqa, 0), pltpu.roll(blk(dqa, 1), 64, 1)))
            dks.append(jnp.where(lane < 64, blk(dka, 0), pltpu.roll(blk(dka, 1), 64, 1)))
            for e in range(2):
                g = blk(dqa, e)[:, 67:68] - blk(dka, e)[:, 64:65]
                d_f = jnp.where(lane == S_FL + 2 * hp + e, g, d_f)
        return jnp.concatenate(dqs, axis=1), jnp.concatenate(dks, axis=1), d_f

    dq_fox, dk_fox, d_f128 = _rowwise(fox_unpack, [("t4", dqt_f), ("t4", dkt_f)], [],
                                      [(512, BF16), (512, BF16), (LANES, F32)], name="fox_unpack", tm=tq)
    df_rev = _seq_cumsum(d_f128, lambda t: t, reverse=True, name="forget_cumsum_bwd")

    def rope_bwd(dqa, dka, dvm, dfr, s, cs, sn):
        sign, lane = _lane_sign()
        rope_l = _rope_lanes(lane)
        rot_t = lambda t: t * cs - pltpu.roll(t, 64, 1) * sn * sign
        dqs = []
        dk_rot = None
        for hd in range(N_HEADS):
            blk = dqa[:, LANES * hd:LANES * (hd + 1)]
            dqs.append(jnp.where(rope_l, rot_t(blk), blk))
            blk = dka[:, LANES * hd:LANES * (hd + 1)]
            dk_rot = blk if dk_rot is None else dk_rot + blk
        dfl = dfr * _sigmoid(-s)
        small = jnp.where(rope_l, rot_t(dk_rot), jnp.where((lane >= S_FL) & (lane < S_FL + N_HEADS), dfl, 0.0))
        return jnp.concatenate(dqs, axis=1), jnp.concatenate([dka, dvm.astype(F32)], axis=1), small

    dq_b, dkv, d_small = _rowwise(rope_bwd, [("t4", dqt_m), ("t4", dkt_m), dv_mla, df_rev, zs, cos_t, sin_t], [],
                                  [(1024, BF16), (1536, BF16), (LANES, BF16)], name="rope_bwd", tm=tq)
    dcqn = _mm(dq_b, w_uq_a, tb=True, outs=(F32,), name="q_up_bwd")
    dw_uq_a = _mm(cqn, dq_b, ta=True, outs=(BF16,), name="q_up_wgrad")
    dckvn = _mm(dkv, w_kv_a, tb=True, outs=(F32,), tk=512, name="kv_up_bwd")
    dw_kv_a = _mm(ckvn, dkv, ta=True, outs=(BF16,), tn=512, name="kv_up_wgrad")

    def lora_norm_bwd(cq, ckv, dq_, dkv_, a, b):
        d1, dga_ = _rms_bwd(cq.astype(F32), a, dq_)
        d2, dgb_ = _rms_bwd(ckv.astype(F32), b, dkv_)
        return d1, d2, dga_, dgb_

    dcq, dckv, dgq, dgkv = _rowwise(lora_norm_bwd, [(z, 256, P_CQ // 256), (z, 128, P_CKV // 128), dcqn, dckvn],
                                    [gq, gkv], [(Q_LORA, BF16), (KV_LORA, BF16)], [Q_LORA, KV_LORA],
                                    name="lora_norm_bwd")
    dz = jnp.concatenate([dga, dgb, dq_fox, dk_fox, dv_fox, dcq, dckv, d_small], -1)
    dw_in_p, db_in_p = _mm(h, dz, ta=True, outs=(BF16,), tk=2048, colsum=True, name="in_proj_wgrad")
    grads_a = dict(w_in=_unperm_in_to_shards(dw_in_p), w_uq=_unaug_uq_cols(dw_uq_a),
                   w_uk=_unaug_uk_cols(dw_kv_a[:, :1024]), w_uv=dw_kv_a[:, 1024:])
    dh, pieces_a = _mm(dz, w_in_p, tb=True, outs=(F32,), tm=1024, tk=2048, name="in_proj_bwd",
                       exchange=_pack_full(GROUP_A, grads_a, BF16))

    def pre_mix_bwd(xv, dhv, dx1v, g):
        d, dg = _rms_bwd(xv, g, dhv)
        return dx1v + d, dg

    grad_x, dg1 = _rowwise(pre_mix_bwd, [x2, dh, dx1], [g1], [(D_MODEL, F32)], [D_MODEL], name="pre_mix_bwd")

    grad_sh, delta_sh, newm_sh, newv_sh = {}, {}, {}, {}
    for group, pieces, tag in ((GROUP_A, pieces_a, "a"), (GROUP_B, pieces_b, "b")):
        packed = _sum_adamw(pieces, _pack_shards(group, w_sh, F32), _pack_shards(group, m_sh, F32),
                            _pack_shards(group, v_sh, F32), name="sum_pieces_adamw_" + tag)
        for dst, arr in zip((grad_sh, delta_sh, newm_sh, newv_sh), packed):
            dst.update(_unpack_shards(group, arr))

    small_part = _pack_small(dict(ln_pre_mix=dg1, ln_post_mix=dg2, ln_pre_mlp=dg3, ln_post_mlp=dg4,
                                  b_in=_unperm_in_cols(db_in_p), q_a_norm=dgq, kv_a_norm=dgkv), extra=loss_cols)
    sg, sd, sm, sv, loss_blk = _small_allreduce_adamw(small_part, _pack_small(small_w), _pack_small(small_m),
                                                      _pack_small(small_v))
    grad_sm, delta_sm, newm_sm, newv_sm = (_unpack_small(t) for t in (sg, sd, sm, sv))
    loss = loss_blk[0, 0]

    order = ["ln_pre_mix", "ln_post_mix", "ln_pre_mlp", "ln_post_mlp", "w_in", "b_in", "q_a_norm", "w_uq",
             "kv_a_norm", "w_uk", "w_uv", "w_o_mla", "w_o_fox", "w_out", "w_ff1", "w_ff2"]

    def pick(sm_d, sh_d):
        return [sm_d[n] if n in sm_d else sh_d[n] for n in order]

    return (loss, grad_x.reshape(1, T, D_MODEL), *pick(grad_sm, grad_sh), *pick(delta_sm, delta_sh),
            *pick(newm_sm, newm_sh), *pick(newv_sm, newv_sh))
```

```python
import numpy as np
import jax
import jax.numpy as jnp
from jax import lax
from jax.experimental import pallas as pl
from jax.experimental.pallas import tpu as pltpu

F32 = jnp.float32
BF16 = jnp.bfloat16
MESH = pl.DeviceIdType.MESH

D_MODEL = 1024
N_HEADS = 8
Q_LORA = 256
KV_LORA = 128
NOPE = 64
ROPE = 32
HEAD_V = 64
FOX_D = 64
D_FF = 4096
D_IN = 4008
D_IN_PAD = 4096
ROPE_THETA = 10000.0
NORM_EPS = 1e-6
N_DEV = 8

ADAM_LR = 0.001
ADAM_B1 = 0.9
ADAM_B2 = 0.999
ADAM_EPS = 1e-08
ADAM_WD = 0.01
ADAM_STEP = 10

LANES = 128
ROW_TILE = 512
ATTN_TILE = 512
ATTN_CHUNK = 256
VMEM_LIMIT = 48 * 1024 * 1024
ATTN_BWD_VMEM_LIMIT = 58 * 1024 * 1024

P_GA, P_GB, P_FQ, P_FK, P_FV, P_CQ, P_CKV, P_SMALL = 0, 1024, 2048, 2560, 3072, 3584, 3840, 3968
S_FL = 32

SHARDED = (
    ("w_in", 1024, 501, True), ("w_uq", 256, 96, True), ("w_uk", 128, 64, True), ("w_uv", 128, 64, True),
    ("w_o_mla", 512, 128, True), ("w_o_fox", 512, 128, True), ("w_out", 128, 1024, False),
    ("w_ff1", 1024, 512, True), ("w_ff2", 512, 1024, False),
)
GROUP_A = SHARDED[:4]
GROUP_B = SHARDED[4:]
SMALL_ROWS = 80
SMALL_LOSS_ROW = 72


def _pack_rows(group):
    return -(-sum(r * _packed_cols(c, by_col) for _, r, c, by_col in group) // (LANES * 64)) * 64


def _cparams(sem=None):
    return pltpu.CompilerParams(dimension_semantics=sem, vmem_limit_bytes=VMEM_LIMIT)


def _mm(a, b, *, name, ta=False, tb=False, bias=None, colscale=None, extras=(), epi=None, outs=(BF16,), t4=None,
        tm=ROW_TILE, tn=1024, tk=1024, exchange=None, colsum=False):
    t4 = (False,) * len(outs) if t4 is None else t4
    comm = exchange is not None
    if ta:
        K, M = a.shape
        tm = min(1024, M)
    else:
        M, K = a.shape
        tm = min(tm, M)
    tk = min(tk, K)
    N = b.shape[0] if tb else b.shape[1]
    tn = min(tn, N)
    nk = K // tk
    assert not colsum or (ta and M == tm)
    n_ex = len(extras)
    has_bias = bias is not None
    has_scale = colscale is not None

    def body(*refs):
        a_ref, b_ref = refs[0], refs[1]
        pos = 2
        bias_ref = scale_ref = None
        if has_bias:
            bias_ref = refs[pos]
            pos += 1
        if has_scale:
            scale_ref = refs[pos]
            pos += 1
        ex_refs = refs[pos:pos + n_ex]
        pos += n_ex
        if comm:
            c_src = refs[pos]
            pos += 1
        o_refs = refs[pos:pos + len(outs)]
        pos += len(outs)
        if colsum:
            cs_ref = refs[pos]
            pos += 1
        if comm:
            c_dst = refs[pos]
            pos += 1
            c_sems = refs[len(refs) - 3:]
            ids = [pl.program_id(d) for d in range(3)]

            @pl.when((ids[0] == 0) & (ids[1] == 0) & (ids[2] == 0))
            def _():
                _comm_start("exchange", c_src, c_dst, *c_sems)

        av = a_ref[...].astype(BF16)
        bv = b_ref[...].astype(BF16)
        if ta:
            part = lax.dot_general(av, bv, (((0,), (0,)), ((), ())), preferred_element_type=F32)
        elif tb:
            part = lax.dot_general(av, bv, (((1,), (1,)), ((), ())), preferred_element_type=F32)
        else:
            part = jnp.dot(av, bv, preferred_element_type=F32)
        if colsum:
            cs = jnp.sum(bv.astype(F32), axis=0, keepdims=True)

            @pl.when(pl.program_id(2) == 0)
            def _():
                cs_ref[...] = cs

            @pl.when(pl.program_id(2) > 0)
            def _():
                cs_ref[...] += cs

        def finish(acc):
            if has_bias:
                acc = acc + bias_ref[...]
            if has_scale:
                acc = acc * scale_ref[...]
            res = (acc,) if epi is None else epi(acc, *[r[...] for r in ex_refs])
            for o_ref, val, t in zip(o_refs, res, t4):
                if t:
                    _t4_store(o_ref, val)
                else:
                    o_ref[...] = val.astype(o_ref.dtype)

        if nk == 1:
            finish(part)
        else:
            acc_ref = refs[pos]
            k = pl.program_id(2)

            @pl.when(k == 0)
            def _():
                acc_ref[...] = part

            @pl.when(k > 0)
            def _():
                acc_ref[...] += part

            @pl.when(k == nk - 1)
            def _():
                finish(acc_ref[...])

        if comm:
            @pl.when((ids[0] == M // tm - 1) & (ids[1] == N // tn - 1) & (ids[2] == nk - 1))
            def _():
                _comm_wait("exchange", c_src, c_dst, *c_sems)

    if ta:
        a_spec = pl.BlockSpec((tk, tm), lambda i, j, k: (k, i))
    else:
        a_spec = pl.BlockSpec((tm, tk), lambda i, j, k: (i, k))
    b_spec = pl.BlockSpec((tn, tk), lambda i, j, k: (j, k)) if tb else pl.BlockSpec((tk, tn), lambda i, j, k: (k, j))
    in_specs = [a_spec, b_spec]
    args = [a, b]
    for row in (bias, colscale):
        if row is not None:
            in_specs.append(pl.BlockSpec((1, tn), lambda i, j, k: (0, j)))
            args.append(row)
    for e in extras:
        in_specs.append(pl.BlockSpec((tm, tn), lambda i, j, k: (i, j)))
        args.append(e)
    out_specs = [pl.BlockSpec((N // LANES, 1, LANES, tm), lambda i, j, k: (0, i, 0, 0)) if t
                 else pl.BlockSpec((tm, tn), lambda i, j, k: (i, j)) for t in t4]
    out_shape = [jax.ShapeDtypeStruct((N // LANES, M // tm, LANES, tm) if t else (M, N), dt)
                 for dt, t in zip(outs, t4)]
    scratch = [pltpu.VMEM((tm, tn), F32)] if nk > 1 else []
    if colsum:
        out_specs.append(pl.BlockSpec((1, tn), lambda i, j, k: (0, j)))
        out_shape.append(jax.ShapeDtypeStruct((1, N), F32))
    if comm:
        in_specs.append(pl.BlockSpec(memory_space=pl.ANY))
        args.append(exchange)
        out_specs.append(pl.BlockSpec(memory_space=pl.ANY))
        out_shape.append(jax.ShapeDtypeStruct(exchange.shape, exchange.dtype))
        scratch += _comm_sems()
    res = pl.pallas_call(
        body, name=name, grid=(M // tm, N // tn, nk),
        in_specs=in_specs, out_specs=out_specs, out_shape=out_shape, scratch_shapes=scratch,
        compiler_params=_cparams(("arbitrary",) * 3 if comm else ("parallel", "parallel", "arbitrary")),
    )(*args)
    return res[0] if len(res) == 1 else res


def _t4_store(o_ref, val):
    for c in range(o_ref.shape[0]):
        o_ref[c, 0] = val[:, c * LANES:(c + 1) * LANES].astype(F32).T.astype(o_ref.dtype)


def _rowwise(fn, rows, bcasts, outs, accs=(), *, name, tm=ROW_TILE):
    t4_in = [isinstance(r, tuple) and isinstance(r[0], str) for r in rows]
    T = [r[1].shape[1] * r[1].shape[3] if t else (r[0] if isinstance(r, tuple) else r).shape[0]
         for r, t in zip(rows, t4_in)][0]
    tm = min(tm, T)
    arrs, specs = [], []
    for r, t in zip(rows, t4_in):
        if t:
            arr = r[1]
            specs.append(pl.BlockSpec((arr.shape[0], 1, LANES, tm), lambda i: (0, i, 0, 0)))
        elif isinstance(r, tuple):
            arr, w, cb = r
            specs.append(pl.BlockSpec((tm, w), lambda i, cb=cb: (i, cb)))
        else:
            arr = r
            specs.append(pl.BlockSpec((tm, arr.shape[1]), lambda i: (i, 0)))
        arrs.append(arr)
    n_rows = len(arrs)
    for b in bcasts:
        arrs.append(b)
        specs.append(pl.BlockSpec(b.shape, lambda i: (0, 0)))
    n_in, n_out = len(arrs), len(outs)
    t4_out = [len(o) == 3 for o in outs]

    def body(*refs):
        vals = []
        for k, r in enumerate(refs[:n_in]):
            if k < n_rows and t4_in[k]:
                vals.append(jnp.concatenate([r[c, 0].astype(F32).T for c in range(r.shape[0])], axis=1))
            else:
                vals.append(r[...])
        res = fn(*vals)
        if not isinstance(res, (tuple, list)):
            res = (res,)
        for o_ref, val, t in zip(refs[n_in:n_in + n_out], res[:n_out], t4_out):
            if t:
                _t4_store(o_ref, val)
            else:
                o_ref[...] = val.astype(o_ref.dtype)
        i = pl.program_id(0)
        for a_ref, val in zip(refs[n_in + n_out:], res[n_out:]):
            col = jnp.sum(val.astype(F32), axis=0, keepdims=True)

            @pl.when(i == 0)
            def _(a_ref=a_ref, col=col):
                a_ref[...] = col

            @pl.when(i > 0)
            def _(a_ref=a_ref, col=col):
                a_ref[...] += col

    res = pl.pallas_call(
        body, name=name, grid=(T // tm,),
        in_specs=specs,
        out_specs=[pl.BlockSpec((o[0] // LANES, 1, LANES, tm), lambda i: (0, i, 0, 0)) if t
                   else pl.BlockSpec((tm, o[0]), lambda i: (i, 0)) for o, t in zip(outs, t4_out)]
        + [pl.BlockSpec((1, c), lambda i: (0, 0)) for c in accs],
        out_shape=[jax.ShapeDtypeStruct((o[0] // LANES, T // tm, LANES, tm) if t else (T, o[0]), o[1])
                   for o, t in zip(outs, t4_out)]
        + [jax.ShapeDtypeStruct((1, c), F32) for c in accs],
        compiler_params=_cparams(("arbitrary",)),
    )(*arrs)
    return res


def _dot3(x, sel, left):
    hi = x.astype(BF16)
    r1 = x - hi.astype(F32)
    mid = r1.astype(BF16)
    lo = (r1 - mid.astype(F32)).astype(BF16)
    if left:
        d = lambda t: jnp.dot(sel, t, preferred_element_type=F32)
    else:
        d = lambda t: jnp.dot(t, sel, preferred_element_type=F32)
    return d(hi) + d(mid) + d(lo)


def _seq_cumsum(x, fn, *, reverse, name, tm=256):
    T, C = x.shape
    tm = min(tm, T)
    n = T // tm

    def body(x_ref, o_ref, carry_ref):
        i = pl.program_id(0)

        @pl.when(i == 0)
        def _():
            carry_ref[...] = jnp.zeros_like(carry_ref)

        v = fn(x_ref[...])
        r = lax.broadcasted_iota(jnp.int32, (tm, tm), 0)
        c = lax.broadcasted_iota(jnp.int32, (tm, tm), 1)
        tri = jnp.where((r <= c) if reverse else (r >= c), 1.0, 0.0).astype(BF16)
        carry = carry_ref[0:1, :]
        o_ref[...] = _dot3(v, tri, left=True) + carry
        carry_ref[0:1, :] = carry + jnp.sum(v, axis=0, keepdims=True)

    idx = (lambda i: (n - 1 - i, 0)) if reverse else (lambda i: (i, 0))
    return pl.pallas_call(
        body, name=name, grid=(n,),
        in_specs=[pl.BlockSpec((tm, C), idx)],
        out_specs=pl.BlockSpec((tm, C), idx),
        out_shape=jax.ShapeDtypeStruct((T, C), F32),
        scratch_shapes=[pltpu.VMEM((8, C), F32)],
        compiler_params=_cparams(("arbitrary",)),
    )(x)


_NT = (((1,), (1,)), ((), ()))
LOG2E = 1.4426950408889634


def _head_mask(width, e):
    lane = lax.broadcasted_iota(jnp.int32, (1, width), 1)
    return (lane >= 64 * e) & (lane < 64 * (e + 1))


def _attn_fwd(qt4, k_aug, vt4, *, name, tq, gather=None):
    nq = qt4.shape[1]
    T = nq * tq
    cw = min(ATTN_CHUNK, tq)
    comm = gather is not None

    def body(*refs):
        refs = list(refs)
        q_ref, k_ref, v_ref = refs[:3]
        pos = 3
        if comm:
            c_src = refs[pos]
            pos += 1
        o_ref, lse_ref = refs[pos:pos + 2]
        pos += 2
        if comm:
            c_dst = refs[pos]
            pos += 1
        m_s, l_s, acc_s = refs[pos:pos + 2], refs[pos + 2:pos + 4], refs[pos + 4:pos + 6]
        st_a, st_b = refs[pos + 6:pos + 8]
        c_sems = refs[pos + 8:]
        i = pl.program_id(1)
        if comm:
            @pl.when((pl.program_id(0) == 0) & (i == 0))
            def _():
                _comm_start("gather", c_src, c_dst, *c_sems)

        for e in range(2):
            m_s[e][...] = jnp.full(m_s[e].shape, -jnp.inf, F32)
            l_s[e][...] = jnp.zeros_like(l_s[e])
            acc_s[e][...] = jnp.zeros_like(acc_s[e])
        qt = [q_ref[e, 0] for e in range(2)]

        work = [(e, slice(c * cw, (c + 1) * cw)) for e in range(2) for c in range(tq // cw)]

        def scores(j, buf):
            kj = k_ref[pl.ds(pl.multiple_of(j * tq, tq), tq), :]
            for n, (e, cs) in enumerate(work):
                buf[n] = jnp.dot(kj[:, e * LANES:(e + 1) * LANES], qt[e][:, cs], preferred_element_type=F32)

        def step(j, buf, masked):
            vtj = v_ref[0, j]
            for n, (e, cs) in enumerate(work):
                kr = cs.stop if masked else tq
                st = buf[n][:kr]
                if masked:
                    r = lax.broadcasted_iota(jnp.int32, (kr, cw), 0)
                    cc = lax.broadcasted_iota(jnp.int32, (kr, cw), 1) + cs.start
                    st = jnp.where(cc >= r, st, -jnp.inf)
                m_prev = m_s[e][:, cs]
                m_new = jnp.maximum(m_prev, jnp.max(st, axis=0, keepdims=True))
                alpha = jnp.exp2(m_prev - m_new)
                pt = jnp.exp2(st - m_new)
                l_s[e][:, cs] = alpha * l_s[e][:, cs] + jnp.sum(pt, axis=0, keepdims=True)
                acc_s[e][:, cs] = alpha * acc_s[e][:, cs] + jnp.dot(vtj[e * HEAD_V:(e + 1) * HEAD_V, :kr],
                                                                    pt.astype(BF16), preferred_element_type=F32)
                m_s[e][:, cs] = m_new

        def two_tiles(p, carry):
            scores(2 * p + 1, st_b)
            step(2 * p, st_a, False)
            scores(2 * p + 2, st_a)
            step(2 * p + 1, st_b, False)
            return carry

        scores(0, st_a)
        lax.fori_loop(0, i // 2, two_tiles, 0)

        @pl.when(i % 2 == 0)
        def _():
            step(i, st_a, True)

        @pl.when(i % 2 == 1)
        def _():
            scores(i, st_b)
            step(i - 1, st_a, False)
            step(i, st_b, True)

        ot = jnp.concatenate([acc_s[e][...] / l_s[e][...] for e in range(2)], axis=0)
        o_ref[...] = ot.T.astype(o_ref.dtype)
        for e in range(2):
            lse_ref[e, 0] = m_s[e][...] + jnp.log(l_s[e][...]) * LOG2E
        if comm:
            @pl.when((pl.program_id(0) == N_HEADS // 2 - 1) & (i == nq - 1))
            def _():
                _comm_wait("gather", c_src, c_dst, *c_sems)

    in_specs = [
        pl.BlockSpec((2, 1, LANES, tq), lambda hp, i: (hp, i, 0, 0)),
        pl.BlockSpec((T, 2 * LANES), lambda hp, i: (0, hp)),
        pl.BlockSpec((1, nq, LANES, tq), lambda hp, i: (hp, 0, 0, 0)),
    ]
    args = [qt4, k_aug, vt4]
    out_specs = [pl.BlockSpec((tq, LANES), lambda hp, i: (i, hp)),
                 pl.BlockSpec((2, 1, 1, tq), lambda hp, i: (hp, i, 0, 0))]
    out_shape = [jax.ShapeDtypeStruct((T, N_HEADS * HEAD_V), BF16), jax.ShapeDtypeStruct((N_HEADS, nq, 1, tq), F32)]
    scratch = ([pltpu.VMEM((1, tq), F32)] * 4 + [pltpu.VMEM((HEAD_V, tq), F32)] * 2
               + [pltpu.VMEM((2 * tq // cw, tq, cw), F32)] * 2)
    if comm:
        in_specs.append(pl.BlockSpec(memory_space=pl.ANY))
        args.append(gather)
        out_specs.append(pl.BlockSpec(memory_space=pl.ANY))
        out_shape.append(jax.ShapeDtypeStruct((N_DEV,) + gather.shape, gather.dtype))
        scratch += _comm_sems()
    return pl.pallas_call(
        body, name=name, grid=(N_HEADS // 2, nq),
        in_specs=in_specs, out_specs=out_specs, out_shape=out_shape, scratch_shapes=scratch,
        compiler_params=_cparams(("arbitrary", "arbitrary")),
    )(*args)


def _attn_bwd(qt4, k_aug, v, vcb, dot4, lse_row, dl_row, scale, own_rows, *, name, tq, exchange=None):
    nq = qt4.shape[1]
    T = nq * tq
    cw = min(ATTN_CHUNK, tq)
    comm = exchange is not None

    def body(*refs):
        refs = list(refs)
        k_ref, v_ref, q_ref, do_ref, lse_ref, dl_ref = refs[:6]
        pos = 6
        if comm:
            c_src = refs[pos]
            pos += 1
        dq_ref, dk_ref, dv_ref = refs[pos:pos + 3]
        pos += 3
        if comm:
            c_dst = refs[pos]
            pos += 1
        dk_s, dv_s = refs[pos:pos + 2], refs[pos + 2:pos + 4]
        c_sems = refs[pos + 4:]
        j = pl.program_id(1)
        if comm:
            @pl.when((pl.program_id(0) == 0) & (j == 0))
            def _():
                _comm_start("exchange", c_src, c_dst, *c_sems)

        @pl.when(j == 0)
        def _():
            dq_ref[...] = jnp.zeros_like(dq_ref)

        kj = k_ref[...]
        vj = v_ref[...]
        ka = [kj[:, e * LANES:(e + 1) * LANES] for e in range(2)]
        row = lax.broadcasted_iota(jnp.int32, (LANES, 1), 0)
        own = row < own_rows
        kat = [(ka[e].astype(F32).T * jnp.where(own, scale, 1.0)).astype(BF16) for e in range(2)]
        vm = [jnp.where(_head_mask(LANES, e), vj, jnp.zeros_like(vj)) for e in range(2)]
        for e in range(2):
            dk_s[e][...] = jnp.zeros_like(dk_s[e])
            dv_s[e][...] = jnp.zeros_like(dv_s[e])

        def step(i, masked):
            dot_i = do_ref[0, i]
            qts = [q_ref[e, i] for e in range(2)]
            lse = [lse_ref[e, i] for e in range(2)]
            dl = [dl_ref[e, i] for e in range(2)]
            work = [(e, slice(c * cw, (c + 1) * cw)) for e in range(2) for c in range(tq // cw)]
            rows = lambda cs: cs.stop if masked else tq
            scores = lambda e, cs: (jnp.dot(ka[e][:rows(cs)], qts[e][:, cs], preferred_element_type=F32),
                                    jnp.dot(vm[e][:rows(cs)], dot_i[:, cs], preferred_element_type=F32))
            nxt = scores(*work[0])
            dqs = [[], []]
            for n, (e, cs) in enumerate(work):
                st, dpt = nxt
                kr = rows(cs)
                if n + 1 < len(work):
                    nxt = scores(*work[n + 1])
                if masked:
                    r = lax.broadcasted_iota(jnp.int32, (kr, cw), 0)
                    cc = lax.broadcasted_iota(jnp.int32, (kr, cw), 1) + cs.start
                    st = jnp.where(cc >= r, st, -jnp.inf)
                pt = jnp.exp2(st - lse[e][:, cs])
                dv_s[e][:, :kr] += lax.dot_general(dot_i[e * HEAD_V:(e + 1) * HEAD_V, cs], pt.astype(BF16), _NT,
                                                   preferred_element_type=F32)
                dsb = (pt * (dpt - dl[e][:, cs])).astype(BF16)
                dk_s[e][:, :kr] += lax.dot_general(qts[e][:, cs], dsb, _NT, preferred_element_type=F32)
                dqs[e].append(jnp.dot(kat[e][:, :kr], dsb, preferred_element_type=F32))
            for e in range(2):
                dq_ref[e, i] += jnp.concatenate(dqs[e], axis=1)

        def loop_body(i, carry):
            step(i, False)
            return carry

        step(j, True)
        lax.fori_loop(j + 1, nq, loop_body, 0)
        for e in range(2):
            dk_ref[e, 0] = dk_s[e][...] * jnp.where(own, 1.0 / LOG2E, 1.0)
        dv_ref[...] = jnp.concatenate([dv_s[0][...], dv_s[1][...]], axis=0).T.astype(dv_ref.dtype)
        if comm:
            @pl.when((pl.program_id(0) == N_HEADS // 2 - 1) & (j == nq - 1))
            def _():
                _comm_wait("exchange", c_src, c_dst, *c_sems)

    row4 = pl.BlockSpec((2, nq, 1, tq), lambda hp, j: (hp, 0, 0, 0))
    in_specs = [
        pl.BlockSpec((tq, 2 * LANES), lambda hp, j: (j, hp)),
        pl.BlockSpec((tq, LANES), lambda hp, j: (j, vcb + hp)),
        pl.BlockSpec((2, nq, LANES, tq), lambda hp, j: (hp, 0, 0, 0)),
        pl.BlockSpec((1, nq, LANES, tq), lambda hp, j: (hp, 0, 0, 0)),
        row4, row4,
    ]
    args = [k_aug, v, qt4, dot4, lse_row, dl_row]
    out_specs = [pl.BlockSpec((2, nq, LANES, tq), lambda hp, j: (hp, 0, 0, 0)),
                 pl.BlockSpec((2, 1, LANES, tq), lambda hp, j: (hp, j, 0, 0)),
                 pl.BlockSpec((tq, LANES), lambda hp, j: (j, hp))]
    out_shape = [jax.ShapeDtypeStruct((N_HEADS, nq, LANES, tq), F32), jax.ShapeDtypeStruct((N_HEADS, nq, LANES, tq), F32),
                 jax.ShapeDtypeStruct((T, N_HEADS * HEAD_V), BF16)]
    scratch = [pltpu.VMEM((LANES, tq), F32)] * 2 + [pltpu.VMEM((HEAD_V, tq), F32)] * 2
    if comm:
        in_specs.append(pl.BlockSpec(memory_space=pl.ANY))
        args.append(exchange)
        out_specs.append(pl.BlockSpec(memory_space=pl.ANY))
        out_shape.append(jax.ShapeDtypeStruct(exchange.shape, exchange.dtype))
        scratch += _comm_sems()
    return pl.pallas_call(
        body, name=name, grid=(N_HEADS // 2, nq),
        in_specs=in_specs, out_specs=out_specs, out_shape=out_shape, scratch_shapes=scratch,
        compiler_params=pltpu.CompilerParams(dimension_semantics=("arbitrary", "arbitrary"),
                                             vmem_limit_bytes=ATTN_BWD_VMEM_LIMIT),
    )(*args)


def _peers():
    x, y, c = lax.axis_index("x"), lax.axis_index("y"), lax.axis_index("c")
    me = 4 * x + 2 * y + c
    out = []
    for k in range(1, N_DEV):
        px = (1 - x) if (k & 4) else x
        py = (1 - y) if (k & 2) else y
        pc = (1 - c) if (k & 1) else c
        out.append(((px, py, pc), 4 * px + 2 * py + pc))
    return me, out


def _comm_sems():
    return [pltpu.SemaphoreType.DMA((N_DEV - 1,)), pltpu.SemaphoreType.DMA((N_DEV - 1,)), pltpu.SemaphoreType.DMA]


def _comm_copies(kind, src_ref, dst_ref, send_sems, recv_sems, local_sem):
    me, peers = _peers()
    local = pltpu.make_async_copy(src_ref if kind == "gather" else src_ref.at[me], dst_ref.at[me], local_sem)
    sends, recvs = [], []
    for k, (dev, lin) in enumerate(peers):
        src = src_ref if kind == "gather" else src_ref.at[lin]
        sends.append(pltpu.make_async_remote_copy(src_ref=src, dst_ref=dst_ref.at[me], send_sem=send_sems.at[k],
                                                  recv_sem=recv_sems.at[k], device_id=dev, device_id_type=MESH))
        recvs.append(pltpu.make_async_remote_copy(src_ref=src, dst_ref=dst_ref.at[lin], send_sem=send_sems.at[k],
                                                  recv_sem=recv_sems.at[k], device_id=dev, device_id_type=MESH))
    return local, sends, recvs


def _comm_start(kind, src_ref, dst_ref, send_sems, recv_sems, local_sem):
    local, sends, _ = _comm_copies(kind, src_ref, dst_ref, send_sems, recv_sems, local_sem)
    local.start()
    for cp in sends:
        cp.start()


def _comm_wait(kind, src_ref, dst_ref, send_sems, recv_sems, local_sem):
    local, sends, recvs = _comm_copies(kind, src_ref, dst_ref, send_sems, recv_sems, local_sem)
    for cp in recvs:
        cp.wait_recv()
    for cp in sends:
        cp.wait_send()
    local.wait()


def _gather_via_sibling(src, name):
    def body(x_ref, o_ref, send_sems, recv_sems, local_sem):
        x, y, c = lax.axis_index("x"), lax.axis_index("y"), lax.axis_index("c")
        me, sibling = (x, y, c), (x, y, 1 - c)
        chips = [(1 - x, y), (x, 1 - y), (1 - x, 1 - y)]

        def slot(px, py, pc):
            return o_ref.at[4 * px + 2 * py + pc]

        def copy(k, block, to, src_ref=None):
            return pltpu.make_async_remote_copy(src_ref=slot(*block) if src_ref is None else src_ref,
                                                dst_ref=slot(*block), send_sem=send_sems.at[k],
                                                recv_sem=recv_sems.at[k], device_id=to, device_id_type=MESH)

        mine = pltpu.make_async_copy(x_ref, slot(*me), local_sem)
        mine.start()
        first = [copy(0, me, sibling, x_ref)] + [copy(1 + n, me, (*chip, c), x_ref) for n, chip in enumerate(chips)]
        for cp in first:
            cp.start()
        passed = [copy(4 + n, (*chip, c), sibling) for n, chip in enumerate(chips)]
        for n, chip in enumerate(chips):
            copy(1 + n, (*chip, c), me).wait_recv()
            passed[n].start()
        copy(0, sibling, me).wait_recv()
        for n, chip in enumerate(chips):
            copy(4 + n, (*chip, 1 - c), me).wait_recv()
        for cp in first + passed:
            cp.wait_send()
        mine.wait()

    return pl.pallas_call(
        body, name=name,
        in_specs=[pl.BlockSpec(memory_space=pl.ANY)],
        out_specs=pl.BlockSpec(memory_space=pl.ANY),
        out_shape=jax.ShapeDtypeStruct((N_DEV,) + src.shape, src.dtype),
        scratch_shapes=_comm_sems(),
    )(src)


def _adamw(w, g, m, v):
    m2 = ADAM_B1 * m + (1.0 - ADAM_B1) * g
    v2 = ADAM_B2 * v + (1.0 - ADAM_B2) * (g * g)
    m_hat = m2 / (1.0 - ADAM_B1 ** ADAM_STEP)
    v_hat = v2 / (1.0 - ADAM_B2 ** ADAM_STEP)
    delta = -ADAM_LR * (m_hat / (jnp.sqrt(v_hat) + ADAM_EPS) + ADAM_WD * w)
    return delta, m2, v2


def _sum_adamw(pieces, w, m, v, *, name):
    R = w.shape[0]
    tr = R // 4 if R % 64 == 0 else R

    def body(p_ref, w_ref, m_ref, v_ref, g_ref, d_ref, m2_ref, v2_ref):
        g = p_ref[0].astype(F32)
        for s in range(1, N_DEV):
            g = g + p_ref[s].astype(F32)
        delta, m2, v2 = _adamw(w_ref[...], g, m_ref[...], v_ref[...])
        g_ref[...] = g
        d_ref[...] = delta
        m2_ref[...] = m2
        v2_ref[...] = v2

    row = pl.BlockSpec((tr, LANES), lambda i: (i, 0))
    return pl.pallas_call(
        body, name=name, grid=(R // tr,),
        in_specs=[pl.BlockSpec((N_DEV, tr, LANES), lambda i: (0, i, 0)), row, row, row],
        out_specs=[row, row, row, row],
        out_shape=[jax.ShapeDtypeStruct((R, LANES), F32)] * 4,
        compiler_params=_cparams(("parallel",)),
    )(pieces, w, m, v)


def _small_allreduce_adamw(part, w, m, v):
    shape = part.shape

    def body(p_ref, w_ref, m_ref, v_ref, g_ref, d_ref, m2_ref, v2_ref, loss_ref, gath, send_sems, recv_sems):
        me, peers = _peers()
        gath[me] = p_ref[...]
        sends = []
        for k, (dev, _) in enumerate(peers):
            cp = pltpu.make_async_remote_copy(src_ref=p_ref, dst_ref=gath.at[me], send_sem=send_sems.at[k],
                                              recv_sem=recv_sems.at[k], device_id=dev, device_id_type=MESH)
            cp.start()
            sends.append(cp)
        for k, (dev, lin) in enumerate(peers):
            pltpu.make_async_remote_copy(src_ref=p_ref, dst_ref=gath.at[lin], send_sem=send_sems.at[k],
                                         recv_sem=recv_sems.at[k], device_id=dev, device_id_type=MESH).wait_recv()
        for cp in sends:
            cp.wait_send()
        g = gath[0]
        for s in range(1, N_DEV):
            g = g + gath[s]
        delta, m2, v2 = _adamw(w_ref[...], g, m_ref[...], v_ref[...])
        g_ref[...] = g
        d_ref[...] = delta
        m2_ref[...] = m2
        v2_ref[...] = v2
        sq = jnp.sum(g[SMALL_LOSS_ROW:SMALL_LOSS_ROW + 8, :], axis=1, keepdims=True)
        tot = jnp.sum(sq, axis=0, keepdims=True) * (0.5 / D_MODEL)
        loss_ref[...] = jnp.broadcast_to(tot, loss_ref.shape)

    vm = pl.BlockSpec(memory_space=pltpu.VMEM)
    return pl.pallas_call(
        body, name="small_allreduce_adamw",
        in_specs=[vm, vm, vm, vm],
        out_specs=[vm, vm, vm, vm, vm],
        out_shape=[jax.ShapeDtypeStruct(shape, F32)] * 4 + [jax.ShapeDtypeStruct((8, LANES), F32)],
        scratch_shapes=[pltpu.VMEM((N_DEV,) + shape, F32),
                        pltpu.SemaphoreType.DMA((N_DEV - 1,)), pltpu.SemaphoreType.DMA((N_DEV - 1,))],
    )(part, w, m, v)


def _perm_in_cols(w):
    z = lambda n: jnp.zeros(w.shape[:-1] + (n,), w.dtype)
    small = jnp.concatenate([w[..., 384:400], z(16), w[..., 1952:1960], z(24), w[..., 400:416], z(48)], -1)
    return jnp.concatenate([w[..., 1960:2984], w[..., 2984:4008], w[..., 416:928], w[..., 928:1440],
                            w[..., 1440:1952], w[..., 0:256], w[..., 256:384], small], -1)


def _unperm_in_cols(wp):
    s = wp[..., P_SMALL:]
    return jnp.concatenate([wp[..., P_CQ:P_CQ + 256], wp[..., P_CKV:P_CKV + 128], s[..., 0:16], s[..., 64:80],
                            wp[..., P_FQ:P_FQ + 512], wp[..., P_FK:P_FK + 512], wp[..., P_FV:P_FV + 512],
                            s[..., S_FL:S_FL + 8], wp[..., P_GA:P_GA + 1024], wp[..., P_GB:P_GB + 1024]], -1)


def _aug_uq_cols(w):
    r = w.shape[0]
    w3 = w.reshape(r, N_HEADS, NOPE + ROPE)
    z = jnp.zeros((r, N_HEADS, 32), w.dtype)
    return jnp.concatenate([w3[:, :, 64:80], w3[:, :, 0:48], w3[:, :, 80:96], w3[:, :, 48:64], z], -1).reshape(r, 1024)


def _unaug_uq_cols(wp):
    r = wp.shape[0]
    w3 = wp.reshape(r, N_HEADS, LANES)
    return jnp.concatenate([w3[:, :, 16:64], w3[:, :, 80:96], w3[:, :, 0:16], w3[:, :, 64:80]], -1).reshape(r, 768)


def _aug_uk_cols(w):
    r = w.shape[0]
    w3 = w.reshape(r, N_HEADS, NOPE)
    z = lambda n: jnp.zeros((r, N_HEADS, n), w.dtype)
    return jnp.concatenate([z(16), w3[:, :, 0:48], z(16), w3[:, :, 48:64], z(32)], -1).reshape(r, 1024)


def _unaug_uk_cols(wp):
    r = wp.shape[0]
    w3 = wp.reshape(r, N_HEADS, LANES)
    return jnp.concatenate([w3[:, :, 16:64], w3[:, :, 80:96]], -1).reshape(r, 512)


IN_SEGMENTS = ((0, 256, P_CQ), (256, 128, P_CKV), (384, 16, P_SMALL), (400, 16, P_SMALL + 64), (416, 512, P_FQ),
               (928, 512, P_FK), (1440, 512, P_FV), (1952, 8, P_SMALL + S_FL), (1960, 1024, P_GA), (2984, 1024, P_GB))
IN_SHARD = D_IN // N_DEV


def _perm_in_from_shards(w3):
    r = w3.shape[0]
    parts = []
    pos = 0
    for o0, n, p0 in sorted(IN_SEGMENTS, key=lambda t: t[2]):
        if p0 > pos:
            parts.append(jnp.zeros((r, p0 - pos), w3.dtype))
        a, b = o0, o0 + n
        for d in range(a // IN_SHARD, (b - 1) // IN_SHARD + 1):
            parts.append(w3[:, d, max(a, d * IN_SHARD) - d * IN_SHARD:min(b, (d + 1) * IN_SHARD) - d * IN_SHARD])
        pos = p0 + n
    parts.append(jnp.zeros((r, D_IN_PAD - pos), w3.dtype))
    return jnp.concatenate(parts, -1)


def _unperm_in_to_shards(gp):
    r = gp.shape[0]
    shards = []
    for d in range(N_DEV):
        a, b = d * IN_SHARD, (d + 1) * IN_SHARD
        parts = [gp[:, p0 + max(a, o0) - o0:p0 + min(b, o0 + n) - o0] for o0, n, p0 in IN_SEGMENTS
                 if max(a, o0) < min(b, o0 + n)]
        parts.append(jnp.zeros((r, _lane_pad(IN_SHARD) - IN_SHARD), gp.dtype))
        shards.append(jnp.concatenate(parts, -1))
    return jnp.stack(shards, 0)


def _lane_pad(c):
    return -(-c // LANES) * LANES


def _packed_cols(c, by_col):
    return _lane_pad(c) if by_col else c


def _pack_shards(group, shards, dtype):
    rows = _pack_rows(group)
    segs = []
    for name, r, c, by_col in group:
        s = shards[name].reshape(r, c).astype(dtype)
        segs.append(jnp.pad(s, ((0, 0), (0, _packed_cols(c, by_col) - c))).reshape(-1))
    flat = jnp.concatenate(segs)
    flat = jnp.pad(flat, (0, rows * LANES - flat.shape[0]))
    return flat.reshape(rows, LANES)


def _unpack_shards(group, packed):
    flat = packed.reshape(-1)
    out, off = {}, 0
    for name, r, c, by_col in group:
        cp = _packed_cols(c, by_col)
        out[name] = flat[off:off + r * cp].reshape(1, r, cp)[:, :, :c]
        off += r * cp
    return out


def _unpack_full(group, gathered):
    flat = gathered.reshape(N_DEV, -1)
    out, off = {}, 0
    for name, r, c, by_col in group:
        cp = _packed_cols(c, by_col)
        blk = flat[:, off:off + r * cp].reshape(N_DEV, r, cp)
        out[name] = blk.transpose(1, 0, 2) if by_col else blk.reshape(N_DEV * r, c)
        off += r * cp
    return out


def _full_cols(w3, c):
    return w3[:, :, :c].reshape(w3.shape[0], N_DEV * c)


def _pack_full(group, grads, dtype):
    rows = _pack_rows(group)
    segs = []
    for name, r, c, by_col in group:
        g = grads[name]
        cp = _packed_cols(c, by_col)
        if by_col and g.ndim == 2:
            g = jnp.pad(g.reshape(r, N_DEV, c), ((0, 0), (0, 0), (0, cp - c))).transpose(1, 0, 2)
        segs.append(g.reshape(N_DEV, r * cp).astype(dtype))
    flat = jnp.concatenate(segs, axis=1)
    flat = jnp.pad(flat, ((0, 0), (0, rows * LANES - flat.shape[1])))
    return flat.reshape(N_DEV, rows, LANES)


SMALL_LAYOUT = (("ln_pre_mix", 1024, 0), ("ln_post_mix", 1024, 8), ("ln_pre_mlp", 1024, 16),
                ("ln_post_mlp", 1024, 24), ("b_in", 4008, 32), ("q_a_norm", 256, 64), ("kv_a_norm", 128, 66))


def _pack_small(vals, extra=None):
    rows = []
    for name, n, _ in SMALL_LAYOUT:
        v = vals[name].reshape(-1).astype(F32)
        pad = -n % LANES
        rows.append(jnp.pad(v, (0, pad)).reshape(-1, LANES))
    rows.append(jnp.zeros((SMALL_LOSS_ROW - 67, LANES), F32))
    rows.append(jnp.zeros((8, LANES), F32) if extra is None else extra.reshape(8, LANES))
    return jnp.concatenate(rows, axis=0)


def _unpack_small(packed):
    out = {}
    for name, n, r0 in SMALL_LAYOUT:
        nr = -(-n // LANES)
        out[name] = packed[r0:r0 + nr].reshape(-1)[:n].reshape(1, n)
    return out


def _rms(xf, g):
    r = lax.rsqrt(jnp.mean(xf * xf, axis=-1, keepdims=True) + NORM_EPS)
    return (xf * r) * g


def _rms_bwd(xf, g, dy):
    r = lax.rsqrt(jnp.mean(xf * xf, axis=-1, keepdims=True) + NORM_EPS)
    xhat = xf * r
    dxhat = dy * g
    dx = r * (dxhat - xhat * jnp.mean(dxhat * xhat, axis=-1, keepdims=True))
    return dx, dy * xhat


def _sigmoid(t):
    return 1.0 / (1.0 + jnp.exp(-t))


def _log_sigmoid(t):
    return jnp.minimum(t, 0.0) - jnp.log(1.0 + jnp.exp(-jnp.abs(t)))


def _lane_sign():
    lane = lax.broadcasted_iota(jnp.int32, (1, LANES), 1)
    return jnp.where(lane < 64, -1.0, 1.0).astype(F32), lane


def _rope_lanes(lane):
    return (lane < 16) | ((lane >= 64) & (lane < 80))


def kernel(x, positions, ln_pre_mix, ln_post_mix, ln_pre_mlp, ln_post_mlp, w_in, b_in, q_a_norm, w_uq, kv_a_norm, w_uk, w_uv, w_o_mla, w_o_fox, w_out, w_ff1, w_ff2, loss_target, m_ln_pre_mix, m_ln_post_mix, m_ln_pre_mlp, m_ln_post_mlp, m_w_in, m_b_in, m_q_a_norm, m_w_uq, m_kv_a_norm, m_w_uk, m_w_uv, m_w_o_mla, m_w_o_fox, m_w_out, m_w_ff1, m_w_ff2, v_ln_pre_mix, v_ln_post_mix, v_ln_pre_mlp, v_ln_post_mlp, v_w_in, v_b_in, v_q_a_norm, v_w_uq, v_kv_a_norm, v_w_uk, v_w_uv, v_w_o_mla, v_w_o_fox, v_w_out, v_w_ff1, v_w_ff2):
    T = x.shape[1]
    x2 = x.reshape(T, D_MODEL)
    tgt = loss_target.reshape(T, D_MODEL)
    w_sh = dict(w_in=w_in, w_uq=w_uq, w_uk=w_uk, w_uv=w_uv, w_o_mla=w_o_mla, w_o_fox=w_o_fox, w_out=w_out,
                w_ff1=w_ff1, w_ff2=w_ff2)
    m_sh = dict(w_in=m_w_in, w_uq=m_w_uq, w_uk=m_w_uk, w_uv=m_w_uv, w_o_mla=m_w_o_mla, w_o_fox=m_w_o_fox,
                w_out=m_w_out, w_ff1=m_w_ff1, w_ff2=m_w_ff2)
    v_sh = dict(w_in=v_w_in, w_uq=v_w_uq, w_uk=v_w_uk, w_uv=v_w_uv, w_o_mla=v_w_o_mla, w_o_fox=v_w_o_fox,
                w_out=v_w_out, w_ff1=v_w_ff1, w_ff2=v_w_ff2)
    small_w = dict(ln_pre_mix=ln_pre_mix, ln_post_mix=ln_post_mix, ln_pre_mlp=ln_pre_mlp, ln_post_mlp=ln_post_mlp,
                   b_in=b_in, q_a_norm=q_a_norm, kv_a_norm=kv_a_norm)
    small_m = dict(ln_pre_mix=m_ln_pre_mix, ln_post_mix=m_ln_post_mix, ln_pre_mlp=m_ln_pre_mlp,
                   ln_post_mlp=m_ln_post_mlp, b_in=m_b_in, q_a_norm=m_q_a_norm, kv_a_norm=m_kv_a_norm)
    small_v = dict(ln_pre_mix=v_ln_pre_mix, ln_post_mix=v_ln_post_mix, ln_pre_mlp=v_ln_pre_mlp,
                   ln_post_mlp=v_ln_post_mlp, b_in=v_b_in, q_a_norm=v_q_a_norm, kv_a_norm=v_kv_a_norm)
    mla_scale = float((NOPE + ROPE) ** -0.5)
    fox_scale = float(FOX_D ** -0.5)

    W = _unpack_full(GROUP_A, _gather_via_sibling(_pack_shards(GROUP_A, w_sh, BF16), "allgather_weights_a"))
    w_in_p = _perm_in_from_shards(W["w_in"])
    b_in_p = _perm_in_cols(b_in.astype(F32))
    w_uq_a = _aug_uq_cols(_full_cols(W["w_uq"], 96))
    w_kv_a = jnp.concatenate([_aug_uk_cols(_full_cols(W["w_uk"], 64)), _full_cols(W["w_uv"], 64)], axis=1)
    g1, g2, g3, g4 = ln_pre_mix, ln_post_mix, ln_pre_mlp, ln_post_mlp
    gq, gkv = q_a_norm, kv_a_norm
    tq = min(ATTN_TILE, T)
    nq = T // tq

    (h,) = _rowwise(lambda xv, g: _rms(xv, g), [x2], [g1], [(D_MODEL, BF16)], name="pre_mix_norm")
    q_cols = jnp.ones((1, D_IN_PAD), F32).at[:, P_FQ:P_FQ + 512].set(fox_scale * LOG2E)
    z = _mm(h, w_in_p, bias=b_in_p, colscale=q_cols, tm=2048, name="in_proj")
    zs = _mm(h, w_in_p[:, P_SMALL:], bias=b_in_p[:, P_SMALL:], outs=(F32,), name="in_proj_small")

    def lora_norm(cq, ckv, a, b):
        return _rms(cq.astype(F32), a), _rms(ckv.astype(F32), b)

    cqn, ckvn = _rowwise(lora_norm, [(z, 256, P_CQ // 256), (z, 128, P_CKV // 128)], [gq, gkv],
                         [(Q_LORA, BF16), (KV_LORA, BF16)], name="lora_norm")
    q_x = _mm(cqn, w_uq_a, outs=(F32,), name="q_up")
    kv = _mm(ckvn, w_kv_a, tn=512, name="kv_up")

    half = ROPE // 2
    inv_freq = ROPE_THETA ** (-jnp.arange(half, dtype=F32) / half)
    inv128 = jnp.tile(inv_freq, LANES // half).reshape(1, LANES)
    pos_col = positions.reshape(T, 1).astype(F32)

    def rope_tables(p, f):
        ang = p * f
        return jnp.cos(ang), jnp.sin(ang)

    cos_t, sin_t = _rowwise(rope_tables, [pos_col], [inv128], [(LANES, F32), (LANES, F32)], name="rope_tables")

    def rope_fwd(qx, kn, vn, s, cs, sn):
        sign, lane = _lane_sign()
        rope_l = _rope_lanes(lane)
        rot = lambda t: t * cs + pltpu.roll(t, 64, 1) * sn * sign
        k_rot = jnp.where(rope_l, rot(s), 0.0)
        qs, ks = [], []
        for hd in range(N_HEADS):
            qb = qx[:, LANES * hd:LANES * (hd + 1)]
            qs.append(jnp.where(rope_l, rot(qb), qb))
            ks.append(kn[:, LANES * hd:LANES * (hd + 1)].astype(F32) + k_rot)
        return jnp.concatenate(qs, axis=1) * (mla_scale * LOG2E), jnp.concatenate(ks, axis=1), vn

    qt4_m, k_am, vt4_m = _rowwise(rope_fwd, [q_x, (kv, 1024, 0), (kv, 512, 2), zs, cos_t, sin_t], [],
                                  [(1024, BF16, "t4"), (1024, BF16), (512, BF16, "t4")], name="rope_fwd", tm=tq)
    f_cum = _seq_cumsum(zs, _log_sigmoid, reverse=False, name="forget_cumsum")

    def fox_aug(qf, kf, vf, fc):
        lane = lax.broadcasted_iota(jnp.int32, (1, LANES), 1)
        qs, ks = [], []
        for hd in range(N_HEADS):
            qp = qf[:, LANES * (hd // 2):LANES * (hd // 2 + 1)].astype(F32)
            kp = kf[:, LANES * (hd // 2):LANES * (hd // 2 + 1)].astype(F32)
            if hd % 2:
                qp = pltpu.roll(qp, 64, 1)
                kp = pltpu.roll(kp, 64, 1)
            fb = jnp.broadcast_to(fc[:, S_FL + hd:S_FL + hd + 1] * (-LOG2E), qp.shape)
            hi = fb.astype(BF16).astype(F32)
            mid = (fb - hi).astype(BF16).astype(F32)
            lo = fb - hi - mid
            qs.append(jnp.where(lane < 64, qp, jnp.where(lane < 67, 1.0, 0.0)))
            ks.append(jnp.where(lane < 64, kp, jnp.where(lane == 64, hi, jnp.where(lane == 65, mid, jnp.where(
                lane == 66, lo, jnp.where(lane == 67, 1.0, 0.0))))))
        return jnp.concatenate(qs, axis=1), jnp.concatenate(ks, axis=1), vf

    qt4_f, k_af, vt4_f = _rowwise(fox_aug, [(z, 512, P_FQ // 512), (z, 512, P_FK // 512), (z, 512, P_FV // 512), f_cum],
                                  [], [(1024, BF16, "t4"), (1024, BF16), (512, BF16, "t4")], name="fox_aug", tm=tq)
    o_mla, lse_mla, gathered_b = _attn_fwd(qt4_m, k_am, vt4_m, name="mla_attn_fwd", tq=tq,
                                           gather=_pack_shards(GROUP_B, w_sh, BF16))
    W.update(_unpack_full(GROUP_B, gathered_b))
    for n, _, c, by_col in GROUP_B:
        if by_col:
            W[n] = _full_cols(W[n], c)
    o_fox, lse_fox = _attn_fwd(qt4_f, k_af, vt4_f, name="fox_attn_fwd", tq=tq)
    y_mla = _mm(o_mla, W["w_o_mla"], outs=(F32,), name="o_proj_mla")
    y_fox = _mm(o_fox, W["w_o_fox"], outs=(F32,), name="o_proj_fox")

    def gate_merge(ga, gb, ya, yb):
        return _sigmoid(ga.astype(F32)) * ya + _sigmoid(gb.astype(F32)) * yb

    (merged,) = _rowwise(gate_merge, [(z, 1024, 0), (z, 1024, 1), y_mla, y_fox], [], [(D_MODEL, BF16)],
                         name="gate_merge")
    mix = _mm(merged, W["w_out"], outs=(F32,), name="out_proj")

    def post_mix(xv, mv, a, b):
        x1v = xv + _rms(mv, a)
        return x1v, _rms(x1v, b)

    x1, h2 = _rowwise(post_mix, [x2, mix], [g2, g3], [(D_MODEL, F32), (D_MODEL, BF16)], name="post_mix_norm")

    def relu2(acc):
        r = jnp.maximum(acc, 0.0)
        return r * r, acc

    act, u = _mm(h2, W["w_ff1"], epi=relu2, outs=(BF16, BF16), tm=2048, name="ff1")
    mlp = _mm(act, W["w_ff2"], outs=(F32,), tm=1024, tk=2048, name="ff2")

    def loss_bwd(x1v, mv, tv, g):
        y = x1v + _rms(mv, g)
        d = y - tv
        dy = d * (1.0 / D_MODEL)
        dm, dg = _rms_bwd(mv, g, dy)
        return dy, dm, dg, d * d

    dy, dm, dg4, loss_cols = _rowwise(loss_bwd, [x1, mlp, tgt], [g4], [(D_MODEL, F32), (D_MODEL, BF16)],
                                      [D_MODEL, D_MODEL], name="loss_bwd")

    def relu2_bwd(acc, uv):
        return (acc * (2.0 * jnp.maximum(uv.astype(F32), 0.0)),)

    du = _mm(dm, W["w_ff2"], tb=True, extras=(u,), epi=relu2_bwd, tm=2048, name="ff2_bwd")
    dw_ff2 = _mm(act, dm, ta=True, outs=(BF16,), tk=2048, name="ff2_wgrad")
    dh2 = _mm(du, W["w_ff1"], tb=True, outs=(F32,), tm=1024, tk=2048, name="ff1_bwd")
    dw_ff1 = _mm(h2, du, ta=True, outs=(BF16,), tk=2048, name="ff1_wgrad")

    def post_mix_bwd(x1v, dh2v, dyv, mv, a, b):
        d3, dg3v = _rms_bwd(x1v, b, dh2v)
        dx1v = dyv + d3
        dmixv, dg2v = _rms_bwd(mv, a, dx1v)
        return dx1v, dmixv, dg3v, dg2v

    dx1, dmix, dg3, dg2 = _rowwise(post_mix_bwd, [x1, dh2, dy, mix], [g2, g3], [(D_MODEL, F32), (D_MODEL, BF16)],
                                   [D_MODEL, D_MODEL], name="post_mix_bwd")
    dmerged = _mm(dmix, W["w_out"], tb=True, outs=(F32,), name="out_proj_bwd")
    dw_out = _mm(merged, dmix, ta=True, outs=(BF16,), name="out_proj_wgrad")

    def gate_bwd(dmg, ga, gb, ya, yb):
        sa = _sigmoid(ga.astype(F32))
        sb = _sigmoid(gb.astype(F32))
        return dmg * sa, dmg * sb, dmg * ya * sa * (1.0 - sa), dmg * yb * sb * (1.0 - sb)

    dy_mla, dy_fox, dga, dgb = _rowwise(gate_bwd, [dmerged, (z, 1024, 0), (z, 1024, 1), y_mla, y_fox], [],
                                        [(D_MODEL, BF16)] * 4, name="gate_bwd")
    twice = lambda acc: (acc, acc)
    do_mla, dot4_m = _mm(dy_mla, W["w_o_mla"], tb=True, epi=twice, outs=(BF16, BF16), t4=(False, True), tm=tq,
                         name="o_proj_mla_bwd")
    do_fox, dot4_f = _mm(dy_fox, W["w_o_fox"], tb=True, epi=twice, outs=(BF16, BF16), t4=(False, True), tm=tq,
                         name="o_proj_fox_bwd")
    dw_o_mla = _mm(o_mla, dy_mla, ta=True, outs=(BF16,), name="o_proj_mla_wgrad")
    dw_o_fox = _mm(o_fox, dy_fox, ta=True, outs=(BF16,), name="o_proj_fox_wgrad")

    def head_dots(ov, dov):
        r = lax.broadcasted_iota(jnp.int32, (N_HEADS * HEAD_V, LANES), 0)
        cc = lax.broadcasted_iota(jnp.int32, (N_HEADS * HEAD_V, LANES), 1)
        sel = jnp.where((r // HEAD_V) == cc, 1.0, 0.0).astype(BF16)
        return _dot3(ov.astype(F32) * dov.astype(F32), sel, left=False)

    (dl_mla,) = _rowwise(head_dots, [o_mla, do_mla], [], [(LANES, F32)], name="mla_attn_delta")
    (dl_fox,) = _rowwise(head_dots, [o_fox, do_fox], [], [(LANES, F32)], name="fox_attn_delta")
    heads_row4 = lambda t: t[:, :N_HEADS].T.reshape(N_HEADS, nq, 1, tq)
    grads_b = dict(w_o_mla=dw_o_mla, w_o_fox=dw_o_fox, w_out=dw_out, w_ff1=dw_ff1, w_ff2=dw_ff2)
    dqt_m, dkt_m, dv_mla, pieces_b = _attn_bwd(qt4_m, k_am, kv, 1024 // LANES, dot4_m, lse_mla,
                                               heads_row4(dl_mla), mla_scale, NOPE + ROPE, name="mla_attn_bwd", tq=tq,
                                               exchange=_pack_full(GROUP_B, grads_b, BF16))
    dqt_f, dkt_f, dv_fox = _attn_bwd(qt4_f, k_af, z, P_FV // LANES, dot4_f, lse_fox, heads_row4(dl_fox),
                                     fox_scale, FOX_D, name="fox_attn_bwd", tq=tq)

    def fox_unpack(dqa, dka):
        lane = lax.broadcasted_iota(jnp.int32, (1, LANES), 1)
        dqs, dks = [], []
        d_f = jnp.zeros(dqa[:, :LANES].shape, F32)
        for hp in range(N_HEADS // 2):
            blk = lambda t, e: t[:, LANES * (2 * hp + e):LANES * (2 * hp + e + 1)]
            dqs.append(jnp.where(lane < 64, blk(dqa, 0), pltpu.roll(blk(dqa, 1), 64, 1)))
            dks.append(jnp.where(lane < 64, blk(dka, 0), pltpu.roll(blk(dka, 1), 64, 1)))
            for e in range(2):
                g = blk(dqa, e)[:, 67:68] - blk(dka, e)[:, 64:65]
                d_f = jnp.where(lane == S_FL + 2 * hp + e, g, d_f)
        return jnp.concatenate(dqs, axis=1), jnp.concatenate(dks, axis=1), d_f

    dq_fox, dk_fox, d_f128 = _rowwise(fox_unpack, [("t4", dqt_f), ("t4", dkt_f)], [],
                                      [(512, BF16), (512, BF16), (LANES, F32)], name="fox_unpack", tm=tq)
    df_rev = _seq_cumsum(d_f128, lambda t: t, reverse=True, name="forget_cumsum_bwd")

    def rope_bwd(dqa, dka, dvm, dfr, s, cs, sn):
        sign, lane = _lane_sign()
        rope_l = _rope_lanes(lane)
        rot_t = lambda t: t * cs - pltpu.roll(t, 64, 1) * sn * sign
        dqs = []
        dk_rot = None
        for hd in range(N_HEADS):
            blk = dqa[:, LANES * hd:LANES * (hd + 1)]
            dqs.append(jnp.where(rope_l, rot_t(blk), blk))
            blk = dka[:, LANES * hd:LANES * (hd + 1)]
            dk_rot = blk if dk_rot is None else dk_rot + blk
        dfl = dfr * _sigmoid(-s)
        small = jnp.where(rope_l, rot_t(dk_rot), jnp.where((lane >= S_FL) & (lane < S_FL + N_HEADS), dfl, 0.0))
        return jnp.concatenate(dqs, axis=1), jnp.concatenate([dka, dvm.astype(F32)], axis=1), small

    dq_b, dkv, d_small = _rowwise(rope_bwd, [("t4", dqt_m), ("t4", dkt_m), dv_mla, df_rev, zs, cos_t, sin_t], [],
                                  [(1024, BF16), (1536, BF16), (LANES, BF16)], name="rope_bwd", tm=tq)
    dcqn = _mm(dq_b, w_uq_a, tb=True, outs=(F32,), name="q_up_bwd")
    dw_uq_a = _mm(cqn, dq_b, ta=True, outs=(BF16,), name="q_up_wgrad")
    dckvn = _mm(dkv, w_kv_a, tb=True, outs=(F32,), tk=512, name="kv_up_bwd")
    dw_kv_a = _mm(ckvn, dkv, ta=True, outs=(BF16,), tn=512, name="kv_up_wgrad")

    def lora_norm_bwd(cq, ckv, dq_, dkv_, a, b):
        d1, dga_ = _rms_bwd(cq.astype(F32), a, dq_)
        d2, dgb_ = _rms_bwd(ckv.astype(F32), b, dkv_)
        return d1, d2, dga_, dgb_

    dcq, dckv, dgq, dgkv = _rowwise(lora_norm_bwd, [(z, 256, P_CQ // 256), (z, 128, P_CKV // 128), dcqn, dckvn],
                                    [gq, gkv], [(Q_LORA, BF16), (KV_LORA, BF16)], [Q_LORA, KV_LORA],
                                    name="lora_norm_bwd")
    dz = jnp.concatenate([dga, dgb, dq_fox, dk_fox, dv_fox, dcq, dckv, d_small], -1)
    dw_in_p, db_in_p = _mm(h, dz, ta=True, outs=(BF16,), tk=2048, colsum=True, name="in_proj_wgrad")
    grads_a = dict(w_in=_unperm_in_to_shards(dw_in_p), w_uq=_unaug_uq_cols(dw_uq_a),
                   w_uk=_unaug_uk_cols(dw_kv_a[:, :1024]), w_uv=dw_kv_a[:, 1024:])
    dh, pieces_a = _mm(dz, w_in_p, tb=True, outs=(F32,), tm=1024, tk=2048, name="in_proj_bwd",
                       exchange=_pack_full(GROUP_A, grads_a, BF16))

    def pre_mix_bwd(xv, dhv, dx1v, g):
        d, dg = _rms_bwd(xv, g, dhv)
        return dx1v + d, dg

    grad_x, dg1 = _rowwise(pre_mix_bwd, [x2, dh, dx1], [g1], [(D_MODEL, F32)], [D_MODEL], name="pre_mix_bwd")

    grad_sh, delta_sh, newm_sh, newv_sh = {}, {}, {}, {}
    for group, pieces, tag in ((GROUP_A, pieces_a, "a"), (GROUP_B, pieces_b, "b")):
        packed = _sum_adamw(pieces, _pack_shards(group, w_sh, F32), _pack_shards(group, m_sh, F32),
                            _pack_shards(group, v_sh, F32), name="sum_pieces_adamw_" + tag)
        for dst, arr in zip((grad_sh, delta_sh, newm_sh, newv_sh), packed):
            dst.update(_unpack_shards(group, arr))

    small_part = _pack_small(dict(ln_pre_mix=dg1, ln_post_mix=dg2, ln_pre_mlp=dg3, ln_post_mlp=dg4,
                                  b_in=_unperm_in_cols(db_in_p), q_a_norm=dgq, kv_a_norm=dgkv), extra=loss_cols)
    sg, sd, sm, sv, loss_blk = _small_allreduce_adamw(small_part, _pack_small(small_w), _pack_small(small_m),
                                                      _pack_small(small_v))
    grad_sm, delta_sm, newm_sm, newv_sm = (_unpack_small(t) for t in (sg, sd, sm, sv))
    loss = loss_blk[0, 0]

    order = ["ln_pre_mix", "ln_post_mix", "ln_pre_mlp", "ln_post_mlp", "w_in", "b_in", "q_a_norm", "w_uq",
             "kv_a_norm", "w_uk", "w_uv", "w_o_mla", "w_o_fox", "w_out", "w_ff1", "w_ff2"]

    def pick(sm_d, sh_d):
        return [sm_d[n] if n in sm_d else sh_d[n] for n in order]

    return (loss, grad_x.reshape(1, T, D_MODEL), *pick(grad_sm, grad_sh), *pick(delta_sm, delta_sh),
            *pick(newm_sm, newm_sh), *pick(newv_sm, newv_sh))
```

```python
import numpy as np
import jax
import jax.numpy as jnp
from jax import lax
from jax.experimental import pallas as pl
from jax.experimental.pallas import tpu as pltpu

F32 = jnp.float32
BF16 = jnp.bfloat16
MESH = pl.DeviceIdType.MESH

D_MODEL = 1024
N_HEADS = 8
Q_LORA = 256
KV_LORA = 128
NOPE = 64
ROPE = 32
HEAD_V = 64
FOX_D = 64
D_FF = 4096
D_IN = 4008
D_IN_PAD = 4096
ROPE_THETA = 10000.0
NORM_EPS = 1e-6
N_DEV = 8

ADAM_LR = 0.001
ADAM_B1 = 0.9
ADAM_B2 = 0.999
ADAM_EPS = 1e-08
ADAM_WD = 0.01
ADAM_STEP = 10

LANES = 128
ROW_TILE = 512
ATTN_TILE = 512
ATTN_CHUNK = 256
BWD_UNROLL = 2
VMEM_LIMIT = 48 * 1024 * 1024
ATTN_BWD_VMEM_LIMIT = 58 * 1024 * 1024

P_GA, P_GB, P_FQ, P_FK, P_FV, P_CQ, P_CKV, P_SMALL = 0, 1024, 2048, 2560, 3072, 3584, 3840, 3968
S_FL = 32

SHARDED = (
    ("w_in", 1024, 501, True), ("w_uq", 256, 96, True), ("w_uk", 128, 64, True), ("w_uv", 128, 64, True),
    ("w_o_mla", 512, 128, True), ("w_o_fox", 512, 128, True), ("w_out", 128, 1024, False),
    ("w_ff1", 1024, 512, True), ("w_ff2", 512, 1024, False),
)
GROUP_A = SHARDED[:4]
GROUP_B = SHARDED[4:]
SMALL_ROWS = 80
SMALL_LOSS_ROW = 72


def _pack_rows(group):
    return -(-sum(r * _packed_cols(c, by_col) for _, r, c, by_col in group) // (LANES * 64)) * 64


def _cparams(sem=None):
    return pltpu.CompilerParams(dimension_semantics=sem, vmem_limit_bytes=VMEM_LIMIT)


def _mm(a, b, *, name, ta=False, tb=False, bias=None, colscale=None, extras=(), epi=None, outs=(BF16,), t4=None,
        tm=ROW_TILE, tn=1024, tk=1024, exchange=None, colsum=False):
    t4 = (False,) * len(outs) if t4 is None else t4
    comm = exchange is not None
    if ta:
        K, M = a.shape
        tm = min(1024, M)
    else:
        M, K = a.shape
        tm = min(tm, M)
    tk = min(tk, K)
    N = b.shape[0] if tb else b.shape[1]
    tn = min(tn, N)
    nk = K // tk
    assert not colsum or (ta and M == tm)
    n_ex = len(extras)
    has_bias = bias is not None
    has_scale = colscale is not None

    def body(*refs):
        a_ref, b_ref = refs[0], refs[1]
        pos = 2
        bias_ref = scale_ref = None
        if has_bias:
            bias_ref = refs[pos]
            pos += 1
        if has_scale:
            scale_ref = refs[pos]
            pos += 1
        ex_refs = refs[pos:pos + n_ex]
        pos += n_ex
        if comm:
            c_src = refs[pos]
            pos += 1
        o_refs = refs[pos:pos + len(outs)]
        pos += len(outs)
        if colsum:
            cs_ref = refs[pos]
            pos += 1
        if comm:
            c_dst = refs[pos]
            pos += 1
            c_sems = refs[len(refs) - 3:]
            ids = [pl.program_id(d) for d in range(3)]

            @pl.when((ids[0] == 0) & (ids[1] == 0) & (ids[2] == 0))
            def _():
                _comm_start("exchange", c_src, c_dst, *c_sems)

        av = a_ref[...].astype(BF16)
        bv = b_ref[...].astype(BF16)
        if ta:
            part = lax.dot_general(av, bv, (((0,), (0,)), ((), ())), preferred_element_type=F32)
        elif tb:
            part = lax.dot_general(av, bv, (((1,), (1,)), ((), ())), preferred_element_type=F32)
        else:
            part = jnp.dot(av, bv, preferred_element_type=F32)
        if colsum:
            cs = jnp.sum(bv.astype(F32), axis=0, keepdims=True)

            @pl.when(pl.program_id(2) == 0)
            def _():
                cs_ref[...] = cs

            @pl.when(pl.program_id(2) > 0)
            def _():
                cs_ref[...] += cs

        def finish(acc):
            if has_bias:
                acc = acc + bias_ref[...]
            if has_scale:
                acc = acc * scale_ref[...]
            res = (acc,) if epi is None else epi(acc, *[r[...] for r in ex_refs])
            for o_ref, val, t in zip(o_refs, res, t4):
                if t:
                    _t4_store(o_ref, val)
                else:
                    o_ref[...] = val.astype(o_ref.dtype)

        if nk == 1:
            finish(part)
        else:
            acc_ref = refs[pos]
            k = pl.program_id(2)

            @pl.when(k == 0)
            def _():
                acc_ref[...] = part

            @pl.when(k > 0)
            def _():
                acc_ref[...] += part

            @pl.when(k == nk - 1)
            def _():
                finish(acc_ref[...])

        if comm:
            @pl.when((ids[0] == M // tm - 1) & (ids[1] == N // tn - 1) & (ids[2] == nk - 1))
            def _():
                _comm_wait("exchange", c_src, c_dst, *c_sems)

    if ta:
        a_spec = pl.BlockSpec((tk, tm), lambda i, j, k: (k, i))
    else:
        a_spec = pl.BlockSpec((tm, tk), lambda i, j, k: (i, k))
    b_spec = pl.BlockSpec((tn, tk), lambda i, j, k: (j, k)) if tb else pl.BlockSpec((tk, tn), lambda i, j, k: (k, j))
    in_specs = [a_spec, b_spec]
    args = [a, b]
    for row in (bias, colscale):
        if row is not None:
            in_specs.append(pl.BlockSpec((1, tn), lambda i, j, k: (0, j)))
            args.append(row)
    for e in extras:
        in_specs.append(pl.BlockSpec((tm, tn), lambda i, j, k: (i, j)))
        args.append(e)
    out_specs = [pl.BlockSpec((N // LANES, 1, LANES, tm), lambda i, j, k: (0, i, 0, 0)) if t
                 else pl.BlockSpec((tm, tn), lambda i, j, k: (i, j)) for t in t4]
    out_shape = [jax.ShapeDtypeStruct((N // LANES, M // tm, LANES, tm) if t else (M, N), dt)
                 for dt, t in zip(outs, t4)]
    scratch = [pltpu.VMEM((tm, tn), F32)] if nk > 1 else []
    if colsum:
        out_specs.append(pl.BlockSpec((1, tn), lambda i, j, k: (0, j)))
        out_shape.append(jax.ShapeDtypeStruct((1, N), F32))
    if comm:
        in_specs.append(pl.BlockSpec(memory_space=pl.ANY))
        args.append(exchange)
        out_specs.append(pl.BlockSpec(memory_space=pl.ANY))
        out_shape.append(jax.ShapeDtypeStruct(exchange.shape, exchange.dtype))
        scratch += _comm_sems()
    res = pl.pallas_call(
        body, name=name, grid=(M // tm, N // tn, nk),
        in_specs=in_specs, out_specs=out_specs, out_shape=out_shape, scratch_shapes=scratch,
        compiler_params=_cparams(("arbitrary",) * 3 if comm else ("parallel", "parallel", "arbitrary")),
    )(*args)
    return res[0] if len(res) == 1 else res


def _t4_store(o_ref, val):
    for c in range(o_ref.shape[0]):
        o_ref[c, 0] = val[:, c * LANES:(c + 1) * LANES].astype(F32).T.astype(o_ref.dtype)


def _rowwise(fn, rows, bcasts, outs, accs=(), *, name, tm=ROW_TILE):
    t4_in = [isinstance(r, tuple) and isinstance(r[0], str) for r in rows]
    T = [r[1].shape[1] * r[1].shape[3] if t else (r[0] if isinstance(r, tuple) else r).shape[0]
         for r, t in zip(rows, t4_in)][0]
    tm = min(tm, T)
    arrs, specs = [], []
    for r, t in zip(rows, t4_in):
        if t:
            arr = r[1]
            specs.append(pl.BlockSpec((arr.shape[0], 1, LANES, tm), lambda i: (0, i, 0, 0)))
        elif isinstance(r, tuple):
            arr, w, cb = r
            specs.append(pl.BlockSpec((tm, w), lambda i, cb=cb: (i, cb)))
        else:
            arr = r
            specs.append(pl.BlockSpec((tm, arr.shape[1]), lambda i: (i, 0)))
        arrs.append(arr)
    n_rows = len(arrs)
    for b in bcasts:
        arrs.append(b)
        specs.append(pl.BlockSpec(b.shape, lambda i: (0, 0)))
    n_in, n_out = len(arrs), len(outs)
    t4_out = [len(o) == 3 for o in outs]

    def body(*refs):
        vals = []
        for k, r in enumerate(refs[:n_in]):
            if k < n_rows and t4_in[k]:
                vals.append(jnp.concatenate([r[c, 0].astype(F32).T for c in range(r.shape[0])], axis=1))
            else:
                vals.append(r[...])
        res = fn(*vals)
        if not isinstance(res, (tuple, list)):
            res = (res,)
        for o_ref, val, t in zip(refs[n_in:n_in + n_out], res[:n_out], t4_out):
            if t:
                _t4_store(o_ref, val)
            else:
                o_ref[...] = val.astype(o_ref.dtype)
        i = pl.program_id(0)
        for a_ref, val in zip(refs[n_in + n_out:], res[n_out:]):
            col = jnp.sum(val.astype(F32), axis=0, keepdims=True)

            @pl.when(i == 0)
            def _(a_ref=a_ref, col=col):
                a_ref[...] = col

            @pl.when(i > 0)
            def _(a_ref=a_ref, col=col):
                a_ref[...] += col

    res = pl.pallas_call(
        body, name=name, grid=(T // tm,),
        in_specs=specs,
        out_specs=[pl.BlockSpec((o[0] // LANES, 1, LANES, tm), lambda i: (0, i, 0, 0)) if t
                   else pl.BlockSpec((tm, o[0]), lambda i: (i, 0)) for o, t in zip(outs, t4_out)]
        + [pl.BlockSpec((1, c), lambda i: (0, 0)) for c in accs],
        out_shape=[jax.ShapeDtypeStruct((o[0] // LANES, T // tm, LANES, tm) if t else (T, o[0]), o[1])
                   for o, t in zip(outs, t4_out)]
        + [jax.ShapeDtypeStruct((1, c), F32) for c in accs],
        compiler_params=_cparams(("arbitrary",)),
    )(*arrs)
    return res


def _dot3(x, sel, left):
    hi = x.astype(BF16)
    r1 = x - hi.astype(F32)
    mid = r1.astype(BF16)
    lo = (r1 - mid.astype(F32)).astype(BF16)
    if left:
        d = lambda t: jnp.dot(sel, t, preferred_element_type=F32)
    else:
        d = lambda t: jnp.dot(t, sel, preferred_element_type=F32)
    return d(hi) + d(mid) + d(lo)


def _seq_cumsum(x, fn, *, reverse, name, tm=256):
    T, C = x.shape
    tm = min(tm, T)
    n = T // tm

    def body(x_ref, o_ref, carry_ref):
        i = pl.program_id(0)

        @pl.when(i == 0)
        def _():
            carry_ref[...] = jnp.zeros_like(carry_ref)

        v = fn(x_ref[...])
        r = lax.broadcasted_iota(jnp.int32, (tm, tm), 0)
        c = lax.broadcasted_iota(jnp.int32, (tm, tm), 1)
        tri = jnp.where((r <= c) if reverse else (r >= c), 1.0, 0.0).astype(BF16)
        carry = carry_ref[0:1, :]
        o_ref[...] = _dot3(v, tri, left=True) + carry
        carry_ref[0:1, :] = carry + jnp.sum(v, axis=0, keepdims=True)

    idx = (lambda i: (n - 1 - i, 0)) if reverse else (lambda i: (i, 0))
    return pl.pallas_call(
        body, name=name, grid=(n,),
        in_specs=[pl.BlockSpec((tm, C), idx)],
        out_specs=pl.BlockSpec((tm, C), idx),
        out_shape=jax.ShapeDtypeStruct((T, C), F32),
        scratch_shapes=[pltpu.VMEM((8, C), F32)],
        compiler_params=_cparams(("arbitrary",)),
    )(x)


_NT = (((1,), (1,)), ((), ()))
LOG2E = 1.4426950408889634


def _head_mask(width, e):
    lane = lax.broadcasted_iota(jnp.int32, (1, width), 1)
    return (lane >= 64 * e) & (lane < 64 * (e + 1))


def _attn_fwd(qt4, k_aug, vt4, *, name, tq, gather=None):
    nq = qt4.shape[1]
    T = nq * tq
    cw = min(ATTN_CHUNK, tq)
    comm = gather is not None

    def body(*refs):
        refs = list(refs)
        q_ref, k_ref, v_ref = refs[:3]
        pos = 3
        if comm:
            c_src = refs[pos]
            pos += 1
        o_ref, lse_ref = refs[pos:pos + 2]
        pos += 2
        if comm:
            c_dst = refs[pos]
            pos += 1
        m_s, l_s, acc_s = refs[pos:pos + 2], refs[pos + 2:pos + 4], refs[pos + 4:pos + 6]
        st_a, st_b = refs[pos + 6:pos + 8]
        c_sems = refs[pos + 8:]
        i = pl.program_id(1)
        if comm:
            @pl.when((pl.program_id(0) == 0) & (i == 0))
            def _():
                _comm_start("gather", c_src, c_dst, *c_sems)

        for e in range(2):
            m_s[e][...] = jnp.full(m_s[e].shape, -jnp.inf, F32)
            l_s[e][...] = jnp.zeros_like(l_s[e])
            acc_s[e][...] = jnp.zeros_like(acc_s[e])
        qt = [q_ref[e, 0] for e in range(2)]

        work = [(e, slice(c * cw, (c + 1) * cw)) for e in range(2) for c in range(tq // cw)]

        def scores(j, buf):
            kj = k_ref[pl.ds(pl.multiple_of(j * tq, tq), tq), :]
            for n, (e, cs) in enumerate(work):
                buf[n] = jnp.dot(kj[:, e * LANES:(e + 1) * LANES], qt[e][:, cs], preferred_element_type=F32)

        def step(j, buf, masked):
            vtj = v_ref[0, j]
            for n, (e, cs) in enumerate(work):
                kr = cs.stop if masked else tq
                st = buf[n][:kr]
                if masked:
                    r = lax.broadcasted_iota(jnp.int32, (kr, cw), 0)
                    cc = lax.broadcasted_iota(jnp.int32, (kr, cw), 1) + cs.start
                    st = jnp.where(cc >= r, st, -jnp.inf)
                m_prev = m_s[e][:, cs]
                m_new = jnp.maximum(m_prev, jnp.max(st, axis=0, keepdims=True))
                alpha = jnp.exp2(m_prev - m_new)
                pt = jnp.exp2(st - m_new)
                l_s[e][:, cs] = alpha * l_s[e][:, cs] + jnp.sum(pt, axis=0, keepdims=True)
                acc_s[e][:, cs] = alpha * acc_s[e][:, cs] + jnp.dot(vtj[e * HEAD_V:(e + 1) * HEAD_V, :kr],
                                                                    pt.astype(BF16), preferred_element_type=F32)
                m_s[e][:, cs] = m_new

        def two_tiles(p, carry):
            scores(2 * p + 1, st_b)
            step(2 * p, st_a, False)
            scores(2 * p + 2, st_a)
            step(2 * p + 1, st_b, False)
            return carry

        scores(0, st_a)
        lax.fori_loop(0, i // 2, two_tiles, 0)

        @pl.when(i % 2 == 0)
        def _():
            step(i, st_a, True)

        @pl.when(i % 2 == 1)
        def _():
            scores(i, st_b)
            step(i - 1, st_a, False)
            step(i, st_b, True)

        ot = jnp.concatenate([acc_s[e][...] / l_s[e][...] for e in range(2)], axis=0)
        o_ref[...] = ot.T.astype(o_ref.dtype)
        for e in range(2):
            lse_ref[e, 0] = m_s[e][...] + jnp.log(l_s[e][...]) * LOG2E
        if comm:
            @pl.when((pl.program_id(0) == N_HEADS // 2 - 1) & (i == nq - 1))
            def _():
                _comm_wait("gather", c_src, c_dst, *c_sems)

    in_specs = [
        pl.BlockSpec((2, 1, LANES, tq), lambda hp, i: (hp, i, 0, 0)),
        pl.BlockSpec((T, 2 * LANES), lambda hp, i: (0, hp)),
        pl.BlockSpec((1, nq, LANES, tq), lambda hp, i: (hp, 0, 0, 0)),
    ]
    args = [qt4, k_aug, vt4]
    out_specs = [pl.BlockSpec((tq, LANES), lambda hp, i: (i, hp)),
                 pl.BlockSpec((2, 1, 1, tq), lambda hp, i: (hp, i, 0, 0))]
    out_shape = [jax.ShapeDtypeStruct((T, N_HEADS * HEAD_V), BF16), jax.ShapeDtypeStruct((N_HEADS, nq, 1, tq), F32)]
    scratch = ([pltpu.VMEM((1, tq), F32)] * 4 + [pltpu.VMEM((HEAD_V, tq), F32)] * 2
               + [pltpu.VMEM((2 * tq // cw, tq, cw), F32)] * 2)
    if comm:
        in_specs.append(pl.BlockSpec(memory_space=pl.ANY))
        args.append(gather)
        out_specs.append(pl.BlockSpec(memory_space=pl.ANY))
        out_shape.append(jax.ShapeDtypeStruct((N_DEV,) + gather.shape, gather.dtype))
        scratch += _comm_sems()
    return pl.pallas_call(
        body, name=name, grid=(N_HEADS // 2, nq),
        in_specs=in_specs, out_specs=out_specs, out_shape=out_shape, scratch_shapes=scratch,
        compiler_params=_cparams(("arbitrary", "arbitrary")),
    )(*args)


def _attn_bwd(qt4, k_aug, v, vcb, dot4, lse_row, dl_row, scale, own_rows, *, name, tq, exchange=None):
    nq = qt4.shape[1]
    T = nq * tq
    cw = min(ATTN_CHUNK, tq)
    comm = exchange is not None

    def body(*refs):
        refs = list(refs)
        k_ref, v_ref, q_ref, do_ref, lse_ref, dl_ref = refs[:6]
        pos = 6
        if comm:
            c_src = refs[pos]
            pos += 1
        dq_ref, dk_ref, dv_ref = refs[pos:pos + 3]
        pos += 3
        if comm:
            c_dst = refs[pos]
            pos += 1
        dk_s, dv_s = refs[pos:pos + 2], refs[pos + 2:pos + 4]
        c_sems = refs[pos + 4:]
        j = pl.program_id(1)
        if comm:
            @pl.when((pl.program_id(0) == 0) & (j == 0))
            def _():
                _comm_start("exchange", c_src, c_dst, *c_sems)

        @pl.when(j == 0)
        def _():
            dq_ref[...] = jnp.zeros_like(dq_ref)

        kj = k_ref[...]
        vj = v_ref[...]
        ka = [kj[:, e * LANES:(e + 1) * LANES] for e in range(2)]
        row = lax.broadcasted_iota(jnp.int32, (LANES, 1), 0)
        own = row < own_rows
        kat = [(ka[e].astype(F32).T * jnp.where(own, scale, 1.0)).astype(BF16) for e in range(2)]
        vm = [jnp.where(_head_mask(LANES, e), vj, jnp.zeros_like(vj)) for e in range(2)]
        for e in range(2):
            dk_s[e][...] = jnp.zeros_like(dk_s[e])
            dv_s[e][...] = jnp.zeros_like(dv_s[e])

        def step(i, masked):
            dot_i = do_ref[0, i]
            qts = [q_ref[e, i] for e in range(2)]
            lse = [lse_ref[e, i] for e in range(2)]
            dl = [dl_ref[e, i] for e in range(2)]
            work = [(e, slice(c * cw, (c + 1) * cw)) for e in range(2) for c in range(tq // cw)]
            rows = lambda cs: cs.stop if masked else tq
            scores = lambda e, cs: (jnp.dot(ka[e][:rows(cs)], qts[e][:, cs], preferred_element_type=F32),
                                    jnp.dot(vm[e][:rows(cs)], dot_i[:, cs], preferred_element_type=F32))
            nxt = scores(*work[0])
            dqs = [[], []]
            for n, (e, cs) in enumerate(work):
                st, dpt = nxt
                kr = rows(cs)
                if n + 1 < len(work):
                    nxt = scores(*work[n + 1])
                if masked:
                    r = lax.broadcasted_iota(jnp.int32, (kr, cw), 0)
                    cc = lax.broadcasted_iota(jnp.int32, (kr, cw), 1) + cs.start
                    st = jnp.where(cc >= r, st, -jnp.inf)
                pt = jnp.exp2(st - lse[e][:, cs])
                dv_s[e][:, :kr] += lax.dot_general(dot_i[e * HEAD_V:(e + 1) * HEAD_V, cs], pt.astype(BF16), _NT,
                                                   preferred_element_type=F32)
                dsb = (pt * (dpt - dl[e][:, cs])).astype(BF16)
                dk_s[e][:, :kr] += lax.dot_general(qts[e][:, cs], dsb, _NT, preferred_element_type=F32)
                dqs[e].append(jnp.dot(kat[e][:, :kr], dsb, preferred_element_type=F32))
            for e in range(2):
                dq_ref[e, i] += jnp.concatenate(dqs[e], axis=1)

        def some_tiles(p, carry):
            for t in range(BWD_UNROLL):
                step(j + 1 + BWD_UNROLL * p + t, False)
            return carry

        def one_tile(i, carry):
            step(i, False)
            return carry

        trips = (nq - 1 - j) // BWD_UNROLL
        step(j, True)
        lax.fori_loop(0, trips, some_tiles, 0)
        lax.fori_loop(j + 1 + BWD_UNROLL * trips, nq, one_tile, 0)

        for e in range(2):
            dk_ref[e, 0] = dk_s[e][...] * jnp.where(own, 1.0 / LOG2E, 1.0)
        dv_ref[...] = jnp.concatenate([dv_s[0][...], dv_s[1][...]], axis=0).T.astype(dv_ref.dtype)
        if comm:
            @pl.when((pl.program_id(0) == N_HEADS // 2 - 1) & (j == nq - 1))
            def _():
                _comm_wait("exchange", c_src, c_dst, *c_sems)

    row4 = pl.BlockSpec((2, nq, 1, tq), lambda hp, j: (hp, 0, 0, 0))
    in_specs = [
        pl.BlockSpec((tq, 2 * LANES), lambda hp, j: (j, hp)),
        pl.BlockSpec((tq, LANES), lambda hp, j: (j, vcb + hp)),
        pl.BlockSpec((2, nq, LANES, tq), lambda hp, j: (hp, 0, 0, 0)),
        pl.BlockSpec((1, nq, LANES, tq), lambda hp, j: (hp, 0, 0, 0)),
        row4, row4,
    ]
    args = [k_aug, v, qt4, dot4, lse_row, dl_row]
    out_specs = [pl.BlockSpec((2, nq, LANES, tq), lambda hp, j: (hp, 0, 0, 0)),
                 pl.BlockSpec((2, 1, LANES, tq), lambda hp, j: (hp, j, 0, 0)),
                 pl.BlockSpec((tq, LANES), lambda hp, j: (j, hp))]
    out_shape = [jax.ShapeDtypeStruct((N_HEADS, nq, LANES, tq), F32), jax.ShapeDtypeStruct((N_HEADS, nq, LANES, tq), F32),
                 jax.ShapeDtypeStruct((T, N_HEADS * HEAD_V), BF16)]
    scratch = [pltpu.VMEM((LANES, tq), F32)] * 2 + [pltpu.VMEM((HEAD_V, tq), F32)] * 2
    if comm:
        in_specs.append(pl.BlockSpec(memory_space=pl.ANY))
        args.append(exchange)
        out_specs.append(pl.BlockSpec(memory_space=pl.ANY))
        out_shape.append(jax.ShapeDtypeStruct(exchange.shape, exchange.dtype))
        scratch += _comm_sems()
    return pl.pallas_call(
        body, name=name, grid=(N_HEADS // 2, nq),
        in_specs=in_specs, out_specs=out_specs, out_shape=out_shape, scratch_shapes=scratch,
        compiler_params=pltpu.CompilerParams(dimension_semantics=("arbitrary", "arbitrary"),
                                             vmem_limit_bytes=ATTN_BWD_VMEM_LIMIT),
    )(*args)


def _peers():
    x, y, c = lax.axis_index("x"), lax.axis_index("y"), lax.axis_index("c")
    me = 4 * x + 2 * y + c
    out = []
    for k in range(1, N_DEV):
        px = (1 - x) if (k & 4) else x
        py = (1 - y) if (k & 2) else y
        pc = (1 - c) if (k & 1) else c
        out.append(((px, py, pc), 4 * px + 2 * py + pc))
    return me, out


def _comm_sems():
    return [pltpu.SemaphoreType.DMA((N_DEV - 1,)), pltpu.SemaphoreType.DMA((N_DEV - 1,)), pltpu.SemaphoreType.DMA]


def _comm_copies(kind, src_ref, dst_ref, send_sems, recv_sems, local_sem):
    me, peers = _peers()
    local = pltpu.make_async_copy(src_ref if kind == "gather" else src_ref.at[me], dst_ref.at[me], local_sem)
    sends, recvs = [], []
    for k, (dev, lin) in enumerate(peers):
        src = src_ref if kind == "gather" else src_ref.at[lin]
        sends.append(pltpu.make_async_remote_copy(src_ref=src, dst_ref=dst_ref.at[me], send_sem=send_sems.at[k],
                                                  recv_sem=recv_sems.at[k], device_id=dev, device_id_type=MESH))
        recvs.append(pltpu.make_async_remote_copy(src_ref=src, dst_ref=dst_ref.at[lin], send_sem=send_sems.at[k],
                                                  recv_sem=recv_sems.at[k], device_id=dev, device_id_type=MESH))
    return local, sends, recvs


def _comm_start(kind, src_ref, dst_ref, send_sems, recv_sems, local_sem):
    local, sends, _ = _comm_copies(kind, src_ref, dst_ref, send_sems, recv_sems, local_sem)
    local.start()
    for cp in sends:
        cp.start()


def _comm_wait(kind, src_ref, dst_ref, send_sems, recv_sems, local_sem):
    local, sends, recvs = _comm_copies(kind, src_ref, dst_ref, send_sems, recv_sems, local_sem)
    for cp in recvs:
        cp.wait_recv()
    for cp in sends:
        cp.wait_send()
    local.wait()


def _gather_via_sibling(src, name):
    def body(x_ref, o_ref, send_sems, recv_sems, local_sem):
        x, y, c = lax.axis_index("x"), lax.axis_index("y"), lax.axis_index("c")
        me, sibling = (x, y, c), (x, y, 1 - c)
        chips = [(1 - x, y), (x, 1 - y), (1 - x, 1 - y)]

        def slot(px, py, pc):
            return o_ref.at[4 * px + 2 * py + pc]

        def copy(k, block, to, src_ref=None):
            return pltpu.make_async_remote_copy(src_ref=slot(*block) if src_ref is None else src_ref,
                                                dst_ref=slot(*block), send_sem=send_sems.at[k],
                                                recv_sem=recv_sems.at[k], device_id=to, device_id_type=MESH)

        mine = pltpu.make_async_copy(x_ref, slot(*me), local_sem)
        mine.start()
        first = [copy(0, me, sibling, x_ref)] + [copy(1 + n, me, (*chip, c), x_ref) for n, chip in enumerate(chips)]
        for cp in first:
            cp.start()
        passed = [copy(4 + n, (*chip, c), sibling) for n, chip in enumerate(chips)]
        for n, chip in enumerate(chips):
            copy(1 + n, (*chip, c), me).wait_recv()
            passed[n].start()
        copy(0, sibling, me).wait_recv()
        for n, chip in enumerate(chips):
            copy(4 + n, (*chip, 1 - c), me).wait_recv()
        for cp in first + passed:
            cp.wait_send()
        mine.wait()

    return pl.pallas_call(
        body, name=name,
        in_specs=[pl.BlockSpec(memory_space=pl.ANY)],
        out_specs=pl.BlockSpec(memory_space=pl.ANY),
        out_shape=jax.ShapeDtypeStruct((N_DEV,) + src.shape, src.dtype),
        scratch_shapes=_comm_sems(),
    )(src)


def _adamw(w, g, m, v):
    m2 = ADAM_B1 * m + (1.0 - ADAM_B1) * g
    v2 = ADAM_B2 * v + (1.0 - ADAM_B2) * (g * g)
    m_hat = m2 / (1.0 - ADAM_B1 ** ADAM_STEP)
    v_hat = v2 / (1.0 - ADAM_B2 ** ADAM_STEP)
    delta = -ADAM_LR * (m_hat / (jnp.sqrt(v_hat) + ADAM_EPS) + ADAM_WD * w)
    return delta, m2, v2


def _sum_adamw(pieces, w, m, v, *, name):
    R = w.shape[0]
    tr = R // 4 if R % 64 == 0 else R

    def body(p_ref, w_ref, m_ref, v_ref, g_ref, d_ref, m2_ref, v2_ref):
        g = p_ref[0].astype(F32)
        for s in range(1, N_DEV):
            g = g + p_ref[s].astype(F32)
        delta, m2, v2 = _adamw(w_ref[...], g, m_ref[...], v_ref[...])
        g_ref[...] = g
        d_ref[...] = delta
        m2_ref[...] = m2
        v2_ref[...] = v2

    row = pl.BlockSpec((tr, LANES), lambda i: (i, 0))
    return pl.pallas_call(
        body, name=name, grid=(R // tr,),
        in_specs=[pl.BlockSpec((N_DEV, tr, LANES), lambda i: (0, i, 0)), row, row, row],
        out_specs=[row, row, row, row],
        out_shape=[jax.ShapeDtypeStruct((R, LANES), F32)] * 4,
        compiler_params=_cparams(("parallel",)),
    )(pieces, w, m, v)


def _small_allreduce_adamw(part, w, m, v):
    shape = part.shape

    def body(p_ref, w_ref, m_ref, v_ref, g_ref, d_ref, m2_ref, v2_ref, loss_ref, gath, send_sems, recv_sems):
        me, peers = _peers()
        gath[me] = p_ref[...]
        sends = []
        for k, (dev, _) in enumerate(peers):
            cp = pltpu.make_async_remote_copy(src_ref=p_ref, dst_ref=gath.at[me], send_sem=send_sems.at[k],
                                              recv_sem=recv_sems.at[k], device_id=dev, device_id_type=MESH)
            cp.start()
            sends.append(cp)
        for k, (dev, lin) in enumerate(peers):
            pltpu.make_async_remote_copy(src_ref=p_ref, dst_ref=gath.at[lin], send_sem=send_sems.at[k],
                                         recv_sem=recv_sems.at[k], device_id=dev, device_id_type=MESH).wait_recv()
        for cp in sends:
            cp.wait_send()
        g = gath[0]
        for s in range(1, N_DEV):
            g = g + gath[s]
        delta, m2, v2 = _adamw(w_ref[...], g, m_ref[...], v_ref[...])
        g_ref[...] = g
        d_ref[...] = delta
        m2_ref[...] = m2
        v2_ref[...] = v2
        sq = jnp.sum(g[SMALL_LOSS_ROW:SMALL_LOSS_ROW + 8, :], axis=1, keepdims=True)
        tot = jnp.sum(sq, axis=0, keepdims=True) * (0.5 / D_MODEL)
        loss_ref[...] = jnp.broadcast_to(tot, loss_ref.shape)

    vm = pl.BlockSpec(memory_space=pltpu.VMEM)
    return pl.pallas_call(
        body, name="small_allreduce_adamw",
        in_specs=[vm, vm, vm, vm],
        out_specs=[vm, vm, vm, vm, vm],
        out_shape=[jax.ShapeDtypeStruct(shape, F32)] * 4 + [jax.ShapeDtypeStruct((8, LANES), F32)],
        scratch_shapes=[pltpu.VMEM((N_DEV,) + shape, F32),
                        pltpu.SemaphoreType.DMA((N_DEV - 1,)), pltpu.SemaphoreType.DMA((N_DEV - 1,))],
    )(part, w, m, v)


def _perm_in_cols(w):
    z = lambda n: jnp.zeros(w.shape[:-1] + (n,), w.dtype)
    small = jnp.concatenate([w[..., 384:400], z(16), w[..., 1952:1960], z(24), w[..., 400:416], z(48)], -1)
    return jnp.concatenate([w[..., 1960:2984], w[..., 2984:4008], w[..., 416:928], w[..., 928:1440],
                            w[..., 1440:1952], w[..., 0:256], w[..., 256:384], small], -1)


def _unperm_in_cols(wp):
    s = wp[..., P_SMALL:]
    return jnp.concatenate([wp[..., P_CQ:P_CQ + 256], wp[..., P_CKV:P_CKV + 128], s[..., 0:16], s[..., 64:80],
                            wp[..., P_FQ:P_FQ + 512], wp[..., P_FK:P_FK + 512], wp[..., P_FV:P_FV + 512],
                            s[..., S_FL:S_FL + 8], wp[..., P_GA:P_GA + 1024], wp[..., P_GB:P_GB + 1024]], -1)


def _aug_uq_cols(w):
    r = w.shape[0]
    w3 = w.reshape(r, N_HEADS, NOPE + ROPE)
    z = jnp.zeros((r, N_HEADS, 32), w.dtype)
    return jnp.concatenate([w3[:, :, 64:80], w3[:, :, 0:48], w3[:, :, 80:96], w3[:, :, 48:64], z], -1).reshape(r, 1024)


def _unaug_uq_cols(wp):
    r = wp.shape[0]
    w3 = wp.reshape(r, N_HEADS, LANES)
    return jnp.concatenate([w3[:, :, 16:64], w3[:, :, 80:96], w3[:, :, 0:16], w3[:, :, 64:80]], -1).reshape(r, 768)


def _aug_uk_cols(w):
    r = w.shape[0]
    w3 = w.reshape(r, N_HEADS, NOPE)
    z = lambda n: jnp.zeros((r, N_HEADS, n), w.dtype)
    return jnp.concatenate([z(16), w3[:, :, 0:48], z(16), w3[:, :, 48:64], z(32)], -1).reshape(r, 1024)


def _unaug_uk_cols(wp):
    r = wp.shape[0]
    w3 = wp.reshape(r, N_HEADS, LANES)
    return jnp.concatenate([w3[:, :, 16:64], w3[:, :, 80:96]], -1).reshape(r, 512)


IN_SEGMENTS = ((0, 256, P_CQ), (256, 128, P_CKV), (384, 16, P_SMALL), (400, 16, P_SMALL + 64), (416, 512, P_FQ),
               (928, 512, P_FK), (1440, 512, P_FV), (1952, 8, P_SMALL + S_FL), (1960, 1024, P_GA), (2984, 1024, P_GB))
IN_SHARD = D_IN // N_DEV


def _perm_in_from_shards(w3):
    r = w3.shape[0]
    parts = []
    pos = 0
    for o0, n, p0 in sorted(IN_SEGMENTS, key=lambda t: t[2]):
        if p0 > pos:
            parts.append(jnp.zeros((r, p0 - pos), w3.dtype))
        a, b = o0, o0 + n
        for d in range(a // IN_SHARD, (b - 1) // IN_SHARD + 1):
            parts.append(w3[:, d, max(a, d * IN_SHARD) - d * IN_SHARD:min(b, (d + 1) * IN_SHARD) - d * IN_SHARD])
        pos = p0 + n
    parts.append(jnp.zeros((r, D_IN_PAD - pos), w3.dtype))
    return jnp.concatenate(parts, -1)


def _unperm_in_to_shards(gp):
    r = gp.shape[0]
    shards = []
    for d in range(N_DEV):
        a, b = d * IN_SHARD, (d + 1) * IN_SHARD
        parts = [gp[:, p0 + max(a, o0) - o0:p0 + min(b, o0 + n) - o0] for o0, n, p0 in IN_SEGMENTS
                 if max(a, o0) < min(b, o0 + n)]
        parts.append(jnp.zeros((r, _lane_pad(IN_SHARD) - IN_SHARD), gp.dtype))
        shards.append(jnp.concatenate(parts, -1))
    return jnp.stack(shards, 0)


def _lane_pad(c):
    return -(-c // LANES) * LANES


def _packed_cols(c, by_col):
    return _lane_pad(c) if by_col else c


def _pack_shards(group, shards, dtype):
    rows = _pack_rows(group)
    segs = []
    for name, r, c, by_col in group:
        s = shards[name].reshape(r, c).astype(dtype)
        segs.append(jnp.pad(s, ((0, 0), (0, _packed_cols(c, by_col) - c))).reshape(-1))
    flat = jnp.concatenate(segs)
    flat = jnp.pad(flat, (0, rows * LANES - flat.shape[0]))
    return flat.reshape(rows, LANES)


def _unpack_shards(group, packed):
    flat = packed.reshape(-1)
    out, off = {}, 0
    for name, r, c, by_col in group:
        cp = _packed_cols(c, by_col)
        out[name] = flat[off:off + r * cp].reshape(1, r, cp)[:, :, :c]
        off += r * cp
    return out


def _unpack_full(group, gathered):
    flat = gathered.reshape(N_DEV, -1)
    out, off = {}, 0
    for name, r, c, by_col in group:
        cp = _packed_cols(c, by_col)
        blk = flat[:, off:off + r * cp].reshape(N_DEV, r, cp)
        out[name] = blk.transpose(1, 0, 2) if by_col else blk.reshape(N_DEV * r, c)
        off += r * cp
    return out


def _full_cols(w3, c):
    return w3[:, :, :c].reshape(w3.shape[0], N_DEV * c)


def _pack_full(group, grads, dtype):
    rows = _pack_rows(group)
    segs = []
    for name, r, c, by_col in group:
        g = grads[name]
        cp = _packed_cols(c, by_col)
        if by_col and g.ndim == 2:
            g = jnp.pad(g.reshape(r, N_DEV, c), ((0, 0), (0, 0), (0, cp - c))).transpose(1, 0, 2)
        segs.append(g.reshape(N_DEV, r * cp).astype(dtype))
    flat = jnp.concatenate(segs, axis=1)
    flat = jnp.pad(flat, ((0, 0), (0, rows * LANES - flat.shape[1])))
    return flat.reshape(N_DEV, rows, LANES)


SMALL_LAYOUT = (("ln_pre_mix", 1024, 0), ("ln_post_mix", 1024, 8), ("ln_pre_mlp", 1024, 16),
                ("ln_post_mlp", 1024, 24), ("b_in", 4008, 32), ("q_a_norm", 256, 64), ("kv_a_norm", 128, 66))


def _pack_small(vals, extra=None):
    rows = []
    for name, n, _ in SMALL_LAYOUT:
        v = vals[name].reshape(-1).astype(F32)
        pad = -n % LANES
        rows.append(jnp.pad(v, (0, pad)).reshape(-1, LANES))
    rows.append(jnp.zeros((SMALL_LOSS_ROW - 67, LANES), F32))
    rows.append(jnp.zeros((8, LANES), F32) if extra is None else extra.reshape(8, LANES))
    return jnp.concatenate(rows, axis=0)


def _unpack_small(packed):
    out = {}
    for name, n, r0 in SMALL_LAYOUT:
        nr = -(-n // LANES)
        out[name] = packed[r0:r0 + nr].reshape(-1)[:n].reshape(1, n)
    return out


def _rms(xf, g):
    r = lax.rsqrt(jnp.mean(xf * xf, axis=-1, keepdims=True) + NORM_EPS)
    return (xf * r) * g


def _rms_bwd(xf, g, dy):
    r = lax.rsqrt(jnp.mean(xf * xf, axis=-1, keepdims=True) + NORM_EPS)
    xhat = xf * r
    dxhat = dy * g
    dx = r * (dxhat - xhat * jnp.mean(dxhat * xhat, axis=-1, keepdims=True))
    return dx, dy * xhat


def _sigmoid(t):
    return 1.0 / (1.0 + jnp.exp(-t))


def _log_sigmoid(t):
    return jnp.minimum(t, 0.0) - jnp.log(1.0 + jnp.exp(-jnp.abs(t)))


def _lane_sign():
    lane = lax.broadcasted_iota(jnp.int32, (1, LANES), 1)
    return jnp.where(lane < 64, -1.0, 1.0).astype(F32), lane


def _rope_lanes(lane):
    return (lane < 16) | ((lane >= 64) & (lane < 80))


def kernel(x, positions, ln_pre_mix, ln_post_mix, ln_pre_mlp, ln_post_mlp, w_in, b_in, q_a_norm, w_uq, kv_a_norm, w_uk, w_uv, w_o_mla, w_o_fox, w_out, w_ff1, w_ff2, loss_target, m_ln_pre_mix, m_ln_post_mix, m_ln_pre_mlp, m_ln_post_mlp, m_w_in, m_b_in, m_q_a_norm, m_w_uq, m_kv_a_norm, m_w_uk, m_w_uv, m_w_o_mla, m_w_o_fox, m_w_out, m_w_ff1, m_w_ff2, v_ln_pre_mix, v_ln_post_mix, v_ln_pre_mlp, v_ln_post_mlp, v_w_in, v_b_in, v_q_a_norm, v_w_uq, v_kv_a_norm, v_w_uk, v_w_uv, v_w_o_mla, v_w_o_fox, v_w_out, v_w_ff1, v_w_ff2):
    T = x.shape[1]
    x2 = x.reshape(T, D_MODEL)
    tgt = loss_target.reshape(T, D_MODEL)
    w_sh = dict(w_in=w_in, w_uq=w_uq, w_uk=w_uk, w_uv=w_uv, w_o_mla=w_o_mla, w_o_fox=w_o_fox, w_out=w_out,
                w_ff1=w_ff1, w_ff2=w_ff2)
    m_sh = dict(w_in=m_w_in, w_uq=m_w_uq, w_uk=m_w_uk, w_uv=m_w_uv, w_o_mla=m_w_o_mla, w_o_fox=m_w_o_fox,
                w_out=m_w_out, w_ff1=m_w_ff1, w_ff2=m_w_ff2)
    v_sh = dict(w_in=v_w_in, w_uq=v_w_uq, w_uk=v_w_uk, w_uv=v_w_uv, w_o_mla=v_w_o_mla, w_o_fox=v_w_o_fox,
                w_out=v_w_out, w_ff1=v_w_ff1, w_ff2=v_w_ff2)
    small_w = dict(ln_pre_mix=ln_pre_mix, ln_post_mix=ln_post_mix, ln_pre_mlp=ln_pre_mlp, ln_post_mlp=ln_post_mlp,
                   b_in=b_in, q_a_norm=q_a_norm, kv_a_norm=kv_a_norm)
    small_m = dict(ln_pre_mix=m_ln_pre_mix, ln_post_mix=m_ln_post_mix, ln_pre_mlp=m_ln_pre_mlp,
                   ln_post_mlp=m_ln_post_mlp, b_in=m_b_in, q_a_norm=m_q_a_norm, kv_a_norm=m_kv_a_norm)
    small_v = dict(ln_pre_mix=v_ln_pre_mix, ln_post_mix=v_ln_post_mix, ln_pre_mlp=v_ln_pre_mlp,
                   ln_post_mlp=v_ln_post_mlp, b_in=v_b_in, q_a_norm=v_q_a_norm, kv_a_norm=v_kv_a_norm)
    mla_scale = float((NOPE + ROPE) ** -0.5)
    fox_scale = float(FOX_D ** -0.5)

    W = _unpack_full(GROUP_A, _gather_via_sibling(_pack_shards(GROUP_A, w_sh, BF16), "allgather_weights_a"))
    w_in_p = _perm_in_from_shards(W["w_in"])
    b_in_p = _perm_in_cols(b_in.astype(F32))
    w_uq_a = _aug_uq_cols(_full_cols(W["w_uq"], 96))
    w_kv_a = jnp.concatenate([_aug_uk_cols(_full_cols(W["w_uk"], 64)), _full_cols(W["w_uv"], 64)], axis=1)
    g1, g2, g3, g4 = ln_pre_mix, ln_post_mix, ln_pre_mlp, ln_post_mlp
    gq, gkv = q_a_norm, kv_a_norm
    tq = min(ATTN_TILE, T)
    nq = T // tq

    (h,) = _rowwise(lambda xv, g: _rms(xv, g), [x2], [g1], [(D_MODEL, BF16)], name="pre_mix_norm")
    q_cols = jnp.ones((1, D_IN_PAD), F32).at[:, P_FQ:P_FQ + 512].set(fox_scale * LOG2E)
    z = _mm(h, w_in_p, bias=b_in_p, colscale=q_cols, tm=2048, name="in_proj")
    zs = _mm(h, w_in_p[:, P_SMALL:], bias=b_in_p[:, P_SMALL:], outs=(F32,), name="in_proj_small")

    def lora_norm(cq, ckv, a, b):
        return _rms(cq.astype(F32), a), _rms(ckv.astype(F32), b)

    cqn, ckvn = _rowwise(lora_norm, [(z, 256, P_CQ // 256), (z, 128, P_CKV // 128)], [gq, gkv],
                         [(Q_LORA, BF16), (KV_LORA, BF16)], name="lora_norm")
    q_x = _mm(cqn, w_uq_a, outs=(F32,), name="q_up")
    kv = _mm(ckvn, w_kv_a, tn=512, name="kv_up")

    half = ROPE // 2
    inv_freq = ROPE_THETA ** (-jnp.arange(half, dtype=F32) / half)
    inv128 = jnp.tile(inv_freq, LANES // half).reshape(1, LANES)
    pos_col = positions.reshape(T, 1).astype(F32)

    def rope_tables(p, f):
        ang = p * f
        return jnp.cos(ang), jnp.sin(ang)

    cos_t, sin_t = _rowwise(rope_tables, [pos_col], [inv128], [(LANES, F32), (LANES, F32)], name="rope_tables")

    def rope_fwd(qx, kn, vn, s, cs, sn):
        sign, lane = _lane_sign()
        rope_l = _rope_lanes(lane)
        rot = lambda t: t * cs + pltpu.roll(t, 64, 1) * sn * sign
        k_rot = jnp.where(rope_l, rot(s), 0.0)
        qs, ks = [], []
        for hd in range(N_HEADS):
            qb = qx[:, LANES * hd:LANES * (hd + 1)]
            qs.append(jnp.where(rope_l, rot(qb), qb))
            ks.append(kn[:, LANES * hd:LANES * (hd + 1)].astype(F32) + k_rot)
        return jnp.concatenate(qs, axis=1) * (mla_scale * LOG2E), jnp.concatenate(ks, axis=1), vn

    qt4_m, k_am, vt4_m = _rowwise(rope_fwd, [q_x, (kv, 1024, 0), (kv, 512, 2), zs, cos_t, sin_t], [],
                                  [(1024, BF16, "t4"), (1024, BF16), (512, BF16, "t4")], name="rope_fwd", tm=tq)
    f_cum = _seq_cumsum(zs, _log_sigmoid, reverse=False, name="forget_cumsum")

    def fox_aug(qf, kf, vf, fc):
        lane = lax.broadcasted_iota(jnp.int32, (1, LANES), 1)
        qs, ks = [], []
        for hd in range(N_HEADS):
            qp = qf[:, LANES * (hd // 2):LANES * (hd // 2 + 1)].astype(F32)
            kp = kf[:, LANES * (hd // 2):LANES * (hd // 2 + 1)].astype(F32)
            if hd % 2:
                qp = pltpu.roll(qp, 64, 1)
                kp = pltpu.roll(kp, 64, 1)
            fb = jnp.broadcast_to(fc[:, S_FL + hd:S_FL + hd + 1] * (-LOG2E), qp.shape)
            hi = fb.astype(BF16).astype(F32)
            mid = (fb - hi).astype(BF16).astype(F32)
            lo = fb - hi - mid
            qs.append(jnp.where(lane < 64, qp, jnp.where(lane < 67, 1.0, 0.0)))
            ks.append(jnp.where(lane < 64, kp, jnp.where(lane == 64, hi, jnp.where(lane == 65, mid, jnp.where(
                lane == 66, lo, jnp.where(lane == 67, 1.0, 0.0))))))
        return jnp.concatenate(qs, axis=1), jnp.concatenate(ks, axis=1), vf

    qt4_f, k_af, vt4_f = _rowwise(fox_aug, [(z, 512, P_FQ // 512), (z, 512, P_FK // 512), (z, 512, P_FV // 512), f_cum],
                                  [], [(1024, BF16, "t4"), (1024, BF16), (512, BF16, "t4")], name="fox_aug", tm=tq)
    o_mla, lse_mla, gathered_b = _attn_fwd(qt4_m, k_am, vt4_m, name="mla_attn_fwd", tq=tq,
                                           gather=_pack_shards(GROUP_B, w_sh, BF16))
    W.update(_unpack_full(GROUP_B, gathered_b))
    for n, _, c, by_col in GROUP_B:
        if by_col:
            W[n] = _full_cols(W[n], c)
    o_fox, lse_fox = _attn_fwd(qt4_f, k_af, vt4_f, name="fox_attn_fwd", tq=tq)
    y_mla = _mm(o_mla, W["w_o_mla"], outs=(F32,), name="o_proj_mla")
    y_fox = _mm(o_fox, W["w_o_fox"], outs=(F32,), name="o_proj_fox")

    def gate_merge(ga, gb, ya, yb):
        return _sigmoid(ga.astype(F32)) * ya + _sigmoid(gb.astype(F32)) * yb

    (merged,) = _rowwise(gate_merge, [(z, 1024, 0), (z, 1024, 1), y_mla, y_fox], [], [(D_MODEL, BF16)],
                         name="gate_merge")
    mix = _mm(merged, W["w_out"], outs=(F32,), name="out_proj")

    def post_mix(xv, mv, a, b):
        x1v = xv + _rms(mv, a)
        return x1v, _rms(x1v, b)

    x1, h2 = _rowwise(post_mix, [x2, mix], [g2, g3], [(D_MODEL, F32), (D_MODEL, BF16)], name="post_mix_norm")

    def relu2(acc):
        r = jnp.maximum(acc, 0.0)
        return r * r, acc

    act, u = _mm(h2, W["w_ff1"], epi=relu2, outs=(BF16, BF16), tm=2048, name="ff1")
    mlp = _mm(act, W["w_ff2"], outs=(F32,), tm=1024, tk=2048, name="ff2")

    def loss_bwd(x1v, mv, tv, g):
        y = x1v + _rms(mv, g)
        d = y - tv
        dy = d * (1.0 / D_MODEL)
        dm, dg = _rms_bwd(mv, g, dy)
        return dy, dm, dg, d * d

    dy, dm, dg4, loss_cols = _rowwise(loss_bwd, [x1, mlp, tgt], [g4], [(D_MODEL, F32), (D_MODEL, BF16)],
                                      [D_MODEL, D_MODEL], name="loss_bwd")

    def relu2_bwd(acc, uv):
        return (acc * (2.0 * jnp.maximum(uv.astype(F32), 0.0)),)

    du = _mm(dm, W["w_ff2"], tb=True, extras=(u,), epi=relu2_bwd, tm=2048, name="ff2_bwd")
    dw_ff2 = _mm(act, dm, ta=True, outs=(BF16,), tk=2048, name="ff2_wgrad")
    dh2 = _mm(du, W["w_ff1"], tb=True, outs=(F32,), tm=1024, tk=2048, name="ff1_bwd")
    dw_ff1 = _mm(h2, du, ta=True, outs=(BF16,), tk=2048, name="ff1_wgrad")

    def post_mix_bwd(x1v, dh2v, dyv, mv, a, b):
        d3, dg3v = _rms_bwd(x1v, b, dh2v)
        dx1v = dyv + d3
        dmixv, dg2v = _rms_bwd(mv, a, dx1v)
        return dx1v, dmixv, dg3v, dg2v

    dx1, dmix, dg3, dg2 = _rowwise(post_mix_bwd, [x1, dh2, dy, mix], [g2, g3], [(D_MODEL, F32), (D_MODEL, BF16)],
                                   [D_MODEL, D_MODEL], name="post_mix_bwd")
    dmerged = _mm(dmix, W["w_out"], tb=True, outs=(F32,), name="out_proj_bwd")
    dw_out = _mm(merged, dmix, ta=True, outs=(BF16,), name="out_proj_wgrad")

    def gate_bwd(dmg, ga, gb, ya, yb):
        sa = _sigmoid(ga.astype(F32))
        sb = _sigmoid(gb.astype(F32))
        return dmg * sa, dmg * sb, dmg * ya * sa * (1.0 - sa), dmg * yb * sb * (1.0 - sb)

    dy_mla, dy_fox, dga, dgb = _rowwise(gate_bwd, [dmerged, (z, 1024, 0), (z, 1024, 1), y_mla, y_fox], [],
                                        [(D_MODEL, BF16)] * 4, name="gate_bwd")
    twice = lambda acc: (acc, acc)
    do_mla, dot4_m = _mm(dy_mla, W["w_o_mla"], tb=True, epi=twice, outs=(BF16, BF16), t4=(False, True), tm=tq,
                         name="o_proj_mla_bwd")
    do_fox, dot4_f = _mm(dy_fox, W["w_o_fox"], tb=True, epi=twice, outs=(BF16, BF16), t4=(False, True), tm=tq,
                         name="o_proj_fox_bwd")
    dw_o_mla = _mm(o_mla, dy_mla, ta=True, outs=(BF16,), name="o_proj_mla_wgrad")
    dw_o_fox = _mm(o_fox, dy_fox, ta=True, outs=(BF16,), name="o_proj_fox_wgrad")

    def head_dots(ov, dov):
        r = lax.broadcasted_iota(jnp.int32, (N_HEADS * HEAD_V, LANES), 0)
        cc = lax.broadcasted_iota(jnp.int32, (N_HEADS * HEAD_V, LANES), 1)
        sel = jnp.where((r // HEAD_V) == cc, 1.0, 0.0).astype(BF16)
        return _dot3(ov.astype(F32) * dov.astype(F32), sel, left=False)

    (dl_mla,) = _rowwise(head_dots, [o_mla, do_mla], [], [(LANES, F32)], name="mla_attn_delta")
    (dl_fox,) = _rowwise(head_dots, [o_fox, do_fox], [], [(LANES, F32)], name="fox_attn_delta")
    heads_row4 = lambda t: t[:, :N_HEADS].T.reshape(N_HEADS, nq, 1, tq)
    grads_b = dict(w_o_mla=dw_o_mla, w_o_fox=dw_o_fox, w_out=dw_out, w_ff1=dw_ff1, w_ff2=dw_ff2)
    dqt_m, dkt_m, dv_mla, pieces_b = _attn_bwd(qt4_m, k_am, kv, 1024 // LANES, dot4_m, lse_mla,
                                               heads_row4(dl_mla), mla_scale, NOPE + ROPE, name="mla_attn_bwd", tq=tq,
                                               exchange=_pack_full(GROUP_B, grads_b, BF16))
    dqt_f, dkt_f, dv_fox = _attn_bwd(qt4_f, k_af, z, P_FV // LANES, dot4_f, lse_fox, heads_row4(dl_fox),
                                     fox_scale, FOX_D, name="fox_attn_bwd", tq=tq)

    def fox_unpack(dqa, dka):
        lane = lax.broadcasted_iota(jnp.int32, (1, LANES), 1)
        dqs, dks = [], []
        d_f = jnp.zeros(dqa[:, :LANES].shape, F32)
        for hp in range(N_HEADS // 2):
            blk = lambda t, e: t[:, LANES * (2 * hp + e):LANES * (2 * hp + e + 1)]
            dqs.append(jnp.where(lane < 64, blk(dqa, 0), pltpu.roll(blk(dqa, 1), 64, 1)))
            dks.append(jnp.where(lane < 64, blk(dka, 0), pltpu.roll(blk(dka, 1), 64, 1)))
            for e in range(2):
                g = blk(dqa, e)[:, 67:68] - blk(dka, e)[:, 64:65]
                d_f = jnp.where(lane == S_FL + 2 * hp + e, g, d_f)
        return jnp.concatenate(dqs, axis=1), jnp.concatenate(dks, axis=1), d_f

    dq_fox, dk_fox, d_f128 = _rowwise(fox_unpack, [("t4", dqt_f), ("t4", dkt_f)], [],
                                      [(512, BF16), (512, BF16), (LANES, F32)], name="fox_unpack", tm=tq)
    df_rev = _seq_cumsum(d_f128, lambda t: t, reverse=True, name="forget_cumsum_bwd")

    def rope_bwd(dqa, dka, dvm, dfr, s, cs, sn):
        sign, lane = _lane_sign()
        rope_l = _rope_lanes(lane)
        rot_t = lambda t: t * cs - pltpu.roll(t, 64, 1) * sn * sign
        dqs = []
        dk_rot = None
        for hd in range(N_HEADS):
            blk = dqa[:, LANES * hd:LANES * (hd + 1)]
            dqs.append(jnp.where(rope_l, rot_t(blk), blk))
            blk = dka[:, LANES * hd:LANES * (hd + 1)]
            dk_rot = blk if dk_rot is None else dk_rot + blk
        dfl = dfr * _sigmoid(-s)
        small = jnp.where(rope_l, rot_t(dk_rot), jnp.where((lane >= S_FL) & (lane < S_FL + N_HEADS), dfl, 0.0))
        return jnp.concatenate(dqs, axis=1), jnp.concatenate([dka, dvm.astype(F32)], axis=1), small

    dq_b, dkv, d_small = _rowwise(rope_bwd, [("t4", dqt_m), ("t4", dkt_m), dv_mla, df_rev, zs, cos_t, sin_t], [],
                                  [(1024, BF16), (1536, BF16), (LANES, BF16)], name="rope_bwd", tm=tq)
    dcqn = _mm(dq_b, w_uq_a, tb=True, outs=(F32,), name="q_up_bwd")
    dw_uq_a = _mm(cqn, dq_b, ta=True, outs=(BF16,), name="q_up_wgrad")
    dckvn = _mm(dkv, w_kv_a, tb=True, outs=(F32,), tk=512, name="kv_up_bwd")
    dw_kv_a = _mm(ckvn, dkv, ta=True, outs=(BF16,), tn=512, name="kv_up_wgrad")

    def lora_norm_bwd(cq, ckv, dq_, dkv_, a, b):
        d1, dga_ = _rms_bwd(cq.astype(F32), a, dq_)
        d2, dgb_ = _rms_bwd(ckv.astype(F32), b, dkv_)
        return d1, d2, dga_, dgb_

    dcq, dckv, dgq, dgkv = _rowwise(lora_norm_bwd, [(z, 256, P_CQ // 256), (z, 128, P_CKV // 128), dcqn, dckvn],
                                    [gq, gkv], [(Q_LORA, BF16), (KV_LORA, BF16)], [Q_LORA, KV_LORA],
                                    name="lora_norm_bwd")
    dz = jnp.concatenate([dga, dgb, dq_fox, dk_fox, dv_fox, dcq, dckv, d_small], -1)
    dw_in_p, db_in_p = _mm(h, dz, ta=True, outs=(BF16,), tk=2048, colsum=True, name="in_proj_wgrad")
    grads_a = dict(w_in=_unperm_in_to_shards(dw_in_p), w_uq=_unaug_uq_cols(dw_uq_a),
                   w_uk=_unaug_uk_cols(dw_kv_a[:, :1024]), w_uv=dw_kv_a[:, 1024:])
    dh, pieces_a = _mm(dz, w_in_p, tb=True, outs=(F32,), tm=1024, tk=2048, name="in_proj_bwd",
                       exchange=_pack_full(GROUP_A, grads_a, BF16))

    def pre_mix_bwd(xv, dhv, dx1v, g):
        d, dg = _rms_bwd(xv, g, dhv)
        return dx1v + d, dg

    grad_x, dg1 = _rowwise(pre_mix_bwd, [x2, dh, dx1], [g1], [(D_MODEL, F32)], [D_MODEL], name="pre_mix_bwd")

    grad_sh, delta_sh, newm_sh, newv_sh = {}, {}, {}, {}
    for group, pieces, tag in ((GROUP_A, pieces_a, "a"), (GROUP_B, pieces_b, "b")):
        packed = _sum_adamw(pieces, _pack_shards(group, w_sh, F32), _pack_shards(group, m_sh, F32),
                            _pack_shards(group, v_sh, F32), name="sum_pieces_adamw_" + tag)
        for dst, arr in zip((grad_sh, delta_sh, newm_sh, newv_sh), packed):
            dst.update(_unpack_shards(group, arr))

    small_part = _pack_small(dict(ln_pre_mix=dg1, ln_post_mix=dg2, ln_pre_mlp=dg3, ln_post_mlp=dg4,
                                  b_in=_unperm_in_cols(db_in_p), q_a_norm=dgq, kv_a_norm=dgkv), extra=loss_cols)
    sg, sd, sm, sv, loss_blk = _small_allreduce_adamw(small_part, _pack_small(small_w), _pack_small(small_m),
                                                      _pack_small(small_v))
    grad_sm, delta_sm, newm_sm, newv_sm = (_unpack_small(t) for t in (sg, sd, sm, sv))
    loss = loss_blk[0, 0]

    order = ["ln_pre_mix", "ln_post_mix", "ln_pre_mlp", "ln_post_mlp", "w_in", "b_in", "q_a_norm", "w_uq",
             "kv_a_norm", "w_uk", "w_uv", "w_o_mla", "w_o_fox", "w_out", "w_ff1", "w_ff2"]

    def pick(sm_d, sh_d):
        return [sm_d[n] if n in sm_d else sh_d[n] for n in order]

    return (loss, grad_x.reshape(1, T, D_MODEL), *pick(grad_sm, grad_sh), *pick(delta_sm, delta_sh),
            *pick(newm_sm, newm_sh), *pick(newv_sm, newv_sh))
```

```python
import numpy as np
import jax
import jax.numpy as jnp
from jax import lax
from jax.experimental import pallas as pl
from jax.experimental.pallas import tpu as pltpu

F32 = jnp.float32
BF16 = jnp.bfloat16
MESH = pl.DeviceIdType.MESH

D_MODEL = 1024
N_HEADS = 8
Q_LORA = 256
KV_LORA = 128
NOPE = 64
ROPE = 32
HEAD_V = 64
FOX_D = 64
D_FF = 4096
D_IN = 4008
D_IN_PAD = 4096
ROPE_THETA = 10000.0
NORM_EPS = 1e-6
N_DEV = 8

ADAM_LR = 0.001
ADAM_B1 = 0.9
ADAM_B2 = 0.999
ADAM_EPS = 1e-08
ADAM_WD = 0.01
ADAM_STEP = 10

LANES = 128
ROW_TILE = 512
ATTN_TILE = 512
ATTN_CHUNK = 256
BWD_UNROLL = 4
VMEM_LIMIT = 48 * 1024 * 1024
ATTN_BWD_VMEM_LIMIT = 58 * 1024 * 1024

P_GA, P_GB, P_FQ, P_FK, P_FV, P_CQ, P_CKV, P_SMALL = 0, 1024, 2048, 2560, 3072, 3584, 3840, 3968
S_FL = 32

SHARDED = (
    ("w_in", 1024, 501, True), ("w_uq", 256, 96, True), ("w_uk", 128, 64, True), ("w_uv", 128, 64, True),
    ("w_o_mla", 512, 128, True), ("w_o_fox", 512, 128, True), ("w_out", 128, 1024, False),
    ("w_ff1", 1024, 512, True), ("w_ff2", 512, 1024, False),
)
GROUP_A = SHARDED[:4]
GROUP_B = SHARDED[4:]
SMALL_ROWS = 80
SMALL_LOSS_ROW = 72


def _pack_rows(group):
    return -(-sum(r * _packed_cols(c, by_col) for _, r, c, by_col in group) // (LANES * 64)) * 64


def _cparams(sem=None):
    return pltpu.CompilerParams(dimension_semantics=sem, vmem_limit_bytes=VMEM_LIMIT)


def _mm(a, b, *, name, ta=False, tb=False, bias=None, colscale=None, extras=(), epi=None, outs=(BF16,), t4=None,
        tm=ROW_TILE, tn=1024, tk=1024, exchange=None, colsum=False):
    t4 = (False,) * len(outs) if t4 is None else t4
    comm = exchange is not None
    if ta:
        K, M = a.shape
        tm = min(1024, M)
    else:
        M, K = a.shape
        tm = min(tm, M)
    tk = min(tk, K)
    N = b.shape[0] if tb else b.shape[1]
    tn = min(tn, N)
    nk = K // tk
    assert not colsum or (ta and M == tm)
    n_ex = len(extras)
    has_bias = bias is not None
    has_scale = colscale is not None

    def body(*refs):
        a_ref, b_ref = refs[0], refs[1]
        pos = 2
        bias_ref = scale_ref = None
        if has_bias:
            bias_ref = refs[pos]
            pos += 1
        if has_scale:
            scale_ref = refs[pos]
            pos += 1
        ex_refs = refs[pos:pos + n_ex]
        pos += n_ex
        if comm:
            c_src = refs[pos]
            pos += 1
        o_refs = refs[pos:pos + len(outs)]
        pos += len(outs)
        if colsum:
            cs_ref = refs[pos]
            pos += 1
        if comm:
            c_dst = refs[pos]
            pos += 1
            c_sems = refs[len(refs) - 3:]
            ids = [pl.program_id(d) for d in range(3)]

            @pl.when((ids[0] == 0) & (ids[1] == 0) & (ids[2] == 0))
            def _():
                _comm_start("exchange", c_src, c_dst, *c_sems)

        av = a_ref[...].astype(BF16)
        bv = b_ref[...].astype(BF16)
        if ta:
            part = lax.dot_general(av, bv, (((0,), (0,)), ((), ())), preferred_element_type=F32)
        elif tb:
            part = lax.dot_general(av, bv, (((1,), (1,)), ((), ())), preferred_element_type=F32)
        else:
            part = jnp.dot(av, bv, preferred_element_type=F32)
        if colsum:
            cs = jnp.sum(bv.astype(F32), axis=0, keepdims=True)

            @pl.when(pl.program_id(2) == 0)
            def _():
                cs_ref[...] = cs

            @pl.when(pl.program_id(2) > 0)
            def _():
                cs_ref[...] += cs

        def finish(acc):
            if has_bias:
                acc = acc + bias_ref[...]
            if has_scale:
                acc = acc * scale_ref[...]
            res = (acc,) if epi is None else epi(acc, *[r[...] for r in ex_refs])
            for o_ref, val, t in zip(o_refs, res, t4):
                if t:
                    _t4_store(o_ref, val)
                else:
                    o_ref[...] = val.astype(o_ref.dtype)

        if nk == 1:
            finish(part)
        else:
            acc_ref = refs[pos]
            k = pl.program_id(2)

            @pl.when(k == 0)
            def _():
                acc_ref[...] = part

            @pl.when(k > 0)
            def _():
                acc_ref[...] += part

            @pl.when(k == nk - 1)
            def _():
                finish(acc_ref[...])

        if comm:
            @pl.when((ids[0] == M // tm - 1) & (ids[1] == N // tn - 1) & (ids[2] == nk - 1))
            def _():
                _comm_wait("exchange", c_src, c_dst, *c_sems)

    if ta:
        a_spec = pl.BlockSpec((tk, tm), lambda i, j, k: (k, i))
    else:
        a_spec = pl.BlockSpec((tm, tk), lambda i, j, k: (i, k))
    b_spec = pl.BlockSpec((tn, tk), lambda i, j, k: (j, k)) if tb else pl.BlockSpec((tk, tn), lambda i, j, k: (k, j))
    in_specs = [a_spec, b_spec]
    args = [a, b]
    for row in (bias, colscale):
        if row is not None:
            in_specs.append(pl.BlockSpec((1, tn), lambda i, j, k: (0, j)))
            args.append(row)
    for e in extras:
        in_specs.append(pl.BlockSpec((tm, tn), lambda i, j, k: (i, j)))
        args.append(e)
    out_specs = [pl.BlockSpec((N // LANES, 1, LANES, tm), lambda i, j, k: (0, i, 0, 0)) if t
                 else pl.BlockSpec((tm, tn), lambda i, j, k: (i, j)) for t in t4]
    out_shape = [jax.ShapeDtypeStruct((N // LANES, M // tm, LANES, tm) if t else (M, N), dt)
                 for dt, t in zip(outs, t4)]
    scratch = [pltpu.VMEM((tm, tn), F32)] if nk > 1 else []
    if colsum:
        out_specs.append(pl.BlockSpec((1, tn), lambda i, j, k: (0, j)))
        out_shape.append(jax.ShapeDtypeStruct((1, N), F32))
    if comm:
        in_specs.append(pl.BlockSpec(memory_space=pl.ANY))
        args.append(exchange)
        out_specs.append(pl.BlockSpec(memory_space=pl.ANY))
        out_shape.append(jax.ShapeDtypeStruct(exchange.shape, exchange.dtype))
        scratch += _comm_sems()
    res = pl.pallas_call(
        body, name=name, grid=(M // tm, N // tn, nk),
        in_specs=in_specs, out_specs=out_specs, out_shape=out_shape, scratch_shapes=scratch,
        compiler_params=_cparams(("arbitrary",) * 3 if comm else ("parallel", "parallel", "arbitrary")),
    )(*args)
    return res[0] if len(res) == 1 else res


def _t4_store(o_ref, val):
    for c in range(o_ref.shape[0]):
        o_ref[c, 0] = val[:, c * LANES:(c + 1) * LANES].astype(F32).T.astype(o_ref.dtype)


def _rowwise(fn, rows, bcasts, outs, accs=(), *, name, tm=ROW_TILE):
    t4_in = [isinstance(r, tuple) and isinstance(r[0], str) for r in rows]
    T = [r[1].shape[1] * r[1].shape[3] if t else (r[0] if isinstance(r, tuple) else r).shape[0]
         for r, t in zip(rows, t4_in)][0]
    tm = min(tm, T)
    arrs, specs = [], []
    for r, t in zip(rows, t4_in):
        if t:
            arr = r[1]
            specs.append(pl.BlockSpec((arr.shape[0], 1, LANES, tm), lambda i: (0, i, 0, 0)))
        elif isinstance(r, tuple):
            arr, w, cb = r
            specs.append(pl.BlockSpec((tm, w), lambda i, cb=cb: (i, cb)))
        else:
            arr = r
            specs.append(pl.BlockSpec((tm, arr.shape[1]), lambda i: (i, 0)))
        arrs.append(arr)
    n_rows = len(arrs)
    for b in bcasts:
        arrs.append(b)
        specs.append(pl.BlockSpec(b.shape, lambda i: (0, 0)))
    n_in, n_out = len(arrs), len(outs)
    t4_out = [len(o) == 3 for o in outs]

    def body(*refs):
        vals = []
        for k, r in enumerate(refs[:n_in]):
            if k < n_rows and t4_in[k]:
                vals.append(jnp.concatenate([r[c, 0].astype(F32).T for c in range(r.shape[0])], axis=1))
            else:
                vals.append(r[...])
        res = fn(*vals)
        if not isinstance(res, (tuple, list)):
            res = (res,)
        for o_ref, val, t in zip(refs[n_in:n_in + n_out], res[:n_out], t4_out):
            if t:
                _t4_store(o_ref, val)
            else:
                o_ref[...] = val.astype(o_ref.dtype)
        i = pl.program_id(0)
        for a_ref, val in zip(refs[n_in + n_out:], res[n_out:]):
            col = jnp.sum(val.astype(F32), axis=0, keepdims=True)

            @pl.when(i == 0)
            def _(a_ref=a_ref, col=col):
                a_ref[...] = col

            @pl.when(i > 0)
            def _(a_ref=a_ref, col=col):
                a_ref[...] += col

    res = pl.pallas_call(
        body, name=name, grid=(T // tm,),
        in_specs=specs,
        out_specs=[pl.BlockSpec((o[0] // LANES, 1, LANES, tm), lambda i: (0, i, 0, 0)) if t
                   else pl.BlockSpec((tm, o[0]), lambda i: (i, 0)) for o, t in zip(outs, t4_out)]
        + [pl.BlockSpec((1, c), lambda i: (0, 0)) for c in accs],
        out_shape=[jax.ShapeDtypeStruct((o[0] // LANES, T // tm, LANES, tm) if t else (T, o[0]), o[1])
                   for o, t in zip(outs, t4_out)]
        + [jax.ShapeDtypeStruct((1, c), F32) for c in accs],
        compiler_params=_cparams(("arbitrary",)),
    )(*arrs)
    return res


def _dot3(x, sel, left):
    hi = x.astype(BF16)
    r1 = x - hi.astype(F32)
    mid = r1.astype(BF16)
    lo = (r1 - mid.astype(F32)).astype(BF16)
    if left:
        d = lambda t: jnp.dot(sel, t, preferred_element_type=F32)
    else:
        d = lambda t: jnp.dot(t, sel, preferred_element_type=F32)
    return d(hi) + d(mid) + d(lo)


def _seq_cumsum(x, fn, *, reverse, name, tm=256):
    T, C = x.shape
    tm = min(tm, T)
    n = T // tm

    def body(x_ref, o_ref, carry_ref):
        i = pl.program_id(0)

        @pl.when(i == 0)
        def _():
            carry_ref[...] = jnp.zeros_like(carry_ref)

        v = fn(x_ref[...])
        r = lax.broadcasted_iota(jnp.int32, (tm, tm), 0)
        c = lax.broadcasted_iota(jnp.int32, (tm, tm), 1)
        tri = jnp.where((r <= c) if reverse else (r >= c), 1.0, 0.0).astype(BF16)
        carry = carry_ref[0:1, :]
        o_ref[...] = _dot3(v, tri, left=True) + carry
        carry_ref[0:1, :] = carry + jnp.sum(v, axis=0, keepdims=True)

    idx = (lambda i: (n - 1 - i, 0)) if reverse else (lambda i: (i, 0))
    return pl.pallas_call(
        body, name=name, grid=(n,),
        in_specs=[pl.BlockSpec((tm, C), idx)],
        out_specs=pl.BlockSpec((tm, C), idx),
        out_shape=jax.ShapeDtypeStruct((T, C), F32),
        scratch_shapes=[pltpu.VMEM((8, C), F32)],
        compiler_params=_cparams(("arbitrary",)),
    )(x)


_NT = (((1,), (1,)), ((), ()))
LOG2E = 1.4426950408889634


def _head_mask(width, e):
    lane = lax.broadcasted_iota(jnp.int32, (1, width), 1)
    return (lane >= 64 * e) & (lane < 64 * (e + 1))


def _attn_fwd(qt4, k_aug, vt4, *, name, tq, gather=None):
    nq = qt4.shape[1]
    T = nq * tq
    cw = min(ATTN_CHUNK, tq)
    comm = gather is not None

    def body(*refs):
        refs = list(refs)
        q_ref, k_ref, v_ref = refs[:3]
        pos = 3
        if comm:
            c_src = refs[pos]
            pos += 1
        o_ref, lse_ref = refs[pos:pos + 2]
        pos += 2
        if comm:
            c_dst = refs[pos]
            pos += 1
        m_s, l_s, acc_s = refs[pos:pos + 2], refs[pos + 2:pos + 4], refs[pos + 4:pos + 6]
        st_a, st_b = refs[pos + 6:pos + 8]
        c_sems = refs[pos + 8:]
        i = pl.program_id(1)
        if comm:
            @pl.when((pl.program_id(0) == 0) & (i == 0))
            def _():
                _comm_start("gather", c_src, c_dst, *c_sems)

        for e in range(2):
            m_s[e][...] = jnp.full(m_s[e].shape, -jnp.inf, F32)
            l_s[e][...] = jnp.zeros_like(l_s[e])
            acc_s[e][...] = jnp.zeros_like(acc_s[e])
        qt = [q_ref[e, 0] for e in range(2)]

        work = [(e, slice(c * cw, (c + 1) * cw)) for e in range(2) for c in range(tq // cw)]

        def scores(j, buf):
            kj = k_ref[pl.ds(pl.multiple_of(j * tq, tq), tq), :]
            for n, (e, cs) in enumerate(work):
                buf[n] = jnp.dot(kj[:, e * LANES:(e + 1) * LANES], qt[e][:, cs], preferred_element_type=F32)

        def step(j, buf, masked):
            vtj = v_ref[0, j]
            for n, (e, cs) in enumerate(work):
                kr = cs.stop if masked else tq
                st = buf[n][:kr]
                if masked:
                    r = lax.broadcasted_iota(jnp.int32, (kr, cw), 0)
                    cc = lax.broadcasted_iota(jnp.int32, (kr, cw), 1) + cs.start
                    st = jnp.where(cc >= r, st, -jnp.inf)
                m_prev = m_s[e][:, cs]
                m_new = jnp.maximum(m_prev, jnp.max(st, axis=0, keepdims=True))
                alpha = jnp.exp2(m_prev - m_new)
                pt = jnp.exp2(st - m_new)
                l_s[e][:, cs] = alpha * l_s[e][:, cs] + jnp.sum(pt, axis=0, keepdims=True)
                acc_s[e][:, cs] = alpha * acc_s[e][:, cs] + jnp.dot(vtj[e * HEAD_V:(e + 1) * HEAD_V, :kr],
                                                                    pt.astype(BF16), preferred_element_type=F32)
                m_s[e][:, cs] = m_new

        def two_tiles(p, carry):
            scores(2 * p + 1, st_b)
            step(2 * p, st_a, False)
            scores(2 * p + 2, st_a)
            step(2 * p + 1, st_b, False)
            return carry

        def four_tiles(p, carry):
            two_tiles(2 * p, carry)
            return two_tiles(2 * p + 1, carry)

        scores(0, st_a)
        lax.fori_loop(0, i // 4, four_tiles, 0)
        lax.fori_loop(2 * (i // 4), i // 2, two_tiles, 0)

        @pl.when(i % 2 == 0)
        def _():
            step(i, st_a, True)

        @pl.when(i % 2 == 1)
        def _():
            scores(i, st_b)
            step(i - 1, st_a, False)
            step(i, st_b, True)

        ot = jnp.concatenate([acc_s[e][...] / l_s[e][...] for e in range(2)], axis=0)
        o_ref[...] = ot.T.astype(o_ref.dtype)
        for e in range(2):
            lse_ref[e, 0] = m_s[e][...] + jnp.log(l_s[e][...]) * LOG2E
        if comm:
            @pl.when((pl.program_id(0) == N_HEADS // 2 - 1) & (i == nq - 1))
            def _():
                _comm_wait("gather", c_src, c_dst, *c_sems)

    in_specs = [
        pl.BlockSpec((2, 1, LANES, tq), lambda hp, i: (hp, i, 0, 0)),
        pl.BlockSpec((T, 2 * LANES), lambda hp, i: (0, hp)),
        pl.BlockSpec((1, nq, LANES, tq), lambda hp, i: (hp, 0, 0, 0)),
    ]
    args = [qt4, k_aug, vt4]
    out_specs = [pl.BlockSpec((tq, LANES), lambda hp, i: (i, hp)),
                 pl.BlockSpec((2, 1, 1, tq), lambda hp, i: (hp, i, 0, 0))]
    out_shape = [jax.ShapeDtypeStruct((T, N_HEADS * HEAD_V), BF16), jax.ShapeDtypeStruct((N_HEADS, nq, 1, tq), F32)]
    scratch = ([pltpu.VMEM((1, tq), F32)] * 4 + [pltpu.VMEM((HEAD_V, tq), F32)] * 2
               + [pltpu.VMEM((2 * tq // cw, tq, cw), F32)] * 2)
    if comm:
        in_specs.append(pl.BlockSpec(memory_space=pl.ANY))
        args.append(gather)
        out_specs.append(pl.BlockSpec(memory_space=pl.ANY))
        out_shape.append(jax.ShapeDtypeStruct((N_DEV,) + gather.shape, gather.dtype))
        scratch += _comm_sems()
    return pl.pallas_call(
        body, name=name, grid=(N_HEADS // 2, nq),
        in_specs=in_specs, out_specs=out_specs, out_shape=out_shape, scratch_shapes=scratch,
        compiler_params=_cparams(("arbitrary", "arbitrary")),
    )(*args)


def _attn_bwd(qt4, k_aug, v, vcb, dot4, lse_row, dl_row, scale, own_rows, *, name, tq, exchange=None):
    nq = qt4.shape[1]
    T = nq * tq
    cw = min(ATTN_CHUNK, tq)
    comm = exchange is not None

    def body(*refs):
        refs = list(refs)
        k_ref, v_ref, q_ref, do_ref, lse_ref, dl_ref = refs[:6]
        pos = 6
        if comm:
            c_src = refs[pos]
            pos += 1
        dq_ref, dk_ref, dv_ref = refs[pos:pos + 3]
        pos += 3
        if comm:
            c_dst = refs[pos]
            pos += 1
        dk_s, dv_s = refs[pos:pos + 2], refs[pos + 2:pos + 4]
        c_sems = refs[pos + 4:]
        j = pl.program_id(1)
        if comm:
            @pl.when((pl.program_id(0) == 0) & (j == 0))
            def _():
                _comm_start("exchange", c_src, c_dst, *c_sems)

        @pl.when(j == 0)
        def _():
            dq_ref[...] = jnp.zeros_like(dq_ref)

        kj = k_ref[...]
        vj = v_ref[...]
        ka = [kj[:, e * LANES:(e + 1) * LANES] for e in range(2)]
        row = lax.broadcasted_iota(jnp.int32, (LANES, 1), 0)
        own = row < own_rows
        kat = [(ka[e].astype(F32).T * jnp.where(own, scale, 1.0)).astype(BF16) for e in range(2)]
        vm = [jnp.where(_head_mask(LANES, e), vj, jnp.zeros_like(vj)) for e in range(2)]
        for e in range(2):
            dk_s[e][...] = jnp.zeros_like(dk_s[e])
            dv_s[e][...] = jnp.zeros_like(dv_s[e])

        def step(i, masked):
            dot_i = do_ref[0, i]
            qts = [q_ref[e, i] for e in range(2)]
            lse = [lse_ref[e, i] for e in range(2)]
            dl = [dl_ref[e, i] for e in range(2)]
            work = [(e, slice(c * cw, (c + 1) * cw)) for e in range(2) for c in range(tq // cw)]
            rows = lambda cs: cs.stop if masked else tq
            scores = lambda e, cs: (jnp.dot(ka[e][:rows(cs)], qts[e][:, cs], preferred_element_type=F32),
                                    jnp.dot(vm[e][:rows(cs)], dot_i[:, cs], preferred_element_type=F32))
            nxt = scores(*work[0])
            dqs = [[], []]
            for n, (e, cs) in enumerate(work):
                st, dpt = nxt
                kr = rows(cs)
                if n + 1 < len(work):
                    nxt = scores(*work[n + 1])
                if masked:
                    r = lax.broadcasted_iota(jnp.int32, (kr, cw), 0)
                    cc = lax.broadcasted_iota(jnp.int32, (kr, cw), 1) + cs.start
                    st = jnp.where(cc >= r, st, -jnp.inf)
                pt = jnp.exp2(st - lse[e][:, cs])
                dv_s[e][:, :kr] += lax.dot_general(dot_i[e * HEAD_V:(e + 1) * HEAD_V, cs], pt.astype(BF16), _NT,
                                                   preferred_element_type=F32)
                dsb = (pt * (dpt - dl[e][:, cs])).astype(BF16)
                dk_s[e][:, :kr] += lax.dot_general(qts[e][:, cs], dsb, _NT, preferred_element_type=F32)
                dqs[e].append(jnp.dot(kat[e][:, :kr], dsb, preferred_element_type=F32))
            for e in range(2):
                dq_ref[e, i] += jnp.concatenate(dqs[e], axis=1)

        def some_tiles(p, carry):
            for t in range(BWD_UNROLL):
                step(j + 1 + BWD_UNROLL * p + t, False)
            return carry

        def one_tile(i, carry):
            step(i, False)
            return carry

        trips = (nq - 1 - j) // BWD_UNROLL
        step(j, True)
        lax.fori_loop(0, trips, some_tiles, 0)
        lax.fori_loop(j + 1 + BWD_UNROLL * trips, nq, one_tile, 0)

        for e in range(2):
            dk_ref[e, 0] = dk_s[e][...] * jnp.where(own, 1.0 / LOG2E, 1.0)
        dv_ref[...] = jnp.concatenate([dv_s[0][...], dv_s[1][...]], axis=0).T.astype(dv_ref.dtype)
        if comm:
            @pl.when((pl.program_id(0) == N_HEADS // 2 - 1) & (j == nq - 1))
            def _():
                _comm_wait("exchange", c_src, c_dst, *c_sems)

    row4 = pl.BlockSpec((2, nq, 1, tq), lambda hp, j: (hp, 0, 0, 0))
    in_specs = [
        pl.BlockSpec((tq, 2 * LANES), lambda hp, j: (j, hp)),
        pl.BlockSpec((tq, LANES), lambda hp, j: (j, vcb + hp)),
        pl.BlockSpec((2, nq, LANES, tq), lambda hp, j: (hp, 0, 0, 0)),
        pl.BlockSpec((1, nq, LANES, tq), lambda hp, j: (hp, 0, 0, 0)),
        row4, row4,
    ]
    args = [k_aug, v, qt4, dot4, lse_row, dl_row]
    out_specs = [pl.BlockSpec((2, nq, LANES, tq), lambda hp, j: (hp, 0, 0, 0)),
                 pl.BlockSpec((2, 1, LANES, tq), lambda hp, j: (hp, j, 0, 0)),
                 pl.BlockSpec((tq, LANES), lambda hp, j: (j, hp))]
    out_shape = [jax.ShapeDtypeStruct((N_HEADS, nq, LANES, tq), F32), jax.ShapeDtypeStruct((N_HEADS, nq, LANES, tq), F32),
                 jax.ShapeDtypeStruct((T, N_HEADS * HEAD_V), BF16)]
    scratch = [pltpu.VMEM((LANES, tq), F32)] * 2 + [pltpu.VMEM((HEAD_V, tq), F32)] * 2
    if comm:
        in_specs.append(pl.BlockSpec(memory_space=pl.ANY))
        args.append(exchange)
        out_specs.append(pl.BlockSpec(memory_space=pl.ANY))
        out_shape.append(jax.ShapeDtypeStruct(exchange.shape, exchange.dtype))
        scratch += _comm_sems()
    return pl.pallas_call(
        body, name=name, grid=(N_HEADS // 2, nq),
        in_specs=in_specs, out_specs=out_specs, out_shape=out_shape, scratch_shapes=scratch,
        compiler_params=pltpu.CompilerParams(dimension_semantics=("arbitrary", "arbitrary"),
                                             vmem_limit_bytes=ATTN_BWD_VMEM_LIMIT),
    )(*args)


def _peers():
    x, y, c = lax.axis_index("x"), lax.axis_index("y"), lax.axis_index("c")
    me = 4 * x + 2 * y + c
    out = []
    for k in range(1, N_DEV):
        px = (1 - x) if (k & 4) else x
        py = (1 - y) if (k & 2) else y
        pc = (1 - c) if (k & 1) else c
        out.append(((px, py, pc), 4 * px + 2 * py + pc))
    return me, out


def _comm_sems():
    return [pltpu.SemaphoreType.DMA((N_DEV - 1,)), pltpu.SemaphoreType.DMA((N_DEV - 1,)), pltpu.SemaphoreType.DMA]


def _comm_copies(kind, src_ref, dst_ref, send_sems, recv_sems, local_sem):
    me, peers = _peers()
    local = pltpu.make_async_copy(src_ref if kind == "gather" else src_ref.at[me], dst_ref.at[me], local_sem)
    sends, recvs = [], []
    for k, (dev, lin) in enumerate(peers):
        src = src_ref if kind == "gather" else src_ref.at[lin]
        sends.append(pltpu.make_async_remote_copy(src_ref=src, dst_ref=dst_ref.at[me], send_sem=send_sems.at[k],
                                                  recv_sem=recv_sems.at[k], device_id=dev, device_id_type=MESH))
        recvs.append(pltpu.make_async_remote_copy(src_ref=src, dst_ref=dst_ref.at[lin], send_sem=send_sems.at[k],
                                                  recv_sem=recv_sems.at[k], device_id=dev, device_id_type=MESH))
    return local, sends, recvs


def _comm_start(kind, src_ref, dst_ref, send_sems, recv_sems, local_sem):
    local, sends, _ = _comm_copies(kind, src_ref, dst_ref, send_sems, recv_sems, local_sem)
    local.start()
    for cp in sends:
        cp.start()


def _comm_wait(kind, src_ref, dst_ref, send_sems, recv_sems, local_sem):
    local, sends, recvs = _comm_copies(kind, src_ref, dst_ref, send_sems, recv_sems, local_sem)
    for cp in recvs:
        cp.wait_recv()
    for cp in sends:
        cp.wait_send()
    local.wait()


def _gather_via_sibling(src, name):
    def body(x_ref, o_ref, send_sems, recv_sems, local_sem):
        x, y, c = lax.axis_index("x"), lax.axis_index("y"), lax.axis_index("c")
        me, sibling = (x, y, c), (x, y, 1 - c)
        chips = [(1 - x, y), (x, 1 - y), (1 - x, 1 - y)]

        def slot(px, py, pc):
            return o_ref.at[4 * px + 2 * py + pc]

        def copy(k, block, to, src_ref=None):
            return pltpu.make_async_remote_copy(src_ref=slot(*block) if src_ref is None else src_ref,
                                                dst_ref=slot(*block), send_sem=send_sems.at[k],
                                                recv_sem=recv_sems.at[k], device_id=to, device_id_type=MESH)

        mine = pltpu.make_async_copy(x_ref, slot(*me), local_sem)
        mine.start()
        first = [copy(0, me, sibling, x_ref)] + [copy(1 + n, me, (*chip, c), x_ref) for n, chip in enumerate(chips)]
        for cp in first:
            cp.start()
        passed = [copy(4 + n, (*chip, c), sibling) for n, chip in enumerate(chips)]
        for n, chip in enumerate(chips):
            copy(1 + n, (*chip, c), me).wait_recv()
            passed[n].start()
        copy(0, sibling, me).wait_recv()
        for n, chip in enumerate(chips):
            copy(4 + n, (*chip, 1 - c), me).wait_recv()
        for cp in first + passed:
            cp.wait_send()
        mine.wait()

    return pl.pallas_call(
        body, name=name,
        in_specs=[pl.BlockSpec(memory_space=pl.ANY)],
        out_specs=pl.BlockSpec(memory_space=pl.ANY),
        out_shape=jax.ShapeDtypeStruct((N_DEV,) + src.shape, src.dtype),
        scratch_shapes=_comm_sems(),
    )(src)


def _adamw(w, g, m, v):
    m2 = ADAM_B1 * m + (1.0 - ADAM_B1) * g
    v2 = ADAM_B2 * v + (1.0 - ADAM_B2) * (g * g)
    m_hat = m2 / (1.0 - ADAM_B1 ** ADAM_STEP)
    v_hat = v2 / (1.0 - ADAM_B2 ** ADAM_STEP)
    delta = -ADAM_LR * (m_hat / (jnp.sqrt(v_hat) + ADAM_EPS) + ADAM_WD * w)
    return delta, m2, v2


def _sum_adamw(pieces, w, m, v, *, name):
    R = w.shape[0]
    tr = R // 4 if R % 64 == 0 else R

    def body(p_ref, w_ref, m_ref, v_ref, g_ref, d_ref, m2_ref, v2_ref):
        g = p_ref[0].astype(F32)
        for s in range(1, N_DEV):
            g = g + p_ref[s].astype(F32)
        delta, m2, v2 = _adamw(w_ref[...], g, m_ref[...], v_ref[...])
        g_ref[...] = g
        d_ref[...] = delta
        m2_ref[...] = m2
        v2_ref[...] = v2

    row = pl.BlockSpec((tr, LANES), lambda i: (i, 0))
    return pl.pallas_call(
        body, name=name, grid=(R // tr,),
        in_specs=[pl.BlockSpec((N_DEV, tr, LANES), lambda i: (0, i, 0)), row, row, row],
        out_specs=[row, row, row, row],
        out_shape=[jax.ShapeDtypeStruct((R, LANES), F32)] * 4,
        compiler_params=_cparams(("parallel",)),
    )(pieces, w, m, v)


def _small_allreduce_adamw(part, w, m, v):
    shape = part.shape

    def body(p_ref, w_ref, m_ref, v_ref, g_ref, d_ref, m2_ref, v2_ref, loss_ref, gath, send_sems, recv_sems):
        me, peers = _peers()
        gath[me] = p_ref[...]
        sends = []
        for k, (dev, _) in enumerate(peers):
            cp = pltpu.make_async_remote_copy(src_ref=p_ref, dst_ref=gath.at[me], send_sem=send_sems.at[k],
                                              recv_sem=recv_sems.at[k], device_id=dev, device_id_type=MESH)
            cp.start()
            sends.append(cp)
        for k, (dev, lin) in enumerate(peers):
            pltpu.make_async_remote_copy(src_ref=p_ref, dst_ref=gath.at[lin], send_sem=send_sems.at[k],
                                         recv_sem=recv_sems.at[k], device_id=dev, device_id_type=MESH).wait_recv()
        for cp in sends:
            cp.wait_send()
        g = gath[0]
        for s in range(1, N_DEV):
            g = g + gath[s]
        delta, m2, v2 = _adamw(w_ref[...], g, m_ref[...], v_ref[...])
        g_ref[...] = g
        d_ref[...] = delta
        m2_ref[...] = m2
        v2_ref[...] = v2
        sq = jnp.sum(g[SMALL_LOSS_ROW:SMALL_LOSS_ROW + 8, :], axis=1, keepdims=True)
        tot = jnp.sum(sq, axis=0, keepdims=True) * (0.5 / D_MODEL)
        loss_ref[...] = jnp.broadcast_to(tot, loss_ref.shape)

    vm = pl.BlockSpec(memory_space=pltpu.VMEM)
    return pl.pallas_call(
        body, name="small_allreduce_adamw",
        in_specs=[vm, vm, vm, vm],
        out_specs=[vm, vm, vm, vm, vm],
        out_shape=[jax.ShapeDtypeStruct(shape, F32)] * 4 + [jax.ShapeDtypeStruct((8, LANES), F32)],
        scratch_shapes=[pltpu.VMEM((N_DEV,) + shape, F32),
                        pltpu.SemaphoreType.DMA((N_DEV - 1,)), pltpu.SemaphoreType.DMA((N_DEV - 1,))],
    )(part, w, m, v)


def _perm_in_cols(w):
    z = lambda n: jnp.zeros(w.shape[:-1] + (n,), w.dtype)
    small = jnp.concatenate([w[..., 384:400], z(16), w[..., 1952:1960], z(24), w[..., 400:416], z(48)], -1)
    return jnp.concatenate([w[..., 1960:2984], w[..., 2984:4008], w[..., 416:928], w[..., 928:1440],
                            w[..., 1440:1952], w[..., 0:256], w[..., 256:384], small], -1)


def _unperm_in_cols(wp):
    s = wp[..., P_SMALL:]
    return jnp.concatenate([wp[..., P_CQ:P_CQ + 256], wp[..., P_CKV:P_CKV + 128], s[..., 0:16], s[..., 64:80],
                            wp[..., P_FQ:P_FQ + 512], wp[..., P_FK:P_FK + 512], wp[..., P_FV:P_FV + 512],
                            s[..., S_FL:S_FL + 8], wp[..., P_GA:P_GA + 1024], wp[..., P_GB:P_GB + 1024]], -1)


def _aug_uq_cols(w):
    r = w.shape[0]
    w3 = w.reshape(r, N_HEADS, NOPE + ROPE)
    z = jnp.zeros((r, N_HEADS, 32), w.dtype)
    return jnp.concatenate([w3[:, :, 64:80], w3[:, :, 0:48], w3[:, :, 80:96], w3[:, :, 48:64], z], -1).reshape(r, 1024)


def _unaug_uq_cols(wp):
    r = wp.shape[0]
    w3 = wp.reshape(r, N_HEADS, LANES)
    return jnp.concatenate([w3[:, :, 16:64], w3[:, :, 80:96], w3[:, :, 0:16], w3[:, :, 64:80]], -1).reshape(r, 768)


def _aug_uk_cols(w):
    r = w.shape[0]
    w3 = w.reshape(r, N_HEADS, NOPE)
    z = lambda n: jnp.zeros((r, N_HEADS, n), w.dtype)
    return jnp.concatenate([z(16), w3[:, :, 0:48], z(16), w3[:, :, 48:64], z(32)], -1).reshape(r, 1024)


def _unaug_uk_cols(wp):
    r = wp.shape[0]
    w3 = wp.reshape(r, N_HEADS, LANES)
    return jnp.concatenate([w3[:, :, 16:64], w3[:, :, 80:96]], -1).reshape(r, 512)


IN_SEGMENTS = ((0, 256, P_CQ), (256, 128, P_CKV), (384, 16, P_SMALL), (400, 16, P_SMALL + 64), (416, 512, P_FQ),
               (928, 512, P_FK), (1440, 512, P_FV), (1952, 8, P_SMALL + S_FL), (1960, 1024, P_GA), (2984, 1024, P_GB))
IN_SHARD = D_IN // N_DEV


def _perm_in_from_shards(w3):
    r = w3.shape[0]
    parts = []
    pos = 0
    for o0, n, p0 in sorted(IN_SEGMENTS, key=lambda t: t[2]):
        if p0 > pos:
            parts.append(jnp.zeros((r, p0 - pos), w3.dtype))
        a, b = o0, o0 + n
        for d in range(a // IN_SHARD, (b - 1) // IN_SHARD + 1):
            parts.append(w3[:, d, max(a, d * IN_SHARD) - d * IN_SHARD:min(b, (d + 1) * IN_SHARD) - d * IN_SHARD])
        pos = p0 + n
    parts.append(jnp.zeros((r, D_IN_PAD - pos), w3.dtype))
    return jnp.concatenate(parts, -1)


def _unperm_in_to_shards(gp):
    r = gp.shape[0]
    shards = []
    for d in range(N_DEV):
        a, b = d * IN_SHARD, (d + 1) * IN_SHARD
        parts = [gp[:, p0 + max(a, o0) - o0:p0 + min(b, o0 + n) - o0] for o0, n, p0 in IN_SEGMENTS
                 if max(a, o0) < min(b, o0 + n)]
        parts.append(jnp.zeros((r, _lane_pad(IN_SHARD) - IN_SHARD), gp.dtype))
        shards.append(jnp.concatenate(parts, -1))
    return jnp.stack(shards, 0)


def _lane_pad(c):
    return -(-c // LANES) * LANES


def _packed_cols(c, by_col):
    return _lane_pad(c) if by_col else c


def _pack_shards(group, shards, dtype):
    rows = _pack_rows(group)
    segs = []
    for name, r, c, by_col in group:
        s = shards[name].reshape(r, c).astype(dtype)
        segs.append(jnp.pad(s, ((0, 0), (0, _packed_cols(c, by_col) - c))).reshape(-1))
    flat = jnp.concatenate(segs)
    flat = jnp.pad(flat, (0, rows * LANES - flat.shape[0]))
    return flat.reshape(rows, LANES)


def _unpack_shards(group, packed):
    flat = packed.reshape(-1)
    out, off = {}, 0
    for name, r, c, by_col in group:
        cp = _packed_cols(c, by_col)
        out[name] = flat[off:off + r * cp].reshape(1, r, cp)[:, :, :c]
        off += r * cp
    return out


def _unpack_full(group, gathered):
    flat = gathered.reshape(N_DEV, -1)
    out, off = {}, 0
    for name, r, c, by_col in group:
        cp = _packed_cols(c, by_col)
        blk = flat[:, off:off + r * cp].reshape(N_DEV, r, cp)
        out[name] = blk.transpose(1, 0, 2) if by_col else blk.reshape(N_DEV * r, c)
        off += r * cp
    return out


def _full_cols(w3, c):
    return w3[:, :, :c].reshape(w3.shape[0], N_DEV * c)


def _pack_full(group, grads, dtype):
    rows = _pack_rows(group)
    segs = []
    for name, r, c, by_col in group:
        g = grads[name]
        cp = _packed_cols(c, by_col)
        if by_col and g.ndim == 2:
            g = jnp.pad(g.reshape(r, N_DEV, c), ((0, 0), (0, 0), (0, cp - c))).transpose(1, 0, 2)
        segs.append(g.reshape(N_DEV, r * cp).astype(dtype))
    flat = jnp.concatenate(segs, axis=1)
    flat = jnp.pad(flat, ((0, 0), (0, rows * LANES - flat.shape[1])))
    return flat.reshape(N_DEV, rows, LANES)


SMALL_LAYOUT = (("ln_pre_mix", 1024, 0), ("ln_post_mix", 1024, 8), ("ln_pre_mlp", 1024, 16),
                ("ln_post_mlp", 1024, 24), ("b_in", 4008, 32), ("q_a_norm", 256, 64), ("kv_a_norm", 128, 66))


def _pack_small(vals, extra=None):
    rows = []
    for name, n, _ in SMALL_LAYOUT:
        v = vals[name].reshape(-1).astype(F32)
        pad = -n % LANES
        rows.append(jnp.pad(v, (0, pad)).reshape(-1, LANES))
    rows.append(jnp.zeros((SMALL_LOSS_ROW - 67, LANES), F32))
    rows.append(jnp.zeros((8, LANES), F32) if extra is None else extra.reshape(8, LANES))
    return jnp.concatenate(rows, axis=0)


def _unpack_small(packed):
    out = {}
    for name, n, r0 in SMALL_LAYOUT:
        nr = -(-n // LANES)
        out[name] = packed[r0:r0 + nr].reshape(-1)[:n].reshape(1, n)
    return out


def _rms(xf, g):
    r = lax.rsqrt(jnp.mean(xf * xf, axis=-1, keepdims=True) + NORM_EPS)
    return (xf * r) * g


def _rms_bwd(xf, g, dy):
    r = lax.rsqrt(jnp.mean(xf * xf, axis=-1, keepdims=True) + NORM_EPS)
    xhat = xf * r
    dxhat = dy * g
    dx = r * (dxhat - xhat * jnp.mean(dxhat * xhat, axis=-1, keepdims=True))
    return dx, dy * xhat


def _sigmoid(t):
    return 1.0 / (1.0 + jnp.exp(-t))


def _log_sigmoid(t):
    return jnp.minimum(t, 0.0) - jnp.log(1.0 + jnp.exp(-jnp.abs(t)))


def _lane_sign():
    lane = lax.broadcasted_iota(jnp.int32, (1, LANES), 1)
    return jnp.where(lane < 64, -1.0, 1.0).astype(F32), lane


def _rope_lanes(lane):
    return (lane < 16) | ((lane >= 64) & (lane < 80))


def kernel(x, positions, ln_pre_mix, ln_post_mix, ln_pre_mlp, ln_post_mlp, w_in, b_in, q_a_norm, w_uq, kv_a_norm, w_uk, w_uv, w_o_mla, w_o_fox, w_out, w_ff1, w_ff2, loss_target, m_ln_pre_mix, m_ln_post_mix, m_ln_pre_mlp, m_ln_post_mlp, m_w_in, m_b_in, m_q_a_norm, m_w_uq, m_kv_a_norm, m_w_uk, m_w_uv, m_w_o_mla, m_w_o_fox, m_w_out, m_w_ff1, m_w_ff2, v_ln_pre_mix, v_ln_post_mix, v_ln_pre_mlp, v_ln_post_mlp, v_w_in, v_b_in, v_q_a_norm, v_w_uq, v_kv_a_norm, v_w_uk, v_w_uv, v_w_o_mla, v_w_o_fox, v_w_out, v_w_ff1, v_w_ff2):
    T = x.shape[1]
    x2 = x.reshape(T, D_MODEL)
    tgt = loss_target.reshape(T, D_MODEL)
    w_sh = dict(w_in=w_in, w_uq=w_uq, w_uk=w_uk, w_uv=w_uv, w_o_mla=w_o_mla, w_o_fox=w_o_fox, w_out=w_out,
                w_ff1=w_ff1, w_ff2=w_ff2)
    m_sh = dict(w_in=m_w_in, w_uq=m_w_uq, w_uk=m_w_uk, w_uv=m_w_uv, w_o_mla=m_w_o_mla, w_o_fox=m_w_o_fox,
                w_out=m_w_out, w_ff1=m_w_ff1, w_ff2=m_w_ff2)
    v_sh = dict(w_in=v_w_in, w_uq=v_w_uq, w_uk=v_w_uk, w_uv=v_w_uv, w_o_mla=v_w_o_mla, w_o_fox=v_w_o_fox,
                w_out=v_w_out, w_ff1=v_w_ff1, w_ff2=v_w_ff2)
    small_w = dict(ln_pre_mix=ln_pre_mix, ln_post_mix=ln_post_mix, ln_pre_mlp=ln_pre_mlp, ln_post_mlp=ln_post_mlp,
                   b_in=b_in, q_a_norm=q_a_norm, kv_a_norm=kv_a_norm)
    small_m = dict(ln_pre_mix=m_ln_pre_mix, ln_post_mix=m_ln_post_mix, ln_pre_mlp=m_ln_pre_mlp,
                   ln_post_mlp=m_ln_post_mlp, b_in=m_b_in, q_a_norm=m_q_a_norm, kv_a_norm=m_kv_a_norm)
    small_v = dict(ln_pre_mix=v_ln_pre_mix, ln_post_mix=v_ln_post_mix, ln_pre_mlp=v_ln_pre_mlp,
                   ln_post_mlp=v_ln_post_mlp, b_in=v_b_in, q_a_norm=v_q_a_norm, kv_a_norm=v_kv_a_norm)
    mla_scale = float((NOPE + ROPE) ** -0.5)
    fox_scale = float(FOX_D ** -0.5)

    W = _unpack_full(GROUP_A, _gather_via_sibling(_pack_shards(GROUP_A, w_sh, BF16), "allgather_weights_a"))
    w_in_p = _perm_in_from_shards(W["w_in"])
    b_in_p = _perm_in_cols(b_in.astype(F32))
    w_uq_a = _aug_uq_cols(_full_cols(W["w_uq"], 96))
    w_kv_a = jnp.concatenate([_aug_uk_cols(_full_cols(W["w_uk"], 64)), _full_cols(W["w_uv"], 64)], axis=1)
    g1, g2, g3, g4 = ln_pre_mix, ln_post_mix, ln_pre_mlp, ln_post_mlp
    gq, gkv = q_a_norm, kv_a_norm
    tq = min(ATTN_TILE, T)
    nq = T // tq

    (h,) = _rowwise(lambda xv, g: _rms(xv, g), [x2], [g1], [(D_MODEL, BF16)], name="pre_mix_norm")
    q_cols = jnp.ones((1, D_IN_PAD), F32).at[:, P_FQ:P_FQ + 512].set(fox_scale * LOG2E)
    z = _mm(h, w_in_p, bias=b_in_p, colscale=q_cols, tm=2048, name="in_proj")
    zs = _mm(h, w_in_p[:, P_SMALL:], bias=b_in_p[:, P_SMALL:], outs=(F32,), name="in_proj_small")

    def lora_norm(cq, ckv, a, b):
        return _rms(cq.astype(F32), a), _rms(ckv.astype(F32), b)

    cqn, ckvn = _rowwise(lora_norm, [(z, 256, P_CQ // 256), (z, 128, P_CKV // 128)], [gq, gkv],
                         [(Q_LORA, BF16), (KV_LORA, BF16)], name="lora_norm")
    q_x = _mm(cqn, w_uq_a, outs=(F32,), name="q_up")
    kv = _mm(ckvn, w_kv_a, tn=512, name="kv_up")

    half = ROPE // 2
    inv_freq = ROPE_THETA ** (-jnp.arange(half, dtype=F32) / half)
    inv128 = jnp.tile(inv_freq, LANES // half).reshape(1, LANES)
    pos_col = positions.reshape(T, 1).astype(F32)

    def rope_tables(p, f):
        ang = p * f
        return jnp.cos(ang), jnp.sin(ang)

    cos_t, sin_t = _rowwise(rope_tables, [pos_col], [inv128], [(LANES, F32), (LANES, F32)], name="rope_tables")

    def rope_fwd(qx, kn, vn, s, cs, sn):
        sign, lane = _lane_sign()
        rope_l = _rope_lanes(lane)
        rot = lambda t: t * cs + pltpu.roll(t, 64, 1) * sn * sign
        k_rot = jnp.where(rope_l, rot(s), 0.0)
        qs, ks = [], []
        for hd in range(N_HEADS):
            qb = qx[:, LANES * hd:LANES * (hd + 1)]
            qs.append(jnp.where(rope_l, rot(qb), qb))
            ks.append(kn[:, LANES * hd:LANES * (hd + 1)].astype(F32) + k_rot)
        return jnp.concatenate(qs, axis=1) * (mla_scale * LOG2E), jnp.concatenate(ks, axis=1), vn

    qt4_m, k_am, vt4_m = _rowwise(rope_fwd, [q_x, (kv, 1024, 0), (kv, 512, 2), zs, cos_t, sin_t], [],
                                  [(1024, BF16, "t4"), (1024, BF16), (512, BF16, "t4")], name="rope_fwd", tm=tq)
    f_cum = _seq_cumsum(zs, _log_sigmoid, reverse=False, name="forget_cumsum")

    def fox_aug(qf, kf, vf, fc):
        lane = lax.broadcasted_iota(jnp.int32, (1, LANES), 1)
        qs, ks = [], []
        for hd in range(N_HEADS):
            qp = qf[:, LANES * (hd // 2):LANES * (hd // 2 + 1)].astype(F32)
            kp = kf[:, LANES * (hd // 2):LANES * (hd // 2 + 1)].astype(F32)
            if hd % 2:
                qp = pltpu.roll(qp, 64, 1)
                kp = pltpu.roll(kp, 64, 1)
            fb = jnp.broadcast_to(fc[:, S_FL + hd:S_FL + hd + 1] * (-LOG2E), qp.shape)
            hi = fb.astype(BF16).astype(F32)
            mid = (fb - hi).astype(BF16).astype(F32)
            lo = fb - hi - mid
            qs.append(jnp.where(lane < 64, qp, jnp.where(lane < 67, 1.0, 0.0)))
            ks.append(jnp.where(lane < 64, kp, jnp.where(lane == 64, hi, jnp.where(lane == 65, mid, jnp.where(
                lane == 66, lo, jnp.where(lane == 67, 1.0, 0.0))))))
        return jnp.concatenate(qs, axis=1), jnp.concatenate(ks, axis=1), vf

    qt4_f, k_af, vt4_f = _rowwise(fox_aug, [(z, 512, P_FQ // 512), (z, 512, P_FK // 512), (z, 512, P_FV // 512), f_cum],
                                  [], [(1024, BF16, "t4"), (1024, BF16), (512, BF16, "t4")], name="fox_aug", tm=tq)
    o_mla, lse_mla, gathered_b = _attn_fwd(qt4_m, k_am, vt4_m, name="mla_attn_fwd", tq=tq,
                                           gather=_pack_shards(GROUP_B, w_sh, BF16))
    W.update(_unpack_full(GROUP_B, gathered_b))
    for n, _, c, by_col in GROUP_B:
        if by_col:
            W[n] = _full_cols(W[n], c)
    o_fox, lse_fox = _attn_fwd(qt4_f, k_af, vt4_f, name="fox_attn_fwd", tq=tq)
    y_mla = _mm(o_mla, W["w_o_mla"], outs=(F32,), name="o_proj_mla")
    y_fox = _mm(o_fox, W["w_o_fox"], outs=(F32,), name="o_proj_fox")

    def gate_merge(ga, gb, ya, yb):
        return _sigmoid(ga.astype(F32)) * ya + _sigmoid(gb.astype(F32)) * yb

    (merged,) = _rowwise(gate_merge, [(z, 1024, 0), (z, 1024, 1), y_mla, y_fox], [], [(D_MODEL, BF16)],
                         name="gate_merge")
    mix = _mm(merged, W["w_out"], outs=(F32,), name="out_proj")

    def post_mix(xv, mv, a, b):
        x1v = xv + _rms(mv, a)
        return x1v, _rms(x1v, b)

    x1, h2 = _rowwise(post_mix, [x2, mix], [g2, g3], [(D_MODEL, F32), (D_MODEL, BF16)], name="post_mix_norm")

    def relu2(acc):
        r = jnp.maximum(acc, 0.0)
        return r * r, acc

    act, u = _mm(h2, W["w_ff1"], epi=relu2, outs=(BF16, BF16), tm=2048, name="ff1")
    mlp = _mm(act, W["w_ff2"], outs=(F32,), tm=1024, tk=2048, name="ff2")

    def loss_bwd(x1v, mv, tv, g):
        y = x1v + _rms(mv, g)
        d = y - tv
        dy = d * (1.0 / D_MODEL)
        dm, dg = _rms_bwd(mv, g, dy)
        return dy, dm, dg, d * d

    dy, dm, dg4, loss_cols = _rowwise(loss_bwd, [x1, mlp, tgt], [g4], [(D_MODEL, F32), (D_MODEL, BF16)],
                                      [D_MODEL, D_MODEL], name="loss_bwd")

    def relu2_bwd(acc, uv):
        return (acc * (2.0 * jnp.maximum(uv.astype(F32), 0.0)),)

    du = _mm(dm, W["w_ff2"], tb=True, extras=(u,), epi=relu2_bwd, tm=2048, name="ff2_bwd")
    dw_ff2 = _mm(act, dm, ta=True, outs=(BF16,), tk=2048, name="ff2_wgrad")
    dh2 = _mm(du, W["w_ff1"], tb=True, outs=(F32,), tm=1024, tk=2048, name="ff1_bwd")
    dw_ff1 = _mm(h2, du, ta=True, outs=(BF16,), tk=2048, name="ff1_wgrad")

    def post_mix_bwd(x1v, dh2v, dyv, mv, a, b):
        d3, dg3v = _rms_bwd(x1v, b, dh2v)
        dx1v = dyv + d3
        dmixv, dg2v = _rms_bwd(mv, a, dx1v)
        return dx1v, dmixv, dg3v, dg2v

    dx1, dmix, dg3, dg2 = _rowwise(post_mix_bwd, [x1, dh2, dy, mix], [g2, g3], [(D_MODEL, F32), (D_MODEL, BF16)],
                                   [D_MODEL, D_MODEL], name="post_mix_bwd")
    dmerged = _mm(dmix, W["w_out"], tb=True, outs=(F32,), name="out_proj_bwd")
    dw_out = _mm(merged, dmix, ta=True, outs=(BF16,), name="out_proj_wgrad")

    def gate_bwd(dmg, ga, gb, ya, yb):
        sa = _sigmoid(ga.astype(F32))
        sb = _sigmoid(gb.astype(F32))
        return dmg * sa, dmg * sb, dmg * ya * sa * (1.0 - sa), dmg * yb * sb * (1.0 - sb)

    dy_mla, dy_fox, dga, dgb = _rowwise(gate_bwd, [dmerged, (z, 1024, 0), (z, 1024, 1), y_mla, y_fox], [],
                                        [(D_MODEL, BF16)] * 4, name="gate_bwd")
    twice = lambda acc: (acc, acc)
    do_mla, dot4_m = _mm(dy_mla, W["w_o_mla"], tb=True, epi=twice, outs=(BF16, BF16), t4=(False, True), tm=tq,
                         name="o_proj_mla_bwd")
    do_fox, dot4_f = _mm(dy_fox, W["w_o_fox"], tb=True, epi=twice, outs=(BF16, BF16), t4=(False, True), tm=tq,
                         name="o_proj_fox_bwd")
    dw_o_mla = _mm(o_mla, dy_mla, ta=True, outs=(BF16,), name="o_proj_mla_wgrad")
    dw_o_fox = _mm(o_fox, dy_fox, ta=True, outs=(BF16,), name="o_proj_fox_wgrad")

    def head_dots(ov, dov):
        r = lax.broadcasted_iota(jnp.int32, (N_HEADS * HEAD_V, LANES), 0)
        cc = lax.broadcasted_iota(jnp.int32, (N_HEADS * HEAD_V, LANES), 1)
        sel = jnp.where((r // HEAD_V) == cc, 1.0, 0.0).astype(BF16)
        return _dot3(ov.astype(F32) * dov.astype(F32), sel, left=False)

    (dl_mla,) = _rowwise(head_dots, [o_mla, do_mla], [], [(LANES, F32)], name="mla_attn_delta")
    (dl_fox,) = _rowwise(head_dots, [o_fox, do_fox], [], [(LANES, F32)], name="fox_attn_delta")
    heads_row4 = lambda t: t[:, :N_HEADS].T.reshape(N_HEADS, nq, 1, tq)
    grads_b = dict(w_o_mla=dw_o_mla, w_o_fox=dw_o_fox, w_out=dw_out, w_ff1=dw_ff1, w_ff2=dw_ff2)
    dqt_m, dkt_m, dv_mla, pieces_b = _attn_bwd(qt4_m, k_am, kv, 1024 // LANES, dot4_m, lse_mla,
                                               heads_row4(dl_mla), mla_scale, NOPE + ROPE, name="mla_attn_bwd", tq=tq,
                                               exchange=_pack_full(GROUP_B, grads_b, BF16))
    dqt_f, dkt_f, dv_fox = _attn_bwd(qt4_f, k_af, z, P_FV // LANES, dot4_f, lse_fox, heads_row4(dl_fox),
                                     fox_scale, FOX_D, name="fox_attn_bwd", tq=tq)

    def fox_unpack(dqa, dka):
        lane = lax.broadcasted_iota(jnp.int32, (1, LANES), 1)
        dqs, dks = [], []
        d_f = jnp.zeros(dqa[:, :LANES].shape, F32)
        for hp in range(N_HEADS // 2):
            blk = lambda t, e: t[:, LANES * (2 * hp + e):LANES * (2 * hp + e + 1)]
            dqs.append(jnp.where(lane < 64, blk(dqa, 0), pltpu.roll(blk(dqa, 1), 64, 1)))
            dks.append(jnp.where(lane < 64, blk(dka, 0), pltpu.roll(blk(dka, 1), 64, 1)))
            for e in range(2):
                g = blk(dqa, e)[:, 67:68] - blk(dka, e)[:, 64:65]
                d_f = jnp.where(lane == S_FL + 2 * hp + e, g, d_f)
        return jnp.concatenate(dqs, axis=1), jnp.concatenate(dks, axis=1), d_f

    dq_fox, dk_fox, d_f128 = _rowwise(fox_unpack, [("t4", dqt_f), ("t4", dkt_f)], [],
                                      [(512, BF16), (512, BF16), (LANES, F32)], name="fox_unpack", tm=tq)
    df_rev = _seq_cumsum(d_f128, lambda t: t, reverse=True, name="forget_cumsum_bwd")

    def rope_bwd(dqa, dka, dvm, dfr, s, cs, sn):
        sign, lane = _lane_sign()
        rope_l = _rope_lanes(lane)
        rot_t = lambda t: t * cs - pltpu.roll(t, 64, 1) * sn * sign
        dqs = []
        dk_rot = None
        for hd in range(N_HEADS):
            blk = dqa[:, LANES * hd:LANES * (hd + 1)]
            dqs.append(jnp.where(rope_l, rot_t(blk), blk))
            blk = dka[:, LANES * hd:LANES * (hd + 1)]
            dk_rot = blk if dk_rot is None else dk_rot + blk
        dfl = dfr * _sigmoid(-s)
        small = jnp.where(rope_l, rot_t(dk_rot), jnp.where((lane >= S_FL) & (lane < S_FL + N_HEADS), dfl, 0.0))
        return jnp.concatenate(dqs, axis=1), jnp.concatenate([dka, dvm.astype(F32)], axis=1), small

    dq_b, dkv, d_small = _rowwise(rope_bwd, [("t4", dqt_m), ("t4", dkt_m), dv_mla, df_rev, zs, cos_t, sin_t], [],
                                  [(1024, BF16), (1536, BF16), (LANES, BF16)], name="rope_bwd", tm=tq)
    dcqn = _mm(dq_b, w_uq_a, tb=True, outs=(F32,), name="q_up_bwd")
    dw_uq_a = _mm(cqn, dq_b, ta=True, outs=(BF16,), name="q_up_wgrad")
    dckvn = _mm(dkv, w_kv_a, tb=True, outs=(F32,), tk=512, name="kv_up_bwd")
    dw_kv_a = _mm(ckvn, dkv, ta=True, outs=(BF16,), tn=512, name="kv_up_wgrad")

    def lora_norm_bwd(cq, ckv, dq_, dkv_, a, b):
        d1, dga_ = _rms_bwd(cq.astype(F32), a, dq_)
        d2, dgb_ = _rms_bwd(ckv.astype(F32), b, dkv_)
        return d1, d2, dga_, dgb_

    dcq, dckv, dgq, dgkv = _rowwise(lora_norm_bwd, [(z, 256, P_CQ // 256), (z, 128, P_CKV // 128), dcqn, dckvn],
                                    [gq, gkv], [(Q_LORA, BF16), (KV_LORA, BF16)], [Q_LORA, KV_LORA],
                                    name="lora_norm_bwd")
    dz = jnp.concatenate([dga, dgb, dq_fox, dk_fox, dv_fox, dcq, dckv, d_small], -1)
    dw_in_p, db_in_p = _mm(h, dz, ta=True, outs=(BF16,), tk=2048, colsum=True, name="in_proj_wgrad")
    grads_a = dict(w_in=_unperm_in_to_shards(dw_in_p), w_uq=_unaug_uq_cols(dw_uq_a),
                   w_uk=_unaug_uk_cols(dw_kv_a[:, :1024]), w_uv=dw_kv_a[:, 1024:])
    dh, pieces_a = _mm(dz, w_in_p, tb=True, outs=(F32,), tm=1024, tk=2048, name="in_proj_bwd",
                       exchange=_pack_full(GROUP_A, grads_a, BF16))

    def pre_mix_bwd(xv, dhv, dx1v, g):
        d, dg = _rms_bwd(xv, g, dhv)
        return dx1v + d, dg

    grad_x, dg1 = _rowwise(pre_mix_bwd, [x2, dh, dx1], [g1], [(D_MODEL, F32)], [D_MODEL], name="pre_mix_bwd")

    grad_sh, delta_sh, newm_sh, newv_sh = {}, {}, {}, {}
    for group, pieces, tag in ((GROUP_A, pieces_a, "a"), (GROUP_B, pieces_b, "b")):
        packed = _sum_adamw(pieces, _pack_shards(group, w_sh, F32), _pack_shards(group, m_sh, F32),
                            _pack_shards(group, v_sh, F32), name="sum_pieces_adamw_" + tag)
        for dst, arr in zip((grad_sh, delta_sh, newm_sh, newv_sh), packed):
            dst.update(_unpack_shards(group, arr))

    small_part = _pack_small(dict(ln_pre_mix=dg1, ln_post_mix=dg2, ln_pre_mlp=dg3, ln_post_mlp=dg4,
                                  b_in=_unperm_in_cols(db_in_p), q_a_norm=dgq, kv_a_norm=dgkv), extra=loss_cols)
    sg, sd, sm, sv, loss_blk = _small_allreduce_adamw(small_part, _pack_small(small_w), _pack_small(small_m),
                                                      _pack_small(small_v))
    grad_sm, delta_sm, newm_sm, newv_sm = (_unpack_small(t) for t in (sg, sd, sm, sv))
    loss = loss_blk[0, 0]

    order = ["ln_pre_mix", "ln_post_mix", "ln_pre_mlp", "ln_post_mlp", "w_in", "b_in", "q_a_norm", "w_uq",
             "kv_a_norm", "w_uk", "w_uv", "w_o_mla", "w_o_fox", "w_out", "w_ff1", "w_ff2"]

    def pick(sm_d, sh_d):
        return [sm_d[n] if n in sm_d else sh_d[n] for n in order]

    return (loss, grad_x.reshape(1, T, D_MODEL), *pick(grad_sm, grad_sh), *pick(delta_sm, delta_sh),
            *pick(newm_sm, newm_sh), *pick(newv_sm, newv_sh))
```

```python
import numpy as np
import jax
import jax.numpy as jnp
from jax import lax
from jax.experimental import pallas as pl
from jax.experimental.pallas import tpu as pltpu

F32 = jnp.float32
BF16 = jnp.bfloat16
MESH = pl.DeviceIdType.MESH

D_MODEL = 1024
N_HEADS = 8
Q_LORA = 256
KV_LORA = 128
NOPE = 64
ROPE = 32
HEAD_V = 64
FOX_D = 64
D_FF = 4096
D_IN = 4008
D_IN_PAD = 4096
ROPE_THETA = 10000.0
NORM_EPS = 1e-6
N_DEV = 8

ADAM_LR = 0.001
ADAM_B1 = 0.9
ADAM_B2 = 0.999
ADAM_EPS = 1e-08
ADAM_WD = 0.01
ADAM_STEP = 10

LANES = 128
ROW_TILE = 512
ATTN_TILE = 512
ATTN_CHUNK = 256
BWD_UNROLL = 4
VMEM_LIMIT = 48 * 1024 * 1024
ATTN_BWD_VMEM_LIMIT = 58 * 1024 * 1024

P_GA, P_GB, P_FQ, P_FK, P_FV, P_CQ, P_CKV, P_SMALL = 0, 1024, 2048, 2560, 3072, 3584, 3840, 3968
S_FL = 32

SHARDED = (
    ("w_in", 1024, 501, True), ("w_uq", 256, 96, True), ("w_uk", 128, 64, True), ("w_uv", 128, 64, True),
    ("w_o_mla", 512, 128, True), ("w_o_fox", 512, 128, True), ("w_out", 128, 1024, False),
    ("w_ff1", 1024, 512, True), ("w_ff2", 512, 1024, False),
)
GROUP_A = SHARDED[:4]
GROUP_B = SHARDED[4:]
SMALL_ROWS = 80
SMALL_LOSS_ROW = 72


def _pack_rows(group):
    return -(-sum(r * _packed_cols(c, by_col) for _, r, c, by_col in group) // (LANES * 64)) * 64


def _cparams(sem=None):
    return pltpu.CompilerParams(dimension_semantics=sem, vmem_limit_bytes=VMEM_LIMIT)


def _mm(a, b, *, name, ta=False, tb=False, bias=None, colscale=None, extras=(), epi=None, outs=(BF16,), t4=None,
        tm=ROW_TILE, tn=1024, tk=1024, exchange=None, colsum=False, tail_f32=0):
    t4 = (False,) * len(outs) if t4 is None else t4
    comm = exchange is not None
    if ta:
        K, M = a.shape
        tm = min(1024, M)
    else:
        M, K = a.shape
        tm = min(tm, M)
    tk = min(tk, K)
    N = b.shape[0] if tb else b.shape[1]
    tn = min(tn, N)
    nk = K // tk
    assert not colsum or (ta and M == tm)
    n_ex = len(extras)
    has_bias = bias is not None
    has_scale = colscale is not None

    def body(*refs):
        a_ref, b_ref = refs[0], refs[1]
        pos = 2
        bias_ref = scale_ref = None
        if has_bias:
            bias_ref = refs[pos]
            pos += 1
        if has_scale:
            scale_ref = refs[pos]
            pos += 1
        ex_refs = refs[pos:pos + n_ex]
        pos += n_ex
        if comm:
            c_src = refs[pos]
            pos += 1
        o_refs = refs[pos:pos + len(outs)]
        pos += len(outs)
        if tail_f32:
            tail_ref = refs[pos]
            pos += 1
        if colsum:
            cs_ref = refs[pos]
            pos += 1
        if comm:
            c_dst = refs[pos]
            pos += 1
            c_sems = refs[len(refs) - 3:]
            ids = [pl.program_id(d) for d in range(3)]

            @pl.when((ids[0] == 0) & (ids[1] == 0) & (ids[2] == 0))
            def _():
                _comm_start("exchange", c_src, c_dst, *c_sems)

        av = a_ref[...].astype(BF16)
        bv = b_ref[...].astype(BF16)
        if ta:
            part = lax.dot_general(av, bv, (((0,), (0,)), ((), ())), preferred_element_type=F32)
        elif tb:
            part = lax.dot_general(av, bv, (((1,), (1,)), ((), ())), preferred_element_type=F32)
        else:
            part = jnp.dot(av, bv, preferred_element_type=F32)
        if colsum:
            cs = jnp.sum(bv.astype(F32), axis=0, keepdims=True)

            @pl.when(pl.program_id(2) == 0)
            def _():
                cs_ref[...] = cs

            @pl.when(pl.program_id(2) > 0)
            def _():
                cs_ref[...] += cs

        def finish(acc):
            if has_bias:
                acc = acc + bias_ref[...]
            if has_scale:
                acc = acc * scale_ref[...]
            if tail_f32:
                @pl.when(pl.program_id(1) == N // tn - 1)
                def _():
                    tail_ref[...] = acc[:, tn - tail_f32:]

            res = (acc,) if epi is None else epi(acc, *[r[...] for r in ex_refs])
            for o_ref, val, t in zip(o_refs, res, t4):
                if t:
                    _t4_store(o_ref, val)
                else:
                    o_ref[...] = val.astype(o_ref.dtype)

        if nk == 1:
            finish(part)
        else:
            acc_ref = refs[pos]
            k = pl.program_id(2)

            @pl.when(k == 0)
            def _():
                acc_ref[...] = part

            @pl.when(k > 0)
            def _():
                acc_ref[...] += part

            @pl.when(k == nk - 1)
            def _():
                finish(acc_ref[...])

        if comm:
            @pl.when((ids[0] == M // tm - 1) & (ids[1] == N // tn - 1) & (ids[2] == nk - 1))
            def _():
                _comm_wait("exchange", c_src, c_dst, *c_sems)

    if ta:
        a_spec = pl.BlockSpec((tk, tm), lambda i, j, k: (k, i))
    else:
        a_spec = pl.BlockSpec((tm, tk), lambda i, j, k: (i, k))
    b_spec = pl.BlockSpec((tn, tk), lambda i, j, k: (j, k)) if tb else pl.BlockSpec((tk, tn), lambda i, j, k: (k, j))
    in_specs = [a_spec, b_spec]
    args = [a, b]
    for row in (bias, colscale):
        if row is not None:
            in_specs.append(pl.BlockSpec((1, tn), lambda i, j, k: (0, j)))
            args.append(row)
    for e in extras:
        in_specs.append(pl.BlockSpec((tm, tn), lambda i, j, k: (i, j)))
        args.append(e)
    out_specs = [pl.BlockSpec((N // LANES, 1, LANES, tm), lambda i, j, k: (0, i, 0, 0)) if t
                 else pl.BlockSpec((tm, tn), lambda i, j, k: (i, j)) for t in t4]
    out_shape = [jax.ShapeDtypeStruct((N // LANES, M // tm, LANES, tm) if t else (M, N), dt)
                 for dt, t in zip(outs, t4)]
    scratch = [pltpu.VMEM((tm, tn), F32)] if nk > 1 else []
    if tail_f32:
        assert nk == 1
        out_specs.append(pl.BlockSpec((tm, tail_f32), lambda i, j, k: (i, 0)))
        out_shape.append(jax.ShapeDtypeStruct((M, tail_f32), F32))
    if colsum:
        out_specs.append(pl.BlockSpec((1, tn), lambda i, j, k: (0, j)))
        out_shape.append(jax.ShapeDtypeStruct((1, N), F32))
    if comm:
        in_specs.append(pl.BlockSpec(memory_space=pl.ANY))
        args.append(exchange)
        out_specs.append(pl.BlockSpec(memory_space=pl.ANY))
        out_shape.append(jax.ShapeDtypeStruct(exchange.shape, exchange.dtype))
        scratch += _comm_sems()
    res = pl.pallas_call(
        body, name=name, grid=(M // tm, N // tn, nk),
        in_specs=in_specs, out_specs=out_specs, out_shape=out_shape, scratch_shapes=scratch,
        compiler_params=_cparams(("arbitrary",) * 3 if comm or tail_f32 else ("parallel", "parallel", "arbitrary")),
    )(*args)
    return res[0] if len(res) == 1 else res


def _t4_store(o_ref, val):
    for c in range(o_ref.shape[0]):
        o_ref[c, 0] = val[:, c * LANES:(c + 1) * LANES].astype(F32).T.astype(o_ref.dtype)


def _rowwise(fn, rows, bcasts, outs, accs=(), *, name, tm=ROW_TILE):
    t4_in = [isinstance(r, tuple) and isinstance(r[0], str) for r in rows]
    T = [r[1].shape[1] * r[1].shape[3] if t else (r[0] if isinstance(r, tuple) else r).shape[0]
         for r, t in zip(rows, t4_in)][0]
    tm = min(tm, T)
    arrs, specs = [], []
    for r, t in zip(rows, t4_in):
        if t:
            arr = r[1]
            specs.append(pl.BlockSpec((arr.shape[0], 1, LANES, tm), lambda i: (0, i, 0, 0)))
        elif isinstance(r, tuple):
            arr, w, cb = r
            specs.append(pl.BlockSpec((tm, w), lambda i, cb=cb: (i, cb)))
        else:
            arr = r
            specs.append(pl.BlockSpec((tm, arr.shape[1]), lambda i: (i, 0)))
        arrs.append(arr)
    n_rows = len(arrs)
    for b in bcasts:
        arrs.append(b)
        specs.append(pl.BlockSpec(b.shape, lambda i: (0, 0)))
    n_in, n_out = len(arrs), len(outs)
    t4_out = [len(o) == 3 for o in outs]

    def body(*refs):
        vals = []
        for k, r in enumerate(refs[:n_in]):
            if k < n_rows and t4_in[k]:
                vals.append(jnp.concatenate([r[c, 0].astype(F32).T for c in range(r.shape[0])], axis=1))
            else:
                vals.append(r[...])
        res = fn(*vals)
        if not isinstance(res, (tuple, list)):
            res = (res,)
        for o_ref, val, t in zip(refs[n_in:n_in + n_out], res[:n_out], t4_out):
            if t:
                _t4_store(o_ref, val)
            else:
                o_ref[...] = val.astype(o_ref.dtype)
        i = pl.program_id(0)
        for a_ref, val in zip(refs[n_in + n_out:], res[n_out:]):
            col = jnp.sum(val.astype(F32), axis=0, keepdims=True)

            @pl.when(i == 0)
            def _(a_ref=a_ref, col=col):
                a_ref[...] = col

            @pl.when(i > 0)
            def _(a_ref=a_ref, col=col):
                a_ref[...] += col

    res = pl.pallas_call(
        body, name=name, grid=(T // tm,),
        in_specs=specs,
        out_specs=[pl.BlockSpec((o[0] // LANES, 1, LANES, tm), lambda i: (0, i, 0, 0)) if t
                   else pl.BlockSpec((tm, o[0]), lambda i: (i, 0)) for o, t in zip(outs, t4_out)]
        + [pl.BlockSpec((1, c), lambda i: (0, 0)) for c in accs],
        out_shape=[jax.ShapeDtypeStruct((o[0] // LANES, T // tm, LANES, tm) if t else (T, o[0]), o[1])
                   for o, t in zip(outs, t4_out)]
        + [jax.ShapeDtypeStruct((1, c), F32) for c in accs],
        compiler_params=_cparams(("arbitrary",)),
    )(*arrs)
    return res


def _dot3(x, sel, left):
    hi = x.astype(BF16)
    r1 = x - hi.astype(F32)
    mid = r1.astype(BF16)
    lo = (r1 - mid.astype(F32)).astype(BF16)
    if left:
        d = lambda t: jnp.dot(sel, t, preferred_element_type=F32)
    else:
        d = lambda t: jnp.dot(t, sel, preferred_element_type=F32)
    return d(hi) + d(mid) + d(lo)


def _seq_cumsum(x, fn, *, reverse, name, tm=256):
    T, C = x.shape
    tm = min(tm, T)
    n = T // tm

    def body(x_ref, o_ref, carry_ref):
        i = pl.program_id(0)

        @pl.when(i == 0)
        def _():
            carry_ref[...] = jnp.zeros_like(carry_ref)

        v = fn(x_ref[...])
        r = lax.broadcasted_iota(jnp.int32, (tm, tm), 0)
        c = lax.broadcasted_iota(jnp.int32, (tm, tm), 1)
        tri = jnp.where((r <= c) if reverse else (r >= c), 1.0, 0.0).astype(BF16)
        carry = carry_ref[0:1, :]
        o_ref[...] = _dot3(v, tri, left=True) + carry
        carry_ref[0:1, :] = carry + jnp.sum(v, axis=0, keepdims=True)

    idx = (lambda i: (n - 1 - i, 0)) if reverse else (lambda i: (i, 0))
    return pl.pallas_call(
        body, name=name, grid=(n,),
        in_specs=[pl.BlockSpec((tm, C), idx)],
        out_specs=pl.BlockSpec((tm, C), idx),
        out_shape=jax.ShapeDtypeStruct((T, C), F32),
        scratch_shapes=[pltpu.VMEM((8, C), F32)],
        compiler_params=_cparams(("arbitrary",)),
    )(x)


_NT = (((1,), (1,)), ((), ()))
LOG2E = 1.4426950408889634


def _head_mask(width, e):
    lane = lax.broadcasted_iota(jnp.int32, (1, width), 1)
    return (lane >= 64 * e) & (lane < 64 * (e + 1))


def _attn_fwd(qt4, k_aug, vt4, *, name, tq, gather=None):
    nq = qt4.shape[1]
    T = nq * tq
    cw = min(ATTN_CHUNK, tq)
    comm = gather is not None

    def body(*refs):
        refs = list(refs)
        q_ref, k_ref, v_ref = refs[:3]
        pos = 3
        if comm:
            c_src = refs[pos]
            pos += 1
        o_ref, lse_ref = refs[pos:pos + 2]
        pos += 2
        if comm:
            c_dst = refs[pos]
            pos += 1
        m_s, l_s, acc_s = refs[pos:pos + 2], refs[pos + 2:pos + 4], refs[pos + 4:pos + 6]
        st_a, st_b = refs[pos + 6:pos + 8]
        c_sems = refs[pos + 8:]
        i = pl.program_id(1)
        if comm:
            @pl.when((pl.program_id(0) == 0) & (i == 0))
            def _():
                _comm_start("gather", c_src, c_dst, *c_sems)

        for e in range(2):
            m_s[e][...] = jnp.full(m_s[e].shape, -jnp.inf, F32)
            l_s[e][...] = jnp.zeros_like(l_s[e])
            acc_s[e][...] = jnp.zeros_like(acc_s[e])
        qt = [q_ref[e, 0] for e in range(2)]

        work = [(e, slice(c * cw, (c + 1) * cw)) for e in range(2) for c in range(tq // cw)]

        def scores(j, buf):
            kj = k_ref[pl.ds(pl.multiple_of(j * tq, tq), tq), :]
            for n, (e, cs) in enumerate(work):
                buf[n] = jnp.dot(kj[:, e * LANES:(e + 1) * LANES], qt[e][:, cs], preferred_element_type=F32)

        def step(j, buf, masked):
            vtj = v_ref[0, j]
            for n, (e, cs) in enumerate(work):
                kr = cs.stop if masked else tq
                st = buf[n][:kr]
                if masked:
                    r = lax.broadcasted_iota(jnp.int32, (kr, cw), 0)
                    cc = lax.broadcasted_iota(jnp.int32, (kr, cw), 1) + cs.start
                    st = jnp.where(cc >= r, st, -jnp.inf)
                m_prev = m_s[e][:, cs]
                m_new = jnp.maximum(m_prev, jnp.max(st, axis=0, keepdims=True))
                alpha = jnp.exp2(m_prev - m_new)
                pt = jnp.exp2(st - m_new)
                l_s[e][:, cs] = alpha * l_s[e][:, cs] + jnp.sum(pt, axis=0, keepdims=True)
                acc_s[e][:, cs] = alpha * acc_s[e][:, cs] + jnp.dot(vtj[e * HEAD_V:(e + 1) * HEAD_V, :kr],
                                                                    pt.astype(BF16), preferred_element_type=F32)
                m_s[e][:, cs] = m_new

        def two_tiles(p, carry):
            scores(2 * p + 1, st_b)
            step(2 * p, st_a, False)
            scores(2 * p + 2, st_a)
            step(2 * p + 1, st_b, False)
            return carry

        def four_tiles(p, carry):
            two_tiles(2 * p, carry)
            return two_tiles(2 * p + 1, carry)

        scores(0, st_a)
        lax.fori_loop(0, i // 4, four_tiles, 0)
        lax.fori_loop(2 * (i // 4), i // 2, two_tiles, 0)

        @pl.when(i % 2 == 0)
        def _():
            step(i, st_a, True)

        @pl.when(i % 2 == 1)
        def _():
            scores(i, st_b)
            step(i - 1, st_a, False)
            step(i, st_b, True)

        ot = jnp.concatenate([acc_s[e][...] / l_s[e][...] for e in range(2)], axis=0)
        o_ref[...] = ot.T.astype(o_ref.dtype)
        for e in range(2):
            lse_ref[e, 0] = m_s[e][...] + jnp.log(l_s[e][...]) * LOG2E
        if comm:
            @pl.when((pl.program_id(0) == N_HEADS // 2 - 1) & (i == nq - 1))
            def _():
                _comm_wait("gather", c_src, c_dst, *c_sems)

    in_specs = [
        pl.BlockSpec((2, 1, LANES, tq), lambda hp, i: (hp, i, 0, 0)),
        pl.BlockSpec((T, 2 * LANES), lambda hp, i: (0, hp)),
        pl.BlockSpec((1, nq, LANES, tq), lambda hp, i: (hp, 0, 0, 0)),
    ]
    args = [qt4, k_aug, vt4]
    out_specs = [pl.BlockSpec((tq, LANES), lambda hp, i: (i, hp)),
                 pl.BlockSpec((2, 1, 1, tq), lambda hp, i: (hp, i, 0, 0))]
    out_shape = [jax.ShapeDtypeStruct((T, N_HEADS * HEAD_V), BF16), jax.ShapeDtypeStruct((N_HEADS, nq, 1, tq), F32)]
    scratch = ([pltpu.VMEM((1, tq), F32)] * 4 + [pltpu.VMEM((HEAD_V, tq), F32)] * 2
               + [pltpu.VMEM((2 * tq // cw, tq, cw), F32)] * 2)
    if comm:
        in_specs.append(pl.BlockSpec(memory_space=pl.ANY))
        args.append(gather)
        out_specs.append(pl.BlockSpec(memory_space=pl.ANY))
        out_shape.append(jax.ShapeDtypeStruct((N_DEV,) + gather.shape, gather.dtype))
        scratch += _comm_sems()
    return pl.pallas_call(
        body, name=name, grid=(N_HEADS // 2, nq),
        in_specs=in_specs, out_specs=out_specs, out_shape=out_shape, scratch_shapes=scratch,
        compiler_params=_cparams(("arbitrary", "arbitrary")),
    )(*args)


def _attn_bwd(qt4, k_aug, v, vcb, dot4, lse_row, dl_row, scale, own_rows, *, name, tq, exchange=None):
    nq = qt4.shape[1]
    T = nq * tq
    cw = min(ATTN_CHUNK, tq)
    comm = exchange is not None

    def body(*refs):
        refs = list(refs)
        k_ref, v_ref, q_ref, do_ref, lse_ref, dl_ref = refs[:6]
        pos = 6
        if comm:
            c_src = refs[pos]
            pos += 1
        dq_ref, dk_ref, dv_ref = refs[pos:pos + 3]
        pos += 3
        if comm:
            c_dst = refs[pos]
            pos += 1
        dk_s, dv_s = refs[pos:pos + 2], refs[pos + 2:pos + 4]
        c_sems = refs[pos + 4:]
        j = pl.program_id(1)
        if comm:
            @pl.when((pl.program_id(0) == 0) & (j == 0))
            def _():
                _comm_start("exchange", c_src, c_dst, *c_sems)

        @pl.when(j == 0)
        def _():
            dq_ref[...] = jnp.zeros_like(dq_ref)

        kj = k_ref[...]
        vj = v_ref[...]
        ka = [kj[:, e * LANES:(e + 1) * LANES] for e in range(2)]
        row = lax.broadcasted_iota(jnp.int32, (LANES, 1), 0)
        own = row < own_rows
        kat = [(ka[e].astype(F32).T * jnp.where(own, scale, 1.0)).astype(BF16) for e in range(2)]
        vm = [jnp.where(_head_mask(LANES, e), vj, jnp.zeros_like(vj)) for e in range(2)]
        for e in range(2):
            dk_s[e][...] = jnp.zeros_like(dk_s[e])
            dv_s[e][...] = jnp.zeros_like(dv_s[e])

        def step(i, masked):
            dot_i = do_ref[0, i]
            qts = [q_ref[e, i] for e in range(2)]
            lse = [lse_ref[e, i] for e in range(2)]
            dl = [dl_ref[e, i] for e in range(2)]
            work = [(e, slice(c * cw, (c + 1) * cw)) for e in range(2) for c in range(tq // cw)]
            rows = lambda cs: cs.stop if masked else tq
            scores = lambda e, cs: (jnp.dot(ka[e][:rows(cs)], qts[e][:, cs], preferred_element_type=F32),
                                    jnp.dot(vm[e][:rows(cs)], dot_i[:, cs], preferred_element_type=F32))
            nxt = scores(*work[0])
            dqs = [[], []]
            for n, (e, cs) in enumerate(work):
                st, dpt = nxt
                kr = rows(cs)
                if n + 1 < len(work):
                    nxt = scores(*work[n + 1])
                if masked:
                    r = lax.broadcasted_iota(jnp.int32, (kr, cw), 0)
                    cc = lax.broadcasted_iota(jnp.int32, (kr, cw), 1) + cs.start
                    st = jnp.where(cc >= r, st, -jnp.inf)
                pt = jnp.exp2(st - lse[e][:, cs])
                dv_s[e][:, :kr] += lax.dot_general(dot_i[e * HEAD_V:(e + 1) * HEAD_V, cs], pt.astype(BF16), _NT,
                                                   preferred_element_type=F32)
                dsb = (pt * (dpt - dl[e][:, cs])).astype(BF16)
                dk_s[e][:, :kr] += lax.dot_general(qts[e][:, cs], dsb, _NT, preferred_element_type=F32)
                dqs[e].append(jnp.dot(kat[e][:, :kr], dsb, preferred_element_type=F32))
            for e in range(2):
                dq_ref[e, i] += jnp.concatenate(dqs[e], axis=1)

        def some_tiles(p, carry):
            for t in range(BWD_UNROLL):
                step(j + 1 + BWD_UNROLL * p + t, False)
            return carry

        def one_tile(i, carry):
            step(i, False)
            return carry

        trips = (nq - 1 - j) // BWD_UNROLL
        step(j, True)
        lax.fori_loop(0, trips, some_tiles, 0)
        lax.fori_loop(j + 1 + BWD_UNROLL * trips, nq, one_tile, 0)

        for e in range(2):
            dk_ref[e, 0] = dk_s[e][...] * jnp.where(own, 1.0 / LOG2E, 1.0)
        dv_ref[...] = jnp.concatenate([dv_s[0][...], dv_s[1][...]], axis=0).T.astype(dv_ref.dtype)
        if comm:
            @pl.when((pl.program_id(0) == N_HEADS // 2 - 1) & (j == nq - 1))
            def _():
                _comm_wait("exchange", c_src, c_dst, *c_sems)

    row4 = pl.BlockSpec((2, nq, 1, tq), lambda hp, j: (hp, 0, 0, 0))
    in_specs = [
        pl.BlockSpec((tq, 2 * LANES), lambda hp, j: (j, hp)),
        pl.BlockSpec((tq, LANES), lambda hp, j: (j, vcb + hp)),
        pl.BlockSpec((2, nq, LANES, tq), lambda hp, j: (hp, 0, 0, 0)),
        pl.BlockSpec((1, nq, LANES, tq), lambda hp, j: (hp, 0, 0, 0)),
        row4, row4,
    ]
    args = [k_aug, v, qt4, dot4, lse_row, dl_row]
    out_specs = [pl.BlockSpec((2, nq, LANES, tq), lambda hp, j: (hp, 0, 0, 0)),
                 pl.BlockSpec((2, 1, LANES, tq), lambda hp, j: (hp, j, 0, 0)),
                 pl.BlockSpec((tq, LANES), lambda hp, j: (j, hp))]
    out_shape = [jax.ShapeDtypeStruct((N_HEADS, nq, LANES, tq), F32), jax.ShapeDtypeStruct((N_HEADS, nq, LANES, tq), F32),
                 jax.ShapeDtypeStruct((T, N_HEADS * HEAD_V), BF16)]
    scratch = [pltpu.VMEM((LANES, tq), F32)] * 2 + [pltpu.VMEM((HEAD_V, tq), F32)] * 2
    if comm:
        in_specs.append(pl.BlockSpec(memory_space=pl.ANY))
        args.append(exchange)
        out_specs.append(pl.BlockSpec(memory_space=pl.ANY))
        out_shape.append(jax.ShapeDtypeStruct(exchange.shape, exchange.dtype))
        scratch += _comm_sems()
    return pl.pallas_call(
        body, name=name, grid=(N_HEADS // 2, nq),
        in_specs=in_specs, out_specs=out_specs, out_shape=out_shape, scratch_shapes=scratch,
        compiler_params=pltpu.CompilerParams(dimension_semantics=("arbitrary", "arbitrary"),
                                             vmem_limit_bytes=ATTN_BWD_VMEM_LIMIT),
    )(*args)


def _peers():
    x, y, c = lax.axis_index("x"), lax.axis_index("y"), lax.axis_index("c")
    me = 4 * x + 2 * y + c
    out = []
    for k in range(1, N_DEV):
        px = (1 - x) if (k & 4) else x
        py = (1 - y) if (k & 2) else y
        pc = (1 - c) if (k & 1) else c
        out.append(((px, py, pc), 4 * px + 2 * py + pc))
    return me, out


def _comm_sems():
    return [pltpu.SemaphoreType.DMA((N_DEV - 1,)), pltpu.SemaphoreType.DMA((N_DEV - 1,)), pltpu.SemaphoreType.DMA]


def _comm_copies(kind, src_ref, dst_ref, send_sems, recv_sems, local_sem):
    me, peers = _peers()
    local = pltpu.make_async_copy(src_ref if kind == "gather" else src_ref.at[me], dst_ref.at[me], local_sem)
    sends, recvs = [], []
    for k, (dev, lin) in enumerate(peers):
        src = src_ref if kind == "gather" else src_ref.at[lin]
        sends.append(pltpu.make_async_remote_copy(src_ref=src, dst_ref=dst_ref.at[me], send_sem=send_sems.at[k],
                                                  recv_sem=recv_sems.at[k], device_id=dev, device_id_type=MESH))
        recvs.append(pltpu.make_async_remote_copy(src_ref=src, dst_ref=dst_ref.at[lin], send_sem=send_sems.at[k],
                                                  recv_sem=recv_sems.at[k], device_id=dev, device_id_type=MESH))
    return local, sends, recvs


def _comm_start(kind, src_ref, dst_ref, send_sems, recv_sems, local_sem):
    local, sends, _ = _comm_copies(kind, src_ref, dst_ref, send_sems, recv_sems, local_sem)
    local.start()
    for cp in sends:
        cp.start()


def _comm_wait(kind, src_ref, dst_ref, send_sems, recv_sems, local_sem):
    local, sends, recvs = _comm_copies(kind, src_ref, dst_ref, send_sems, recv_sems, local_sem)
    for cp in recvs:
        cp.wait_recv()
    for cp in sends:
        cp.wait_send()
    local.wait()


def _gather_via_sibling(src, name):
    def body(x_ref, o_ref, send_sems, recv_sems, local_sem):
        x, y, c = lax.axis_index("x"), lax.axis_index("y"), lax.axis_index("c")
        me, sibling = (x, y, c), (x, y, 1 - c)
        chips = [(1 - x, y), (x, 1 - y), (1 - x, 1 - y)]

        def slot(px, py, pc):
            return o_ref.at[4 * px + 2 * py + pc]

        def copy(k, block, to, src_ref=None):
            return pltpu.make_async_remote_copy(src_ref=slot(*block) if src_ref is None else src_ref,
                                                dst_ref=slot(*block), send_sem=send_sems.at[k],
                                                recv_sem=recv_sems.at[k], device_id=to, device_id_type=MESH)

        mine = pltpu.make_async_copy(x_ref, slot(*me), local_sem)
        mine.start()
        first = [copy(0, me, sibling, x_ref)] + [copy(1 + n, me, (*chip, c), x_ref) for n, chip in enumerate(chips)]
        for cp in first:
            cp.start()
        passed = [copy(4 + n, (*chip, c), sibling) for n, chip in enumerate(chips)]
        for n, chip in enumerate(chips):
            copy(1 + n, (*chip, c), me).wait_recv()
            passed[n].start()
        copy(0, sibling, me).wait_recv()
        for n, chip in enumerate(chips):
            copy(4 + n, (*chip, 1 - c), me).wait_recv()
        for cp in first + passed:
            cp.wait_send()
        mine.wait()

    return pl.pallas_call(
        body, name=name,
        in_specs=[pl.BlockSpec(memory_space=pl.ANY)],
        out_specs=pl.BlockSpec(memory_space=pl.ANY),
        out_shape=jax.ShapeDtypeStruct((N_DEV,) + src.shape, src.dtype),
        scratch_shapes=_comm_sems(),
    )(src)


def _adamw(w, g, m, v):
    m2 = ADAM_B1 * m + (1.0 - ADAM_B1) * g
    v2 = ADAM_B2 * v + (1.0 - ADAM_B2) * (g * g)
    m_hat = m2 / (1.0 - ADAM_B1 ** ADAM_STEP)
    v_hat = v2 / (1.0 - ADAM_B2 ** ADAM_STEP)
    delta = -ADAM_LR * (m_hat / (jnp.sqrt(v_hat) + ADAM_EPS) + ADAM_WD * w)
    return delta, m2, v2


def _sum_adamw(pieces, w, m, v, *, name):
    R = w.shape[0]
    tr = R // 4 if R % 64 == 0 else R

    def body(p_ref, w_ref, m_ref, v_ref, g_ref, d_ref, m2_ref, v2_ref):
        g = p_ref[0].astype(F32)
        for s in range(1, N_DEV):
            g = g + p_ref[s].astype(F32)
        delta, m2, v2 = _adamw(w_ref[...], g, m_ref[...], v_ref[...])
        g_ref[...] = g
        d_ref[...] = delta
        m2_ref[...] = m2
        v2_ref[...] = v2

    row = pl.BlockSpec((tr, LANES), lambda i: (i, 0))
    return pl.pallas_call(
        body, name=name, grid=(R // tr,),
        in_specs=[pl.BlockSpec((N_DEV, tr, LANES), lambda i: (0, i, 0)), row, row, row],
        out_specs=[row, row, row, row],
        out_shape=[jax.ShapeDtypeStruct((R, LANES), F32)] * 4,
        compiler_params=_cparams(("parallel",)),
    )(pieces, w, m, v)


def _small_allreduce_adamw(part, w, m, v):
    shape = part.shape

    def body(p_ref, w_ref, m_ref, v_ref, g_ref, d_ref, m2_ref, v2_ref, loss_ref, gath, send_sems, recv_sems):
        me, peers = _peers()
        gath[me] = p_ref[...]
        sends = []
        for k, (dev, _) in enumerate(peers):
            cp = pltpu.make_async_remote_copy(src_ref=p_ref, dst_ref=gath.at[me], send_sem=send_sems.at[k],
                                              recv_sem=recv_sems.at[k], device_id=dev, device_id_type=MESH)
            cp.start()
            sends.append(cp)
        for k, (dev, lin) in enumerate(peers):
            pltpu.make_async_remote_copy(src_ref=p_ref, dst_ref=gath.at[lin], send_sem=send_sems.at[k],
                                         recv_sem=recv_sems.at[k], device_id=dev, device_id_type=MESH).wait_recv()
        for cp in sends:
            cp.wait_send()
        g = gath[0]
        for s in range(1, N_DEV):
            g = g + gath[s]
        delta, m2, v2 = _adamw(w_ref[...], g, m_ref[...], v_ref[...])
        g_ref[...] = g
        d_ref[...] = delta
        m2_ref[...] = m2
        v2_ref[...] = v2
        sq = jnp.sum(g[SMALL_LOSS_ROW:SMALL_LOSS_ROW + 8, :], axis=1, keepdims=True)
        tot = jnp.sum(sq, axis=0, keepdims=True) * (0.5 / D_MODEL)
        loss_ref[...] = jnp.broadcast_to(tot, loss_ref.shape)

    vm = pl.BlockSpec(memory_space=pltpu.VMEM)
    return pl.pallas_call(
        body, name="small_allreduce_adamw",
        in_specs=[vm, vm, vm, vm],
        out_specs=[vm, vm, vm, vm, vm],
        out_shape=[jax.ShapeDtypeStruct(shape, F32)] * 4 + [jax.ShapeDtypeStruct((8, LANES), F32)],
        scratch_shapes=[pltpu.VMEM((N_DEV,) + shape, F32),
                        pltpu.SemaphoreType.DMA((N_DEV - 1,)), pltpu.SemaphoreType.DMA((N_DEV - 1,))],
    )(part, w, m, v)


def _perm_in_cols(w):
    z = lambda n: jnp.zeros(w.shape[:-1] + (n,), w.dtype)
    small = jnp.concatenate([w[..., 384:400], z(16), w[..., 1952:1960], z(24), w[..., 400:416], z(48)], -1)
    return jnp.concatenate([w[..., 1960:2984], w[..., 2984:4008], w[..., 416:928], w[..., 928:1440],
                            w[..., 1440:1952], w[..., 0:256], w[..., 256:384], small], -1)


def _unperm_in_cols(wp):
    s = wp[..., P_SMALL:]
    return jnp.concatenate([wp[..., P_CQ:P_CQ + 256], wp[..., P_CKV:P_CKV + 128], s[..., 0:16], s[..., 64:80],
                            wp[..., P_FQ:P_FQ + 512], wp[..., P_FK:P_FK + 512], wp[..., P_FV:P_FV + 512],
                            s[..., S_FL:S_FL + 8], wp[..., P_GA:P_GA + 1024], wp[..., P_GB:P_GB + 1024]], -1)


def _aug_uq_cols(w):
    r = w.shape[0]
    w3 = w.reshape(r, N_HEADS, NOPE + ROPE)
    z = jnp.zeros((r, N_HEADS, 32), w.dtype)
    return jnp.concatenate([w3[:, :, 64:80], w3[:, :, 0:48], w3[:, :, 80:96], w3[:, :, 48:64], z], -1).reshape(r, 1024)


def _unaug_uq_cols(wp):
    r = wp.shape[0]
    w3 = wp.reshape(r, N_HEADS, LANES)
    return jnp.concatenate([w3[:, :, 16:64], w3[:, :, 80:96], w3[:, :, 0:16], w3[:, :, 64:80]], -1).reshape(r, 768)


def _aug_uk_cols(w):
    r = w.shape[0]
    w3 = w.reshape(r, N_HEADS, NOPE)
    z = lambda n: jnp.zeros((r, N_HEADS, n), w.dtype)
    return jnp.concatenate([z(16), w3[:, :, 0:48], z(16), w3[:, :, 48:64], z(32)], -1).reshape(r, 1024)


def _unaug_uk_cols(wp):
    r = wp.shape[0]
    w3 = wp.reshape(r, N_HEADS, LANES)
    return jnp.concatenate([w3[:, :, 16:64], w3[:, :, 80:96]], -1).reshape(r, 512)


IN_SEGMENTS = ((0, 256, P_CQ), (256, 128, P_CKV), (384, 16, P_SMALL), (400, 16, P_SMALL + 64), (416, 512, P_FQ),
               (928, 512, P_FK), (1440, 512, P_FV), (1952, 8, P_SMALL + S_FL), (1960, 1024, P_GA), (2984, 1024, P_GB))
IN_SHARD = D_IN // N_DEV


def _perm_in_from_shards(w3):
    r = w3.shape[0]
    parts = []
    pos = 0
    for o0, n, p0 in sorted(IN_SEGMENTS, key=lambda t: t[2]):
        if p0 > pos:
            parts.append(jnp.zeros((r, p0 - pos), w3.dtype))
        a, b = o0, o0 + n
        for d in range(a // IN_SHARD, (b - 1) // IN_SHARD + 1):
            parts.append(w3[:, d, max(a, d * IN_SHARD) - d * IN_SHARD:min(b, (d + 1) * IN_SHARD) - d * IN_SHARD])
        pos = p0 + n
    parts.append(jnp.zeros((r, D_IN_PAD - pos), w3.dtype))
    return jnp.concatenate(parts, -1)


def _unperm_in_to_shards(gp):
    r = gp.shape[0]
    shards = []
    for d in range(N_DEV):
        a, b = d * IN_SHARD, (d + 1) * IN_SHARD
        parts = [gp[:, p0 + max(a, o0) - o0:p0 + min(b, o0 + n) - o0] for o0, n, p0 in IN_SEGMENTS
                 if max(a, o0) < min(b, o0 + n)]
        parts.append(jnp.zeros((r, _lane_pad(IN_SHARD) - IN_SHARD), gp.dtype))
        shards.append(jnp.concatenate(parts, -1))
    return jnp.stack(shards, 0)


def _lane_pad(c):
    return -(-c // LANES) * LANES


def _packed_cols(c, by_col):
    return _lane_pad(c) if by_col else c


def _pack_shards(group, shards, dtype):
    rows = _pack_rows(group)
    segs = []
    for name, r, c, by_col in group:
        s = shards[name].reshape(r, c).astype(dtype)
        segs.append(jnp.pad(s, ((0, 0), (0, _packed_cols(c, by_col) - c))).reshape(-1))
    flat = jnp.concatenate(segs)
    flat = jnp.pad(flat, (0, rows * LANES - flat.shape[0]))
    return flat.reshape(rows, LANES)


def _unpack_shards(group, packed):
    flat = packed.reshape(-1)
    out, off = {}, 0
    for name, r, c, by_col in group:
        cp = _packed_cols(c, by_col)
        out[name] = flat[off:off + r * cp].reshape(1, r, cp)[:, :, :c]
        off += r * cp
    return out


def _unpack_full(group, gathered):
    flat = gathered.reshape(N_DEV, -1)
    out, off = {}, 0
    for name, r, c, by_col in group:
        cp = _packed_cols(c, by_col)
        blk = flat[:, off:off + r * cp].reshape(N_DEV, r, cp)
        out[name] = blk.transpose(1, 0, 2) if by_col else blk.reshape(N_DEV * r, c)
        off += r * cp
    return out


def _full_cols(w3, c):
    return w3[:, :, :c].reshape(w3.shape[0], N_DEV * c)


def _pack_full(group, grads, dtype):
    rows = _pack_rows(group)
    segs = []
    for name, r, c, by_col in group:
        g = grads[name]
        cp = _packed_cols(c, by_col)
        if by_col and g.ndim == 2:
            g = jnp.pad(g.reshape(r, N_DEV, c), ((0, 0), (0, 0), (0, cp - c))).transpose(1, 0, 2)
        segs.append(g.reshape(N_DEV, r * cp).astype(dtype))
    flat = jnp.concatenate(segs, axis=1)
    flat = jnp.pad(flat, ((0, 0), (0, rows * LANES - flat.shape[1])))
    return flat.reshape(N_DEV, rows, LANES)


SMALL_LAYOUT = (("ln_pre_mix", 1024, 0), ("ln_post_mix", 1024, 8), ("ln_pre_mlp", 1024, 16),
                ("ln_post_mlp", 1024, 24), ("b_in", 4008, 32), ("q_a_norm", 256, 64), ("kv_a_norm", 128, 66))


def _pack_small(vals, extra=None):
    rows = []
    for name, n, _ in SMALL_LAYOUT:
        v = vals[name].reshape(-1).astype(F32)
        pad = -n % LANES
        rows.append(jnp.pad(v, (0, pad)).reshape(-1, LANES))
    rows.append(jnp.zeros((SMALL_LOSS_ROW - 67, LANES), F32))
    rows.append(jnp.zeros((8, LANES), F32) if extra is None else extra.reshape(8, LANES))
    return jnp.concatenate(rows, axis=0)


def _unpack_small(packed):
    out = {}
    for name, n, r0 in SMALL_LAYOUT:
        nr = -(-n // LANES)
        out[name] = packed[r0:r0 + nr].reshape(-1)[:n].reshape(1, n)
    return out


def _rms(xf, g):
    r = lax.rsqrt(jnp.mean(xf * xf, axis=-1, keepdims=True) + NORM_EPS)
    return (xf * r) * g


def _rms_bwd(xf, g, dy):
    r = lax.rsqrt(jnp.mean(xf * xf, axis=-1, keepdims=True) + NORM_EPS)
    xhat = xf * r
    dxhat = dy * g
    dx = r * (dxhat - xhat * jnp.mean(dxhat * xhat, axis=-1, keepdims=True))
    return dx, dy * xhat


def _sigmoid(t):
    return 1.0 / (1.0 + jnp.exp(-t))


def _log_sigmoid(t):
    return jnp.minimum(t, 0.0) - jnp.log(1.0 + jnp.exp(-jnp.abs(t)))


def _lane_sign():
    lane = lax.broadcasted_iota(jnp.int32, (1, LANES), 1)
    return jnp.where(lane < 64, -1.0, 1.0).astype(F32), lane


def _rope_lanes(lane):
    return (lane < 16) | ((lane >= 64) & (lane < 80))


def kernel(x, positions, ln_pre_mix, ln_post_mix, ln_pre_mlp, ln_post_mlp, w_in, b_in, q_a_norm, w_uq, kv_a_norm, w_uk, w_uv, w_o_mla, w_o_fox, w_out, w_ff1, w_ff2, loss_target, m_ln_pre_mix, m_ln_post_mix, m_ln_pre_mlp, m_ln_post_mlp, m_w_in, m_b_in, m_q_a_norm, m_w_uq, m_kv_a_norm, m_w_uk, m_w_uv, m_w_o_mla, m_w_o_fox, m_w_out, m_w_ff1, m_w_ff2, v_ln_pre_mix, v_ln_post_mix, v_ln_pre_mlp, v_ln_post_mlp, v_w_in, v_b_in, v_q_a_norm, v_w_uq, v_kv_a_norm, v_w_uk, v_w_uv, v_w_o_mla, v_w_o_fox, v_w_out, v_w_ff1, v_w_ff2):
    T = x.shape[1]
    x2 = x.reshape(T, D_MODEL)
    tgt = loss_target.reshape(T, D_MODEL)
    w_sh = dict(w_in=w_in, w_uq=w_uq, w_uk=w_uk, w_uv=w_uv, w_o_mla=w_o_mla, w_o_fox=w_o_fox, w_out=w_out,
                w_ff1=w_ff1, w_ff2=w_ff2)
    m_sh = dict(w_in=m_w_in, w_uq=m_w_uq, w_uk=m_w_uk, w_uv=m_w_uv, w_o_mla=m_w_o_mla, w_o_fox=m_w_o_fox,
                w_out=m_w_out, w_ff1=m_w_ff1, w_ff2=m_w_ff2)
    v_sh = dict(w_in=v_w_in, w_uq=v_w_uq, w_uk=v_w_uk, w_uv=v_w_uv, w_o_mla=v_w_o_mla, w_o_fox=v_w_o_fox,
                w_out=v_w_out, w_ff1=v_w_ff1, w_ff2=v_w_ff2)
    small_w = dict(ln_pre_mix=ln_pre_mix, ln_post_mix=ln_post_mix, ln_pre_mlp=ln_pre_mlp, ln_post_mlp=ln_post_mlp,
                   b_in=b_in, q_a_norm=q_a_norm, kv_a_norm=kv_a_norm)
    small_m = dict(ln_pre_mix=m_ln_pre_mix, ln_post_mix=m_ln_post_mix, ln_pre_mlp=m_ln_pre_mlp,
                   ln_post_mlp=m_ln_post_mlp, b_in=m_b_in, q_a_norm=m_q_a_norm, kv_a_norm=m_kv_a_norm)
    small_v = dict(ln_pre_mix=v_ln_pre_mix, ln_post_mix=v_ln_post_mix, ln_pre_mlp=v_ln_pre_mlp,
                   ln_post_mlp=v_ln_post_mlp, b_in=v_b_in, q_a_norm=v_q_a_norm, kv_a_norm=v_kv_a_norm)
    mla_scale = float((NOPE + ROPE) ** -0.5)
    fox_scale = float(FOX_D ** -0.5)

    W = _unpack_full(GROUP_A, _gather_via_sibling(_pack_shards(GROUP_A, w_sh, BF16), "allgather_weights_a"))
    w_in_p = _perm_in_from_shards(W["w_in"])
    b_in_p = _perm_in_cols(b_in.astype(F32))
    w_uq_a = _aug_uq_cols(_full_cols(W["w_uq"], 96))
    w_kv_a = jnp.concatenate([_aug_uk_cols(_full_cols(W["w_uk"], 64)), _full_cols(W["w_uv"], 64)], axis=1)
    g1, g2, g3, g4 = ln_pre_mix, ln_post_mix, ln_pre_mlp, ln_post_mlp
    gq, gkv = q_a_norm, kv_a_norm
    tq = min(ATTN_TILE, T)
    nq = T // tq

    (h,) = _rowwise(lambda xv, g: _rms(xv, g), [x2], [g1], [(D_MODEL, BF16)], name="pre_mix_norm")
    q_cols = jnp.ones((1, D_IN_PAD), F32).at[:, P_FQ:P_FQ + 512].set(fox_scale * LOG2E)
    z, zs = _mm(h, w_in_p, bias=b_in_p, colscale=q_cols, tm=2048, tail_f32=LANES, name="in_proj")

    def lora_norm(cq, ckv, a, b):
        return _rms(cq.astype(F32), a), _rms(ckv.astype(F32), b)

    cqn, ckvn = _rowwise(lora_norm, [(z, 256, P_CQ // 256), (z, 128, P_CKV // 128)], [gq, gkv],
                         [(Q_LORA, BF16), (KV_LORA, BF16)], name="lora_norm")
    q_x = _mm(cqn, w_uq_a, outs=(F32,), name="q_up")
    kv = _mm(ckvn, w_kv_a, tn=512, name="kv_up")

    half = ROPE // 2
    inv_freq = ROPE_THETA ** (-jnp.arange(half, dtype=F32) / half)
    inv128 = jnp.tile(inv_freq, LANES // half).reshape(1, LANES)
    pos_col = positions.reshape(T, 1).astype(F32)

    def rope_tables(p, f):
        ang = p * f
        return jnp.cos(ang), jnp.sin(ang)

    cos_t, sin_t = _rowwise(rope_tables, [pos_col], [inv128], [(LANES, F32), (LANES, F32)], name="rope_tables")

    def rope_fwd(qx, kn, vn, s, cs, sn):
        sign, lane = _lane_sign()
        rope_l = _rope_lanes(lane)
        rot = lambda t: t * cs + pltpu.roll(t, 64, 1) * sn * sign
        k_rot = jnp.where(rope_l, rot(s), 0.0)
        qs, ks = [], []
        for hd in range(N_HEADS):
            qb = qx[:, LANES * hd:LANES * (hd + 1)]
            qs.append(jnp.where(rope_l, rot(qb), qb))
            ks.append(kn[:, LANES * hd:LANES * (hd + 1)].astype(F32) + k_rot)
        return jnp.concatenate(qs, axis=1) * (mla_scale * LOG2E), jnp.concatenate(ks, axis=1), vn

    qt4_m, k_am, vt4_m = _rowwise(rope_fwd, [q_x, (kv, 1024, 0), (kv, 512, 2), zs, cos_t, sin_t], [],
                                  [(1024, BF16, "t4"), (1024, BF16), (512, BF16, "t4")], name="rope_fwd", tm=tq)
    f_cum = _seq_cumsum(zs, _log_sigmoid, reverse=False, name="forget_cumsum")

    def fox_aug(qf, kf, vf, fc):
        lane = lax.broadcasted_iota(jnp.int32, (1, LANES), 1)
        qs, ks = [], []
        for hd in range(N_HEADS):
            qp = qf[:, LANES * (hd // 2):LANES * (hd // 2 + 1)].astype(F32)
            kp = kf[:, LANES * (hd // 2):LANES * (hd // 2 + 1)].astype(F32)
            if hd % 2:
                qp = pltpu.roll(qp, 64, 1)
                kp = pltpu.roll(kp, 64, 1)
            fb = jnp.broadcast_to(fc[:, S_FL + hd:S_FL + hd + 1] * (-LOG2E), qp.shape)
            hi = fb.astype(BF16).astype(F32)
            mid = (fb - hi).astype(BF16).astype(F32)
            lo = fb - hi - mid
            qs.append(jnp.where(lane < 64, qp, jnp.where(lane < 67, 1.0, 0.0)))
            ks.append(jnp.where(lane < 64, kp, jnp.where(lane == 64, hi, jnp.where(lane == 65, mid, jnp.where(
                lane == 66, lo, jnp.where(lane == 67, 1.0, 0.0))))))
        return jnp.concatenate(qs, axis=1), jnp.concatenate(ks, axis=1), vf

    qt4_f, k_af, vt4_f = _rowwise(fox_aug, [(z, 512, P_FQ // 512), (z, 512, P_FK // 512), (z, 512, P_FV // 512), f_cum],
                                  [], [(1024, BF16, "t4"), (1024, BF16), (512, BF16, "t4")], name="fox_aug", tm=tq)
    o_mla, lse_mla, gathered_b = _attn_fwd(qt4_m, k_am, vt4_m, name="mla_attn_fwd", tq=tq,
                                           gather=_pack_shards(GROUP_B, w_sh, BF16))
    W.update(_unpack_full(GROUP_B, gathered_b))
    for n, _, c, by_col in GROUP_B:
        if by_col:
            W[n] = _full_cols(W[n], c)
    o_fox, lse_fox = _attn_fwd(qt4_f, k_af, vt4_f, name="fox_attn_fwd", tq=tq)
    y_mla = _mm(o_mla, W["w_o_mla"], outs=(F32,), name="o_proj_mla")
    y_fox = _mm(o_fox, W["w_o_fox"], outs=(F32,), name="o_proj_fox")

    def gate_merge(ga, gb, ya, yb):
        return _sigmoid(ga.astype(F32)) * ya + _sigmoid(gb.astype(F32)) * yb

    (merged,) = _rowwise(gate_merge, [(z, 1024, 0), (z, 1024, 1), y_mla, y_fox], [], [(D_MODEL, BF16)],
                         name="gate_merge")
    mix = _mm(merged, W["w_out"], outs=(F32,), name="out_proj")

    def post_mix(xv, mv, a, b):
        x1v = xv + _rms(mv, a)
        return x1v, _rms(x1v, b)

    x1, h2 = _rowwise(post_mix, [x2, mix], [g2, g3], [(D_MODEL, F32), (D_MODEL, BF16)], name="post_mix_norm")

    def relu2(acc):
        r = jnp.maximum(acc, 0.0)
        return r * r, acc

    act, u = _mm(h2, W["w_ff1"], epi=relu2, outs=(BF16, BF16), tm=2048, name="ff1")
    mlp = _mm(act, W["w_ff2"], outs=(F32,), tm=1024, tk=2048, name="ff2")

    def loss_bwd(x1v, mv, tv, g):
        y = x1v + _rms(mv, g)
        d = y - tv
        dy = d * (1.0 / D_MODEL)
        dm, dg = _rms_bwd(mv, g, dy)
        return dy, dm, dg, d * d

    dy, dm, dg4, loss_cols = _rowwise(loss_bwd, [x1, mlp, tgt], [g4], [(D_MODEL, F32), (D_MODEL, BF16)],
                                      [D_MODEL, D_MODEL], name="loss_bwd")

    def relu2_bwd(acc, uv):
        return (acc * (2.0 * jnp.maximum(uv.astype(F32), 0.0)),)

    du = _mm(dm, W["w_ff2"], tb=True, extras=(u,), epi=relu2_bwd, tm=2048, name="ff2_bwd")
    dw_ff2 = _mm(act, dm, ta=True, outs=(BF16,), tk=2048, name="ff2_wgrad")
    dh2 = _mm(du, W["w_ff1"], tb=True, outs=(F32,), tm=1024, tk=2048, name="ff1_bwd")
    dw_ff1 = _mm(h2, du, ta=True, outs=(BF16,), tk=2048, name="ff1_wgrad")

    def post_mix_bwd(x1v, dh2v, dyv, mv, a, b):
        d3, dg3v = _rms_bwd(x1v, b, dh2v)
        dx1v = dyv + d3
        dmixv, dg2v = _rms_bwd(mv, a, dx1v)
        return dx1v, dmixv, dg3v, dg2v

    dx1, dmix, dg3, dg2 = _rowwise(post_mix_bwd, [x1, dh2, dy, mix], [g2, g3], [(D_MODEL, F32), (D_MODEL, BF16)],
                                   [D_MODEL, D_MODEL], name="post_mix_bwd")
    dmerged = _mm(dmix, W["w_out"], tb=True, outs=(F32,), name="out_proj_bwd")
    dw_out = _mm(merged, dmix, ta=True, outs=(BF16,), name="out_proj_wgrad")

    def gate_bwd(dmg, ga, gb, ya, yb):
        sa = _sigmoid(ga.astype(F32))
        sb = _sigmoid(gb.astype(F32))
        return dmg * sa, dmg * sb, dmg * ya * sa * (1.0 - sa), dmg * yb * sb * (1.0 - sb)

    dy_mla, dy_fox, dga, dgb = _rowwise(gate_bwd, [dmerged, (z, 1024, 0), (z, 1024, 1), y_mla, y_fox], [],
                                        [(D_MODEL, BF16)] * 4, name="gate_bwd")
    twice = lambda acc: (acc, acc)
    do_mla, dot4_m = _mm(dy_mla, W["w_o_mla"], tb=True, epi=twice, outs=(BF16, BF16), t4=(False, True), tm=tq,
                         name="o_proj_mla_bwd")
    do_fox, dot4_f = _mm(dy_fox, W["w_o_fox"], tb=True, epi=twice, outs=(BF16, BF16), t4=(False, True), tm=tq,
                         name="o_proj_fox_bwd")
    dw_o_mla = _mm(o_mla, dy_mla, ta=True, outs=(BF16,), name="o_proj_mla_wgrad")
    dw_o_fox = _mm(o_fox, dy_fox, ta=True, outs=(BF16,), name="o_proj_fox_wgrad")

    def head_dots(ov, dov):
        r = lax.broadcasted_iota(jnp.int32, (N_HEADS * HEAD_V, LANES), 0)
        cc = lax.broadcasted_iota(jnp.int32, (N_HEADS * HEAD_V, LANES), 1)
        sel = jnp.where((r // HEAD_V) == cc, 1.0, 0.0).astype(BF16)
        return _dot3(ov.astype(F32) * dov.astype(F32), sel, left=False)

    (dl_mla,) = _rowwise(head_dots, [o_mla, do_mla], [], [(LANES, F32)], name="mla_attn_delta")
    (dl_fox,) = _rowwise(head_dots, [o_fox, do_fox], [], [(LANES, F32)], name="fox_attn_delta")
    heads_row4 = lambda t: t[:, :N_HEADS].T.reshape(N_HEADS, nq, 1, tq)
    grads_b = dict(w_o_mla=dw_o_mla, w_o_fox=dw_o_fox, w_out=dw_out, w_ff1=dw_ff1, w_ff2=dw_ff2)
    dqt_m, dkt_m, dv_mla, pieces_b = _attn_bwd(qt4_m, k_am, kv, 1024 // LANES, dot4_m, lse_mla,
                                               heads_row4(dl_mla), mla_scale, NOPE + ROPE, name="mla_attn_bwd", tq=tq,
                                               exchange=_pack_full(GROUP_B, grads_b, BF16))
    dqt_f, dkt_f, dv_fox = _attn_bwd(qt4_f, k_af, z, P_FV // LANES, dot4_f, lse_fox, heads_row4(dl_fox),
                                     fox_scale, FOX_D, name="fox_attn_bwd", tq=tq)

    def fox_unpack(dqa, dka):
        lane = lax.broadcasted_iota(jnp.int32, (1, LANES), 1)
        dqs, dks = [], []
        d_f = jnp.zeros(dqa[:, :LANES].shape, F32)
        for hp in range(N_HEADS // 2):
            blk = lambda t, e: t[:, LANES * (2 * hp + e):LANES * (2 * hp + e + 1)]
            dqs.append(jnp.where(lane < 64, blk(dqa, 0), pltpu.roll(blk(dqa, 1), 64, 1)))
            dks.append(jnp.where(lane < 64, blk(dka, 0), pltpu.roll(blk(dka, 1), 64, 1)))
            for e in range(2):
                g = blk(dqa, e)[:, 67:68] - blk(dka, e)[:, 64:65]
                d_f = jnp.where(lane == S_FL + 2 * hp + e, g, d_f)
        return jnp.concatenate(dqs, axis=1), jnp.concatenate(dks, axis=1), d_f

    dq_fox, dk_fox, d_f128 = _rowwise(fox_unpack, [("t4", dqt_f), ("t4", dkt_f)], [],
                                      [(512, BF16), (512, BF16), (LANES, F32)], name="fox_unpack", tm=tq)
    df_rev = _seq_cumsum(d_f128, lambda t: t, reverse=True, name="forget_cumsum_bwd")

    def rope_bwd(dqa, dka, dvm, dfr, s, cs, sn):
        sign, lane = _lane_sign()
        rope_l = _rope_lanes(lane)
        rot_t = lambda t: t * cs - pltpu.roll(t, 64, 1) * sn * sign
        dqs = []
        dk_rot = None
        for hd in range(N_HEADS):
            blk = dqa[:, LANES * hd:LANES * (hd + 1)]
            dqs.append(jnp.where(rope_l, rot_t(blk), blk))
            blk = dka[:, LANES * hd:LANES * (hd + 1)]
            dk_rot = blk if dk_rot is None else dk_rot + blk
        dfl = dfr * _sigmoid(-s)
        small = jnp.where(rope_l, rot_t(dk_rot), jnp.where((lane >= S_FL) & (lane < S_FL + N_HEADS), dfl, 0.0))
        return jnp.concatenate(dqs, axis=1), jnp.concatenate([dka, dvm.astype(F32)], axis=1), small

    dq_b, dkv, d_small = _rowwise(rope_bwd, [("t4", dqt_m), ("t4", dkt_m), dv_mla, df_rev, zs, cos_t, sin_t], [],
                                  [(1024, BF16), (1536, BF16), (LANES, BF16)], name="rope_bwd", tm=tq)
    dcqn = _mm(dq_b, w_uq_a, tb=True, outs=(F32,), name="q_up_bwd")
    dw_uq_a = _mm(cqn, dq_b, ta=True, outs=(BF16,), name="q_up_wgrad")
    dckvn = _mm(dkv, w_kv_a, tb=True, outs=(F32,), tk=512, name="kv_up_bwd")
    dw_kv_a = _mm(ckvn, dkv, ta=True, outs=(BF16,), tn=512, name="kv_up_wgrad")

    def lora_norm_bwd(cq, ckv, dq_, dkv_, a, b):
        d1, dga_ = _rms_bwd(cq.astype(F32), a, dq_)
        d2, dgb_ = _rms_bwd(ckv.astype(F32), b, dkv_)
        return d1, d2, dga_, dgb_

    dcq, dckv, dgq, dgkv = _rowwise(lora_norm_bwd, [(z, 256, P_CQ // 256), (z, 128, P_CKV // 128), dcqn, dckvn],
                                    [gq, gkv], [(Q_LORA, BF16), (KV_LORA, BF16)], [Q_LORA, KV_LORA],
                                    name="lora_norm_bwd")
    dz = jnp.concatenate([dga, dgb, dq_fox, dk_fox, dv_fox, dcq, dckv, d_small], -1)
    dw_in_p, db_in_p = _mm(h, dz, ta=True, outs=(BF16,), tk=2048, colsum=True, name="in_proj_wgrad")
    grads_a = dict(w_in=_unperm_in_to_shards(dw_in_p), w_uq=_unaug_uq_cols(dw_uq_a),
                   w_uk=_unaug_uk_cols(dw_kv_a[:, :1024]), w_uv=dw_kv_a[:, 1024:])
    dh, pieces_a = _mm(dz, w_in_p, tb=True, outs=(F32,), tm=1024, tk=2048, name="in_proj_bwd",
                       exchange=_pack_full(GROUP_A, grads_a, BF16))

    def pre_mix_bwd(xv, dhv, dx1v, g):
        d, dg = _rms_bwd(xv, g, dhv)
        return dx1v + d, dg

    grad_x, dg1 = _rowwise(pre_mix_bwd, [x2, dh, dx1], [g1], [(D_MODEL, F32)], [D_MODEL], name="pre_mix_bwd")

    grad_sh, delta_sh, newm_sh, newv_sh = {}, {}, {}, {}
    for group, pieces, tag in ((GROUP_A, pieces_a, "a"), (GROUP_B, pieces_b, "b")):
        packed = _sum_adamw(pieces, _pack_shards(group, w_sh, F32), _pack_shards(group, m_sh, F32),
                            _pack_shards(group, v_sh, F32), name="sum_pieces_adamw_" + tag)
        for dst, arr in zip((grad_sh, delta_sh, newm_sh, newv_sh), packed):
            dst.update(_unpack_shards(group, arr))

    small_part = _pack_small(dict(ln_pre_mix=dg1, ln_post_mix=dg2, ln_pre_mlp=dg3, ln_post_mlp=dg4,
                                  b_in=_unperm_in_cols(db_in_p), q_a_norm=dgq, kv_a_norm=dgkv), extra=loss_cols)
    sg, sd, sm, sv, loss_blk = _small_allreduce_adamw(small_part, _pack_small(small_w), _pack_small(small_m),
                                                      _pack_small(small_v))
    grad_sm, delta_sm, newm_sm, newv_sm = (_unpack_small(t) for t in (sg, sd, sm, sv))
    loss = loss_blk[0, 0]

    order = ["ln_pre_mix", "ln_post_mix", "ln_pre_mlp", "ln_post_mlp", "w_in", "b_in", "q_a_norm", "w_uq",
             "kv_a_norm", "w_uk", "w_uv", "w_o_mla", "w_o_fox", "w_out", "w_ff1", "w_ff2"]

    def pick(sm_d, sh_d):
        return [sm_d[n] if n in sm_d else sh_d[n] for n in order]

    return (loss, grad_x.reshape(1, T, D_MODEL), *pick(grad_sm, grad_sh), *pick(delta_sm, delta_sh),
            *pick(newm_sm, newm_sh), *pick(newv_sm, newv_sh))
```
